```python
import math
import jax, jax.numpy as jnp
from jax import lax
import numpy as np

D_MODEL = 2048
BATCH = 2
SEQ = 16384
DEPTH = 2
DEC_BATCH = 32
DEC_SEQ = 32
PAST_LEN = 2048

CHUNK = 64
N_MIXERS = 2
N_S5_LAYERS = (DEPTH + 1) // 2
N_GDN_LAYERS = DEPTH // 2
RMS_EPS = 1e-6
L2_EPS = 1e-6
S5_GROUP = 16
S5_GROUPS = D_MODEL // S5_GROUP
S5_STATE = 64
S5_DT_MIN = 1e-3
S5_DT_MAX = 1e-1
GDN_HEAD_DIM = 128
GDN_QK_HEADS = D_MODEL // GDN_HEAD_DIM
GDN_V_HEADS = 2 * GDN_QK_HEADS
GDN_REP = GDN_V_HEADS // GDN_QK_HEADS
GDN_KEY_DIM = GDN_QK_HEADS * GDN_HEAD_DIM
GDN_VAL_DIM = GDN_V_HEADS * GDN_HEAD_DIM
GDN_CONV_DIM = 2 * GDN_KEY_DIM + GDN_VAL_DIM
GDN_CONV_WIDTH = 4
GDN_IN_DIM = GDN_CONV_DIM + GDN_VAL_DIM + 2 * GDN_V_HEADS
D_FF = 4 * D_MODEL

kernel_name = 'hybrid_s5_gdn_stream_step'

F32 = jnp.float32


def _rmsnorm(x, w):
    x32 = x.astype(F32)
    y = x32 * lax.rsqrt(jnp.mean(x32 * x32, axis=-1, keepdims=True) + RMS_EPS)
    return (y * w.astype(F32)).astype(x.dtype)


def _l2norm(x):
    return x * lax.rsqrt(jnp.sum(x * x, axis=-1, keepdims=True) + L2_EPS)


def _sq_relu_mlp(h, w_up, w_down):
    a = jax.nn.relu(h @ w_up)
    return (a * a) @ w_down


def _s5_discretise(lam_re, lam_im, log_dt, b_re, b_im):
    lr = jnp.minimum(lam_re.astype(F32), -1e-4)
    li = lam_im.astype(F32)
    dt = jnp.exp(log_dt.astype(F32))[:, None]
    mag = jnp.exp(lr * dt)
    ang = li * dt
    a_re = mag * jnp.cos(ang)
    a_im = mag * jnp.sin(ang)
    nr = a_re - 1.0
    den = lr * lr + li * li
    k_re = (nr * lr + a_im * li) / den
    k_im = (a_im * lr - nr * li) / den
    br = b_re.astype(F32)
    bi = b_im.astype(F32)
    bb_re = k_re[..., None] * br - k_im[..., None] * bi
    bb_im = k_re[..., None] * bi + k_im[..., None] * br
    return a_re, a_im, bb_re, bb_im


def _complex_affine_combine(e1, e2):
    a1r, a1i, b1r, b1i = e1
    a2r, a2i, b2r, b2i = e2
    return (a2r * a1r - a2i * a1i,
            a2r * a1i + a2i * a1r,
            a2r * b1r - a2i * b1i + b2r,
            a2r * b1i + a2i * b1r + b2i)


def _s5_scan(u, s0_re, s0_im, a_re, a_im, bb_re, bb_im, c_re, c_im, block):
    n_b, t_len, _ = u.shape
    n_blk = t_len // block
    uc = u.astype(F32).reshape(n_b, n_blk, block, S5_GROUPS, S5_GROUP).transpose(1, 0, 2, 3, 4)
    shape = (n_b, block, S5_GROUPS, S5_STATE)
    a_re_b = jnp.broadcast_to(a_re, shape)
    a_im_b = jnp.broadcast_to(a_im, shape)
    cr = c_re.astype(F32)
    ci = c_im.astype(F32)

    def step(carry, u_blk):
        sr, si = carry
        bu_re = jnp.einsum('blgh,gph->blgp', u_blk, bb_re)
        bu_im = jnp.einsum('blgh,gph->blgp', u_blk, bb_im)
        pr, pi, qr, qi = lax.associative_scan(
            _complex_affine_combine, (a_re_b, a_im_b, bu_re, bu_im), axis=1)
        st_re = pr * sr[:, None] - pi * si[:, None] + qr
        st_im = pr * si[:, None] + pi * sr[:, None] + qi
        y = jnp.einsum('blgp,ghp->blgh', st_re, cr) - jnp.einsum('blgp,ghp->blgh', st_im, ci)
        return (st_re[:, -1], st_im[:, -1]), y

    (sr, si), ys = lax.scan(step, (s0_re.astype(F32), s0_im.astype(F32)), uc)
    y = ys.transpose(1, 0, 2, 3, 4).reshape(n_b, t_len, D_MODEL)
    return y, sr, si


def _s5_mixer(h, s0_re, s0_im, lam_re, lam_im, log_dt, b_re, b_im, c_re, c_im,
              d_skip, w_glu, block):
    a_re, a_im, bb_re, bb_im = _s5_discretise(lam_re, lam_im, log_dt, b_re, b_im)
    y, sr, si = _s5_scan(h, s0_re, s0_im, a_re, a_im, bb_re, bb_im, c_re, c_im, block)
    y = y + d_skip.astype(F32) * h.astype(F32)
    g = jax.nn.gelu(y).astype(h.dtype)
    zz = g @ w_glu
    out = zz[..., :D_MODEL] * jax.nn.sigmoid(zz[..., D_MODEL:])
    return out, sr, si


def _gated_delta_rule(q, k, v, g, beta, s0, block):
    n_b, t_len = q.shape[:2]
    n_blk = t_len // block

    def blocks4(t):
        return t.reshape((n_b, n_blk, block) + t.shape[2:]).transpose(1, 0, 3, 2, 4)

    def blocks3(t):
        return t.reshape(n_b, n_blk, block, t.shape[2]).transpose(1, 0, 3, 2)

    causal = jnp.tril(jnp.ones((block, block), dtype=bool))

    def step(S, blk):
        qb, kb, vb, gb, bb = blk
        qb = jnp.repeat(qb, GDN_REP, axis=1)
        kb = jnp.repeat(kb, GDN_REP, axis=1)
        gc = jnp.cumsum(gb, axis=-1)
        diff = gc[..., :, None] - gc[..., None, :]
        decay = jnp.exp(jnp.where(causal, diff, -jnp.inf))
        kbeta = kb * bb[..., None]
        lower = jnp.einsum('bhid,bhjd->bhij', kbeta, kb) * decay
        rhs = jnp.concatenate([vb * bb[..., None], kbeta * jnp.exp(gc)[..., None]], axis=-1)
        sol = lax.linalg.triangular_solve(lower, rhs, left_side=True, lower=True,
                                          unit_diagonal=True)
        u_blk = sol[..., :GDN_HEAD_DIM]
        w_blk = sol[..., GDN_HEAD_DIM:]
        v_new = u_blk - jnp.einsum('bhld,bhde->bhle', w_blk, S)
        intra = jnp.einsum('bhid,bhjd->bhij', qb, kb) * decay
        o = (jnp.einsum('bhld,bhde->bhle', qb * jnp.exp(gc)[..., None], S)
             + jnp.einsum('bhij,bhje->bhie', intra, v_new))
        g_last = gc[..., -1:]
        S_new = (S * jnp.exp(g_last)[..., None]
                 + jnp.einsum('bhld,bhle->bhde', kb * jnp.exp(g_last - gc)[..., None], v_new))
        return S_new, o

    xs = (blocks4(q), blocks4(k), blocks4(v), blocks3(g), blocks3(beta))
    s_fin, os_ = lax.scan(step, s0.astype(F32), xs)
    o = os_.transpose(1, 0, 3, 2, 4).reshape(n_b, t_len, GDN_V_HEADS, GDN_HEAD_DIM)
    return o, s_fin


def _gdn_mixer(h, conv_prev, s0, w_in, conv_w, a_log, dt_bias, norm_w, w_out, block):
    n_b, t_len, _ = h.shape
    proj = h @ w_in
    qkv = proj[..., :GDN_CONV_DIM]
    z = proj[..., GDN_CONV_DIM:GDN_CONV_DIM + GDN_VAL_DIM]
    b = proj[..., GDN_CONV_DIM + GDN_VAL_DIM:GDN_CONV_DIM + GDN_VAL_DIM + GDN_V_HEADS]
    a = proj[..., GDN_CONV_DIM + GDN_VAL_DIM + GDN_V_HEADS:]
    xpad = jnp.concatenate([conv_prev.astype(qkv.dtype), qkv], axis=1)
    new_conv = xpad[:, -(GDN_CONV_WIDTH - 1):]
    conv = xpad[:, 0:t_len] * conv_w[0]
    for j in range(1, GDN_CONV_WIDTH):
        conv = conv + xpad[:, j:j + t_len] * conv_w[j]
    conv = jax.nn.silu(conv).astype(F32)
    q = conv[..., :GDN_KEY_DIM].reshape(n_b, t_len, GDN_QK_HEADS, GDN_HEAD_DIM)
    k = conv[..., GDN_KEY_DIM:2 * GDN_KEY_DIM].reshape(n_b, t_len, GDN_QK_HEADS, GDN_HEAD_DIM)
    v = conv[..., 2 * GDN_KEY_DIM:].reshape(n_b, t_len, GDN_V_HEADS, GDN_HEAD_DIM)
    q = _l2norm(q) * (GDN_HEAD_DIM ** -0.5)
    k = _l2norm(k)
    g = -jnp.exp(a_log.astype(F32)) * jax.nn.softplus(a.astype(F32) + dt_bias.astype(F32))
    beta = jax.nn.sigmoid(b.astype(F32))
    o, s_fin = _gated_delta_rule(q, k, v, g, beta, s0, block)
    o = o * lax.rsqrt(jnp.mean(o * o, axis=-1, keepdims=True) + RMS_EPS) * norm_w.astype(F32)
    o = o * jax.nn.silu(z.astype(F32).reshape(n_b, t_len, GDN_V_HEADS, GDN_HEAD_DIM))
    out = o.reshape(n_b, t_len, GDN_VAL_DIM).astype(h.dtype) @ w_out
    return out, new_conv, s_fin


def setup_inputs(seed: int = 0) -> dict:
    key = jax.random.key(seed)
    ks = jax.random.split(key, 32)
    nrm = jax.random.normal
    n_a, n_g = N_S5_LAYERS, N_GDN_LAYERS
    x_prompt = nrm(ks[0], (BATCH, SEQ, D_MODEL), F32)
    x_sample = nrm(ks[1], (DEC_BATCH, DEC_SEQ, D_MODEL), F32)
    state_s5_re = 0.5 * nrm(ks[2], (n_a, DEC_BATCH, S5_GROUPS, S5_STATE), F32)
    state_s5_im = 0.5 * nrm(ks[3], (n_a, DEC_BATCH, S5_GROUPS, S5_STATE), F32)
    state_gdn_conv = nrm(ks[4], (n_g, DEC_BATCH, GDN_CONV_WIDTH - 1, GDN_CONV_DIM), F32)
    state_gdn_s = 0.5 * nrm(ks[5], (n_g, DEC_BATCH, GDN_V_HEADS, GDN_HEAD_DIM, GDN_HEAD_DIM), F32)
    norm_mix = 1.0 + 0.02 * nrm(ks[6], (DEPTH, D_MODEL), F32)
    norm_ffn = 1.0 + 0.02 * nrm(ks[7], (DEPTH, D_MODEL), F32)
    norm_final = 1.0 + 0.02 * nrm(ks[8], (D_MODEL,), F32)
    n_idx = jnp.arange(S5_STATE, dtype=F32)
    s5_lam_re = -0.5 + 0.01 * nrm(ks[9], (n_a, S5_GROUPS, S5_STATE), F32)
    s5_lam_im = math.pi * n_idx + 0.01 * nrm(ks[10], (n_a, S5_GROUPS, S5_STATE), F32)
    s5_log_dt = jax.random.uniform(ks[11], (n_a, S5_GROUPS), F32,
                                   math.log(S5_DT_MIN), math.log(S5_DT_MAX))
    b_scale = (2.0 * S5_GROUP) ** -0.5
    c_scale = (2.0 * S5_STATE) ** -0.5
    s5_b_re = b_scale * nrm(ks[12], (n_a, S5_GROUPS, S5_STATE, S5_GROUP), F32)
    s5_b_im = b_scale * nrm(ks[13], (n_a, S5_GROUPS, S5_STATE, S5_GROUP), F32)
    s5_c_re = c_scale * nrm(ks[14], (n_a, S5_GROUPS, S5_GROUP, S5_STATE), F32)
    s5_c_im = c_scale * nrm(ks[15], (n_a, S5_GROUPS, S5_GROUP, S5_STATE), F32)
    s5_d = nrm(ks[16], (n_a, D_MODEL), F32)
    s5_w_glu = D_MODEL ** -0.5 * nrm(ks[17], (n_a, D_MODEL, 2 * D_MODEL), F32)
    gdn_w_in = D_MODEL ** -0.5 * nrm(ks[18], (n_g, D_MODEL, GDN_IN_DIM), F32)
    gdn_conv_w = 0.5 * nrm(ks[19], (n_g, GDN_CONV_WIDTH, GDN_CONV_DIM), F32)
    gdn_a_log = jnp.log(jax.random.uniform(ks[20], (n_g, GDN_V_HEADS), F32, 1.0, 16.0))
    dt = jnp.exp(jax.random.uniform(ks[21], (n_g, GDN_V_HEADS), F32,
                                    math.log(1e-3), math.log(1e-1)))
    gdn_dt_bias = dt + jnp.log(-jnp.expm1(-dt))
    gdn_norm_w = 1.0 + 0.02 * nrm(ks[22], (n_g, GDN_HEAD_DIM), F32)
    gdn_w_out = GDN_VAL_DIM ** -0.5 * nrm(ks[23], (n_g, GDN_VAL_DIM, D_MODEL), F32)
    ffn_w_up = D_MODEL ** -0.5 * nrm(ks[24], (DEPTH, D_MODEL, D_FF), F32)
    ffn_w_down = D_FF ** -0.5 * nrm(ks[25], (DEPTH, D_FF, D_MODEL), F32)
    return {'x_prompt': x_prompt, 'x_sample': x_sample,
            'state_s5_re': state_s5_re, 'state_s5_im': state_s5_im,
            'state_gdn_conv': state_gdn_conv, 'state_gdn_s': state_gdn_s,
            'norm_mix': norm_mix, 'norm_ffn': norm_ffn, 'norm_final': norm_final,
            's5_lam_re': s5_lam_re, 's5_lam_im': s5_lam_im, 's5_log_dt': s5_log_dt,
            's5_b_re': s5_b_re, 's5_b_im': s5_b_im, 's5_c_re': s5_c_re, 's5_c_im': s5_c_im,
            's5_d': s5_d, 's5_w_glu': s5_w_glu,
            'gdn_w_in': gdn_w_in, 'gdn_conv_w': gdn_conv_w, 'gdn_a_log': gdn_a_log,
            'gdn_dt_bias': gdn_dt_bias, 'gdn_norm_w': gdn_norm_w, 'gdn_w_out': gdn_w_out,
            'ffn_w_up': ffn_w_up, 'ffn_w_down': ffn_w_down}


def reference(x_prompt, x_sample, state_s5_re, state_s5_im, state_gdn_conv, state_gdn_s,
              norm_mix, norm_ffn, norm_final,
              s5_lam_re, s5_lam_im, s5_log_dt, s5_b_re, s5_b_im, s5_c_re, s5_c_im,
              s5_d, s5_w_glu,
              gdn_w_in, gdn_conv_w, gdn_a_log, gdn_dt_bias, gdn_norm_w, gdn_w_out,
              ffn_w_up, ffn_w_down):

    def trunk(x, s5_re0, s5_im0, conv0, s0, block):
        new_re, new_im, new_conv, new_s = [], [], [], []
        for layer in range(DEPTH):
            idx = layer // N_MIXERS
            h = _rmsnorm(x, norm_mix[layer])
            if layer % N_MIXERS == 0:
                mix, sr, si = _s5_mixer(h, s5_re0[idx], s5_im0[idx], s5_lam_re[idx],
                                        s5_lam_im[idx], s5_log_dt[idx], s5_b_re[idx],
                                        s5_b_im[idx], s5_c_re[idx], s5_c_im[idx],
                                        s5_d[idx], s5_w_glu[idx], block)
                new_re.append(sr)
                new_im.append(si)
            else:
                mix, cv, st = _gdn_mixer(h, conv0[idx], s0[idx], gdn_w_in[idx],
                                         gdn_conv_w[idx], gdn_a_log[idx], gdn_dt_bias[idx],
                                         gdn_norm_w[idx], gdn_w_out[idx], block)
                new_conv.append(cv)
                new_s.append(st)
            x = x + mix.astype(x.dtype)
            x = x + _sq_relu_mlp(_rmsnorm(x, norm_ffn[layer]), ffn_w_up[layer],
                                 ffn_w_down[layer]).astype(x.dtype)
        return (_rmsnorm(x, norm_final), jnp.stack(new_re), jnp.stack(new_im),
                jnp.stack(new_conv), jnp.stack(new_s))

    n_bp = x_prompt.shape[0]
    z_re = jnp.zeros((N_S5_LAYERS, n_bp, S5_GROUPS, S5_STATE), F32)
    z_im = jnp.zeros((N_S5_LAYERS, n_bp, S5_GROUPS, S5_STATE), F32)
    z_conv = jnp.zeros((N_GDN_LAYERS, n_bp, GDN_CONV_WIDTH - 1, GDN_CONV_DIM), x_prompt.dtype)
    z_s = jnp.zeros((N_GDN_LAYERS, n_bp, GDN_V_HEADS, GDN_HEAD_DIM, GDN_HEAD_DIM), F32)

    y_prompt, p_s5_re, p_s5_im, p_gdn_conv, p_gdn_s = trunk(
        x_prompt, z_re, z_im, z_conv, z_s, CHUNK)
    y_sample, s_s5_re, s_s5_im, s_gdn_conv, s_gdn_s = trunk(
        x_sample, state_s5_re, state_s5_im, state_gdn_conv, state_gdn_s, x_sample.shape[1])
    return (y_prompt, y_sample, p_s5_re, p_s5_im, p_gdn_conv, p_gdn_s,
            s_s5_re, s_s5_im, s_gdn_conv, s_gdn_s)
```

```python
import functools
import math

import jax
import jax.numpy as jnp
from jax import lax
from jax.experimental import pallas as pl
from jax.experimental.pallas import tpu as pltpu

F32 = jnp.float32
BF16 = jnp.bfloat16

RMS_EPS = 1e-6
L2_EPS = 1e-6
D_MODEL = 2048
S5_GROUP = 16
S5_STATE = 64
S5_GROUPS = D_MODEL // S5_GROUP
S5_CH = S5_GROUPS * S5_STATE
HEAD = 128
QK_HEADS = D_MODEL // HEAD
V_HEADS = 2 * QK_HEADS
KEY_DIM = QK_HEADS * HEAD
VAL_DIM = V_HEADS * HEAD
CONV_DIM = 2 * KEY_DIM + VAL_DIM
CONV_W = 4

SUBLANES = 8
LANES = 128
MXU_K = 256
S5_KB = D_MODEL // MXU_K
S5_CB = S5_CH // S5_KB
VMEM_LIMIT = 56 * 1024 * 1024


def _cparams(sem):
    return pltpu.CompilerParams(dimension_semantics=sem, vmem_limit_bytes=VMEM_LIMIT)


def _rms(x, w):
    return x * lax.rsqrt(jnp.mean(x * x, axis=-1, keepdims=True) + RMS_EPS) * w


def _s5_body(x_ref, nw_ref, d_ref, wb_ref, wc_ref, coef_ref, s0r_ref, s0i_ref,
             g_ref, sr_ref, si_ref, car_ref, h_ref, bu_ref, st_ref, *, tt):
    t = pl.program_id(1)

    @pl.when(t == 0)
    def _():
        car_ref[0] = jnp.broadcast_to(s0r_ref[0], (SUBLANES, S5_CH))
        car_ref[1] = jnp.broadcast_to(s0i_ref[0], (SUBLANES, S5_CH))

    h_ref[...] = _rms(x_ref[0], nw_ref[...])

    for kb in range(S5_KB):
        ks = slice(kb * MXU_K, (kb + 1) * MXU_K)
        c0 = kb * S5_CB
        cs = slice(c0, c0 + S5_CB)
        bu_ref[...] = jnp.dot(h_ref[:, ks].astype(BF16), wb_ref[kb],
                              preferred_element_type=F32)

        def row_group(r, carry):
            cr, ci = carry
            r8 = pl.multiple_of(r * SUBLANES, SUBLANES)
            xr = bu_ref[pl.ds(r8, SUBLANES), 0:S5_CB]
            xi = bu_ref[pl.ds(r8, SUBLANES), S5_CB:2 * S5_CB]
            for lvl, sh in enumerate((1, 2, 4)):
                ar = coef_ref[lvl, 0, :, cs]
                ai = coef_ref[lvl, 1, :, cs]
                pr = pltpu.roll(xr, sh, 0)
                pi = pltpu.roll(xi, sh, 0)
                xr, xi = xr + ar * pr - ai * pi, xi + ar * pi + ai * pr
            ar = coef_ref[3, 0, :, cs]
            ai = coef_ref[3, 1, :, cs]
            sr = xr + ar * cr - ai * ci
            si = xi + ar * ci + ai * cr
            st_ref[pl.ds(r8, SUBLANES), 0:S5_CB] = sr
            st_ref[pl.ds(r8, SUBLANES), S5_CB:2 * S5_CB] = si
            return (jnp.broadcast_to(sr[SUBLANES - 1:SUBLANES], (SUBLANES, S5_CB)),
                    jnp.broadcast_to(si[SUBLANES - 1:SUBLANES], (SUBLANES, S5_CB)))

        cr, ci = lax.fori_loop(0, tt // SUBLANES, row_group,
                               (car_ref[0, :, cs], car_ref[1, :, cs]))
        car_ref[0, :, cs] = cr
        car_ref[1, :, cs] = ci
        y = jnp.dot(st_ref[...].astype(BF16), wc_ref[kb], preferred_element_type=F32)
        y = y + d_ref[:, ks] * h_ref[:, ks]
        g_ref[0, :, ks] = jax.nn.gelu(y).astype(BF16)

    @pl.when(t == pl.num_programs(1) - 1)
    def _():
        sr_ref[0] = car_ref[0, 0:1, :]
        si_ref[0] = car_ref[1, 0:1, :]


def _s5_call(x, nw, d, wb, wc, coef, s0r, s0i, tt):
    n_b, t_len, _ = x.shape
    const = lambda *shape: pl.BlockSpec(shape, lambda b, t: (0,) * len(shape))
    state = pl.BlockSpec((1, 1, S5_CH), lambda b, t: (b, 0, 0))
    return pl.pallas_call(
        functools.partial(_s5_body, tt=tt),
        grid=(n_b, t_len // tt),
        in_specs=[pl.BlockSpec((1, tt, D_MODEL), lambda b, t: (b, t, 0)),
                  const(1, D_MODEL), const(1, D_MODEL),
                  const(S5_KB, MXU_K, 2 * S5_CB), const(S5_KB, 2 * S5_CB, MXU_K),
                  const(4, 2, SUBLANES, S5_CH), state, state],
        out_specs=[pl.BlockSpec((1, tt, D_MODEL), lambda b, t: (b, t, 0)), state, state],
        out_shape=[jax.ShapeDtypeStruct((n_b, t_len, D_MODEL), BF16),
                   jax.ShapeDtypeStruct((n_b, 1, S5_CH), F32),
                   jax.ShapeDtypeStruct((n_b, 1, S5_CH), F32)],
        scratch_shapes=[pltpu.VMEM((2, SUBLANES, S5_CH), F32),
                        pltpu.VMEM((tt, D_MODEL), F32),
                        pltpu.VMEM((tt, 2 * S5_CB), F32),
                        pltpu.VMEM((tt, 2 * S5_CB), F32)],
        compiler_params=_cparams(("arbitrary", "arbitrary")),
        name="s5_mixer",
    )(x, nw, d, wb, wc, coef, s0r, s0i)


def _s5_params(lam_re, lam_im, log_dt, b_re, b_im, c_re, c_im):
    lr = jnp.minimum(lam_re.astype(F32), -1e-4)
    li = lam_im.astype(F32)
    dt = jnp.exp(log_dt.astype(F32))[:, None]
    mag = jnp.exp(lr * dt)
    ang = li * dt
    a_re = mag * jnp.cos(ang)
    a_im = mag * jnp.sin(ang)
    nr = a_re - 1.0
    den = lr * lr + li * li
    k_re = (nr * lr + a_im * li) / den
    k_im = (a_im * lr - nr * li) / den
    br = b_re.astype(F32)
    bi = b_im.astype(F32)
    bb_re = k_re[..., None] * br - k_im[..., None] * bi
    bb_im = k_re[..., None] * bi + k_im[..., None] * br

    gl = S5_GROUPS // S5_KB
    eye = jnp.eye(gl, dtype=F32)

    def blockdiag_in(bb):
        bbk = bb.reshape(S5_KB, gl, S5_STATE, S5_GROUP)
        w = jnp.einsum('kgph,gf->kghfp', bbk, eye)
        return w.reshape(S5_KB, gl * S5_GROUP, gl * S5_STATE)

    def blockdiag_out(c):
        ck = c.reshape(S5_KB, gl, S5_GROUP, S5_STATE)
        w = jnp.einsum('kghp,gf->kgpfh', ck, eye)
        return w.reshape(S5_KB, gl * S5_STATE, gl * S5_GROUP)

    wb = jnp.concatenate([blockdiag_in(bb_re), blockdiag_in(bb_im)], axis=2).astype(BF16)
    wc = jnp.concatenate([blockdiag_out(c_re.astype(F32)),
                          -blockdiag_out(c_im.astype(F32))], axis=1).astype(BF16)

    ar = a_re.reshape(1, S5_CH)
    ai = a_im.reshape(1, S5_CH)
    pw_r, pw_i = [ar], [ai]
    for _ in range(SUBLANES - 1):
        pr, pi = pw_r[-1], pw_i[-1]
        pw_r.append(pr * ar - pi * ai)
        pw_i.append(pr * ai + pi * ar)
    row = jnp.arange(SUBLANES)[:, None]
    levels = []
    for sh in (1, 2, 4):
        m = (row >= sh).astype(F32)
        levels.append(jnp.stack([m * pw_r[sh - 1], m * pw_i[sh - 1]]))
    levels.append(jnp.stack([jnp.concatenate(pw_r, axis=0), jnp.concatenate(pw_i, axis=0)]))
    coef = jnp.stack(levels)
    return wb, wc, coef


def _glu_body(g_ref, wa_ref, wg_ref, x_ref, o_ref):
    g = g_ref[...]
    za = jnp.dot(g, wa_ref[...], preferred_element_type=F32)
    zg = jnp.dot(g, wg_ref[...], preferred_element_type=F32)
    o_ref[...] = x_ref[...] + za * jax.nn.sigmoid(zg)


def _glu_call(g, w_glu, x, tm, tn):
    m = g.shape[0]
    nj = D_MODEL // tn
    return pl.pallas_call(
        _glu_body,
        grid=(m // tm, nj),
        in_specs=[pl.BlockSpec((tm, D_MODEL), lambda i, j: (i, 0)),
                  pl.BlockSpec((D_MODEL, tn), lambda i, j: (0, j)),
                  pl.BlockSpec((D_MODEL, tn), lambda i, j: (0, j + nj)),
                  pl.BlockSpec((tm, tn), lambda i, j: (i, j))],
        out_specs=pl.BlockSpec((tm, tn), lambda i, j: (i, j)),
        out_shape=jax.ShapeDtypeStruct((m, D_MODEL), F32),
        compiler_params=_cparams(("parallel", "arbitrary")),
        name="glu_residual",
    )(g, w_glu, w_glu, x)


def _ffn_body(x_ref, nw_ref, wu_ref, wd_ref, fw_ref, o_ref, h_ref, acc_ref, *, final_norm):
    f = pl.program_id(1)

    @pl.when(f == 0)
    def _():
        h_ref[...] = _rms(x_ref[...], nw_ref[...]).astype(BF16)
        acc_ref[...] = jnp.zeros_like(acc_ref)

    a = jnp.maximum(jnp.dot(h_ref[...], wu_ref[...], preferred_element_type=F32), 0.0)
    acc_ref[...] += jnp.dot((a * a).astype(BF16), wd_ref[...], preferred_element_type=F32)

    @pl.when(f == pl.num_programs(1) - 1)
    def _():
        y = x_ref[...] + acc_ref[...]
        if final_norm:
            y = _rms(y, fw_ref[...])
        o_ref[...] = y


def _ffn_call(x, nw, w_up, w_down, fw, tm, tf, final_norm):
    m = x.shape[0]
    d_ff = w_up.shape[1]
    return pl.pallas_call(
        functools.partial(_ffn_body, final_norm=final_norm),
        grid=(m // tm, d_ff // tf),
        in_specs=[pl.BlockSpec((tm, D_MODEL), lambda i, f: (i, 0)),
                  pl.BlockSpec((1, D_MODEL), lambda i, f: (0, 0)),
                  pl.BlockSpec((D_MODEL, tf), lambda i, f: (0, f)),
                  pl.BlockSpec((tf, D_MODEL), lambda i, f: (f, 0)),
                  pl.BlockSpec((1, D_MODEL), lambda i, f: (0, 0))],
        out_specs=pl.BlockSpec((tm, D_MODEL), lambda i, f: (i, 0)),
        out_shape=jax.ShapeDtypeStruct((m, D_MODEL), F32),
        scratch_shapes=[pltpu.VMEM((tm, D_MODEL), BF16), pltpu.VMEM((tm, D_MODEL), F32)],
        compiler_params=_cparams(("parallel", "arbitrary")),
        name="ffn_residual",
    )(x, nw, w_up, w_down, fw)


def _norm_matmul_body(x_ref, nw_ref, w_ref, o_ref, h_ref):
    @pl.when(pl.program_id(1) == 0)
    def _():
        h_ref[...] = _rms(x_ref[...], nw_ref[...]).astype(BF16)

    o_ref[...] = jnp.dot(h_ref[...], w_ref[...], preferred_element_type=F32).astype(o_ref.dtype)


def _norm_matmul_call(x, nw, w, tm, tn, out_dtype):
    m = x.shape[0]
    n = w.shape[1]
    return pl.pallas_call(
        _norm_matmul_body,
        grid=(m // tm, n // tn),
        in_specs=[pl.BlockSpec((tm, D_MODEL), lambda i, j: (i, 0)),
                  pl.BlockSpec((1, D_MODEL), lambda i, j: (0, 0)),
                  pl.BlockSpec((D_MODEL, tn), lambda i, j: (0, j))],
        out_specs=pl.BlockSpec((tm, tn), lambda i, j: (i, j)),
        out_shape=jax.ShapeDtypeStruct((m, n), out_dtype),
        scratch_shapes=[pltpu.VMEM((tm, D_MODEL), BF16)],
        compiler_params=_cparams(("parallel", "arbitrary")),
        name="norm_matmul",
    )(x, nw, w)


def _matmul_res_body(a_ref, w_ref, x_ref, o_ref):
    o_ref[...] = x_ref[...] + jnp.dot(a_ref[...], w_ref[...], preferred_element_type=F32)


def _matmul_res_call(a, w, x, tm, tn):
    m, k = a.shape
    n = w.shape[1]
    return pl.pallas_call(
        _matmul_res_body,
        grid=(m // tm, n // tn),
        in_specs=[pl.BlockSpec((tm, k), lambda i, j: (i, 0)),
                  pl.BlockSpec((k, tn), lambda i, j: (0, j)),
                  pl.BlockSpec((tm, tn), lambda i, j: (i, j))],
        out_specs=pl.BlockSpec((tm, tn), lambda i, j: (i, j)),
        out_shape=jax.ShapeDtypeStruct((m, n), F32),
        compiler_params=_cparams(("parallel", "arbitrary")),
        name="matmul_residual",
    )(a, w, x)


def _conv_body(qkv_ref, ba_ref, cprev_ref, cw_ref, alog_ref, dtb_ref,
               q_ref, k_ref, v_ref, g_ref, beta_ref, nc_ref, xp_ref, *, tt):
    t = pl.program_id(1)
    pad = SUBLANES
    tail = CONV_W - 1

    @pl.when(t == 0)
    def _():
        xp_ref[pad - tail:pad, :] = cprev_ref[0]

    xp_ref[pad:pad + tt, :] = qkv_ref[0]

    for j in range(CONV_DIM // LANES):
        cs = slice(j * LANES, (j + 1) * LANES)
        acc = xp_ref[pad - tail:pad - tail + tt, cs] * cw_ref[0:1, cs]
        for tap in range(1, CONV_W):
            acc = acc + xp_ref[pad - tail + tap:pad - tail + tap + tt, cs] * cw_ref[tap:tap + 1, cs]
        c = acc * jax.nn.sigmoid(acc)
        if j < QK_HEADS:
            c = c * lax.rsqrt(jnp.sum(c * c, axis=-1, keepdims=True) + L2_EPS) * (HEAD ** -0.5)
            q_ref[0, :, cs] = c.astype(BF16)
        elif j < 2 * QK_HEADS:
            c = c * lax.rsqrt(jnp.sum(c * c, axis=-1, keepdims=True) + L2_EPS)
            k_ref[0, :, j * LANES - KEY_DIM:(j + 1) * LANES - KEY_DIM] = c.astype(BF16)
        else:
            v_ref[0, :, j * LANES - 2 * KEY_DIM:(j + 1) * LANES - 2 * KEY_DIM] = c.astype(BF16)

    new_tail = xp_ref[pad + tt - tail:pad + tt, :]
    xp_ref[pad - tail:pad, :] = new_tail

    @pl.when(t == pl.num_programs(1) - 1)
    def _():
        nc_ref[0] = new_tail

    b = ba_ref[0, :, 0:V_HEADS]
    a = ba_ref[0, :, V_HEADS:2 * V_HEADS]
    z = a + dtb_ref[...]
    softplus = jnp.maximum(z, 0.0) + jnp.log1p(jnp.exp(-jnp.abs(z)))
    g_ref[0] = -jnp.exp(alog_ref[...]) * softplus
    beta_ref[0] = jax.nn.sigmoid(b)


def _conv_call(qkv, zba, conv_prev, conv_w, a_log, dt_bias, tt):
    n_b, t_len, _ = qkv.shape
    ba_blk = VAL_DIM // LANES
    return pl.pallas_call(
        functools.partial(_conv_body, tt=tt),
        grid=(n_b, t_len // tt),
        in_specs=[pl.BlockSpec((1, tt, CONV_DIM), lambda b, t: (b, t, 0)),
                  pl.BlockSpec((1, tt, LANES), lambda b, t: (b, t, ba_blk)),
                  pl.BlockSpec((1, CONV_W - 1, CONV_DIM), lambda b, t: (b, 0, 0)),
                  pl.BlockSpec((CONV_W, CONV_DIM), lambda b, t: (0, 0)),
                  pl.BlockSpec((1, V_HEADS), lambda b, t: (0, 0)),
                  pl.BlockSpec((1, V_HEADS), lambda b, t: (0, 0))],
        out_specs=[pl.BlockSpec((1, tt, KEY_DIM), lambda b, t: (b, t, 0)),
                   pl.BlockSpec((1, tt, KEY_DIM), lambda b, t: (b, t, 0)),
                   pl.BlockSpec((1, tt, VAL_DIM), lambda b, t: (b, t, 0)),
                   pl.BlockSpec((1, tt, V_HEADS), lambda b, t: (b, t, 0)),
                   pl.BlockSpec((1, tt, V_HEADS), lambda b, t: (b, t, 0)),
                   pl.BlockSpec((1, CONV_W - 1, CONV_DIM), lambda b, t: (b, 0, 0))],
        out_shape=[jax.ShapeDtypeStruct((n_b, t_len, KEY_DIM), BF16),
                   jax.ShapeDtypeStruct((n_b, t_len, KEY_DIM), BF16),
                   jax.ShapeDtypeStruct((n_b, t_len, VAL_DIM), BF16),
                   jax.ShapeDtypeStruct((n_b, t_len, V_HEADS), F32),
                   jax.ShapeDtypeStruct((n_b, t_len, V_HEADS), F32),
                   jax.ShapeDtypeStruct((n_b, CONV_W - 1, CONV_DIM), F32)],
        scratch_shapes=[pltpu.VMEM((tt + SUBLANES, CONV_DIM), F32)],
        compiler_params=_cparams(("arbitrary", "arbitrary")),
        name="gdn_conv_gates",
    )(qkv, zba, conv_prev, conv_w, a_log, dt_bias)


INV_BASE = 16


def _mm16(a, b):
    return jnp.dot(a.astype(BF16), b.astype(BF16), preferred_element_type=F32)


def _tri_inverse(nmat, ri, ci, chunk):
    base = min(INV_BASE, chunk)
    sh = int(math.log2(base))
    same = (ri >> sh) == (ci >> sh)
    m = jnp.where(same, -nmat, 0.0)
    p = (ri == ci).astype(F32) + m
    for _ in range(sh - 1):
        m = _mm16(m, m)
        p = p + _mm16(p, m)
    size = base
    while size < chunk:
        sh += 1
        size *= 2
        same_next = (ri >> sh) == (ci >> sh)
        off = jnp.where(same_next & jnp.logical_not(same), nmat, 0.0)
        p = p - _mm16(_mm16(p, off), p)
        same = same_next
    return p


def _delta_body(q_ref, k_ref, v_ref, z_ref, gc_ref, gr_ref, bc_ref, s0_ref, nw_ref,
                o_ref, sf_ref, s_ref, *, chunk, cpb, hb):
    c_idx = pl.program_id(2)

    @pl.when(c_idx == 0)
    def _():
        s_ref[...] = s0_ref[0]

    ri = lax.broadcasted_iota(jnp.int32, (chunk, chunk), 0)
    ci = lax.broadcasted_iota(jnp.int32, (chunk, chunk), 1)
    causal = ri >= ci
    strict = ri > ci
    tril = causal.astype(F32)
    triu = (ri <= ci).astype(F32)
    hi = lax.Precision.HIGHEST

    def chunk_step(c, _):
        r0 = pl.multiple_of(c * chunk, chunk)
        rows = pl.ds(r0, chunk)
        gcol = jnp.dot(tril, gc_ref[0, 0, rows, :], precision=hi, preferred_element_type=F32)
        grow = jnp.dot(gr_ref[0, 0, c], triu, precision=hi, preferred_element_type=F32)
        bcol = bc_ref[0, 0, rows, :]
        for hh in range(hb):
            kq = hh // 2
            k = k_ref[0, rows, kq * HEAD:(kq + 1) * HEAD].astype(F32)
            q = q_ref[0, rows, kq * HEAD:(kq + 1) * HEAD].astype(F32)
            v = v_ref[0, rows, hh * HEAD:(hh + 1) * HEAD].astype(F32)
            gc = gcol[:, hh:hh + 1]
            gr = grow[hh:hh + 1, :]
            beta = bcol[:, hh:hh + 1]
            decay = jnp.exp(jnp.where(causal, gc - gr, -jnp.inf))
            egc = jnp.exp(gc)
            kbeta = k * beta
            kb16 = k.astype(BF16)
            kk = lax.dot_general(kbeta.astype(BF16), kb16, (((1,), (1,)), ((), ())),
                                 preferred_element_type=F32)
            p = _tri_inverse(jnp.where(strict, kk * decay, 0.0), ri, ci, chunk)
            rhs = jnp.concatenate([v * beta, kbeta * egc], axis=1).astype(BF16)
            sol = jnp.dot(p.astype(BF16), rhs, preferred_element_type=F32)
            u = sol[:, :HEAD]
            w = sol[:, HEAD:]
            s = s_ref[hh]
            wq = jnp.concatenate([w, q * egc], axis=0).astype(BF16)
            ws = jnp.dot(wq, s.astype(BF16), preferred_element_type=F32)
            v_new = u - ws[:chunk]
            v16 = v_new.astype(BF16)
            qk = lax.dot_general(q.astype(BF16), kb16, (((1,), (1,)), ((), ())),
                                 preferred_element_type=F32)
            o = ws[chunk:] + jnp.dot((qk * decay).astype(BF16), v16, preferred_element_type=F32)
            g_last = gc[chunk - 1:chunk, :]
            kd = (k * jnp.exp(g_last - gc)).astype(BF16)
            s_ref[hh] = s * jnp.exp(g_last) + lax.dot_general(
                kd, v16, (((0,), (0,)), ((), ())), preferred_element_type=F32)
            o = o * lax.rsqrt(jnp.mean(o * o, axis=-1, keepdims=True) + RMS_EPS) * nw_ref[...]
            zz = z_ref[0, rows, hh * HEAD:(hh + 1) * HEAD]
            o_ref[0, rows, hh * HEAD:(hh + 1) * HEAD] = (o * (zz * jax.nn.sigmoid(zz))).astype(BF16)
        return 0

    lax.fori_loop(0, cpb, chunk_step, 0)

    @pl.when(c_idx == pl.num_programs(2) - 1)
    def _():
        sf_ref[0] = s_ref[...]


def _delta_call(q, k, v, zba, g, beta, s0, norm_w, chunk, cpb, hb):
    n_b, t_len, _ = q.shape
    nhg = V_HEADS // hb
    tc = chunk * cpb
    n_chunks = t_len // chunk
    g_col = g.reshape(n_b, t_len, nhg, hb).transpose(0, 2, 1, 3)
    b_col = beta.reshape(n_b, t_len, nhg, hb).transpose(0, 2, 1, 3)
    g_row = g.reshape(n_b, n_chunks, chunk, nhg, hb).transpose(0, 3, 1, 4, 2)
    hq = hb // 2
    return pl.pallas_call(
        functools.partial(_delta_body, chunk=chunk, cpb=cpb, hb=hb),
        grid=(n_b, nhg, t_len // tc),
        in_specs=[pl.BlockSpec((1, tc, hq * HEAD), lambda b, h, c: (b, c, h)),
                  pl.BlockSpec((1, tc, hq * HEAD), lambda b, h, c: (b, c, h)),
                  pl.BlockSpec((1, tc, hb * HEAD), lambda b, h, c: (b, c, h)),
                  pl.BlockSpec((1, tc, hb * HEAD), lambda b, h, c: (b, c, h)),
                  pl.BlockSpec((1, 1, tc, hb), lambda b, h, c: (b, h, c, 0)),
                  pl.BlockSpec((1, 1, cpb, hb, chunk), lambda b, h, c: (b, h, c, 0, 0)),
                  pl.BlockSpec((1, 1, tc, hb), lambda b, h, c: (b, h, c, 0)),
                  pl.BlockSpec((1, hb, HEAD, HEAD), lambda b, h, c: (b, h, 0, 0)),
                  pl.BlockSpec((1, HEAD), lambda b, h, c: (0, 0))],
        out_specs=[pl.BlockSpec((1, tc, hb * HEAD), lambda b, h, c: (b, c, h)),
                   pl.BlockSpec((1, hb, HEAD, HEAD), lambda b, h, c: (b, h, 0, 0))],
        out_shape=[jax.ShapeDtypeStruct((n_b, t_len, VAL_DIM), BF16),
                   jax.ShapeDtypeStruct((n_b, V_HEADS, HEAD, HEAD), F32)],
        scratch_shapes=[pltpu.VMEM((hb, HEAD, HEAD), F32)],
        compiler_params=_cparams(("arbitrary", "arbitrary", "arbitrary")),
        name="gated_delta_rule",
    )(q, k, v, zba, g_col, g_row, b_col, s0, norm_w)


def _tile(m, pref):
    t = min(m, pref)
    assert m % t == 0, (m, t)
    return t


def _trunk(x, s5_re0, s5_im0, conv0, s0, p):
    n_b, t_len, _ = x.shape
    m = n_b * t_len
    tm = _tile(m, 512)

    tt = _tile(t_len, 256)
    g, sr, si = _s5_call(x, p['nm0'], p['s5_d'], p['wb'], p['wc'], p['coef'],
                         s5_re0.reshape(n_b, 1, S5_CH), s5_im0.reshape(n_b, 1, S5_CH), tt)
    x2 = x.reshape(m, D_MODEL)
    x2 = _glu_call(g.reshape(m, D_MODEL), p['w_glu'], x2, tm, 512)
    x2 = _ffn_call(x2, p['nf0'], p['w_up0'], p['w_down0'], p['nfin'], tm, 512, False)

    qkv = _norm_matmul_call(x2, p['nm1'], p['w_qkv'], tm, 512, F32)
    zba = _norm_matmul_call(x2, p['nm1'], p['w_zba'], tm, p['w_zba'].shape[1] // 3, F32)
    zba3 = zba.reshape(n_b, t_len, -1)
    q, k, v, gg, beta, new_conv = _conv_call(
        qkv.reshape(n_b, t_len, CONV_DIM), zba3, conv0, p['conv_w'], p['a_log'], p['dt_bias'],
        _tile(t_len, 128))
    chunk = min(t_len, 64)
    cpb = _tile(t_len // chunk, 4)
    og, s_fin = _delta_call(q, k, v, zba3, gg, beta, s0, p['gdn_norm_w'], chunk, cpb, 8)
    x2 = _matmul_res_call(og.reshape(m, VAL_DIM), p['w_out'], x2, tm, 512)
    y = _ffn_call(x2, p['nf1'], p['w_up1'], p['w_down1'], p['nfin'], tm, 512, True)

    return (y.reshape(n_b, t_len, D_MODEL),
            sr.reshape(1, n_b, S5_GROUPS, S5_STATE), si.reshape(1, n_b, S5_GROUPS, S5_STATE),
            new_conv[None], s_fin[None])


def kernel(x_prompt, x_sample, state_s5_re, state_s5_im, state_gdn_conv, state_gdn_s,
           norm_mix, norm_ffn, norm_final,
           s5_lam_re, s5_lam_im, s5_log_dt, s5_b_re, s5_b_im, s5_c_re, s5_c_im,
           s5_d, s5_w_glu,
           gdn_w_in, gdn_conv_w, gdn_a_log, gdn_dt_bias, gdn_norm_w, gdn_w_out,
           ffn_w_up, ffn_w_down):
    wb, wc, coef = _s5_params(s5_lam_re[0], s5_lam_im[0], s5_log_dt[0], s5_b_re[0], s5_b_im[0],
                              s5_c_re[0], s5_c_im[0])
    w_in = gdn_w_in[0]
    ba = w_in[:, CONV_DIM + VAL_DIM:]
    w_zba = jnp.concatenate(
        [w_in[:, CONV_DIM:CONV_DIM + VAL_DIM], ba,
         jnp.zeros((D_MODEL, LANES - ba.shape[1]), w_in.dtype)], axis=1)
    row = lambda a: a.reshape(1, -1).astype(F32)
    p = dict(
        nm0=row(norm_mix[0]), nm1=row(norm_mix[1]), nf0=row(norm_ffn[0]), nf1=row(norm_ffn[1]),
        nfin=row(norm_final), s5_d=row(s5_d[0]), wb=wb, wc=wc, coef=coef,
        w_glu=s5_w_glu[0].astype(BF16),
        w_up0=ffn_w_up[0].astype(BF16), w_down0=ffn_w_down[0].astype(BF16),
        w_up1=ffn_w_up[1].astype(BF16), w_down1=ffn_w_down[1].astype(BF16),
        w_qkv=w_in[:, :CONV_DIM].astype(BF16), w_zba=w_zba.astype(BF16),
        conv_w=gdn_conv_w[0].astype(F32), a_log=row(gdn_a_log[0]), dt_bias=row(gdn_dt_bias[0]),
        gdn_norm_w=row(gdn_norm_w[0]), w_out=gdn_w_out[0].astype(BF16),
    )
    n_bp = x_prompt.shape[0]
    z_state = jnp.zeros((n_bp, S5_GROUPS, S5_STATE), F32)
    z_conv = jnp.zeros((n_bp, CONV_W - 1, CONV_DIM), F32)
    z_s = jnp.zeros((n_bp, V_HEADS, HEAD, HEAD), F32)

    yp, p_re, p_im, p_conv, p_s = _trunk(x_prompt, z_state, z_state, z_conv, z_s, p)
    ys, s_re, s_im, s_conv, s_s = _trunk(x_sample, state_s5_re[0], state_s5_im[0],
                                         state_gdn_conv[0], state_gdn_s[0], p)
    return (yp, ys, p_re, p_im, p_conv, p_s, s_re, s_im, s_conv, s_s)
```

```python
import functools
import math

import jax
import jax.numpy as jnp
from jax import lax
from jax.experimental import pallas as pl
from jax.experimental.pallas import tpu as pltpu

F32 = jnp.float32
BF16 = jnp.bfloat16

RMS_EPS = 1e-6
L2_EPS = 1e-6
D_MODEL = 2048
S5_GROUP = 16
S5_STATE = 64
S5_GROUPS = D_MODEL // S5_GROUP
S5_CH = S5_GROUPS * S5_STATE
HEAD = 128
QK_HEADS = D_MODEL // HEAD
V_HEADS = 2 * QK_HEADS
KEY_DIM = QK_HEADS * HEAD
VAL_DIM = V_HEADS * HEAD
CONV_DIM = 2 * KEY_DIM + VAL_DIM
CONV_W = 4

SUBLANES = 8
LANES = 128
MXU_K = 256
S5_KB = D_MODEL // MXU_K
S5_CB = S5_CH // S5_KB
VMEM_LIMIT = 56 * 1024 * 1024


def _cparams(sem):
    return pltpu.CompilerParams(dimension_semantics=sem, vmem_limit_bytes=VMEM_LIMIT)


def _rms(x, w):
    return x * lax.rsqrt(jnp.mean(x * x, axis=-1, keepdims=True) + RMS_EPS) * w


def _s5_body(x_ref, nw_ref, d_ref, wb_ref, wc_ref, coef_ref, s0r_ref, s0i_ref,
             g_ref, sr_ref, si_ref, car_ref, h_ref, bu_ref, st_ref, *, tt):
    t = pl.program_id(1)

    @pl.when(t == 0)
    def _():
        car_ref[0] = jnp.broadcast_to(s0r_ref[0], (SUBLANES, S5_CH))
        car_ref[1] = jnp.broadcast_to(s0i_ref[0], (SUBLANES, S5_CH))

    h_ref[...] = _rms(x_ref[0], nw_ref[...])

    for kb in range(S5_KB):
        ks = slice(kb * MXU_K, (kb + 1) * MXU_K)
        c0 = kb * S5_CB
        cs = slice(c0, c0 + S5_CB)
        bu_ref[...] = jnp.dot(h_ref[:, ks].astype(BF16), wb_ref[kb],
                              preferred_element_type=F32)

        def row_group(r, carry):
            cr, ci = carry
            r8 = pl.multiple_of(r * SUBLANES, SUBLANES)
            xr = bu_ref[pl.ds(r8, SUBLANES), 0:S5_CB]
            xi = bu_ref[pl.ds(r8, SUBLANES), S5_CB:2 * S5_CB]
            for lvl, sh in enumerate((1, 2, 4)):
                ar = coef_ref[lvl, 0, :, cs]
                ai = coef_ref[lvl, 1, :, cs]
                pr = pltpu.roll(xr, sh, 0)
                pi = pltpu.roll(xi, sh, 0)
                xr, xi = xr + ar * pr - ai * pi, xi + ar * pi + ai * pr
            ar = coef_ref[3, 0, :, cs]
            ai = coef_ref[3, 1, :, cs]
            sr = xr + ar * cr - ai * ci
            si = xi + ar * ci + ai * cr
            st_ref[pl.ds(r8, SUBLANES), 0:S5_CB] = sr
            st_ref[pl.ds(r8, SUBLANES), S5_CB:2 * S5_CB] = si
            return (jnp.broadcast_to(sr[SUBLANES - 1:SUBLANES], (SUBLANES, S5_CB)),
                    jnp.broadcast_to(si[SUBLANES - 1:SUBLANES], (SUBLANES, S5_CB)))

        cr, ci = lax.fori_loop(0, tt // SUBLANES, row_group,
                               (car_ref[0, :, cs], car_ref[1, :, cs]))
        car_ref[0, :, cs] = cr
        car_ref[1, :, cs] = ci
        y = jnp.dot(st_ref[...].astype(BF16), wc_ref[kb], preferred_element_type=F32)
        y = y + d_ref[:, ks] * h_ref[:, ks]
        g_ref[0, :, ks] = jax.nn.gelu(y).astype(BF16)

    @pl.when(t == pl.num_programs(1) - 1)
    def _():
        sr_ref[0] = car_ref[0, 0:1, :]
        si_ref[0] = car_ref[1, 0:1, :]


def _s5_call(x, nw, d, wb, wc, coef, s0r, s0i, tt):
    n_b, t_len, _ = x.shape
    const = lambda *shape: pl.BlockSpec(shape, lambda b, t: (0,) * len(shape))
    state = pl.BlockSpec((1, 1, S5_CH), lambda b, t: (b, 0, 0))
    return pl.pallas_call(
        functools.partial(_s5_body, tt=tt),
        grid=(n_b, t_len // tt),
        in_specs=[pl.BlockSpec((1, tt, D_MODEL), lambda b, t: (b, t, 0)),
                  const(1, D_MODEL), const(1, D_MODEL),
                  const(S5_KB, MXU_K, 2 * S5_CB), const(S5_KB, 2 * S5_CB, MXU_K),
                  const(4, 2, SUBLANES, S5_CH), state, state],
        out_specs=[pl.BlockSpec((1, tt, D_MODEL), lambda b, t: (b, t, 0)), state, state],
        out_shape=[jax.ShapeDtypeStruct((n_b, t_len, D_MODEL), BF16),
                   jax.ShapeDtypeStruct((n_b, 1, S5_CH), F32),
                   jax.ShapeDtypeStruct((n_b, 1, S5_CH), F32)],
        scratch_shapes=[pltpu.VMEM((2, SUBLANES, S5_CH), F32),
                        pltpu.VMEM((tt, D_MODEL), F32),
                        pltpu.VMEM((tt, 2 * S5_CB), F32),
                        pltpu.VMEM((tt, 2 * S5_CB), F32)],
        compiler_params=_cparams(("arbitrary", "arbitrary")),
        name="s5_mixer",
    )(x, nw, d, wb, wc, coef, s0r, s0i)


def _s5_params(lam_re, lam_im, log_dt, b_re, b_im, c_re, c_im):
    lr = jnp.minimum(lam_re.astype(F32), -1e-4)
    li = lam_im.astype(F32)
    dt = jnp.exp(log_dt.astype(F32))[:, None]
    mag = jnp.exp(lr * dt)
    ang = li * dt
    a_re = mag * jnp.cos(ang)
    a_im = mag * jnp.sin(ang)
    nr = a_re - 1.0
    den = lr * lr + li * li
    k_re = (nr * lr + a_im * li) / den
    k_im = (a_im * lr - nr * li) / den
    br = b_re.astype(F32)
    bi = b_im.astype(F32)
    bb_re = k_re[..., None] * br - k_im[..., None] * bi
    bb_im = k_re[..., None] * bi + k_im[..., None] * br

    gl = S5_GROUPS // S5_KB
    eye = jnp.eye(gl, dtype=F32)

    def blockdiag_in(bb):
        bbk = bb.reshape(S5_KB, gl, S5_STATE, S5_GROUP)
        w = jnp.einsum('kgph,gf->kghfp', bbk, eye)
        return w.reshape(S5_KB, gl * S5_GROUP, gl * S5_STATE)

    def blockdiag_out(c):
        ck = c.reshape(S5_KB, gl, S5_GROUP, S5_STATE)
        w = jnp.einsum('kghp,gf->kgpfh', ck, eye)
        return w.reshape(S5_KB, gl * S5_STATE, gl * S5_GROUP)

    wb = jnp.concatenate([blockdiag_in(bb_re), blockdiag_in(bb_im)], axis=2).astype(BF16)
    wc = jnp.concatenate([blockdiag_out(c_re.astype(F32)),
                          -blockdiag_out(c_im.astype(F32))], axis=1).astype(BF16)

    ar = a_re.reshape(1, S5_CH)
    ai = a_im.reshape(1, S5_CH)
    pw_r, pw_i = [ar], [ai]
    for _ in range(SUBLANES - 1):
        pr, pi = pw_r[-1], pw_i[-1]
        pw_r.append(pr * ar - pi * ai)
        pw_i.append(pr * ai + pi * ar)
    row = jnp.arange(SUBLANES)[:, None]
    levels = []
    for sh in (1, 2, 4):
        m = (row >= sh).astype(F32)
        levels.append(jnp.stack([m * pw_r[sh - 1], m * pw_i[sh - 1]]))
    levels.append(jnp.stack([jnp.concatenate(pw_r, axis=0), jnp.concatenate(pw_i, axis=0)]))
    coef = jnp.stack(levels)
    return wb, wc, coef


def _glu_body(g_ref, wa_ref, wg_ref, x_ref, o_ref):
    g = g_ref[...]
    za = jnp.dot(g, wa_ref[...], preferred_element_type=F32)
    zg = jnp.dot(g, wg_ref[...], preferred_element_type=F32)
    o_ref[...] = x_ref[...] + za * jax.nn.sigmoid(zg)


def _glu_call(g, w_glu, x, tm, tn):
    m = g.shape[0]
    nj = D_MODEL // tn
    return pl.pallas_call(
        _glu_body,
        grid=(m // tm, nj),
        in_specs=[pl.BlockSpec((tm, D_MODEL), lambda i, j: (i, 0)),
                  pl.BlockSpec((D_MODEL, tn), lambda i, j: (0, j)),
                  pl.BlockSpec((D_MODEL, tn), lambda i, j: (0, j + nj)),
                  pl.BlockSpec((tm, tn), lambda i, j: (i, j))],
        out_specs=pl.BlockSpec((tm, tn), lambda i, j: (i, j)),
        out_shape=jax.ShapeDtypeStruct((m, D_MODEL), F32),
        compiler_params=_cparams(("parallel", "arbitrary")),
        name="glu_residual",
    )(g, w_glu, w_glu, x)


def _ffn_body(x_ref, nw_ref, wu_ref, wd_ref, fw_ref, o_ref, h_ref, acc_ref, *, final_norm):
    f = pl.program_id(1)

    @pl.when(f == 0)
    def _():
        h_ref[...] = _rms(x_ref[...], nw_ref[...]).astype(BF16)
        acc_ref[...] = jnp.zeros_like(acc_ref)

    a = jnp.maximum(jnp.dot(h_ref[...], wu_ref[...], preferred_element_type=F32), 0.0)
    acc_ref[...] += jnp.dot((a * a).astype(BF16), wd_ref[...], preferred_element_type=F32)

    @pl.when(f == pl.num_programs(1) - 1)
    def _():
        y = x_ref[...] + acc_ref[...]
        if final_norm:
            y = _rms(y, fw_ref[...])
        o_ref[...] = y


def _ffn_call(x, nw, w_up, w_down, fw, tm, tf, final_norm):
    m = x.shape[0]
    d_ff = w_up.shape[1]
    return pl.pallas_call(
        functools.partial(_ffn_body, final_norm=final_norm),
        grid=(m // tm, d_ff // tf),
        in_specs=[pl.BlockSpec((tm, D_MODEL), lambda i, f: (i, 0)),
                  pl.BlockSpec((1, D_MODEL), lambda i, f: (0, 0)),
                  pl.BlockSpec((D_MODEL, tf), lambda i, f: (0, f)),
                  pl.BlockSpec((tf, D_MODEL), lambda i, f: (f, 0)),
                  pl.BlockSpec((1, D_MODEL), lambda i, f: (0, 0))],
        out_specs=pl.BlockSpec((tm, D_MODEL), lambda i, f: (i, 0)),
        out_shape=jax.ShapeDtypeStruct((m, D_MODEL), F32),
        scratch_shapes=[pltpu.VMEM((tm, D_MODEL), BF16), pltpu.VMEM((tm, D_MODEL), F32)],
        compiler_params=_cparams(("parallel", "arbitrary")),
        name="ffn_residual",
    )(x, nw, w_up, w_down, fw)


def _norm_matmul_body(x_ref, nw_ref, w_ref, o_ref, h_ref):
    @pl.when(pl.program_id(1) == 0)
    def _():
        h_ref[...] = _rms(x_ref[...], nw_ref[...]).astype(BF16)

    o_ref[...] = jnp.dot(h_ref[...], w_ref[...], preferred_element_type=F32).astype(o_ref.dtype)


def _norm_matmul_call(x, nw, w, tm, tn, out_dtype):
    m = x.shape[0]
    n = w.shape[1]
    return pl.pallas_call(
        _norm_matmul_body,
        grid=(m // tm, n // tn),
        in_specs=[pl.BlockSpec((tm, D_MODEL), lambda i, j: (i, 0)),
                  pl.BlockSpec((1, D_MODEL), lambda i, j: (0, 0)),
                  pl.BlockSpec((D_MODEL, tn), lambda i, j: (0, j))],
        out_specs=pl.BlockSpec((tm, tn), lambda i, j: (i, j)),
        out_shape=jax.ShapeDtypeStruct((m, n), out_dtype),
        scratch_shapes=[pltpu.VMEM((tm, D_MODEL), BF16)],
        compiler_params=_cparams(("parallel", "arbitrary")),
        name="norm_matmul",
    )(x, nw, w)


def _matmul_res_body(a_ref, w_ref, x_ref, o_ref):
    o_ref[...] = x_ref[...] + jnp.dot(a_ref[...], w_ref[...], preferred_element_type=F32)


def _matmul_res_call(a, w, x, tm, tn):
    m, k = a.shape
    n = w.shape[1]
    return pl.pallas_call(
        _matmul_res_body,
        grid=(m // tm, n // tn),
        in_specs=[pl.BlockSpec((tm, k), lambda i, j: (i, 0)),
                  pl.BlockSpec((k, tn), lambda i, j: (0, j)),
                  pl.BlockSpec((tm, tn), lambda i, j: (i, j))],
        out_specs=pl.BlockSpec((tm, tn), lambda i, j: (i, j)),
        out_shape=jax.ShapeDtypeStruct((m, n), F32),
        compiler_params=_cparams(("parallel", "arbitrary")),
        name="matmul_residual",
    )(a, w, x)


def _conv_body(qkv_ref, ba_ref, cprev_ref, cw_ref, alog_ref, dtb_ref,
               q_ref, k_ref, v_ref, g_ref, beta_ref, nc_ref, xp_ref, *, tt):
    t = pl.program_id(1)
    pad = SUBLANES
    tail = CONV_W - 1

    @pl.when(t == 0)
    def _():
        xp_ref[pad - tail:pad, :] = cprev_ref[0]

    xp_ref[pad:pad + tt, :] = qkv_ref[0]

    for j in range(CONV_DIM // LANES):
        cs = slice(j * LANES, (j + 1) * LANES)
        acc = xp_ref[pad - tail:pad - tail + tt, cs] * cw_ref[0:1, cs]
        for tap in range(1, CONV_W):
            acc = acc + xp_ref[pad - tail + tap:pad - tail + tap + tt, cs] * cw_ref[tap:tap + 1, cs]
        c = acc * jax.nn.sigmoid(acc)
        if j < QK_HEADS:
            c = c * lax.rsqrt(jnp.sum(c * c, axis=-1, keepdims=True) + L2_EPS) * (HEAD ** -0.5)
            q_ref[0, :, cs] = c.astype(BF16)
        elif j < 2 * QK_HEADS:
            c = c * lax.rsqrt(jnp.sum(c * c, axis=-1, keepdims=True) + L2_EPS)
            k_ref[0, :, j * LANES - KEY_DIM:(j + 1) * LANES - KEY_DIM] = c.astype(BF16)
        else:
            v_ref[0, :, j * LANES - 2 * KEY_DIM:(j + 1) * LANES - 2 * KEY_DIM] = c.astype(BF16)

    new_tail = xp_ref[pad + tt - tail:pad + tt, :]
    xp_ref[pad - tail:pad, :] = new_tail

    @pl.when(t == pl.num_programs(1) - 1)
    def _():
        nc_ref[0] = new_tail

    b = ba_ref[0, :, 0:V_HEADS]
    a = ba_ref[0, :, V_HEADS:2 * V_HEADS]
    z = a + dtb_ref[...]
    softplus = jnp.maximum(z, 0.0) + jnp.log1p(jnp.exp(-jnp.abs(z)))
    g_ref[0] = -jnp.exp(alog_ref[...]) * softplus
    beta_ref[0] = jax.nn.sigmoid(b)


def _conv_call(qkv, zba, conv_prev, conv_w, a_log, dt_bias, tt):
    n_b, t_len, _ = qkv.shape
    ba_blk = VAL_DIM // LANES
    return pl.pallas_call(
        functools.partial(_conv_body, tt=tt),
        grid=(n_b, t_len // tt),
        in_specs=[pl.BlockSpec((1, tt, CONV_DIM), lambda b, t: (b, t, 0)),
                  pl.BlockSpec((1, tt, LANES), lambda b, t: (b, t, ba_blk)),
                  pl.BlockSpec((1, CONV_W - 1, CONV_DIM), lambda b, t: (b, 0, 0)),
                  pl.BlockSpec((CONV_W, CONV_DIM), lambda b, t: (0, 0)),
                  pl.BlockSpec((1, V_HEADS), lambda b, t: (0, 0)),
                  pl.BlockSpec((1, V_HEADS), lambda b, t: (0, 0))],
        out_specs=[pl.BlockSpec((1, tt, KEY_DIM), lambda b, t: (b, t, 0)),
                   pl.BlockSpec((1, tt, KEY_DIM), lambda b, t: (b, t, 0)),
                   pl.BlockSpec((1, tt, VAL_DIM), lambda b, t: (b, t, 0)),
                   pl.BlockSpec((1, tt, V_HEADS), lambda b, t: (b, t, 0)),
                   pl.BlockSpec((1, tt, V_HEADS), lambda b, t: (b, t, 0)),
                   pl.BlockSpec((1, CONV_W - 1, CONV_DIM), lambda b, t: (b, 0, 0))],
        out_shape=[jax.ShapeDtypeStruct((n_b, t_len, KEY_DIM), BF16),
                   jax.ShapeDtypeStruct((n_b, t_len, KEY_DIM), BF16),
                   jax.ShapeDtypeStruct((n_b, t_len, VAL_DIM), BF16),
                   jax.ShapeDtypeStruct((n_b, t_len, V_HEADS), F32),
                   jax.ShapeDtypeStruct((n_b, t_len, V_HEADS), F32),
                   jax.ShapeDtypeStruct((n_b, CONV_W - 1, CONV_DIM), F32)],
        scratch_shapes=[pltpu.VMEM((tt + SUBLANES, CONV_DIM), F32)],
        compiler_params=_cparams(("arbitrary", "arbitrary")),
        name="gdn_conv_gates",
    )(qkv, zba, conv_prev, conv_w, a_log, dt_bias)


INV_BASE = 16


def _mm16(a, b):
    return jnp.dot(a.astype(BF16), b.astype(BF16), preferred_element_type=F32)


def _tri_inverse_all(nmats, ri, ci, chunk):
    base = min(INV_BASE, chunk)
    sh = int(math.log2(base))
    same = (ri >> sh) == (ci >> sh)
    eye = (ri == ci).astype(F32)
    ms = [jnp.where(same, -n, 0.0) for n in nmats]
    ps = [eye + m for m in ms]
    for _ in range(sh - 1):
        ms = [_mm16(m, m) for m in ms]
        ps = [p + _mm16(p, m) for p, m in zip(ps, ms)]
    size = base
    while size < chunk:
        sh += 1
        size *= 2
        same_next = (ri >> sh) == (ci >> sh)
        sel = same_next & jnp.logical_not(same)
        ts = [_mm16(p, jnp.where(sel, n, 0.0)) for p, n in zip(ps, nmats)]
        ps = [p - _mm16(t, p) for p, t in zip(ps, ts)]
        same = same_next
    return ps


def _delta_body(q_ref, k_ref, v_ref, z_ref, gc_ref, gr_ref, bc_ref, s0_ref, nw_ref,
                o_ref, sf_ref, s_ref, *, chunk, cpb, hb):
    c_idx = pl.program_id(2)

    @pl.when(c_idx == 0)
    def _():
        s_ref[...] = s0_ref[0]

    ri = lax.broadcasted_iota(jnp.int32, (chunk, chunk), 0)
    ci = lax.broadcasted_iota(jnp.int32, (chunk, chunk), 1)
    causal = ri >= ci
    strict = ri > ci
    tril = causal.astype(F32)
    triu = (ri <= ci).astype(F32)
    hi = lax.Precision.HIGHEST
    nt = (((1,), (1,)), ((), ()))
    tn = (((0,), (0,)), ((), ()))
    hq = hb // 2

    kqk, pre = {}, {}
    for c in range(cpb):
        rows = slice(c * chunk, (c + 1) * chunk)
        for j in range(hq):
            k16 = k_ref[0, rows, j * HEAD:(j + 1) * HEAD]
            q16 = q_ref[0, rows, j * HEAD:(j + 1) * HEAD]
            kqk[c, j] = lax.dot_general(jnp.concatenate([k16, q16], axis=0), k16, nt,
                                        preferred_element_type=F32)
    nmats, keys = [], []
    for c in range(cpb):
        rows = slice(c * chunk, (c + 1) * chunk)
        gcol = jnp.dot(tril, gc_ref[0, 0, rows, :], precision=hi, preferred_element_type=F32)
        grow = jnp.dot(gr_ref[0, 0, c], triu, precision=hi, preferred_element_type=F32)
        bcol = bc_ref[0, 0, rows, :]
        for hh in range(hb):
            j = hh // 2
            k = k_ref[0, rows, j * HEAD:(j + 1) * HEAD].astype(F32)
            q = q_ref[0, rows, j * HEAD:(j + 1) * HEAD].astype(F32)
            v = v_ref[0, rows, hh * HEAD:(hh + 1) * HEAD].astype(F32)
            gc = gcol[:, hh:hh + 1]
            gr = grow[hh:hh + 1, :]
            beta = bcol[:, hh:hh + 1]
            decay = jnp.exp(jnp.where(causal, gc - gr, -jnp.inf))
            egc = jnp.exp(gc)
            g_last = gc[chunk - 1:chunk, :]
            kbeta = k * beta
            kk = kqk[c, j][:chunk]
            qk = kqk[c, j][chunk:]
            nmats.append(jnp.where(strict, (kk * beta) * decay, 0.0))
            keys.append((c, hh))
            pre[c, hh] = dict(
                rhs=jnp.concatenate([v * beta, kbeta * egc], axis=1).astype(BF16),
                qg=(q * egc).astype(BF16),
                intra=(qk * decay).astype(BF16),
                kd=(k * jnp.exp(g_last - gc)).astype(BF16),
                sdec=jnp.exp(g_last))
    tinv = _tri_inverse_all(nmats, ri, ci, chunk)
    for key, t in zip(keys, tinv):
        pre[key]['sol'] = jnp.dot(t.astype(BF16), pre[key]['rhs'], preferred_element_type=F32)

    for c in range(cpb):
        rows = slice(c * chunk, (c + 1) * chunk)
        ws = []
        for hh in range(hb):
            d = pre[c, hh]
            wq = jnp.concatenate([d['sol'][:, HEAD:].astype(BF16), d['qg']], axis=0)
            ws.append(jnp.dot(wq, s_ref[hh].astype(BF16), preferred_element_type=F32))
        for hh in range(hb):
            d = pre[c, hh]
            v16 = (d['sol'][:, :HEAD] - ws[hh][:chunk]).astype(BF16)
            o = ws[hh][chunk:] + jnp.dot(d['intra'], v16, preferred_element_type=F32)
            s_ref[hh] = s_ref[hh] * d['sdec'] + lax.dot_general(d['kd'], v16, tn,
                                                                preferred_element_type=F32)
            o = o * lax.rsqrt(jnp.mean(o * o, axis=-1, keepdims=True) + RMS_EPS) * nw_ref[...]
            zz = z_ref[0, rows, hh * HEAD:(hh + 1) * HEAD]
            o_ref[0, rows, hh * HEAD:(hh + 1) * HEAD] = (o * (zz * jax.nn.sigmoid(zz))).astype(BF16)

    @pl.when(c_idx == pl.num_programs(2) - 1)
    def _():
        sf_ref[0] = s_ref[...]


def _delta_call(q, k, v, zba, g, beta, s0, norm_w, chunk, cpb, hb):
    n_b, t_len, _ = q.shape
    nhg = V_HEADS // hb
    tc = chunk * cpb
    n_chunks = t_len // chunk
    g_col = g.reshape(n_b, t_len, nhg, hb).transpose(0, 2, 1, 3)
    b_col = beta.reshape(n_b, t_len, nhg, hb).transpose(0, 2, 1, 3)
    g_row = g.reshape(n_b, n_chunks, chunk, nhg, hb).transpose(0, 3, 1, 4, 2)
    hq = hb // 2
    return pl.pallas_call(
        functools.partial(_delta_body, chunk=chunk, cpb=cpb, hb=hb),
        grid=(n_b, nhg, t_len // tc),
        in_specs=[pl.BlockSpec((1, tc, hq * HEAD), lambda b, h, c: (b, c, h)),
                  pl.BlockSpec((1, tc, hq * HEAD), lambda b, h, c: (b, c, h)),
                  pl.BlockSpec((1, tc, hb * HEAD), lambda b, h, c: (b, c, h)),
                  pl.BlockSpec((1, tc, hb * HEAD), lambda b, h, c: (b, c, h)),
                  pl.BlockSpec((1, 1, tc, hb), lambda b, h, c: (b, h, c, 0)),
                  pl.BlockSpec((1, 1, cpb, hb, chunk), lambda b, h, c: (b, h, c, 0, 0)),
                  pl.BlockSpec((1, 1, tc, hb), lambda b, h, c: (b, h, c, 0)),
                  pl.BlockSpec((1, hb, HEAD, HEAD), lambda b, h, c: (b, h, 0, 0)),
                  pl.BlockSpec((1, HEAD), lambda b, h, c: (0, 0))],
        out_specs=[pl.BlockSpec((1, tc, hb * HEAD), lambda b, h, c: (b, c, h)),
                   pl.BlockSpec((1, hb, HEAD, HEAD), lambda b, h, c: (b, h, 0, 0))],
        out_shape=[jax.ShapeDtypeStruct((n_b, t_len, VAL_DIM), BF16),
                   jax.ShapeDtypeStruct((n_b, V_HEADS, HEAD, HEAD), F32)],
        scratch_shapes=[pltpu.VMEM((hb, HEAD, HEAD), F32)],
        compiler_params=_cparams(("arbitrary", "arbitrary", "arbitrary")),
        name="gated_delta_rule",
    )(q, k, v, zba, g_col, g_row, b_col, s0, norm_w)


def _tile(m, pref):
    t = min(m, pref)
    assert m % t == 0, (m, t)
    return t


def _trunk(x, s5_re0, s5_im0, conv0, s0, p):
    n_b, t_len, _ = x.shape
    m = n_b * t_len
    tm = _tile(m, 512)

    tt = _tile(t_len, 256)
    g, sr, si = _s5_call(x, p['nm0'], p['s5_d'], p['wb'], p['wc'], p['coef'],
                         s5_re0.reshape(n_b, 1, S5_CH), s5_im0.reshape(n_b, 1, S5_CH), tt)
    x2 = x.reshape(m, D_MODEL)
    x2 = _glu_call(g.reshape(m, D_MODEL), p['w_glu'], x2, tm, 512)
    x2 = _ffn_call(x2, p['nf0'], p['w_up0'], p['w_down0'], p['nfin'], tm, 512, False)

    qkv = _norm_matmul_call(x2, p['nm1'], p['w_qkv'], tm, 512, F32)
    zba = _norm_matmul_call(x2, p['nm1'], p['w_zba'], tm, p['w_zba'].shape[1] // 3, F32)
    zba3 = zba.reshape(n_b, t_len, -1)
    q, k, v, gg, beta, new_conv = _conv_call(
        qkv.reshape(n_b, t_len, CONV_DIM), zba3, conv0, p['conv_w'], p['a_log'], p['dt_bias'],
        _tile(t_len, 128))
    chunk = min(t_len, 64)
    cpb = _tile(t_len // chunk, 4)
    og, s_fin = _delta_call(q, k, v, zba3, gg, beta, s0, p['gdn_norm_w'], chunk, cpb, 8)
    x2 = _matmul_res_call(og.reshape(m, VAL_DIM), p['w_out'], x2, tm, 512)
    y = _ffn_call(x2, p['nf1'], p['w_up1'], p['w_down1'], p['nfin'], tm, 512, True)

    return (y.reshape(n_b, t_len, D_MODEL),
            sr.reshape(1, n_b, S5_GROUPS, S5_STATE), si.reshape(1, n_b, S5_GROUPS, S5_STATE),
            new_conv[None], s_fin[None])


def kernel(x_prompt, x_sample, state_s5_re, state_s5_im, state_gdn_conv, state_gdn_s,
           norm_mix, norm_ffn, norm_final,
           s5_lam_re, s5_lam_im, s5_log_dt, s5_b_re, s5_b_im, s5_c_re, s5_c_im,
           s5_d, s5_w_glu,
           gdn_w_in, gdn_conv_w, gdn_a_log, gdn_dt_bias, gdn_norm_w, gdn_w_out,
           ffn_w_up, ffn_w_down):
    wb, wc, coef = _s5_params(s5_lam_re[0], s5_lam_im[0], s5_log_dt[0], s5_b_re[0], s5_b_im[0],
                              s5_c_re[0], s5_c_im[0])
    w_in = gdn_w_in[0]
    ba = w_in[:, CONV_DIM + VAL_DIM:]
    w_zba = jnp.concatenate(
        [w_in[:, CONV_DIM:CONV_DIM + VAL_DIM], ba,
         jnp.zeros((D_MODEL, LANES - ba.shape[1]), w_in.dtype)], axis=1)
    row = lambda a: a.reshape(1, -1).astype(F32)
    p = dict(
        nm0=row(norm_mix[0]), nm1=row(norm_mix[1]), nf0=row(norm_ffn[0]), nf1=row(norm_ffn[1]),
        nfin=row(norm_final), s5_d=row(s5_d[0]), wb=wb, wc=wc, coef=coef,
        w_glu=s5_w_glu[0].astype(BF16),
        w_up0=ffn_w_up[0].astype(BF16), w_down0=ffn_w_down[0].astype(BF16),
        w_up1=ffn_w_up[1].astype(BF16), w_down1=ffn_w_down[1].astype(BF16),
        w_qkv=w_in[:, :CONV_DIM].astype(BF16), w_zba=w_zba.astype(BF16),
        conv_w=gdn_conv_w[0].astype(F32), a_log=row(gdn_a_log[0]), dt_bias=row(gdn_dt_bias[0]),
        gdn_norm_w=row(gdn_norm_w[0]), w_out=gdn_w_out[0].astype(BF16),
    )
    n_bp = x_prompt.shape[0]
    z_state = jnp.zeros((n_bp, S5_GROUPS, S5_STATE), F32)
    z_conv = jnp.zeros((n_bp, CONV_W - 1, CONV_DIM), F32)
    z_s = jnp.zeros((n_bp, V_HEADS, HEAD, HEAD), F32)

    yp, p_re, p_im, p_conv, p_s = _trunk(x_prompt, z_state, z_state, z_conv, z_s, p)
    ys, s_re, s_im, s_conv, s_s = _trunk(x_sample, state_s5_re[0], state_s5_im[0],
                                         state_gdn_conv[0], state_gdn_s[0], p)
    return (yp, ys, p_re, p_im, p_conv, p_s, s_re, s_im, s_conv, s_s)
```

```python
import functools
import math

import jax
import jax.numpy as jnp
from jax import lax
from jax.experimental import pallas as pl
from jax.experimental.pallas import tpu as pltpu

F32 = jnp.float32
BF16 = jnp.bfloat16

RMS_EPS = 1e-6
L2_EPS = 1e-6
D_MODEL = 2048
S5_GROUP = 16
S5_STATE = 64
S5_GROUPS = D_MODEL // S5_GROUP
S5_CH = S5_GROUPS * S5_STATE
HEAD = 128
QK_HEADS = D_MODEL // HEAD
V_HEADS = 2 * QK_HEADS
KEY_DIM = QK_HEADS * HEAD
VAL_DIM = V_HEADS * HEAD
CONV_DIM = 2 * KEY_DIM + VAL_DIM
CONV_W = 4

SUBLANES = 8
LANES = 128
MXU_K = 256
S5_KB = D_MODEL // MXU_K
S5_CB = S5_CH // S5_KB
VMEM_LIMIT = 56 * 1024 * 1024


def _cparams(sem):
    return pltpu.CompilerParams(dimension_semantics=sem, vmem_limit_bytes=VMEM_LIMIT)


def _rms(x, w):
    return x * lax.rsqrt(jnp.mean(x * x, axis=-1, keepdims=True) + RMS_EPS) * w


def _s5_body(x_ref, nw_ref, d_ref, wb_ref, wc_ref, coef_ref, perm_ref, permt_ref, s0r_ref, s0i_ref,
             g_ref, sr_ref, si_ref, car_ref, h_ref, bu_ref, st_ref, *, tt):
    t = pl.program_id(1)
    seg = tt // SUBLANES

    @pl.when(t == 0)
    def _():
        car_ref[0] = jnp.broadcast_to(s0r_ref[0], (SUBLANES, S5_CH))
        car_ref[1] = jnp.broadcast_to(s0i_ref[0], (SUBLANES, S5_CH))

    h = _rms(x_ref[0], nw_ref[...])
    h_ref[...] = h
    hp = jnp.dot(perm_ref[...], h.astype(BF16), preferred_element_type=F32).astype(BF16)
    row0 = lax.broadcasted_iota(jnp.int32, (SUBLANES, S5_CB), 0) == 0

    def step(ar, ai, sr, si, xr, xi):
        return ar * sr - ai * si + xr, ar * si + ai * sr + xi

    def project_in(kb):
        ks = slice(kb * MXU_K, (kb + 1) * MXU_K)
        bu_ref[kb % 2] = jnp.dot(hp[:, ks], wb_ref[kb], preferred_element_type=F32)

    def project_out(kb):
        ks = slice(kb * MXU_K, (kb + 1) * MXU_K)
        yp = jnp.dot(st_ref[kb % 2].astype(BF16), wc_ref[kb], preferred_element_type=F32)
        y_hi = yp.astype(BF16)
        y_lo = (yp - y_hi.astype(F32)).astype(BF16)
        y2 = jnp.dot(permt_ref[...], jnp.concatenate([y_hi, y_lo], axis=1),
                     preferred_element_type=F32)
        y = y2[:, :MXU_K] + y2[:, MXU_K:] + d_ref[:, ks] * h_ref[:, ks]
        g_ref[0, :, ks] = jax.nn.gelu(y).astype(BF16)

    def scan(kb):
        cs = slice(kb * S5_CB, (kb + 1) * S5_CB)
        buf = kb % 2

        def bu_rows(j):
            rows = slice(j * SUBLANES, (j + 1) * SUBLANES)
            return bu_ref[buf, rows, 0:S5_CB], bu_ref[buf, rows, S5_CB:2 * S5_CB]

        ar = coef_ref[4, 0, :, cs]
        ai = coef_ref[4, 1, :, cs]
        er, ei = bu_rows(0)
        for j in range(1, seg):
            er, ei = step(ar, ai, er, ei, *bu_rows(j))
        for lvl, sh in enumerate((1, 2, 4)):
            cr_ = coef_ref[lvl, 0, :, cs]
            ci_ = coef_ref[lvl, 1, :, cs]
            pr = pltpu.roll(er, sh, 0)
            pi = pltpu.roll(ei, sh, 0)
            er, ei = er + cr_ * pr - ci_ * pi, ei + cr_ * pi + ci_ * pr
        c0r = car_ref[0, :, cs]
        c0i = car_ref[1, :, cs]
        cr_ = coef_ref[3, 0, :, cs]
        ci_ = coef_ref[3, 1, :, cs]
        fr = er + cr_ * c0r - ci_ * c0i
        fi = ei + cr_ * c0i + ci_ * c0r
        car_ref[0, :, cs] = jnp.broadcast_to(fr[SUBLANES - 1:SUBLANES], (SUBLANES, S5_CB))
        car_ref[1, :, cs] = jnp.broadcast_to(fi[SUBLANES - 1:SUBLANES], (SUBLANES, S5_CB))
        sr = jnp.where(row0, c0r, pltpu.roll(fr, 1, 0))
        si = jnp.where(row0, c0i, pltpu.roll(fi, 1, 0))
        for j in range(seg):
            sr, si = step(ar, ai, sr, si, *bu_rows(j))
            rows = slice(j * SUBLANES, (j + 1) * SUBLANES)
            st_ref[buf, rows, 0:S5_CB] = sr
            st_ref[buf, rows, S5_CB:2 * S5_CB] = si

    project_in(0)
    for kb in range(S5_KB):
        if kb + 1 < S5_KB:
            project_in(kb + 1)
        if kb >= 1:
            project_out(kb - 1)
        scan(kb)
    project_out(S5_KB - 1)

    @pl.when(t == pl.num_programs(1) - 1)
    def _():
        sr_ref[0] = car_ref[0, 0:1, :]
        si_ref[0] = car_ref[1, 0:1, :]


def _s5_call(x, nw, d, wb, wc, a_re, a_im, s0r, s0i, tt):
    n_b, t_len, _ = x.shape
    seg = tt // SUBLANES
    coef = _s5_coef(a_re, a_im, seg)
    p_idx = jnp.arange(tt)
    src = (p_idx % SUBLANES) * seg + p_idx // SUBLANES
    perm = (src[:, None] == p_idx[None, :]).astype(BF16)
    const = lambda *shape: pl.BlockSpec(shape, lambda b, t: (0,) * len(shape))
    state = pl.BlockSpec((1, 1, S5_CH), lambda b, t: (b, 0, 0))
    return pl.pallas_call(
        functools.partial(_s5_body, tt=tt),
        grid=(n_b, t_len // tt),
        in_specs=[pl.BlockSpec((1, tt, D_MODEL), lambda b, t: (b, t, 0)),
                  const(1, D_MODEL), const(1, D_MODEL),
                  const(S5_KB, MXU_K, 2 * S5_CB), const(S5_KB, 2 * S5_CB, MXU_K),
                  const(5, 2, SUBLANES, S5_CH), const(tt, tt), const(tt, tt), state, state],
        out_specs=[pl.BlockSpec((1, tt, D_MODEL), lambda b, t: (b, t, 0)), state, state],
        out_shape=[jax.ShapeDtypeStruct((n_b, t_len, D_MODEL), BF16),
                   jax.ShapeDtypeStruct((n_b, 1, S5_CH), F32),
                   jax.ShapeDtypeStruct((n_b, 1, S5_CH), F32)],
        scratch_shapes=[pltpu.VMEM((2, SUBLANES, S5_CH), F32),
                        pltpu.VMEM((tt, D_MODEL), F32),
                        pltpu.VMEM((2, tt, 2 * S5_CB), F32),
                        pltpu.VMEM((2, tt, 2 * S5_CB), F32)],
        compiler_params=_cparams(("arbitrary", "arbitrary")),
        name="s5_mixer",
    )(x, nw, d, wb, wc, coef, perm, perm.T, s0r, s0i)


def _s5_coef(a_re, a_im, seg):
    assert seg & (seg - 1) == 0, seg
    ar, ai = a_re, a_im
    for _ in range(int(math.log2(seg))):
        ar, ai = ar * ar - ai * ai, 2.0 * ar * ai
    pw_r, pw_i = [ar], [ai]
    for _ in range(SUBLANES - 1):
        pr, pi = pw_r[-1], pw_i[-1]
        pw_r.append(pr * ar - pi * ai)
        pw_i.append(pr * ai + pi * ar)
    row = jnp.arange(SUBLANES)[:, None]
    levels = []
    for sh in (1, 2, 4):
        m = (row >= sh).astype(F32)
        levels.append(jnp.stack([m * pw_r[sh - 1], m * pw_i[sh - 1]]))
    levels.append(jnp.stack([jnp.concatenate(pw_r, axis=0), jnp.concatenate(pw_i, axis=0)]))
    ones = jnp.ones((SUBLANES, 1), F32)
    levels.append(jnp.stack([ones * a_re, ones * a_im]))
    return jnp.stack(levels)


def _s5_params(lam_re, lam_im, log_dt, b_re, b_im, c_re, c_im):
    lr = jnp.minimum(lam_re.astype(F32), -1e-4)
    li = lam_im.astype(F32)
    dt = jnp.exp(log_dt.astype(F32))[:, None]
    mag = jnp.exp(lr * dt)
    ang = li * dt
    a_re = mag * jnp.cos(ang)
    a_im = mag * jnp.sin(ang)
    nr = a_re - 1.0
    den = lr * lr + li * li
    k_re = (nr * lr + a_im * li) / den
    k_im = (a_im * lr - nr * li) / den
    br = b_re.astype(F32)
    bi = b_im.astype(F32)
    bb_re = k_re[..., None] * br - k_im[..., None] * bi
    bb_im = k_re[..., None] * bi + k_im[..., None] * br

    gl = S5_GROUPS // S5_KB
    eye = jnp.eye(gl, dtype=F32)

    def blockdiag_in(bb):
        bbk = bb.reshape(S5_KB, gl, S5_STATE, S5_GROUP)
        w = jnp.einsum('kgph,gf->kghfp', bbk, eye)
        return w.reshape(S5_KB, gl * S5_GROUP, gl * S5_STATE)

    def blockdiag_out(c):
        ck = c.reshape(S5_KB, gl, S5_GROUP, S5_STATE)
        w = jnp.einsum('kghp,gf->kgpfh', ck, eye)
        return w.reshape(S5_KB, gl * S5_STATE, gl * S5_GROUP)

    wb = jnp.concatenate([blockdiag_in(bb_re), blockdiag_in(bb_im)], axis=2).astype(BF16)
    wc = jnp.concatenate([blockdiag_out(c_re.astype(F32)),
                          -blockdiag_out(c_im.astype(F32))], axis=1).astype(BF16)

    return wb, wc, a_re.reshape(1, S5_CH), a_im.reshape(1, S5_CH)


def _glu_body(g_ref, wa_ref, wg_ref, x_ref, o_ref):
    g = g_ref[...]
    za = jnp.dot(g, wa_ref[...], preferred_element_type=F32)
    zg = jnp.dot(g, wg_ref[...], preferred_element_type=F32)
    o_ref[...] = x_ref[...] + za * jax.nn.sigmoid(zg)


def _glu_call(g, w_glu, x, tm, tn):
    m = g.shape[0]
    nj = D_MODEL // tn
    return pl.pallas_call(
        _glu_body,
        grid=(m // tm, nj),
        in_specs=[pl.BlockSpec((tm, D_MODEL), lambda i, j: (i, 0)),
                  pl.BlockSpec((D_MODEL, tn), lambda i, j: (0, j)),
                  pl.BlockSpec((D_MODEL, tn), lambda i, j: (0, j + nj)),
                  pl.BlockSpec((tm, tn), lambda i, j: (i, j))],
        out_specs=pl.BlockSpec((tm, tn), lambda i, j: (i, j)),
        out_shape=jax.ShapeDtypeStruct((m, D_MODEL), F32),
        compiler_params=_cparams(("parallel", "arbitrary")),
        name="glu_residual",
    )(g, w_glu, w_glu, x)


def _ffn_body(x_ref, nw_ref, wu_ref, wd_ref, fw_ref, o_ref, h_ref, acc_ref, *, final_norm):
    f = pl.program_id(1)

    @pl.when(f == 0)
    def _():
        h_ref[...] = _rms(x_ref[...], nw_ref[...]).astype(BF16)
        acc_ref[...] = jnp.zeros_like(acc_ref)

    a = jnp.maximum(jnp.dot(h_ref[...], wu_ref[...], preferred_element_type=F32), 0.0)
    acc_ref[...] += jnp.dot((a * a).astype(BF16), wd_ref[...], preferred_element_type=F32)

    @pl.when(f == pl.num_programs(1) - 1)
    def _():
        y = x_ref[...] + acc_ref[...]
        if final_norm:
            y = _rms(y, fw_ref[...])
        o_ref[...] = y


def _ffn_call(x, nw, w_up, w_down, fw, tm, tf, final_norm):
    m = x.shape[0]
    d_ff = w_up.shape[1]
    return pl.pallas_call(
        functools.partial(_ffn_body, final_norm=final_norm),
        grid=(m // tm, d_ff // tf),
        in_specs=[pl.BlockSpec((tm, D_MODEL), lambda i, f: (i, 0)),
                  pl.BlockSpec((1, D_MODEL), lambda i, f: (0, 0)),
                  pl.BlockSpec((D_MODEL, tf), lambda i, f: (0, f)),
                  pl.BlockSpec((tf, D_MODEL), lambda i, f: (f, 0)),
                  pl.BlockSpec((1, D_MODEL), lambda i, f: (0, 0))],
        out_specs=pl.BlockSpec((tm, D_MODEL), lambda i, f: (i, 0)),
        out_shape=jax.ShapeDtypeStruct((m, D_MODEL), F32),
        scratch_shapes=[pltpu.VMEM((tm, D_MODEL), BF16), pltpu.VMEM((tm, D_MODEL), F32)],
        compiler_params=_cparams(("parallel", "arbitrary")),
        name="ffn_residual",
    )(x, nw, w_up, w_down, fw)


def _norm_matmul_body(x_ref, nw_ref, w_ref, o_ref, h_ref):
    @pl.when(pl.program_id(1) == 0)
    def _():
        h_ref[...] = _rms(x_ref[...], nw_ref[...]).astype(BF16)

    o_ref[...] = jnp.dot(h_ref[...], w_ref[...], preferred_element_type=F32).astype(o_ref.dtype)


def _norm_matmul_call(x, nw, w, tm, tn, out_dtype):
    m = x.shape[0]
    n = w.shape[1]
    return pl.pallas_call(
        _norm_matmul_body,
        grid=(m // tm, n // tn),
        in_specs=[pl.BlockSpec((tm, D_MODEL), lambda i, j: (i, 0)),
                  pl.BlockSpec((1, D_MODEL), lambda i, j: (0, 0)),
                  pl.BlockSpec((D_MODEL, tn), lambda i, j: (0, j))],
        out_specs=pl.BlockSpec((tm, tn), lambda i, j: (i, j)),
        out_shape=jax.ShapeDtypeStruct((m, n), out_dtype),
        scratch_shapes=[pltpu.VMEM((tm, D_MODEL), BF16)],
        compiler_params=_cparams(("parallel", "arbitrary")),
        name="norm_matmul",
    )(x, nw, w)


def _matmul_res_body(a_ref, w_ref, x_ref, o_ref):
    o_ref[...] = x_ref[...] + jnp.dot(a_ref[...], w_ref[...], preferred_element_type=F32)


def _matmul_res_call(a, w, x, tm, tn):
    m, k = a.shape
    n = w.shape[1]
    return pl.pallas_call(
        _matmul_res_body,
        grid=(m // tm, n // tn),
        in_specs=[pl.BlockSpec((tm, k), lambda i, j: (i, 0)),
                  pl.BlockSpec((k, tn), lambda i, j: (0, j)),
                  pl.BlockSpec((tm, tn), lambda i, j: (i, j))],
        out_specs=pl.BlockSpec((tm, tn), lambda i, j: (i, j)),
        out_shape=jax.ShapeDtypeStruct((m, n), F32),
        compiler_params=_cparams(("parallel", "arbitrary")),
        name="matmul_residual",
    )(a, w, x)


def _conv_body(qkv_ref, ba_ref, cprev_ref, cw_ref, alog_ref, dtb_ref,
               q_ref, k_ref, v_ref, g_ref, beta_ref, nc_ref, xp_ref, *, tt):
    t = pl.program_id(1)
    pad = SUBLANES
    tail = CONV_W - 1

    @pl.when(t == 0)
    def _():
        xp_ref[pad - tail:pad, :] = cprev_ref[0]

    xp_ref[pad:pad + tt, :] = qkv_ref[0]

    for j in range(CONV_DIM // LANES):
        cs = slice(j * LANES, (j + 1) * LANES)
        acc = xp_ref[pad - tail:pad - tail + tt, cs] * cw_ref[0:1, cs]
        for tap in range(1, CONV_W):
            acc = acc + xp_ref[pad - tail + tap:pad - tail + tap + tt, cs] * cw_ref[tap:tap + 1, cs]
        c = acc * jax.nn.sigmoid(acc)
        if j < QK_HEADS:
            c = c * lax.rsqrt(jnp.sum(c * c, axis=-1, keepdims=True) + L2_EPS) * (HEAD ** -0.5)
            q_ref[0, :, cs] = c.astype(BF16)
        elif j < 2 * QK_HEADS:
            c = c * lax.rsqrt(jnp.sum(c * c, axis=-1, keepdims=True) + L2_EPS)
            k_ref[0, :, j * LANES - KEY_DIM:(j + 1) * LANES - KEY_DIM] = c.astype(BF16)
        else:
            v_ref[0, :, j * LANES - 2 * KEY_DIM:(j + 1) * LANES - 2 * KEY_DIM] = c.astype(BF16)

    new_tail = xp_ref[pad + tt - tail:pad + tt, :]
    xp_ref[pad - tail:pad, :] = new_tail

    @pl.when(t == pl.num_programs(1) - 1)
    def _():
        nc_ref[0] = new_tail

    b = ba_ref[0, :, 0:V_HEADS]
    a = ba_ref[0, :, V_HEADS:2 * V_HEADS]
    z = a + dtb_ref[...]
    softplus = jnp.maximum(z, 0.0) + jnp.log1p(jnp.exp(-jnp.abs(z)))
    g_ref[0] = -jnp.exp(alog_ref[...]) * softplus
    beta_ref[0] = jax.nn.sigmoid(b)


def _conv_call(qkv, zba, conv_prev, conv_w, a_log, dt_bias, tt):
    n_b, t_len, _ = qkv.shape
    ba_blk = VAL_DIM // LANES
    return pl.pallas_call(
        functools.partial(_conv_body, tt=tt),
        grid=(n_b, t_len // tt),
        in_specs=[pl.BlockSpec((1, tt, CONV_DIM), lambda b, t: (b, t, 0)),
                  pl.BlockSpec((1, tt, LANES), lambda b, t: (b, t, ba_blk)),
                  pl.BlockSpec((1, CONV_W - 1, CONV_DIM), lambda b, t: (b, 0, 0)),
                  pl.BlockSpec((CONV_W, CONV_DIM), lambda b, t: (0, 0)),
                  pl.BlockSpec((1, V_HEADS), lambda b, t: (0, 0)),
                  pl.BlockSpec((1, V_HEADS), lambda b, t: (0, 0))],
        out_specs=[pl.BlockSpec((1, tt, KEY_DIM), lambda b, t: (b, t, 0)),
                   pl.BlockSpec((1, tt, KEY_DIM), lambda b, t: (b, t, 0)),
                   pl.BlockSpec((1, tt, VAL_DIM), lambda b, t: (b, t, 0)),
                   pl.BlockSpec((1, tt, V_HEADS), lambda b, t: (b, t, 0)),
                   pl.BlockSpec((1, tt, V_HEADS), lambda b, t: (b, t, 0)),
                   pl.BlockSpec((1, CONV_W - 1, CONV_DIM), lambda b, t: (b, 0, 0))],
        out_shape=[jax.ShapeDtypeStruct((n_b, t_len, KEY_DIM), BF16),
                   jax.ShapeDtypeStruct((n_b, t_len, KEY_DIM), BF16),
                   jax.ShapeDtypeStruct((n_b, t_len, VAL_DIM), BF16),
                   jax.ShapeDtypeStruct((n_b, t_len, V_HEADS), F32),
                   jax.ShapeDtypeStruct((n_b, t_len, V_HEADS), F32),
                   jax.ShapeDtypeStruct((n_b, CONV_W - 1, CONV_DIM), F32)],
        scratch_shapes=[pltpu.VMEM((tt + SUBLANES, CONV_DIM), F32)],
        compiler_params=_cparams(("arbitrary", "arbitrary")),
        name="gdn_conv_gates",
    )(qkv, zba, conv_prev, conv_w, a_log, dt_bias)


INV_BASE = 16


def _mm16(a, b):
    return jnp.dot(a.astype(BF16), b.astype(BF16), preferred_element_type=F32)


def _tri_inverse_all(nmats, ri, ci, chunk):
    base = min(INV_BASE, chunk)
    sh = int(math.log2(base))
    same = (ri >> sh) == (ci >> sh)
    eye = (ri == ci).astype(F32)
    ms = [jnp.where(same, -n, 0.0) for n in nmats]
    ps = [eye + m for m in ms]
    for _ in range(sh - 1):
        ms = [_mm16(m, m) for m in ms]
        ps = [p + _mm16(p, m) for p, m in zip(ps, ms)]
    size = base
    while size < chunk:
        sh += 1
        size *= 2
        same_next = (ri >> sh) == (ci >> sh)
        sel = same_next & jnp.logical_not(same)
        ts = [_mm16(p, jnp.where(sel, n, 0.0)) for p, n in zip(ps, nmats)]
        ps = [p - _mm16(t, p) for p, t in zip(ps, ts)]
        same = same_next
    return ps


def _delta_body(q_ref, k_ref, v_ref, z_ref, gc_ref, gr_ref, bc_ref, s0_ref, nw_ref,
                o_ref, sf_ref, s_ref, *, chunk, cpb, hb):
    c_idx = pl.program_id(2)

    @pl.when(c_idx == 0)
    def _():
        s_ref[...] = s0_ref[0]

    ri = lax.broadcasted_iota(jnp.int32, (chunk, chunk), 0)
    ci = lax.broadcasted_iota(jnp.int32, (chunk, chunk), 1)
    causal = ri >= ci
    strict = ri > ci
    tril = causal.astype(F32)
    triu = (ri <= ci).astype(F32)
    hi = lax.Precision.HIGHEST
    nt = (((1,), (1,)), ((), ()))
    tn = (((0,), (0,)), ((), ()))
    hq = hb // 2

    kqk, pre = {}, {}
    for c in range(cpb):
        rows = slice(c * chunk, (c + 1) * chunk)
        for j in range(hq):
            k16 = k_ref[0, rows, j * HEAD:(j + 1) * HEAD]
            q16 = q_ref[0, rows, j * HEAD:(j + 1) * HEAD]
            kqk[c, j] = lax.dot_general(jnp.concatenate([k16, q16], axis=0), k16, nt,
                                        preferred_element_type=F32)
    nmats, keys = [], []
    for c in range(cpb):
        rows = slice(c * chunk, (c + 1) * chunk)
        gcol = jnp.dot(tril, gc_ref[0, 0, rows, :], precision=hi, preferred_element_type=F32)
        grow = jnp.dot(gr_ref[0, 0, c], triu, precision=hi, preferred_element_type=F32)
        bcol = bc_ref[0, 0, rows, :]
        for hh in range(hb):
            j = hh // 2
            k = k_ref[0, rows, j * HEAD:(j + 1) * HEAD].astype(F32)
            q = q_ref[0, rows, j * HEAD:(j + 1) * HEAD].astype(F32)
            v = v_ref[0, rows, hh * HEAD:(hh + 1) * HEAD].astype(F32)
            gc = gcol[:, hh:hh + 1]
            gr = grow[hh:hh + 1, :]
            beta = bcol[:, hh:hh + 1]
            decay = jnp.exp(jnp.where(causal, gc - gr, -jnp.inf))
            egc = jnp.exp(gc)
            g_last = gc[chunk - 1:chunk, :]
            kbeta = k * beta
            kk = kqk[c, j][:chunk]
            qk = kqk[c, j][chunk:]
            nmats.append(jnp.where(strict, (kk * beta) * decay, 0.0))
            keys.append((c, hh))
            pre[c, hh] = dict(
                rhs=jnp.concatenate([v * beta, kbeta * egc], axis=1).astype(BF16),
                qg=(q * egc).astype(BF16),
                intra=(qk * decay).astype(BF16),
                kd=(k * jnp.exp(g_last - gc)).astype(BF16),
                sdec=jnp.exp(g_last))
    tinv = _tri_inverse_all(nmats, ri, ci, chunk)
    for key, t in zip(keys, tinv):
        pre[key]['sol'] = jnp.dot(t.astype(BF16), pre[key]['rhs'], preferred_element_type=F32)

    for c in range(cpb):
        rows = slice(c * chunk, (c + 1) * chunk)
        ws = []
        for hh in range(hb):
            d = pre[c, hh]
            wq = jnp.concatenate([d['sol'][:, HEAD:].astype(BF16), d['qg']], axis=0)
            ws.append(jnp.dot(wq, s_ref[hh].astype(BF16), preferred_element_type=F32))
        for hh in range(hb):
            d = pre[c, hh]
            v16 = (d['sol'][:, :HEAD] - ws[hh][:chunk]).astype(BF16)
            o = ws[hh][chunk:] + jnp.dot(d['intra'], v16, preferred_element_type=F32)
            s_ref[hh] = s_ref[hh] * d['sdec'] + lax.dot_general(d['kd'], v16, tn,
                                                                preferred_element_type=F32)
            o = o * lax.rsqrt(jnp.mean(o * o, axis=-1, keepdims=True) + RMS_EPS) * nw_ref[...]
            zz = z_ref[0, rows, hh * HEAD:(hh + 1) * HEAD]
            o_ref[0, rows, hh * HEAD:(hh + 1) * HEAD] = (o * (zz * jax.nn.sigmoid(zz))).astype(BF16)

    @pl.when(c_idx == pl.num_programs(2) - 1)
    def _():
        sf_ref[0] = s_ref[...]


def _delta_call(q, k, v, zba, g, beta, s0, norm_w, chunk, cpb, hb):
    n_b, t_len, _ = q.shape
    nhg = V_HEADS // hb
    tc = chunk * cpb
    n_chunks = t_len // chunk
    g_col = g.reshape(n_b, t_len, nhg, hb).transpose(0, 2, 1, 3)
    b_col = beta.reshape(n_b, t_len, nhg, hb).transpose(0, 2, 1, 3)
    g_row = g.reshape(n_b, n_chunks, chunk, nhg, hb).transpose(0, 3, 1, 4, 2)
    hq = hb // 2
    return pl.pallas_call(
        functools.partial(_delta_body, chunk=chunk, cpb=cpb, hb=hb),
        grid=(n_b, nhg, t_len // tc),
        in_specs=[pl.BlockSpec((1, tc, hq * HEAD), lambda b, h, c: (b, c, h)),
                  pl.BlockSpec((1, tc, hq * HEAD), lambda b, h, c: (b, c, h)),
                  pl.BlockSpec((1, tc, hb * HEAD), lambda b, h, c: (b, c, h)),
                  pl.BlockSpec((1, tc, hb * HEAD), lambda b, h, c: (b, c, h)),
                  pl.BlockSpec((1, 1, tc, hb), lambda b, h, c: (b, h, c, 0)),
                  pl.BlockSpec((1, 1, cpb, hb, chunk), lambda b, h, c: (b, h, c, 0, 0)),
                  pl.BlockSpec((1, 1, tc, hb), lambda b, h, c: (b, h, c, 0)),
                  pl.BlockSpec((1, hb, HEAD, HEAD), lambda b, h, c: (b, h, 0, 0)),
                  pl.BlockSpec((1, HEAD), lambda b, h, c: (0, 0))],
        out_specs=[pl.BlockSpec((1, tc, hb * HEAD), lambda b, h, c: (b, c, h)),
                   pl.BlockSpec((1, hb, HEAD, HEAD), lambda b, h, c: (b, h, 0, 0))],
        out_shape=[jax.ShapeDtypeStruct((n_b, t_len, VAL_DIM), BF16),
                   jax.ShapeDtypeStruct((n_b, V_HEADS, HEAD, HEAD), F32)],
        scratch_shapes=[pltpu.VMEM((hb, HEAD, HEAD), F32)],
        compiler_params=_cparams(("arbitrary", "arbitrary", "arbitrary")),
        name="gated_delta_rule",
    )(q, k, v, zba, g_col, g_row, b_col, s0, norm_w)


def _tile(m, pref):
    t = min(m, pref)
    assert m % t == 0, (m, t)
    return t


def _trunk(x, s5_re0, s5_im0, conv0, s0, p):
    n_b, t_len, _ = x.shape
    m = n_b * t_len
    tm = _tile(m, 512)

    tt = _tile(t_len, 256)
    g, sr, si = _s5_call(x, p['nm0'], p['s5_d'], p['wb'], p['wc'], p['a_re'], p['a_im'],
                         s5_re0.reshape(n_b, 1, S5_CH), s5_im0.reshape(n_b, 1, S5_CH), tt)
    x2 = x.reshape(m, D_MODEL)
    x2 = _glu_call(g.reshape(m, D_MODEL), p['w_glu'], x2, tm, 512)
    x2 = _ffn_call(x2, p['nf0'], p['w_up0'], p['w_down0'], p['nfin'], tm, 512, False)

    qkv = _norm_matmul_call(x2, p['nm1'], p['w_qkv'], tm, 512, F32)
    zba = _norm_matmul_call(x2, p['nm1'], p['w_zba'], tm, p['w_zba'].shape[1] // 3, F32)
    zba3 = zba.reshape(n_b, t_len, -1)
    q, k, v, gg, beta, new_conv = _conv_call(
        qkv.reshape(n_b, t_len, CONV_DIM), zba3, conv0, p['conv_w'], p['a_log'], p['dt_bias'],
        _tile(t_len, 128))
    chunk = min(t_len, 64)
    cpb = _tile(t_len // chunk, 4)
    og, s_fin = _delta_call(q, k, v, zba3, gg, beta, s0, p['gdn_norm_w'], chunk, cpb, 8)
    x2 = _matmul_res_call(og.reshape(m, VAL_DIM), p['w_out'], x2, tm, 512)
    y = _ffn_call(x2, p['nf1'], p['w_up1'], p['w_down1'], p['nfin'], tm, 512, True)

    return (y.reshape(n_b, t_len, D_MODEL),
            sr.reshape(1, n_b, S5_GROUPS, S5_STATE), si.reshape(1, n_b, S5_GROUPS, S5_STATE),
            new_conv[None], s_fin[None])


def kernel(x_prompt, x_sample, state_s5_re, state_s5_im, state_gdn_conv, state_gdn_s,
           norm_mix, norm_ffn, norm_final,
           s5_lam_re, s5_lam_im, s5_log_dt, s5_b_re, s5_b_im, s5_c_re, s5_c_im,
           s5_d, s5_w_glu,
           gdn_w_in, gdn_conv_w, gdn_a_log, gdn_dt_bias, gdn_norm_w, gdn_w_out,
           ffn_w_up, ffn_w_down):
    wb, wc, a_re, a_im = _s5_params(s5_lam_re[0], s5_lam_im[0], s5_log_dt[0], s5_b_re[0], s5_b_im[0],
                              s5_c_re[0], s5_c_im[0])
    w_in = gdn_w_in[0]
    ba = w_in[:, CONV_DIM + VAL_DIM:]
    w_zba = jnp.concatenate(
        [w_in[:, CONV_DIM:CONV_DIM + VAL_DIM], ba,
         jnp.zeros((D_MODEL, LANES - ba.shape[1]), w_in.dtype)], axis=1)
    row = lambda a: a.reshape(1, -1).astype(F32)
    p = dict(
        nm0=row(norm_mix[0]), nm1=row(norm_mix[1]), nf0=row(norm_ffn[0]), nf1=row(norm_ffn[1]),
        nfin=row(norm_final), s5_d=row(s5_d[0]), wb=wb, wc=wc, a_re=a_re, a_im=a_im,
        w_glu=s5_w_glu[0].astype(BF16),
        w_up0=ffn_w_up[0].astype(BF16), w_down0=ffn_w_down[0].astype(BF16),
        w_up1=ffn_w_up[1].astype(BF16), w_down1=ffn_w_down[1].astype(BF16),
        w_qkv=w_in[:, :CONV_DIM].astype(BF16), w_zba=w_zba.astype(BF16),
        conv_w=gdn_conv_w[0].astype(F32), a_log=row(gdn_a_log[0]), dt_bias=row(gdn_dt_bias[0]),
        gdn_norm_w=row(gdn_norm_w[0]), w_out=gdn_w_out[0].astype(BF16),
    )
    n_bp = x_prompt.shape[0]
    z_state = jnp.zeros((n_bp, S5_GROUPS, S5_STATE), F32)
    z_conv = jnp.zeros((n_bp, CONV_W - 1, CONV_DIM), F32)
    z_s = jnp.zeros((n_bp, V_HEADS, HEAD, HEAD), F32)

    yp, p_re, p_im, p_conv, p_s = _trunk(x_prompt, z_state, z_state, z_conv, z_s, p)
    ys, s_re, s_im, s_conv, s_s = _trunk(x_sample, state_s5_re[0], state_s5_im[0],
                                         state_gdn_conv[0], state_gdn_s[0], p)
    return (yp, ys, p_re, p_im, p_conv, p_s, s_re, s_im, s_conv, s_s)
```

```python
import functools
import math

import jax
import jax.numpy as jnp
from jax import lax
from jax.experimental import pallas as pl
from jax.experimental.pallas import tpu as pltpu

F32 = jnp.float32
BF16 = jnp.bfloat16

RMS_EPS = 1e-6
L2_EPS = 1e-6
D_MODEL = 2048
S5_GROUP = 16
S5_STATE = 64
S5_GROUPS = D_MODEL // S5_GROUP
S5_CH = S5_GROUPS * S5_STATE
HEAD = 128
QK_HEADS = D_MODEL // HEAD
V_HEADS = 2 * QK_HEADS
KEY_DIM = QK_HEADS * HEAD
VAL_DIM = V_HEADS * HEAD
CONV_DIM = 2 * KEY_DIM + VAL_DIM
CONV_W = 4

SUBLANES = 8
LANES = 128
MXU_K = 256
S5_KB = D_MODEL // MXU_K
S5_CB = S5_CH // S5_KB
VMEM_LIMIT = 56 * 1024 * 1024


def _cparams(sem):
    return pltpu.CompilerParams(dimension_semantics=sem, vmem_limit_bytes=VMEM_LIMIT)


def _rms(x, w):
    return x * lax.rsqrt(jnp.mean(x * x, axis=-1, keepdims=True) + RMS_EPS) * w


def _s5_body(x_ref, nw_ref, d_ref, wb_ref, wc_ref, coef_ref, perm_ref, permt_ref, s0r_ref, s0i_ref,
             g_ref, sr_ref, si_ref, car_ref, h_ref, bu_ref, st_ref, *, tt):
    t = pl.program_id(1)
    seg = tt // SUBLANES

    @pl.when(t == 0)
    def _():
        car_ref[0] = jnp.broadcast_to(s0r_ref[0], (SUBLANES, S5_CH))
        car_ref[1] = jnp.broadcast_to(s0i_ref[0], (SUBLANES, S5_CH))

    h = _rms(x_ref[0], nw_ref[...])
    h_ref[...] = h
    hp = jnp.dot(perm_ref[...], h.astype(BF16), preferred_element_type=F32).astype(BF16)
    row0 = lax.broadcasted_iota(jnp.int32, (SUBLANES, S5_CB), 0) == 0

    def step(ar, ai, sr, si, xr, xi):
        return ar * sr - ai * si + xr, ar * si + ai * sr + xi

    def project_in(kb):
        ks = slice(kb * MXU_K, (kb + 1) * MXU_K)
        bu_ref[kb % 2] = jnp.dot(hp[:, ks], wb_ref[kb], preferred_element_type=F32)

    def project_out(kb):
        ks = slice(kb * MXU_K, (kb + 1) * MXU_K)
        yp = jnp.dot(st_ref[kb % 2].astype(BF16), wc_ref[kb], preferred_element_type=F32)
        y_hi = yp.astype(BF16)
        y_lo = (yp - y_hi.astype(F32)).astype(BF16)
        y2 = jnp.dot(permt_ref[...], jnp.concatenate([y_hi, y_lo], axis=1),
                     preferred_element_type=F32)
        y = y2[:, :MXU_K] + y2[:, MXU_K:] + d_ref[:, ks] * h_ref[:, ks]
        g_ref[0, :, ks] = jax.nn.gelu(y).astype(BF16)

    def scan(kb):
        cs = slice(kb * S5_CB, (kb + 1) * S5_CB)
        buf = kb % 2

        def bu_rows(j):
            rows = slice(j * SUBLANES, (j + 1) * SUBLANES)
            return bu_ref[buf, rows, 0:S5_CB], bu_ref[buf, rows, S5_CB:2 * S5_CB]

        ar = coef_ref[4, 0, :, cs]
        ai = coef_ref[4, 1, :, cs]
        er, ei = bu_rows(0)
        for j in range(1, seg):
            er, ei = step(ar, ai, er, ei, *bu_rows(j))
        for lvl, sh in enumerate((1, 2, 4)):
            cr_ = coef_ref[lvl, 0, :, cs]
            ci_ = coef_ref[lvl, 1, :, cs]
            pr = pltpu.roll(er, sh, 0)
            pi = pltpu.roll(ei, sh, 0)
            er, ei = er + cr_ * pr - ci_ * pi, ei + cr_ * pi + ci_ * pr
        c0r = car_ref[0, :, cs]
        c0i = car_ref[1, :, cs]
        cr_ = coef_ref[3, 0, :, cs]
        ci_ = coef_ref[3, 1, :, cs]
        fr = er + cr_ * c0r - ci_ * c0i
        fi = ei + cr_ * c0i + ci_ * c0r
        car_ref[0, :, cs] = jnp.broadcast_to(fr[SUBLANES - 1:SUBLANES], (SUBLANES, S5_CB))
        car_ref[1, :, cs] = jnp.broadcast_to(fi[SUBLANES - 1:SUBLANES], (SUBLANES, S5_CB))
        sr = jnp.where(row0, c0r, pltpu.roll(fr, 1, 0))
        si = jnp.where(row0, c0i, pltpu.roll(fi, 1, 0))
        for j in range(seg):
            sr, si = step(ar, ai, sr, si, *bu_rows(j))
            rows = slice(j * SUBLANES, (j + 1) * SUBLANES)
            st_ref[buf, rows, 0:S5_CB] = sr
            st_ref[buf, rows, S5_CB:2 * S5_CB] = si

    project_in(0)
    for kb in range(S5_KB):
        if kb + 1 < S5_KB:
            project_in(kb + 1)
        if kb >= 1:
            project_out(kb - 1)
        scan(kb)
    project_out(S5_KB - 1)

    @pl.when(t == pl.num_programs(1) - 1)
    def _():
        sr_ref[0] = car_ref[0, 0:1, :]
        si_ref[0] = car_ref[1, 0:1, :]


def _s5_call(x, nw, d, wb, wc, a_re, a_im, s0r, s0i, tt):
    n_b, t_len, _ = x.shape
    seg = tt // SUBLANES
    coef = _s5_coef(a_re, a_im, seg)
    p_idx = jnp.arange(tt)
    src = (p_idx % SUBLANES) * seg + p_idx // SUBLANES
    perm = (src[:, None] == p_idx[None, :]).astype(BF16)
    const = lambda *shape: pl.BlockSpec(shape, lambda b, t: (0,) * len(shape))
    state = pl.BlockSpec((1, 1, S5_CH), lambda b, t: (b, 0, 0))
    return pl.pallas_call(
        functools.partial(_s5_body, tt=tt),
        grid=(n_b, t_len // tt),
        in_specs=[pl.BlockSpec((1, tt, D_MODEL), lambda b, t: (b, t, 0)),
                  const(1, D_MODEL), const(1, D_MODEL),
                  const(S5_KB, MXU_K, 2 * S5_CB), const(S5_KB, 2 * S5_CB, MXU_K),
                  const(5, 2, SUBLANES, S5_CH), const(tt, tt), const(tt, tt), state, state],
        out_specs=[pl.BlockSpec((1, tt, D_MODEL), lambda b, t: (b, t, 0)), state, state],
        out_shape=[jax.ShapeDtypeStruct((n_b, t_len, D_MODEL), BF16),
                   jax.ShapeDtypeStruct((n_b, 1, S5_CH), F32),
                   jax.ShapeDtypeStruct((n_b, 1, S5_CH), F32)],
        scratch_shapes=[pltpu.VMEM((2, SUBLANES, S5_CH), F32),
                        pltpu.VMEM((tt, D_MODEL), F32),
                        pltpu.VMEM((2, tt, 2 * S5_CB), F32),
                        pltpu.VMEM((2, tt, 2 * S5_CB), F32)],
        compiler_params=_cparams(("arbitrary", "arbitrary")),
        name="s5_mixer",
    )(x, nw, d, wb, wc, coef, perm, perm.T, s0r, s0i)


def _s5_coef(a_re, a_im, seg):
    assert seg & (seg - 1) == 0, seg
    ar, ai = a_re, a_im
    for _ in range(int(math.log2(seg))):
        ar, ai = ar * ar - ai * ai, 2.0 * ar * ai
    pw_r, pw_i = [ar], [ai]
    for _ in range(SUBLANES - 1):
        pr, pi = pw_r[-1], pw_i[-1]
        pw_r.append(pr * ar - pi * ai)
        pw_i.append(pr * ai + pi * ar)
    row = jnp.arange(SUBLANES)[:, None]
    levels = []
    for sh in (1, 2, 4):
        m = (row >= sh).astype(F32)
        levels.append(jnp.stack([m * pw_r[sh - 1], m * pw_i[sh - 1]]))
    levels.append(jnp.stack([jnp.concatenate(pw_r, axis=0), jnp.concatenate(pw_i, axis=0)]))
    ones = jnp.ones((SUBLANES, 1), F32)
    levels.append(jnp.stack([ones * a_re, ones * a_im]))
    return jnp.stack(levels)


def _s5_params(lam_re, lam_im, log_dt, b_re, b_im, c_re, c_im):
    lr = jnp.minimum(lam_re.astype(F32), -1e-4)
    li = lam_im.astype(F32)
    dt = jnp.exp(log_dt.astype(F32))[:, None]
    mag = jnp.exp(lr * dt)
    ang = li * dt
    a_re = mag * jnp.cos(ang)
    a_im = mag * jnp.sin(ang)
    nr = a_re - 1.0
    den = lr * lr + li * li
    k_re = (nr * lr + a_im * li) / den
    k_im = (a_im * lr - nr * li) / den
    br = b_re.astype(F32)
    bi = b_im.astype(F32)
    bb_re = k_re[..., None] * br - k_im[..., None] * bi
    bb_im = k_re[..., None] * bi + k_im[..., None] * br

    gl = S5_GROUPS // S5_KB
    eye = jnp.eye(gl, dtype=F32)

    def blockdiag_in(bb):
        bbk = bb.reshape(S5_KB, gl, S5_STATE, S5_GROUP)
        w = jnp.einsum('kgph,gf->kghfp', bbk, eye)
        return w.reshape(S5_KB, gl * S5_GROUP, gl * S5_STATE)

    def blockdiag_out(c):
        ck = c.reshape(S5_KB, gl, S5_GROUP, S5_STATE)
        w = jnp.einsum('kghp,gf->kgpfh', ck, eye)
        return w.reshape(S5_KB, gl * S5_STATE, gl * S5_GROUP)

    wb = jnp.concatenate([blockdiag_in(bb_re), blockdiag_in(bb_im)], axis=2).astype(BF16)
    wc = jnp.concatenate([blockdiag_out(c_re.astype(F32)),
                          -blockdiag_out(c_im.astype(F32))], axis=1).astype(BF16)

    return wb, wc, a_re.reshape(1, S5_CH), a_im.reshape(1, S5_CH)


def _glu_body(g_ref, wa_ref, wg_ref, x_ref, o_ref):
    g = g_ref[...]
    za = jnp.dot(g, wa_ref[...], preferred_element_type=F32)
    zg = jnp.dot(g, wg_ref[...], preferred_element_type=F32)
    o_ref[...] = x_ref[...] + za * jax.nn.sigmoid(zg)


def _glu_call(g, w_glu, x, tm, tn):
    m = g.shape[0]
    nj = D_MODEL // tn
    return pl.pallas_call(
        _glu_body,
        grid=(m // tm, nj),
        in_specs=[pl.BlockSpec((tm, D_MODEL), lambda i, j: (i, 0)),
                  pl.BlockSpec((D_MODEL, tn), lambda i, j: (0, j)),
                  pl.BlockSpec((D_MODEL, tn), lambda i, j: (0, j + nj)),
                  pl.BlockSpec((tm, tn), lambda i, j: (i, j))],
        out_specs=pl.BlockSpec((tm, tn), lambda i, j: (i, j)),
        out_shape=jax.ShapeDtypeStruct((m, D_MODEL), F32),
        compiler_params=_cparams(("parallel", "arbitrary")),
        name="glu_residual",
    )(g, w_glu, w_glu, x)


def _ffn_body(x_ref, nw_ref, wu_ref, wd_ref, fw_ref, o_ref, h_ref, acc_ref, *, final_norm):
    f = pl.program_id(1)

    @pl.when(f == 0)
    def _():
        h_ref[...] = _rms(x_ref[...], nw_ref[...]).astype(BF16)
        acc_ref[...] = jnp.zeros_like(acc_ref)

    a = jnp.maximum(jnp.dot(h_ref[...], wu_ref[...], preferred_element_type=F32), 0.0)
    acc_ref[...] += jnp.dot((a * a).astype(BF16), wd_ref[...], preferred_element_type=F32)

    @pl.when(f == pl.num_programs(1) - 1)
    def _():
        y = x_ref[...] + acc_ref[...]
        if final_norm:
            y = _rms(y, fw_ref[...])
        o_ref[...] = y


def _ffn_call(x, nw, w_up, w_down, fw, tm, tf, final_norm):
    m = x.shape[0]
    d_ff = w_up.shape[1]
    return pl.pallas_call(
        functools.partial(_ffn_body, final_norm=final_norm),
        grid=(m // tm, d_ff // tf),
        in_specs=[pl.BlockSpec((tm, D_MODEL), lambda i, f: (i, 0)),
                  pl.BlockSpec((1, D_MODEL), lambda i, f: (0, 0)),
                  pl.BlockSpec((D_MODEL, tf), lambda i, f: (0, f)),
                  pl.BlockSpec((tf, D_MODEL), lambda i, f: (f, 0)),
                  pl.BlockSpec((1, D_MODEL), lambda i, f: (0, 0))],
        out_specs=pl.BlockSpec((tm, D_MODEL), lambda i, f: (i, 0)),
        out_shape=jax.ShapeDtypeStruct((m, D_MODEL), F32),
        scratch_shapes=[pltpu.VMEM((tm, D_MODEL), BF16), pltpu.VMEM((tm, D_MODEL), F32)],
        compiler_params=_cparams(("parallel", "arbitrary")),
        name="ffn_residual",
    )(x, nw, w_up, w_down, fw)


def _matmul_res_body(a_ref, w_ref, x_ref, o_ref):
    o_ref[...] = x_ref[...] + jnp.dot(a_ref[...], w_ref[...], preferred_element_type=F32)


def _matmul_res_call(a, w, x, tm, tn):
    m, k = a.shape
    n = w.shape[1]
    return pl.pallas_call(
        _matmul_res_body,
        grid=(m // tm, n // tn),
        in_specs=[pl.BlockSpec((tm, k), lambda i, j: (i, 0)),
                  pl.BlockSpec((k, tn), lambda i, j: (0, j)),
                  pl.BlockSpec((tm, tn), lambda i, j: (i, j))],
        out_specs=pl.BlockSpec((tm, tn), lambda i, j: (i, j)),
        out_shape=jax.ShapeDtypeStruct((m, n), F32),
        compiler_params=_cparams(("parallel", "arbitrary")),
        name="matmul_residual",
    )(a, w, x)


INPROJ_TN = 1024


def _inproj_body(x_ref, nw_ref, w_ref, wba_ref, cprev_ref, cw_ref, alog_ref, dtb_ref,
                 qkv_ref, z_ref, g_ref, beta_ref, nc_ref, hs_ref, r_ref, hist_ref,
                 *, tm, tiles_per_batch):
    i = pl.program_id(0)
    j = pl.program_id(1)
    tn = INPROJ_TN
    n_sub = tn // MXU_K
    n_qk = 2 * KEY_DIM // tn
    n_conv = CONV_DIM // tn
    tail = CONV_W - 1
    pad = SUBLANES

    @pl.when(j == 0)
    def _():
        hs_ref[...] = _rms(x_ref[...], nw_ref[...]).astype(BF16)
        ba = jnp.dot(hs_ref[...], wba_ref[...], preferred_element_type=F32)
        zz = ba[:, V_HEADS:2 * V_HEADS] + dtb_ref[...]
        softplus = jnp.maximum(zz, 0.0) + jnp.log1p(jnp.exp(-jnp.abs(zz)))
        g_ref[...] = -jnp.exp(alog_ref[...]) * softplus
        beta_ref[...] = jax.nn.sigmoid(ba[:, 0:V_HEADS])

    @pl.when((i % tiles_per_batch == 0) & (j < n_conv))
    def _():
        hist_ref[j] = jnp.concatenate([jnp.zeros((pad - tail, tn), F32), cprev_ref[0]], axis=0)

    def project(s):
        r_ref[s, pad:pad + tm, :] = jnp.dot(hs_ref[...], w_ref[:, s * MXU_K:(s + 1) * MXU_K],
                                            preferred_element_type=F32)

    def conv_epilogue(s, l2):
        cols = slice(s * MXU_K, (s + 1) * MXU_K)
        r_ref[s, 0:pad, :] = hist_ref[j, :, cols]
        last = r_ref[s, tm:tm + pad, :]
        hist_ref[j, :, cols] = last
        nc_ref[0, :, cols] = last[pad - tail:pad]
        acc = r_ref[s, pad - tail:pad - tail + tm, :] * cw_ref[0:1, cols]
        for tap in range(1, CONV_W):
            acc = acc + r_ref[s, pad - tail + tap:pad - tail + tap + tm, :] * cw_ref[tap:tap + 1, cols]
        c = acc * jax.nn.sigmoid(acc)
        if l2:
            scale = jnp.where(j < KEY_DIM // tn, HEAD ** -0.5, 1.0)
            for hd in range(MXU_K // HEAD):
                ch = c[:, hd * HEAD:(hd + 1) * HEAD]
                ch = ch * (lax.rsqrt(jnp.sum(ch * ch, axis=-1, keepdims=True) + L2_EPS) * scale)
                qkv_ref[:, s * MXU_K + hd * HEAD:s * MXU_K + (hd + 1) * HEAD] = ch.astype(BF16)
        else:
            qkv_ref[:, cols] = c.astype(BF16)

    def conv_tile(l2):
        project(0)
        for s in range(1, n_sub):
            project(s)
            conv_epilogue(s - 1, l2)
        conv_epilogue(n_sub - 1, l2)

    @pl.when(j < n_qk)
    def _():
        conv_tile(True)

    @pl.when((j >= n_qk) & (j < n_conv))
    def _():
        conv_tile(False)

    @pl.when(j >= n_conv)
    def _():
        z_ref[...] = jnp.dot(hs_ref[...], w_ref[...], preferred_element_type=F32).astype(BF16)


def _inproj_call(x, nw, w_qkvz, w_ba, conv_prev, conv_w, a_log, dt_bias, t_len, tm):
    m = x.shape[0]
    tn = INPROJ_TN
    n_conv = CONV_DIM // tn
    n_z = VAL_DIM // tn
    tpb = t_len // tm
    cj = lambda j: jnp.minimum(j, n_conv - 1)
    return pl.pallas_call(
        functools.partial(_inproj_body, tm=tm, tiles_per_batch=tpb),
        grid=(m // tm, n_conv + n_z),
        in_specs=[pl.BlockSpec((tm, D_MODEL), lambda i, j: (i, 0)),
                  pl.BlockSpec((1, D_MODEL), lambda i, j: (0, 0)),
                  pl.BlockSpec((D_MODEL, tn), lambda i, j: (0, j)),
                  pl.BlockSpec((D_MODEL, LANES), lambda i, j: (0, 0)),
                  pl.BlockSpec((1, CONV_W - 1, tn), lambda i, j: (i // tpb, 0, cj(j))),
                  pl.BlockSpec((CONV_W, tn), lambda i, j: (0, cj(j))),
                  pl.BlockSpec((1, V_HEADS), lambda i, j: (0, 0)),
                  pl.BlockSpec((1, V_HEADS), lambda i, j: (0, 0))],
        out_specs=[pl.BlockSpec((tm, tn), lambda i, j: (i, cj(j))),
                   pl.BlockSpec((tm, tn), lambda i, j: (i, jnp.maximum(j - n_conv, 0))),
                   pl.BlockSpec((tm, V_HEADS), lambda i, j: (i, 0)),
                   pl.BlockSpec((tm, V_HEADS), lambda i, j: (i, 0)),
                   pl.BlockSpec((1, CONV_W - 1, tn), lambda i, j: (i, 0, cj(j)))],
        out_shape=[jax.ShapeDtypeStruct((m, CONV_DIM), BF16),
                   jax.ShapeDtypeStruct((m, VAL_DIM), BF16),
                   jax.ShapeDtypeStruct((m, V_HEADS), F32),
                   jax.ShapeDtypeStruct((m, V_HEADS), F32),
                   jax.ShapeDtypeStruct((m // tm, CONV_W - 1, CONV_DIM), F32)],
        scratch_shapes=[pltpu.VMEM((tm, D_MODEL), BF16),
                        pltpu.VMEM((tn // MXU_K, tm + SUBLANES, MXU_K), F32),
                        pltpu.VMEM((n_conv, SUBLANES, tn), F32)],
        compiler_params=_cparams(("arbitrary", "arbitrary")),
        name="gdn_inproj_conv",
    )(x, nw, w_qkvz, w_ba, conv_prev, conv_w, a_log, dt_bias)


INV_BASE = 16


def _mm16(a, b):
    return jnp.dot(a.astype(BF16), b.astype(BF16), preferred_element_type=F32)


def _tri_inverse_all(nmats, ri, ci, chunk):
    base = min(INV_BASE, chunk)
    sh = int(math.log2(base))
    same = (ri >> sh) == (ci >> sh)
    eye = (ri == ci).astype(F32)
    ms = [jnp.where(same, -n, 0.0) for n in nmats]
    ps = [eye + m for m in ms]
    for _ in range(sh - 1):
        ms = [_mm16(m, m) for m in ms]
        ps = [p + _mm16(p, m) for p, m in zip(ps, ms)]
    size = base
    while size < chunk:
        sh += 1
        size *= 2
        same_next = (ri >> sh) == (ci >> sh)
        sel = same_next & jnp.logical_not(same)
        ts = [_mm16(p, jnp.where(sel, n, 0.0)) for p, n in zip(ps, nmats)]
        ps = [p - _mm16(t, p) for p, t in zip(ps, ts)]
        same = same_next
    return ps


def _delta_body(q_ref, k_ref, v_ref, z_ref, gc_ref, gr_ref, bc_ref, s0_ref, nw_ref,
                o_ref, sf_ref, s_ref, *, chunk, cpb, hb):
    c_idx = pl.program_id(2)

    @pl.when(c_idx == 0)
    def _():
        s_ref[...] = s0_ref[0]

    ri = lax.broadcasted_iota(jnp.int32, (chunk, chunk), 0)
    ci = lax.broadcasted_iota(jnp.int32, (chunk, chunk), 1)
    causal = ri >= ci
    strict = ri > ci
    tril = causal.astype(F32)
    triu = (ri <= ci).astype(F32)
    hi = lax.Precision.HIGHEST
    nt = (((1,), (1,)), ((), ()))
    tn = (((0,), (0,)), ((), ()))
    hq = hb // 2

    kqk, pre = {}, {}
    for c in range(cpb):
        rows = slice(c * chunk, (c + 1) * chunk)
        for j in range(hq):
            k16 = k_ref[0, rows, j * HEAD:(j + 1) * HEAD]
            q16 = q_ref[0, rows, j * HEAD:(j + 1) * HEAD]
            kqk[c, j] = lax.dot_general(jnp.concatenate([k16, q16], axis=0), k16, nt,
                                        preferred_element_type=F32)
    nmats, keys = [], []
    for c in range(cpb):
        rows = slice(c * chunk, (c + 1) * chunk)
        gcol = jnp.dot(tril, gc_ref[0, 0, rows, :], precision=hi, preferred_element_type=F32)
        grow = jnp.dot(gr_ref[0, 0, c], triu, precision=hi, preferred_element_type=F32)
        bcol = bc_ref[0, 0, rows, :]
        for hh in range(hb):
            j = hh // 2
            k = k_ref[0, rows, j * HEAD:(j + 1) * HEAD].astype(F32)
            q = q_ref[0, rows, j * HEAD:(j + 1) * HEAD].astype(F32)
            v = v_ref[0, rows, hh * HEAD:(hh + 1) * HEAD].astype(F32)
            gc = gcol[:, hh:hh + 1]
            gr = grow[hh:hh + 1, :]
            beta = bcol[:, hh:hh + 1]
            decay = jnp.exp(jnp.where(causal, gc - gr, -jnp.inf))
            egc = jnp.exp(gc)
            g_last = gc[chunk - 1:chunk, :]
            kbeta = k * beta
            kk = kqk[c, j][:chunk]
            qk = kqk[c, j][chunk:]
            nmats.append(jnp.where(strict, (kk * beta) * decay, 0.0))
            keys.append((c, hh))
            pre[c, hh] = dict(
                rhs=jnp.concatenate([v * beta, kbeta * egc], axis=1).astype(BF16),
                qg=(q * egc).astype(BF16),
                intra=(qk * decay).astype(BF16),
                kd=(k * jnp.exp(g_last - gc)).astype(BF16),
                sdec=jnp.exp(g_last))
    tinv = _tri_inverse_all(nmats, ri, ci, chunk)
    for key, t in zip(keys, tinv):
        pre[key]['sol'] = jnp.dot(t.astype(BF16), pre[key]['rhs'], preferred_element_type=F32)

    for c in range(cpb):
        rows = slice(c * chunk, (c + 1) * chunk)
        ws = []
        for hh in range(hb):
            d = pre[c, hh]
            wq = jnp.concatenate([d['sol'][:, HEAD:].astype(BF16), d['qg']], axis=0)
            ws.append(jnp.dot(wq, s_ref[hh].astype(BF16), preferred_element_type=F32))
        for hh in range(hb):
            d = pre[c, hh]
            v16 = (d['sol'][:, :HEAD] - ws[hh][:chunk]).astype(BF16)
            o = ws[hh][chunk:] + jnp.dot(d['intra'], v16, preferred_element_type=F32)
            s_ref[hh] = s_ref[hh] * d['sdec'] + lax.dot_general(d['kd'], v16, tn,
                                                                preferred_element_type=F32)
            o = o * lax.rsqrt(jnp.mean(o * o, axis=-1, keepdims=True) + RMS_EPS) * nw_ref[...]
            zz = z_ref[0, rows, hh * HEAD:(hh + 1) * HEAD]
            o_ref[0, rows, hh * HEAD:(hh + 1) * HEAD] = (o * (zz * jax.nn.sigmoid(zz))).astype(BF16)

    @pl.when(c_idx == pl.num_programs(2) - 1)
    def _():
        sf_ref[0] = s_ref[...]


def _delta_call(qkv, z, g, beta, s0, norm_w, chunk, cpb, hb):
    n_b, t_len, _ = qkv.shape
    nhg = V_HEADS // hb
    tc = chunk * cpb
    n_chunks = t_len // chunk
    g_col = g.reshape(n_b, t_len, nhg, hb).transpose(0, 2, 1, 3)
    b_col = beta.reshape(n_b, t_len, nhg, hb).transpose(0, 2, 1, 3)
    g_row = g.reshape(n_b, n_chunks, chunk, nhg, hb).transpose(0, 3, 1, 4, 2)
    hq = hb // 2
    return pl.pallas_call(
        functools.partial(_delta_body, chunk=chunk, cpb=cpb, hb=hb),
        grid=(n_b, nhg, t_len // tc),
        in_specs=[pl.BlockSpec((1, tc, hq * HEAD), lambda b, h, c: (b, c, h)),
                  pl.BlockSpec((1, tc, hq * HEAD), lambda b, h, c: (b, c, QK_HEADS // hq + h)),
                  pl.BlockSpec((1, tc, hb * HEAD), lambda b, h, c: (b, c, 2 * KEY_DIM // (hb * HEAD) + h)),
                  pl.BlockSpec((1, tc, hb * HEAD), lambda b, h, c: (b, c, h)),
                  pl.BlockSpec((1, 1, tc, hb), lambda b, h, c: (b, h, c, 0)),
                  pl.BlockSpec((1, 1, cpb, hb, chunk), lambda b, h, c: (b, h, c, 0, 0)),
                  pl.BlockSpec((1, 1, tc, hb), lambda b, h, c: (b, h, c, 0)),
                  pl.BlockSpec((1, hb, HEAD, HEAD), lambda b, h, c: (b, h, 0, 0)),
                  pl.BlockSpec((1, HEAD), lambda b, h, c: (0, 0))],
        out_specs=[pl.BlockSpec((1, tc, hb * HEAD), lambda b, h, c: (b, c, h)),
                   pl.BlockSpec((1, hb, HEAD, HEAD), lambda b, h, c: (b, h, 0, 0))],
        out_shape=[jax.ShapeDtypeStruct((n_b, t_len, VAL_DIM), BF16),
                   jax.ShapeDtypeStruct((n_b, V_HEADS, HEAD, HEAD), F32)],
        scratch_shapes=[pltpu.VMEM((hb, HEAD, HEAD), F32)],
        compiler_params=_cparams(("arbitrary", "arbitrary", "arbitrary")),
        name="gated_delta_rule",
    )(qkv, qkv, qkv, z, g_col, g_row, b_col, s0, norm_w)


def _tile(m, pref):
    t = min(m, pref)
    assert m % t == 0, (m, t)
    return t


def _trunk(x, s5_re0, s5_im0, conv0, s0, p):
    n_b, t_len, _ = x.shape
    m = n_b * t_len
    tm = _tile(m, 512)

    tt = _tile(t_len, 256)
    g, sr, si = _s5_call(x, p['nm0'], p['s5_d'], p['wb'], p['wc'], p['a_re'], p['a_im'],
                         s5_re0.reshape(n_b, 1, S5_CH), s5_im0.reshape(n_b, 1, S5_CH), tt)
    x2 = x.reshape(m, D_MODEL)
    x2 = _glu_call(g.reshape(m, D_MODEL), p['w_glu'], x2, tm, 512)
    x2 = _ffn_call(x2, p['nf0'], p['w_up0'], p['w_down0'], p['nfin'], tm, 512, False)

    tm_in = _tile(t_len, 512)
    qkv, z, gg, beta, tails = _inproj_call(
        x2, p['nm1'], p['w_qkvz'], p['w_ba'], conv0, p['conv_w'], p['a_log'], p['dt_bias'],
        t_len, tm_in)
    tpb = t_len // tm_in
    new_conv = tails[tpb - 1::tpb]
    chunk = min(t_len, 64)
    cpb = _tile(t_len // chunk, 4)
    og, s_fin = _delta_call(qkv.reshape(n_b, t_len, CONV_DIM), z.reshape(n_b, t_len, VAL_DIM),
                            gg.reshape(n_b, t_len, V_HEADS), beta.reshape(n_b, t_len, V_HEADS),
                            s0, p['gdn_norm_w'], chunk, cpb, 8)
    x2 = _matmul_res_call(og.reshape(m, VAL_DIM), p['w_out'], x2, tm, 512)
    y = _ffn_call(x2, p['nf1'], p['w_up1'], p['w_down1'], p['nfin'], tm, 512, True)

    return (y.reshape(n_b, t_len, D_MODEL),
            sr.reshape(1, n_b, S5_GROUPS, S5_STATE), si.reshape(1, n_b, S5_GROUPS, S5_STATE),
            new_conv[None], s_fin[None])


def kernel(x_prompt, x_sample, state_s5_re, state_s5_im, state_gdn_conv, state_gdn_s,
           norm_mix, norm_ffn, norm_final,
           s5_lam_re, s5_lam_im, s5_log_dt, s5_b_re, s5_b_im, s5_c_re, s5_c_im,
           s5_d, s5_w_glu,
           gdn_w_in, gdn_conv_w, gdn_a_log, gdn_dt_bias, gdn_norm_w, gdn_w_out,
           ffn_w_up, ffn_w_down):
    wb, wc, a_re, a_im = _s5_params(s5_lam_re[0], s5_lam_im[0], s5_log_dt[0], s5_b_re[0], s5_b_im[0],
                              s5_c_re[0], s5_c_im[0])
    w_in = gdn_w_in[0]
    ba = w_in[:, CONV_DIM + VAL_DIM:]
    w_ba = jnp.concatenate([ba, jnp.zeros((D_MODEL, LANES - ba.shape[1]), w_in.dtype)], axis=1)
    row = lambda a: a.reshape(1, -1).astype(F32)
    p = dict(
        nm0=row(norm_mix[0]), nm1=row(norm_mix[1]), nf0=row(norm_ffn[0]), nf1=row(norm_ffn[1]),
        nfin=row(norm_final), s5_d=row(s5_d[0]), wb=wb, wc=wc, a_re=a_re, a_im=a_im,
        w_glu=s5_w_glu[0].astype(BF16),
        w_up0=ffn_w_up[0].astype(BF16), w_down0=ffn_w_down[0].astype(BF16),
        w_up1=ffn_w_up[1].astype(BF16), w_down1=ffn_w_down[1].astype(BF16),
        w_qkvz=w_in[:, :CONV_DIM + VAL_DIM].astype(BF16), w_ba=w_ba.astype(BF16),
        conv_w=gdn_conv_w[0].astype(F32), a_log=row(gdn_a_log[0]), dt_bias=row(gdn_dt_bias[0]),
        gdn_norm_w=row(gdn_norm_w[0]), w_out=gdn_w_out[0].astype(BF16),
    )
    n_bp = x_prompt.shape[0]
    z_state = jnp.zeros((n_bp, S5_GROUPS, S5_STATE), F32)
    z_conv = jnp.zeros((n_bp, CONV_W - 1, CONV_DIM), F32)
    z_s = jnp.zeros((n_bp, V_HEADS, HEAD, HEAD), F32)

    yp, p_re, p_im, p_conv, p_s = _trunk(x_prompt, z_state, z_state, z_conv, z_s, p)
    ys, s_re, s_im, s_conv, s_s = _trunk(x_sample, state_s5_re[0], state_s5_im[0],
                                         state_gdn_conv[0], state_gdn_s[0], p)
    return (yp, ys, p_re, p_im, p_conv, p_s, s_re, s_im, s_conv, s_s)
```

```python
import functools
import math

import jax
import jax.numpy as jnp
from jax import lax
from jax.experimental import pallas as pl
from jax.experimental.pallas import tpu as pltpu

F32 = jnp.float32
BF16 = jnp.bfloat16

RMS_EPS = 1e-6
L2_EPS = 1e-6
D_MODEL = 2048
S5_GROUP = 16
S5_STATE = 64
S5_GROUPS = D_MODEL // S5_GROUP
S5_CH = S5_GROUPS * S5_STATE
HEAD = 128
QK_HEADS = D_MODEL // HEAD
V_HEADS = 2 * QK_HEADS
KEY_DIM = QK_HEADS * HEAD
VAL_DIM = V_HEADS * HEAD
CONV_DIM = 2 * KEY_DIM + VAL_DIM
CONV_W = 4

SUBLANES = 8
LANES = 128
MXU_K = 256
S5_KB = D_MODEL // MXU_K
S5_CB = S5_CH // S5_KB
VMEM_LIMIT = 56 * 1024 * 1024


def _cparams(sem):
    return pltpu.CompilerParams(dimension_semantics=sem, vmem_limit_bytes=VMEM_LIMIT)


def _rms(x, w):
    return x * lax.rsqrt(jnp.mean(x * x, axis=-1, keepdims=True) + RMS_EPS) * w


def _s5_body(x_ref, nw_ref, d_ref, wb_ref, wc_ref, coef_ref, perm_ref, permt_ref, s0r_ref, s0i_ref,
             g_ref, sr_ref, si_ref, car_ref, h_ref, bu_ref, st_ref, *, tt):
    t = pl.program_id(1)
    seg = tt // SUBLANES

    @pl.when(t == 0)
    def _():
        car_ref[0] = jnp.broadcast_to(s0r_ref[0], (SUBLANES, S5_CH))
        car_ref[1] = jnp.broadcast_to(s0i_ref[0], (SUBLANES, S5_CH))

    h = _rms(x_ref[0], nw_ref[...])
    h_ref[...] = h
    hp = jnp.dot(perm_ref[...], h.astype(BF16), preferred_element_type=F32).astype(BF16)
    row0 = lax.broadcasted_iota(jnp.int32, (SUBLANES, S5_CB), 0) == 0

    def step(ar, ai, sr, si, xr, xi):
        return ar * sr - ai * si + xr, ar * si + ai * sr + xi

    def project_in(kb):
        ks = slice(kb * MXU_K, (kb + 1) * MXU_K)
        bu_ref[kb % 2] = jnp.dot(hp[:, ks], wb_ref[kb], preferred_element_type=F32)

    def project_out(kb):
        ks = slice(kb * MXU_K, (kb + 1) * MXU_K)
        yp = jnp.dot(st_ref[kb % 2].astype(BF16), wc_ref[kb], preferred_element_type=F32)
        y_hi = yp.astype(BF16)
        y_lo = (yp - y_hi.astype(F32)).astype(BF16)
        y2 = jnp.dot(permt_ref[...], jnp.concatenate([y_hi, y_lo], axis=1),
                     preferred_element_type=F32)
        y = y2[:, :MXU_K] + y2[:, MXU_K:] + d_ref[:, ks] * h_ref[:, ks]
        g_ref[0, :, ks] = jax.nn.gelu(y).astype(BF16)

    def scan(kb):
        cs = slice(kb * S5_CB, (kb + 1) * S5_CB)
        buf = kb % 2

        def bu_rows(j):
            rows = slice(j * SUBLANES, (j + 1) * SUBLANES)
            return bu_ref[buf, rows, 0:S5_CB], bu_ref[buf, rows, S5_CB:2 * S5_CB]

        ar = coef_ref[4, 0, :, cs]
        ai = coef_ref[4, 1, :, cs]
        er, ei = bu_rows(0)
        for j in range(1, seg):
            er, ei = step(ar, ai, er, ei, *bu_rows(j))
        for lvl, sh in enumerate((1, 2, 4)):
            cr_ = coef_ref[lvl, 0, :, cs]
            ci_ = coef_ref[lvl, 1, :, cs]
            pr = pltpu.roll(er, sh, 0)
            pi = pltpu.roll(ei, sh, 0)
            er, ei = er + cr_ * pr - ci_ * pi, ei + cr_ * pi + ci_ * pr
        c0r = car_ref[0, :, cs]
        c0i = car_ref[1, :, cs]
        cr_ = coef_ref[3, 0, :, cs]
        ci_ = coef_ref[3, 1, :, cs]
        fr = er + cr_ * c0r - ci_ * c0i
        fi = ei + cr_ * c0i + ci_ * c0r
        car_ref[0, :, cs] = jnp.broadcast_to(fr[SUBLANES - 1:SUBLANES], (SUBLANES, S5_CB))
        car_ref[1, :, cs] = jnp.broadcast_to(fi[SUBLANES - 1:SUBLANES], (SUBLANES, S5_CB))
        sr = jnp.where(row0, c0r, pltpu.roll(fr, 1, 0))
        si = jnp.where(row0, c0i, pltpu.roll(fi, 1, 0))
        for j in range(seg):
            sr, si = step(ar, ai, sr, si, *bu_rows(j))
            rows = slice(j * SUBLANES, (j + 1) * SUBLANES)
            st_ref[buf, rows, 0:S5_CB] = sr
            st_ref[buf, rows, S5_CB:2 * S5_CB] = si

    project_in(0)
    for kb in range(S5_KB):
        if kb + 1 < S5_KB:
            project_in(kb + 1)
        if kb >= 1:
            project_out(kb - 1)
        scan(kb)
    project_out(S5_KB - 1)

    @pl.when(t == pl.num_programs(1) - 1)
    def _():
        sr_ref[0] = car_ref[0, 0:1, :]
        si_ref[0] = car_ref[1, 0:1, :]


def _s5_call(x, nw, d, wb, wc, a_re, a_im, s0r, s0i, tt):
    n_b, t_len, _ = x.shape
    seg = tt // SUBLANES
    coef = _s5_coef(a_re, a_im, seg)
    p_idx = jnp.arange(tt)
    src = (p_idx % SUBLANES) * seg + p_idx // SUBLANES
    perm = (src[:, None] == p_idx[None, :]).astype(BF16)
    const = lambda *shape: pl.BlockSpec(shape, lambda b, t: (0,) * len(shape))
    state = pl.BlockSpec((1, 1, S5_CH), lambda b, t: (b, 0, 0))
    return pl.pallas_call(
        functools.partial(_s5_body, tt=tt),
        grid=(n_b, t_len // tt),
        in_specs=[pl.BlockSpec((1, tt, D_MODEL), lambda b, t: (b, t, 0)),
                  const(1, D_MODEL), const(1, D_MODEL),
                  const(S5_KB, MXU_K, 2 * S5_CB), const(S5_KB, 2 * S5_CB, MXU_K),
                  const(5, 2, SUBLANES, S5_CH), const(tt, tt), const(tt, tt), state, state],
        out_specs=[pl.BlockSpec((1, tt, D_MODEL), lambda b, t: (b, t, 0)), state, state],
        out_shape=[jax.ShapeDtypeStruct((n_b, t_len, D_MODEL), BF16),
                   jax.ShapeDtypeStruct((n_b, 1, S5_CH), F32),
                   jax.ShapeDtypeStruct((n_b, 1, S5_CH), F32)],
        scratch_shapes=[pltpu.VMEM((2, SUBLANES, S5_CH), F32),
                        pltpu.VMEM((tt, D_MODEL), F32),
                        pltpu.VMEM((2, tt, 2 * S5_CB), F32),
                        pltpu.VMEM((2, tt, 2 * S5_CB), F32)],
        compiler_params=_cparams(("arbitrary", "arbitrary")),
        name="s5_mixer",
    )(x, nw, d, wb, wc, coef, perm, perm.T, s0r, s0i)


def _s5_coef(a_re, a_im, seg):
    assert seg & (seg - 1) == 0, seg
    ar, ai = a_re, a_im
    for _ in range(int(math.log2(seg))):
        ar, ai = ar * ar - ai * ai, 2.0 * ar * ai
    pw_r, pw_i = [ar], [ai]
    for _ in range(SUBLANES - 1):
        pr, pi = pw_r[-1], pw_i[-1]
        pw_r.append(pr * ar - pi * ai)
        pw_i.append(pr * ai + pi * ar)
    row = jnp.arange(SUBLANES)[:, None]
    levels = []
    for sh in (1, 2, 4):
        m = (row >= sh).astype(F32)
        levels.append(jnp.stack([m * pw_r[sh - 1], m * pw_i[sh - 1]]))
    levels.append(jnp.stack([jnp.concatenate(pw_r, axis=0), jnp.concatenate(pw_i, axis=0)]))
    ones = jnp.ones((SUBLANES, 1), F32)
    levels.append(jnp.stack([ones * a_re, ones * a_im]))
    return jnp.stack(levels)


def _s5_params(lam_re, lam_im, log_dt, b_re, b_im, c_re, c_im):
    lr = jnp.minimum(lam_re.astype(F32), -1e-4)
    li = lam_im.astype(F32)
    dt = jnp.exp(log_dt.astype(F32))[:, None]
    mag = jnp.exp(lr * dt)
    ang = li * dt
    a_re = mag * jnp.cos(ang)
    a_im = mag * jnp.sin(ang)
    nr = a_re - 1.0
    den = lr * lr + li * li
    k_re = (nr * lr + a_im * li) / den
    k_im = (a_im * lr - nr * li) / den
    br = b_re.astype(F32)
    bi = b_im.astype(F32)
    bb_re = k_re[..., None] * br - k_im[..., None] * bi
    bb_im = k_re[..., None] * bi + k_im[..., None] * br

    gl = S5_GROUPS // S5_KB
    eye = jnp.eye(gl, dtype=F32)

    def blockdiag_in(bb):
        bbk = bb.reshape(S5_KB, gl, S5_STATE, S5_GROUP)
        w = jnp.einsum('kgph,gf->kghfp', bbk, eye)
        return w.reshape(S5_KB, gl * S5_GROUP, gl * S5_STATE)

    def blockdiag_out(c):
        ck = c.reshape(S5_KB, gl, S5_GROUP, S5_STATE)
        w = jnp.einsum('kghp,gf->kgpfh', ck, eye)
        return w.reshape(S5_KB, gl * S5_STATE, gl * S5_GROUP)

    wb = jnp.concatenate([blockdiag_in(bb_re), blockdiag_in(bb_im)], axis=2).astype(BF16)
    wc = jnp.concatenate([blockdiag_out(c_re.astype(F32)),
                          -blockdiag_out(c_im.astype(F32))], axis=1).astype(BF16)

    return wb, wc, a_re.reshape(1, S5_CH), a_im.reshape(1, S5_CH)


def _glu_body(g_ref, wa_ref, wg_ref, x_ref, o_ref):
    g = g_ref[...]
    za = jnp.dot(g, wa_ref[...], preferred_element_type=F32)
    zg = jnp.dot(g, wg_ref[...], preferred_element_type=F32)
    o_ref[...] = x_ref[...] + za * jax.nn.sigmoid(zg)


def _glu_call(g, w_glu, x, tm, tn):
    m = g.shape[0]
    nj = D_MODEL // tn
    return pl.pallas_call(
        _glu_body,
        grid=(m // tm, nj),
        in_specs=[pl.BlockSpec((tm, D_MODEL), lambda i, j: (i, 0)),
                  pl.BlockSpec((D_MODEL, tn), lambda i, j: (0, j)),
                  pl.BlockSpec((D_MODEL, tn), lambda i, j: (0, j + nj)),
                  pl.BlockSpec((tm, tn), lambda i, j: (i, j))],
        out_specs=pl.BlockSpec((tm, tn), lambda i, j: (i, j)),
        out_shape=jax.ShapeDtypeStruct((m, D_MODEL), F32),
        compiler_params=_cparams(("parallel", "arbitrary")),
        name="glu_residual",
    )(g, w_glu, w_glu, x)


def _ffn_body(x_ref, nw_ref, wu_ref, wd_ref, fw_ref, o_ref, h_ref, acc_ref, *, final_norm):
    f = pl.program_id(1)

    @pl.when(f == 0)
    def _():
        h_ref[...] = _rms(x_ref[...], nw_ref[...]).astype(BF16)
        acc_ref[...] = jnp.zeros_like(acc_ref)

    a = jnp.maximum(jnp.dot(h_ref[...], wu_ref[...], preferred_element_type=F32), 0.0)
    acc_ref[...] += jnp.dot((a * a).astype(BF16), wd_ref[...], preferred_element_type=F32)

    @pl.when(f == pl.num_programs(1) - 1)
    def _():
        y = x_ref[...] + acc_ref[...]
        if final_norm:
            y = _rms(y, fw_ref[...])
        o_ref[...] = y


def _ffn_call(x, nw, w_up, w_down, fw, tm, tf, final_norm):
    m = x.shape[0]
    d_ff = w_up.shape[1]
    return pl.pallas_call(
        functools.partial(_ffn_body, final_norm=final_norm),
        grid=(m // tm, d_ff // tf),
        in_specs=[pl.BlockSpec((tm, D_MODEL), lambda i, f: (i, 0)),
                  pl.BlockSpec((1, D_MODEL), lambda i, f: (0, 0)),
                  pl.BlockSpec((D_MODEL, tf), lambda i, f: (0, f)),
                  pl.BlockSpec((tf, D_MODEL), lambda i, f: (f, 0)),
                  pl.BlockSpec((1, D_MODEL), lambda i, f: (0, 0))],
        out_specs=pl.BlockSpec((tm, D_MODEL), lambda i, f: (i, 0)),
        out_shape=jax.ShapeDtypeStruct((m, D_MODEL), F32),
        scratch_shapes=[pltpu.VMEM((tm, D_MODEL), BF16), pltpu.VMEM((tm, D_MODEL), F32)],
        compiler_params=_cparams(("parallel", "arbitrary")),
        name="ffn_residual",
    )(x, nw, w_up, w_down, fw)


def _matmul_res_body(a_ref, w_ref, x_ref, o_ref):
    o_ref[...] = x_ref[...] + jnp.dot(a_ref[...], w_ref[...], preferred_element_type=F32)


def _matmul_res_call(a, w, x, tm, tn):
    m, k = a.shape
    n = w.shape[1]
    return pl.pallas_call(
        _matmul_res_body,
        grid=(m // tm, n // tn),
        in_specs=[pl.BlockSpec((tm, k), lambda i, j: (i, 0)),
                  pl.BlockSpec((k, tn), lambda i, j: (0, j)),
                  pl.BlockSpec((tm, tn), lambda i, j: (i, j))],
        out_specs=pl.BlockSpec((tm, tn), lambda i, j: (i, j)),
        out_shape=jax.ShapeDtypeStruct((m, n), F32),
        compiler_params=_cparams(("parallel", "arbitrary")),
        name="matmul_residual",
    )(a, w, x)


INPROJ_TN = 1024
N_CONV_TILES = CONV_DIM // INPROJ_TN
N_Z_TILES = VAL_DIM // INPROJ_TN
N_QK_TILES = 2 * KEY_DIM // INPROJ_TN
N_Q_TILES = KEY_DIM // INPROJ_TN


def _inproj_body(x_ref, nw_ref, w_ref, wba_ref, cprev_ref, cw_ref, alog_ref, dtb_ref,
                 qkv_ref, z_ref, g_ref, beta_ref, nc_ref, hs_ref, hist_ref, ra_ref, rb_ref,
                 *, tm, seq_rows, tiles_per_seq):
    n_seq = tm // seq_rows
    i = pl.program_id(0)
    j = pl.program_id(1)
    tn = INPROJ_TN
    n_sub = tn // MXU_K
    tail = CONV_W - 1
    pad = SUBLANES

    @pl.when(j == 0)
    def _():
        hs_ref[...] = _rms(x_ref[...], nw_ref[...]).astype(BF16)
        ba = jnp.dot(hs_ref[...], wba_ref[...], preferred_element_type=F32)
        zz = ba[:, V_HEADS:2 * V_HEADS] + dtb_ref[...]
        softplus = jnp.maximum(zz, 0.0) + jnp.log1p(jnp.exp(-jnp.abs(zz)))
        g_ref[...] = -jnp.exp(alog_ref[...]) * softplus
        beta_ref[...] = jax.nn.sigmoid(ba[:, 0:V_HEADS])

    @pl.when((i % tiles_per_seq == 0) & (j == 0))
    def _():
        hist_ref[...] = jnp.concatenate(
            [jnp.zeros((pad - tail, CONV_DIM), F32), cprev_ref[0]], axis=0)

    def project(r_ref, half, s):
        c0 = half * tn + s * MXU_K
        r_ref[s, pad:pad + tm, :] = jnp.dot(hs_ref[...], w_ref[:, c0:c0 + MXU_K],
                                            preferred_element_type=F32)

    def conv_rows(win, c0):
        acc = win[pad:] * cw_ref[tail:tail + 1, c0:c0 + MXU_K]
        for d in range(1, CONV_W):
            shifted = pltpu.roll(win, d, 0)[pad:]
            acc = acc + shifted * cw_ref[tail - d:tail - d + 1, c0:c0 + MXU_K]
        return acc

    def epilogue(r_ref, tile, out_half, s):
        c0 = tile * tn + s * MXU_K
        o0 = out_half * tn + s * MXU_K
        l2 = tile < N_QK_TILES
        scale = HEAD ** -0.5 if tile < N_Q_TILES else 1.0

        def finish(acc, rows):
            c = acc * jax.nn.sigmoid(acc)
            if l2:
                for hd in range(MXU_K // HEAD):
                    ch = c[:, hd * HEAD:(hd + 1) * HEAD]
                    ch = ch * (lax.rsqrt(jnp.sum(ch * ch, axis=-1, keepdims=True) + L2_EPS) * scale)
                    qkv_ref[rows, o0 + hd * HEAD:o0 + (hd + 1) * HEAD] = ch.astype(BF16)
            else:
                qkv_ref[rows, o0:o0 + MXU_K] = c.astype(BF16)

        r_ref[s, 0:pad, :] = hist_ref[:, c0:c0 + MXU_K]
        hist_ref[:, c0:c0 + MXU_K] = r_ref[s, tm:tm + pad, :]
        for sq in range(n_seq):
            end = pad + (sq + 1) * seq_rows
            nc_ref[sq, :, c0:c0 + MXU_K] = r_ref[s, end - tail:end, :]
        finish(conv_rows(r_ref[s], c0), slice(0, tm))
        for sq in range(1, n_seq):
            r0 = sq * seq_rows
            win = jnp.concatenate([jnp.zeros((pad - tail, MXU_K), F32), cprev_ref[sq, :, c0:c0 + MXU_K],
                                   r_ref[s, pad + r0:pad + r0 + pad, :]], axis=0)
            finish(conv_rows(win, c0), slice(r0, r0 + pad))

    def conv_step(step):
        ta = 2 * step
        for s in range(n_sub):
            project(ra_ref, 0, s)
            if step > 0:
                epilogue(rb_ref, ta - 1, 0, s)
            else:
                qkv_ref[:, s * MXU_K:(s + 1) * MXU_K] = jnp.zeros((tm, MXU_K), BF16)
        for s in range(n_sub):
            project(rb_ref, 1, s)
            epilogue(ra_ref, ta, 1, s)

    for step in range(N_CONV_TILES // 2):
        pl.when(j == step)(functools.partial(conv_step, step))

    @pl.when(j == N_CONV_TILES // 2)
    def _():
        for s in range(n_sub):
            z_ref[:, s * MXU_K:(s + 1) * MXU_K] = jnp.dot(
                hs_ref[...], w_ref[:, s * MXU_K:(s + 1) * MXU_K],
                preferred_element_type=F32).astype(BF16)
            epilogue(rb_ref, N_CONV_TILES - 1, 0, s)
            qkv_ref[:, tn + s * MXU_K:tn + (s + 1) * MXU_K] = jnp.zeros((tm, MXU_K), BF16)
        z_ref[:, tn:] = jnp.dot(hs_ref[...], w_ref[:, tn:], preferred_element_type=F32).astype(BF16)

    @pl.when(j > N_CONV_TILES // 2)
    def _():
        z_ref[...] = jnp.dot(hs_ref[...], w_ref[...], preferred_element_type=F32).astype(BF16)


def _inproj_call(x, nw, w_qkvz, w_ba, conv_prev, conv_w, a_log, dt_bias, t_len, tm):
    m = x.shape[0]
    tn2 = 2 * INPROJ_TN
    n_cs = N_CONV_TILES // 2
    n_zs = N_Z_TILES // 2
    seq_rows = min(t_len, tm)
    n_seq = tm // seq_rows
    tpb = t_len // seq_rows
    return pl.pallas_call(
        functools.partial(_inproj_body, tm=tm, seq_rows=seq_rows, tiles_per_seq=tpb),
        grid=(m // tm, n_cs + n_zs),
        in_specs=[pl.BlockSpec((tm, D_MODEL), lambda i, j: (i, 0)),
                  pl.BlockSpec((1, D_MODEL), lambda i, j: (0, 0)),
                  pl.BlockSpec((D_MODEL, tn2), lambda i, j: (0, j)),
                  pl.BlockSpec((D_MODEL, LANES), lambda i, j: (0, 0)),
                  pl.BlockSpec((n_seq, CONV_W - 1, CONV_DIM), lambda i, j: (i // tpb, 0, 0)),
                  pl.BlockSpec((CONV_W, CONV_DIM), lambda i, j: (0, 0)),
                  pl.BlockSpec((1, V_HEADS), lambda i, j: (0, 0)),
                  pl.BlockSpec((1, V_HEADS), lambda i, j: (0, 0))],
        out_specs=[pl.BlockSpec((tm, tn2), lambda i, j: (i, jnp.minimum(j, n_cs))),
                   pl.BlockSpec((tm, tn2), lambda i, j: (i, jnp.maximum(j - n_cs, 0))),
                   pl.BlockSpec((tm, V_HEADS), lambda i, j: (i, 0)),
                   pl.BlockSpec((tm, V_HEADS), lambda i, j: (i, 0)),
                   pl.BlockSpec((n_seq, CONV_W - 1, CONV_DIM), lambda i, j: (i, 0, 0))],
        out_shape=[jax.ShapeDtypeStruct((m, (n_cs + 1) * tn2), BF16),
                   jax.ShapeDtypeStruct((m, VAL_DIM), BF16),
                   jax.ShapeDtypeStruct((m, V_HEADS), F32),
                   jax.ShapeDtypeStruct((m, V_HEADS), F32),
                   jax.ShapeDtypeStruct((m // seq_rows, CONV_W - 1, CONV_DIM), F32)],
        scratch_shapes=[pltpu.VMEM((tm, D_MODEL), BF16),
                        pltpu.VMEM((SUBLANES, CONV_DIM), F32),
                        pltpu.VMEM((INPROJ_TN // MXU_K, tm + SUBLANES, MXU_K), F32),
                        pltpu.VMEM((INPROJ_TN // MXU_K, tm + SUBLANES, MXU_K), F32)],
        compiler_params=_cparams(("arbitrary", "arbitrary")),
        name="gdn_inproj_conv",
    )(x, nw, w_qkvz, w_ba, conv_prev, conv_w, a_log, dt_bias)


INV_BASE = 16


def _mm16(a, b):
    return jnp.dot(a.astype(BF16), b.astype(BF16), preferred_element_type=F32)


def _tri_inverse_all(nmats, ri, ci, chunk):
    base = min(INV_BASE, chunk)
    sh = int(math.log2(base))
    same = (ri >> sh) == (ci >> sh)
    eye = (ri == ci).astype(F32)
    ms = [jnp.where(same, -n, 0.0) for n in nmats]
    ps = [eye + m for m in ms]
    for _ in range(sh - 1):
        ms = [_mm16(m, m) for m in ms]
        ps = [p + _mm16(p, m) for p, m in zip(ps, ms)]
    size = base
    while size < chunk:
        sh += 1
        size *= 2
        same_next = (ri >> sh) == (ci >> sh)
        sel = same_next & jnp.logical_not(same)
        ts = [_mm16(p, jnp.where(sel, n, 0.0)) for p, n in zip(ps, nmats)]
        ps = [p - _mm16(t, p) for p, t in zip(ps, ts)]
        same = same_next
    return ps


def _delta_body(q_ref, k_ref, v_ref, z_ref, gc_ref, gr_ref, bc_ref, s0_ref, nw_ref,
                o_ref, sf_ref, s_ref, *, chunk, cpb, hb):
    c_idx = pl.program_id(2)

    @pl.when(c_idx == 0)
    def _():
        s_ref[...] = s0_ref[0]

    ri = lax.broadcasted_iota(jnp.int32, (chunk, chunk), 0)
    ci = lax.broadcasted_iota(jnp.int32, (chunk, chunk), 1)
    causal = ri >= ci
    strict = ri > ci
    tril = causal.astype(F32)
    triu = (ri <= ci).astype(F32)
    hi = lax.Precision.HIGHEST
    nt = (((1,), (1,)), ((), ()))
    tn = (((0,), (0,)), ((), ()))
    hq = hb // 2

    kqk, pre = {}, {}
    for c in range(cpb):
        rows = slice(c * chunk, (c + 1) * chunk)
        for j in range(hq):
            k16 = k_ref[0, rows, j * HEAD:(j + 1) * HEAD]
            q16 = q_ref[0, rows, j * HEAD:(j + 1) * HEAD]
            kqk[c, j] = lax.dot_general(jnp.concatenate([k16, q16], axis=0), k16, nt,
                                        preferred_element_type=F32)
    nmats, keys = [], []
    for c in range(cpb):
        rows = slice(c * chunk, (c + 1) * chunk)
        gcol = jnp.dot(tril, gc_ref[0, 0, rows, :], precision=hi, preferred_element_type=F32)
        grow = jnp.dot(gr_ref[0, 0, c], triu, precision=hi, preferred_element_type=F32)
        bcol = bc_ref[0, 0, rows, :]
        for hh in range(hb):
            j = hh // 2
            k = k_ref[0, rows, j * HEAD:(j + 1) * HEAD].astype(F32)
            q = q_ref[0, rows, j * HEAD:(j + 1) * HEAD].astype(F32)
            v = v_ref[0, rows, hh * HEAD:(hh + 1) * HEAD].astype(F32)
            gc = gcol[:, hh:hh + 1]
            gr = grow[hh:hh + 1, :]
            beta = bcol[:, hh:hh + 1]
            decay = jnp.exp(jnp.where(causal, gc - gr, -jnp.inf))
            egc = jnp.exp(gc)
            g_last = gc[chunk - 1:chunk, :]
            kbeta = k * beta
            kk = kqk[c, j][:chunk]
            qk = kqk[c, j][chunk:]
            nmats.append(jnp.where(strict, (kk * beta) * decay, 0.0))
            keys.append((c, hh))
            pre[c, hh] = dict(
                rhs=jnp.concatenate([v * beta, kbeta * egc], axis=1).astype(BF16),
                qg=(q * egc).astype(BF16),
                intra=(qk * decay).astype(BF16),
                kd=(k * jnp.exp(g_last - gc)).astype(BF16),
                sdec=jnp.exp(g_last))
    tinv = _tri_inverse_all(nmats, ri, ci, chunk)
    for key, t in zip(keys, tinv):
        pre[key]['sol'] = jnp.dot(t.astype(BF16), pre[key]['rhs'], preferred_element_type=F32)

    for c in range(cpb):
        rows = slice(c * chunk, (c + 1) * chunk)
        ws = []
        for hh in range(hb):
            d = pre[c, hh]
            wq = jnp.concatenate([d['sol'][:, HEAD:].astype(BF16), d['qg']], axis=0)
            ws.append(jnp.dot(wq, s_ref[hh].astype(BF16), preferred_element_type=F32))
        for hh in range(hb):
            d = pre[c, hh]
            v16 = (d['sol'][:, :HEAD] - ws[hh][:chunk]).astype(BF16)
            o = ws[hh][chunk:] + jnp.dot(d['intra'], v16, preferred_element_type=F32)
            s_ref[hh] = s_ref[hh] * d['sdec'] + lax.dot_general(d['kd'], v16, tn,
                                                                preferred_element_type=F32)
            o = o * lax.rsqrt(jnp.mean(o * o, axis=-1, keepdims=True) + RMS_EPS) * nw_ref[...]
            zz = z_ref[0, rows, hh * HEAD:(hh + 1) * HEAD]
            o_ref[0, rows, hh * HEAD:(hh + 1) * HEAD] = (o * (zz * jax.nn.sigmoid(zz))).astype(BF16)

    @pl.when(c_idx == pl.num_programs(2) - 1)
    def _():
        sf_ref[0] = s_ref[...]


def _delta_call(qkv, z, g, beta, s0, norm_w, chunk, cpb, hb):
    n_b, t_len, _ = qkv.shape
    nhg = V_HEADS // hb
    tc = chunk * cpb
    n_chunks = t_len // chunk
    g_col = g.reshape(n_b, t_len, nhg, hb).transpose(0, 2, 1, 3)
    b_col = beta.reshape(n_b, t_len, nhg, hb).transpose(0, 2, 1, 3)
    g_row = g.reshape(n_b, n_chunks, chunk, nhg, hb).transpose(0, 3, 1, 4, 2)
    hq = hb // 2
    q_off = INPROJ_TN // (hq * HEAD)
    k_off = (INPROJ_TN + KEY_DIM) // (hq * HEAD)
    v_off = (INPROJ_TN + 2 * KEY_DIM) // (hb * HEAD)
    return pl.pallas_call(
        functools.partial(_delta_body, chunk=chunk, cpb=cpb, hb=hb),
        grid=(n_b, nhg, t_len // tc),
        in_specs=[pl.BlockSpec((1, tc, hq * HEAD), lambda b, h, c: (b, c, q_off + h)),
                  pl.BlockSpec((1, tc, hq * HEAD), lambda b, h, c: (b, c, k_off + h)),
                  pl.BlockSpec((1, tc, hb * HEAD), lambda b, h, c: (b, c, v_off + h)),
                  pl.BlockSpec((1, tc, hb * HEAD), lambda b, h, c: (b, c, h)),
                  pl.BlockSpec((1, 1, tc, hb), lambda b, h, c: (b, h, c, 0)),
                  pl.BlockSpec((1, 1, cpb, hb, chunk), lambda b, h, c: (b, h, c, 0, 0)),
                  pl.BlockSpec((1, 1, tc, hb), lambda b, h, c: (b, h, c, 0)),
                  pl.BlockSpec((1, hb, HEAD, HEAD), lambda b, h, c: (b, h, 0, 0)),
                  pl.BlockSpec((1, HEAD), lambda b, h, c: (0, 0))],
        out_specs=[pl.BlockSpec((1, tc, hb * HEAD), lambda b, h, c: (b, c, h)),
                   pl.BlockSpec((1, hb, HEAD, HEAD), lambda b, h, c: (b, h, 0, 0))],
        out_shape=[jax.ShapeDtypeStruct((n_b, t_len, VAL_DIM), BF16),
                   jax.ShapeDtypeStruct((n_b, V_HEADS, HEAD, HEAD), F32)],
        scratch_shapes=[pltpu.VMEM((hb, HEAD, HEAD), F32)],
        compiler_params=_cparams(("arbitrary", "arbitrary", "arbitrary")),
        name="gated_delta_rule",
    )(qkv, qkv, qkv, z, g_col, g_row, b_col, s0, norm_w)


def _tile(m, pref):
    t = min(m, pref)
    assert m % t == 0, (m, t)
    return t


def _trunk(x, s5_re0, s5_im0, conv0, s0, p):
    n_b, t_len, _ = x.shape
    m = n_b * t_len
    tm = _tile(m, 512)

    tt = _tile(t_len, 256)
    g, sr, si = _s5_call(x, p['nm0'], p['s5_d'], p['wb'], p['wc'], p['a_re'], p['a_im'],
                         s5_re0.reshape(n_b, 1, S5_CH), s5_im0.reshape(n_b, 1, S5_CH), tt)
    x2 = x.reshape(m, D_MODEL)
    x2 = _glu_call(g.reshape(m, D_MODEL), p['w_glu'], x2, tm, 512)
    x2 = _ffn_call(x2, p['nf0'], p['w_up0'], p['w_down0'], p['nfin'], tm, 512, False)

    qkv, z, gg, beta, tails = _inproj_call(
        x2, p['nm1'], p['w_qkvz'], p['w_ba'], conv0, p['conv_w'], p['a_log'], p['dt_bias'],
        t_len, tm)
    tpb = t_len // min(t_len, tm)
    new_conv = tails[tpb - 1::tpb]
    chunk = min(t_len, 64)
    cpb = _tile(t_len // chunk, 4)
    og, s_fin = _delta_call(qkv.reshape(n_b, t_len, -1), z.reshape(n_b, t_len, VAL_DIM),
                            gg.reshape(n_b, t_len, V_HEADS), beta.reshape(n_b, t_len, V_HEADS),
                            s0, p['gdn_norm_w'], chunk, cpb, 8)
    x2 = _matmul_res_call(og.reshape(m, VAL_DIM), p['w_out'], x2, tm, 512)
    y = _ffn_call(x2, p['nf1'], p['w_up1'], p['w_down1'], p['nfin'], tm, 512, True)

    return (y.reshape(n_b, t_len, D_MODEL),
            sr.reshape(1, n_b, S5_GROUPS, S5_STATE), si.reshape(1, n_b, S5_GROUPS, S5_STATE),
            new_conv[None], s_fin[None])


def kernel(x_prompt, x_sample, state_s5_re, state_s5_im, state_gdn_conv, state_gdn_s,
           norm_mix, norm_ffn, norm_final,
           s5_lam_re, s5_lam_im, s5_log_dt, s5_b_re, s5_b_im, s5_c_re, s5_c_im,
           s5_d, s5_w_glu,
           gdn_w_in, gdn_conv_w, gdn_a_log, gdn_dt_bias, gdn_norm_w, gdn_w_out,
           ffn_w_up, ffn_w_down):
    wb, wc, a_re, a_im = _s5_params(s5_lam_re[0], s5_lam_im[0], s5_log_dt[0], s5_b_re[0], s5_b_im[0],
                              s5_c_re[0], s5_c_im[0])
    w_in = gdn_w_in[0]
    ba = w_in[:, CONV_DIM + VAL_DIM:]
    w_ba = jnp.concatenate([ba, jnp.zeros((D_MODEL, LANES - ba.shape[1]), w_in.dtype)], axis=1)
    row = lambda a: a.reshape(1, -1).astype(F32)
    p = dict(
        nm0=row(norm_mix[0]), nm1=row(norm_mix[1]), nf0=row(norm_ffn[0]), nf1=row(norm_ffn[1]),
        nfin=row(norm_final), s5_d=row(s5_d[0]), wb=wb, wc=wc, a_re=a_re, a_im=a_im,
        w_glu=s5_w_glu[0].astype(BF16),
        w_up0=ffn_w_up[0].astype(BF16), w_down0=ffn_w_down[0].astype(BF16),
        w_up1=ffn_w_up[1].astype(BF16), w_down1=ffn_w_down[1].astype(BF16),
        w_qkvz=w_in[:, :CONV_DIM + VAL_DIM].astype(BF16), w_ba=w_ba.astype(BF16),
        conv_w=gdn_conv_w[0].astype(F32), a_log=row(gdn_a_log[0]), dt_bias=row(gdn_dt_bias[0]),
        gdn_norm_w=row(gdn_norm_w[0]), w_out=gdn_w_out[0].astype(BF16),
    )
    n_bp = x_prompt.shape[0]
    z_state = jnp.zeros((n_bp, S5_GROUPS, S5_STATE), F32)
    z_conv = jnp.zeros((n_bp, CONV_W - 1, CONV_DIM), F32)
    z_s = jnp.zeros((n_bp, V_HEADS, HEAD, HEAD), F32)

    yp, p_re, p_im, p_conv, p_s = _trunk(x_prompt, z_state, z_state, z_conv, z_s, p)
    ys, s_re, s_im, s_conv, s_s = _trunk(x_sample, state_s5_re[0], state_s5_im[0],
                                         state_gdn_conv[0], state_gdn_s[0], p)
    return (yp, ys, p_re, p_im, p_conv, p_s, s_re, s_im, s_conv, s_s)
```

```python
import functools
import math

import jax
import jax.numpy as jnp
from jax import lax
from jax.experimental import pallas as pl
from jax.experimental.pallas import tpu as pltpu

F32 = jnp.float32
BF16 = jnp.bfloat16

RMS_EPS = 1e-6
L2_EPS = 1e-6
D_MODEL = 2048
S5_GROUP = 16
S5_STATE = 64
S5_GROUPS = D_MODEL // S5_GROUP
S5_CH = S5_GROUPS * S5_STATE
HEAD = 128
QK_HEADS = D_MODEL // HEAD
V_HEADS = 2 * QK_HEADS
KEY_DIM = QK_HEADS * HEAD
VAL_DIM = V_HEADS * HEAD
CONV_DIM = 2 * KEY_DIM + VAL_DIM
CONV_W = 4

SUBLANES = 8
LANES = 128
MXU_K = 256
S5_KB = D_MODEL // MXU_K
S5_CB = S5_CH // S5_KB
VMEM_LIMIT = 56 * 1024 * 1024


def _cparams(sem):
    return pltpu.CompilerParams(dimension_semantics=sem, vmem_limit_bytes=VMEM_LIMIT)


def _rms(x, w):
    return x * lax.rsqrt(jnp.mean(x * x, axis=-1, keepdims=True) + RMS_EPS) * w


def _s5_body(x_ref, nw_ref, d_ref, wb_ref, wc_ref, coef_ref, perm_ref, permt_ref, s0r_ref, s0i_ref,
             g_ref, sr_ref, si_ref, car_ref, h_ref, bu_ref, st_ref, *, tt):
    t = pl.program_id(1)
    seg = tt // SUBLANES

    @pl.when(t == 0)
    def _():
        car_ref[0] = jnp.broadcast_to(s0r_ref[0], (SUBLANES, S5_CH))
        car_ref[1] = jnp.broadcast_to(s0i_ref[0], (SUBLANES, S5_CH))

    h = _rms(x_ref[0], nw_ref[...])
    h_ref[...] = h
    hp = jnp.dot(perm_ref[...], h.astype(BF16), preferred_element_type=F32).astype(BF16)
    row0 = lax.broadcasted_iota(jnp.int32, (SUBLANES, S5_CB), 0) == 0

    def step(ar, ai, sr, si, xr, xi):
        return ar * sr - ai * si + xr, ar * si + ai * sr + xi

    def project_in(kb):
        ks = slice(kb * MXU_K, (kb + 1) * MXU_K)
        bu_ref[kb % 2] = jnp.dot(hp[:, ks], wb_ref[kb], preferred_element_type=F32)

    def project_out(kb):
        ks = slice(kb * MXU_K, (kb + 1) * MXU_K)
        yp = jnp.dot(st_ref[kb % 2].astype(BF16), wc_ref[kb], preferred_element_type=F32)
        y_hi = yp.astype(BF16)
        y_lo = (yp - y_hi.astype(F32)).astype(BF16)
        y2 = jnp.dot(permt_ref[...], jnp.concatenate([y_hi, y_lo], axis=1),
                     preferred_element_type=F32)
        y = y2[:, :MXU_K] + y2[:, MXU_K:] + d_ref[:, ks] * h_ref[:, ks]
        g_ref[0, :, ks] = jax.nn.gelu(y).astype(BF16)

    def scan(kb):
        cs = slice(kb * S5_CB, (kb + 1) * S5_CB)
        buf = kb % 2

        def bu_rows(j):
            rows = slice(j * SUBLANES, (j + 1) * SUBLANES)
            return bu_ref[buf, rows, 0:S5_CB], bu_ref[buf, rows, S5_CB:2 * S5_CB]

        ar = coef_ref[4, 0, :, cs]
        ai = coef_ref[4, 1, :, cs]
        er, ei = bu_rows(0)
        for j in range(1, seg):
            er, ei = step(ar, ai, er, ei, *bu_rows(j))
        for lvl, sh in enumerate((1, 2, 4)):
            cr_ = coef_ref[lvl, 0, :, cs]
            ci_ = coef_ref[lvl, 1, :, cs]
            pr = pltpu.roll(er, sh, 0)
            pi = pltpu.roll(ei, sh, 0)
            er, ei = er + cr_ * pr - ci_ * pi, ei + cr_ * pi + ci_ * pr
        c0r = car_ref[0, :, cs]
        c0i = car_ref[1, :, cs]
        cr_ = coef_ref[3, 0, :, cs]
        ci_ = coef_ref[3, 1, :, cs]
        fr = er + cr_ * c0r - ci_ * c0i
        fi = ei + cr_ * c0i + ci_ * c0r
        car_ref[0, :, cs] = jnp.broadcast_to(fr[SUBLANES - 1:SUBLANES], (SUBLANES, S5_CB))
        car_ref[1, :, cs] = jnp.broadcast_to(fi[SUBLANES - 1:SUBLANES], (SUBLANES, S5_CB))
        sr = jnp.where(row0, c0r, pltpu.roll(fr, 1, 0))
        si = jnp.where(row0, c0i, pltpu.roll(fi, 1, 0))
        for j in range(seg):
            sr, si = step(ar, ai, sr, si, *bu_rows(j))
            rows = slice(j * SUBLANES, (j + 1) * SUBLANES)
            st_ref[buf, rows, 0:S5_CB] = sr
            st_ref[buf, rows, S5_CB:2 * S5_CB] = si

    project_in(0)
    for kb in range(S5_KB):
        if kb + 1 < S5_KB:
            project_in(kb + 1)
        if kb >= 1:
            project_out(kb - 1)
        scan(kb)
    project_out(S5_KB - 1)

    @pl.when(t == pl.num_programs(1) - 1)
    def _():
        sr_ref[0] = car_ref[0, 0:1, :]
        si_ref[0] = car_ref[1, 0:1, :]


def _s5_call(x, nw, d, wb, wc, a_re, a_im, s0r, s0i, tt):
    n_b, t_len, _ = x.shape
    seg = tt // SUBLANES
    coef = _s5_coef(a_re, a_im, seg)
    p_idx = jnp.arange(tt)
    src = (p_idx % SUBLANES) * seg + p_idx // SUBLANES
    perm = (src[:, None] == p_idx[None, :]).astype(BF16)
    const = lambda *shape: pl.BlockSpec(shape, lambda b, t: (0,) * len(shape))
    state = pl.BlockSpec((1, 1, S5_CH), lambda b, t: (b, 0, 0))
    return pl.pallas_call(
        functools.partial(_s5_body, tt=tt),
        grid=(n_b, t_len // tt),
        in_specs=[pl.BlockSpec((1, tt, D_MODEL), lambda b, t: (b, t, 0)),
                  const(1, D_MODEL), const(1, D_MODEL),
                  const(S5_KB, MXU_K, 2 * S5_CB), const(S5_KB, 2 * S5_CB, MXU_K),
                  const(5, 2, SUBLANES, S5_CH), const(tt, tt), const(tt, tt), state, state],
        out_specs=[pl.BlockSpec((1, tt, D_MODEL), lambda b, t: (b, t, 0)), state, state],
        out_shape=[jax.ShapeDtypeStruct((n_b, t_len, D_MODEL), BF16),
                   jax.ShapeDtypeStruct((n_b, 1, S5_CH), F32),
                   jax.ShapeDtypeStruct((n_b, 1, S5_CH), F32)],
        scratch_shapes=[pltpu.VMEM((2, SUBLANES, S5_CH), F32),
                        pltpu.VMEM((tt, D_MODEL), F32),
                        pltpu.VMEM((2, tt, 2 * S5_CB), F32),
                        pltpu.VMEM((2, tt, 2 * S5_CB), F32)],
        compiler_params=_cparams(("arbitrary", "arbitrary")),
        name="s5_mixer",
    )(x, nw, d, wb, wc, coef, perm, perm.T, s0r, s0i)


def _s5_coef(a_re, a_im, seg):
    assert seg & (seg - 1) == 0, seg
    ar, ai = a_re, a_im
    for _ in range(int(math.log2(seg))):
        ar, ai = ar * ar - ai * ai, 2.0 * ar * ai
    pw_r, pw_i = [ar], [ai]
    for _ in range(SUBLANES - 1):
        pr, pi = pw_r[-1], pw_i[-1]
        pw_r.append(pr * ar - pi * ai)
        pw_i.append(pr * ai + pi * ar)
    row = jnp.arange(SUBLANES)[:, None]
    levels = []
    for sh in (1, 2, 4):
        m = (row >= sh).astype(F32)
        levels.append(jnp.stack([m * pw_r[sh - 1], m * pw_i[sh - 1]]))
    levels.append(jnp.stack([jnp.concatenate(pw_r, axis=0), jnp.concatenate(pw_i, axis=0)]))
    ones = jnp.ones((SUBLANES, 1), F32)
    levels.append(jnp.stack([ones * a_re, ones * a_im]))
    return jnp.stack(levels)


def _s5_params(lam_re, lam_im, log_dt, b_re, b_im, c_re, c_im):
    lr = jnp.minimum(lam_re.astype(F32), -1e-4)
    li = lam_im.astype(F32)
    dt = jnp.exp(log_dt.astype(F32))[:, None]
    mag = jnp.exp(lr * dt)
    ang = li * dt
    a_re = mag * jnp.cos(ang)
    a_im = mag * jnp.sin(ang)
    nr = a_re - 1.0
    den = lr * lr + li * li
    k_re = (nr * lr + a_im * li) / den
    k_im = (a_im * lr - nr * li) / den
    br = b_re.astype(F32)
    bi = b_im.astype(F32)
    bb_re = k_re[..., None] * br - k_im[..., None] * bi
    bb_im = k_re[..., None] * bi + k_im[..., None] * br

    gl = S5_GROUPS // S5_KB
    eye = jnp.eye(gl, dtype=F32)

    def blockdiag_in(bb):
        bbk = bb.reshape(S5_KB, gl, S5_STATE, S5_GROUP)
        w = jnp.einsum('kgph,gf->kghfp', bbk, eye)
        return w.reshape(S5_KB, gl * S5_GROUP, gl * S5_STATE)

    def blockdiag_out(c):
        ck = c.reshape(S5_KB, gl, S5_GROUP, S5_STATE)
        w = jnp.einsum('kghp,gf->kgpfh', ck, eye)
        return w.reshape(S5_KB, gl * S5_STATE, gl * S5_GROUP)

    wb = jnp.concatenate([blockdiag_in(bb_re), blockdiag_in(bb_im)], axis=2).astype(BF16)
    wc = jnp.concatenate([blockdiag_out(c_re.astype(F32)),
                          -blockdiag_out(c_im.astype(F32))], axis=1).astype(BF16)

    return wb, wc, a_re.reshape(1, S5_CH), a_im.reshape(1, S5_CH)


def _glu_body(g_ref, wa_ref, wg_ref, x_ref, o_ref):
    g = g_ref[...]
    za = jnp.dot(g, wa_ref[...], preferred_element_type=F32)
    zg = jnp.dot(g, wg_ref[...], preferred_element_type=F32)
    o_ref[...] = x_ref[...] + za * jax.nn.sigmoid(zg)


def _glu_call(g, w_glu, x, tm, tn):
    m = g.shape[0]
    nj = D_MODEL // tn
    return pl.pallas_call(
        _glu_body,
        grid=(m // tm, nj),
        in_specs=[pl.BlockSpec((tm, D_MODEL), lambda i, j: (i, 0)),
                  pl.BlockSpec((D_MODEL, tn), lambda i, j: (0, j)),
                  pl.BlockSpec((D_MODEL, tn), lambda i, j: (0, j + nj)),
                  pl.BlockSpec((tm, tn), lambda i, j: (i, j))],
        out_specs=pl.BlockSpec((tm, tn), lambda i, j: (i, j)),
        out_shape=jax.ShapeDtypeStruct((m, D_MODEL), F32),
        compiler_params=_cparams(("parallel", "arbitrary")),
        name="glu_residual",
    )(g, w_glu, w_glu, x)


def _ffn_body(x_ref, nw_ref, wu_ref, wd_ref, fw_ref, o_ref, h_ref, acc_ref, *, final_norm):
    f = pl.program_id(1)

    @pl.when(f == 0)
    def _():
        h_ref[...] = _rms(x_ref[...], nw_ref[...]).astype(BF16)
        acc_ref[...] = jnp.zeros_like(acc_ref)

    a = jnp.maximum(jnp.dot(h_ref[...], wu_ref[...], preferred_element_type=F32), 0.0)
    acc_ref[...] += jnp.dot((a * a).astype(BF16), wd_ref[...], preferred_element_type=F32)

    @pl.when(f == pl.num_programs(1) - 1)
    def _():
        y = x_ref[...] + acc_ref[...]
        if final_norm:
            y = _rms(y, fw_ref[...])
        o_ref[...] = y


def _ffn_call(x, nw, w_up, w_down, fw, tm, tf, final_norm):
    m = x.shape[0]
    d_ff = w_up.shape[1]
    return pl.pallas_call(
        functools.partial(_ffn_body, final_norm=final_norm),
        grid=(m // tm, d_ff // tf),
        in_specs=[pl.BlockSpec((tm, D_MODEL), lambda i, f: (i, 0)),
                  pl.BlockSpec((1, D_MODEL), lambda i, f: (0, 0)),
                  pl.BlockSpec((D_MODEL, tf), lambda i, f: (0, f)),
                  pl.BlockSpec((tf, D_MODEL), lambda i, f: (f, 0)),
                  pl.BlockSpec((1, D_MODEL), lambda i, f: (0, 0))],
        out_specs=pl.BlockSpec((tm, D_MODEL), lambda i, f: (i, 0)),
        out_shape=jax.ShapeDtypeStruct((m, D_MODEL), F32),
        scratch_shapes=[pltpu.VMEM((tm, D_MODEL), BF16), pltpu.VMEM((tm, D_MODEL), F32)],
        compiler_params=_cparams(("parallel", "arbitrary")),
        name="ffn_residual",
    )(x, nw, w_up, w_down, fw)


def _matmul_res_body(a_ref, w_ref, x_ref, o_ref):
    o_ref[...] = x_ref[...] + jnp.dot(a_ref[...], w_ref[...], preferred_element_type=F32)


def _matmul_res_call(a, w, x, tm, tn):
    m, k = a.shape
    n = w.shape[1]
    return pl.pallas_call(
        _matmul_res_body,
        grid=(m // tm, n // tn),
        in_specs=[pl.BlockSpec((tm, k), lambda i, j: (i, 0)),
                  pl.BlockSpec((k, tn), lambda i, j: (0, j)),
                  pl.BlockSpec((tm, tn), lambda i, j: (i, j))],
        out_specs=pl.BlockSpec((tm, tn), lambda i, j: (i, j)),
        out_shape=jax.ShapeDtypeStruct((m, n), F32),
        compiler_params=_cparams(("parallel", "arbitrary")),
        name="matmul_residual",
    )(a, w, x)


INPROJ_TN = 1024
N_CONV_TILES = CONV_DIM // INPROJ_TN
N_Z_TILES = VAL_DIM // INPROJ_TN
N_QK_TILES = 2 * KEY_DIM // INPROJ_TN
N_Q_TILES = KEY_DIM // INPROJ_TN


def _inproj_body(x_ref, nw_ref, w_ref, wba_ref, cprev_ref, cw_ref, alog_ref, dtb_ref,
                 qkv_ref, z_ref, g_ref, beta_ref, nc_ref, hs_ref, hist_ref, ra_ref, rb_ref,
                 *, tm, seq_rows, tiles_per_seq):
    n_seq = tm // seq_rows
    i = pl.program_id(0)
    j = pl.program_id(1)
    tn = INPROJ_TN
    n_sub = tn // MXU_K
    tail = CONV_W - 1
    pad = SUBLANES

    @pl.when(j == 0)
    def _():
        hs_ref[...] = _rms(x_ref[...], nw_ref[...]).astype(BF16)
        ba = jnp.dot(hs_ref[...], wba_ref[...], preferred_element_type=F32)
        zz = ba[:, V_HEADS:2 * V_HEADS] + dtb_ref[...]
        softplus = jnp.maximum(zz, 0.0) + jnp.log1p(jnp.exp(-jnp.abs(zz)))
        g_ref[...] = -jnp.exp(alog_ref[...]) * softplus
        beta_ref[...] = jax.nn.sigmoid(ba[:, 0:V_HEADS])

    @pl.when((i % tiles_per_seq == 0) & (j == 0))
    def _():
        hist_ref[...] = jnp.concatenate(
            [jnp.zeros((pad - tail, CONV_DIM), F32), cprev_ref[0]], axis=0)

    def project(r_ref, half, s):
        c0 = half * tn + s * MXU_K
        r_ref[s, pad:pad + tm, :] = jnp.dot(hs_ref[...], w_ref[:, c0:c0 + MXU_K],
                                            preferred_element_type=F32)

    def conv_rows(win, c0):
        acc = win[pad:] * cw_ref[tail:tail + 1, c0:c0 + MXU_K]
        for d in range(1, CONV_W):
            shifted = pltpu.roll(win, d, 0)[pad:]
            acc = acc + shifted * cw_ref[tail - d:tail - d + 1, c0:c0 + MXU_K]
        return acc

    def epilogue(r_ref, tile, out_half, s):
        c0 = tile * tn + s * MXU_K
        o0 = out_half * tn + s * MXU_K
        l2 = tile < N_QK_TILES
        scale = HEAD ** -0.5 if tile < N_Q_TILES else 1.0

        def finish(acc, rows):
            c = acc * jax.nn.sigmoid(acc)
            if l2:
                for hd in range(MXU_K // HEAD):
                    ch = c[:, hd * HEAD:(hd + 1) * HEAD]
                    ch = ch * (lax.rsqrt(jnp.sum(ch * ch, axis=-1, keepdims=True) + L2_EPS) * scale)
                    qkv_ref[rows, o0 + hd * HEAD:o0 + (hd + 1) * HEAD] = ch.astype(BF16)
            else:
                qkv_ref[rows, o0:o0 + MXU_K] = c.astype(BF16)

        r_ref[s, 0:pad, :] = hist_ref[:, c0:c0 + MXU_K]
        hist_ref[:, c0:c0 + MXU_K] = r_ref[s, tm:tm + pad, :]
        for sq in range(n_seq):
            end = pad + (sq + 1) * seq_rows
            nc_ref[sq, :, c0:c0 + MXU_K] = r_ref[s, end - tail:end, :]
        finish(conv_rows(r_ref[s], c0), slice(0, tm))
        for sq in range(1, n_seq):
            r0 = sq * seq_rows
            win = jnp.concatenate([jnp.zeros((pad - tail, MXU_K), F32), cprev_ref[sq, :, c0:c0 + MXU_K],
                                   r_ref[s, pad + r0:pad + r0 + pad, :]], axis=0)
            finish(conv_rows(win, c0), slice(r0, r0 + pad))

    def conv_step(step):
        ta = 2 * step
        for s in range(n_sub):
            project(ra_ref, 0, s)
            if step > 0:
                epilogue(rb_ref, ta - 1, 0, s)
            else:
                qkv_ref[:, s * MXU_K:(s + 1) * MXU_K] = jnp.zeros((tm, MXU_K), BF16)
        for s in range(n_sub):
            project(rb_ref, 1, s)
            epilogue(ra_ref, ta, 1, s)

    for step in range(N_CONV_TILES // 2):
        pl.when(j == step)(functools.partial(conv_step, step))

    @pl.when(j == N_CONV_TILES // 2)
    def _():
        for s in range(n_sub):
            z_ref[:, s * MXU_K:(s + 1) * MXU_K] = jnp.dot(
                hs_ref[...], w_ref[:, s * MXU_K:(s + 1) * MXU_K],
                preferred_element_type=F32).astype(BF16)
            epilogue(rb_ref, N_CONV_TILES - 1, 0, s)
            qkv_ref[:, tn + s * MXU_K:tn + (s + 1) * MXU_K] = jnp.zeros((tm, MXU_K), BF16)
        z_ref[:, tn:] = jnp.dot(hs_ref[...], w_ref[:, tn:], preferred_element_type=F32).astype(BF16)

    @pl.when(j > N_CONV_TILES // 2)
    def _():
        z_ref[...] = jnp.dot(hs_ref[...], w_ref[...], preferred_element_type=F32).astype(BF16)


def _inproj_call(x, nw, w_qkvz, w_ba, conv_prev, conv_w, a_log, dt_bias, t_len, tm):
    m = x.shape[0]
    tn2 = 2 * INPROJ_TN
    n_cs = N_CONV_TILES // 2
    n_zs = N_Z_TILES // 2
    seq_rows = min(t_len, tm)
    n_seq = tm // seq_rows
    tpb = t_len // seq_rows
    return pl.pallas_call(
        functools.partial(_inproj_body, tm=tm, seq_rows=seq_rows, tiles_per_seq=tpb),
        grid=(m // tm, n_cs + n_zs),
        in_specs=[pl.BlockSpec((tm, D_MODEL), lambda i, j: (i, 0)),
                  pl.BlockSpec((1, D_MODEL), lambda i, j: (0, 0)),
                  pl.BlockSpec((D_MODEL, tn2), lambda i, j: (0, j)),
                  pl.BlockSpec((D_MODEL, LANES), lambda i, j: (0, 0)),
                  pl.BlockSpec((n_seq, CONV_W - 1, CONV_DIM), lambda i, j: (i // tpb, 0, 0)),
                  pl.BlockSpec((CONV_W, CONV_DIM), lambda i, j: (0, 0)),
                  pl.BlockSpec((1, V_HEADS), lambda i, j: (0, 0)),
                  pl.BlockSpec((1, V_HEADS), lambda i, j: (0, 0))],
        out_specs=[pl.BlockSpec((tm, tn2), lambda i, j: (i, jnp.minimum(j, n_cs))),
                   pl.BlockSpec((tm, tn2), lambda i, j: (i, jnp.maximum(j - n_cs, 0))),
                   pl.BlockSpec((tm, V_HEADS), lambda i, j: (i, 0)),
                   pl.BlockSpec((tm, V_HEADS), lambda i, j: (i, 0)),
                   pl.BlockSpec((n_seq, CONV_W - 1, CONV_DIM), lambda i, j: (i, 0, 0))],
        out_shape=[jax.ShapeDtypeStruct((m, (n_cs + 1) * tn2), BF16),
                   jax.ShapeDtypeStruct((m, VAL_DIM), BF16),
                   jax.ShapeDtypeStruct((m, V_HEADS), F32),
                   jax.ShapeDtypeStruct((m, V_HEADS), F32),
                   jax.ShapeDtypeStruct((m // seq_rows, CONV_W - 1, CONV_DIM), F32)],
        scratch_shapes=[pltpu.VMEM((tm, D_MODEL), BF16),
                        pltpu.VMEM((SUBLANES, CONV_DIM), F32),
                        pltpu.VMEM((INPROJ_TN // MXU_K, tm + SUBLANES, MXU_K), F32),
                        pltpu.VMEM((INPROJ_TN // MXU_K, tm + SUBLANES, MXU_K), F32)],
        compiler_params=_cparams(("arbitrary", "arbitrary")),
        name="gdn_inproj_conv",
    )(x, nw, w_qkvz, w_ba, conv_prev, conv_w, a_log, dt_bias)


INV_BASE = 16


def _mm16(a, b):
    return jnp.dot(a.astype(BF16), b.astype(BF16), preferred_element_type=F32)


def _tri_inverse_all(nmats, ri, ci, chunk):
    base = min(INV_BASE, chunk)
    sh = int(math.log2(base))
    same = (ri >> sh) == (ci >> sh)
    eye = (ri == ci).astype(F32)
    ms = [jnp.where(same, -n, 0.0) for n in nmats]
    ps = [eye + m for m in ms]
    for _ in range(sh - 1):
        ms = [_mm16(m, m) for m in ms]
        ps = [p + _mm16(p, m) for p, m in zip(ps, ms)]
    size = base
    while size < chunk:
        sh += 1
        size *= 2
        same_next = (ri >> sh) == (ci >> sh)
        sel = same_next & jnp.logical_not(same)
        ts = [_mm16(p, jnp.where(sel, n, 0.0)) for p, n in zip(ps, nmats)]
        ps = [p - _mm16(t, p) for p, t in zip(ps, ts)]
        same = same_next
    return ps


def _delta_body(q_ref, k_ref, v_ref, z_ref, gc_ref, gr_ref, bc_ref, s0_ref, nw_ref,
                o_ref, sf_ref, s_ref, *, chunk, cpb, hb):
    c_idx = pl.program_id(2)

    @pl.when(c_idx == 0)
    def _():
        s_ref[...] = s0_ref[0]

    ri = lax.broadcasted_iota(jnp.int32, (chunk, chunk), 0)
    ci = lax.broadcasted_iota(jnp.int32, (chunk, chunk), 1)
    causal = ri >= ci
    strict = ri > ci
    tril = causal.astype(F32)
    triu = (ri <= ci).astype(F32)
    hi = lax.Precision.HIGHEST
    nt = (((1,), (1,)), ((), ()))
    tn = (((0,), (0,)), ((), ()))
    hq = hb // 2

    kqk, pre = {}, {}
    for c in range(cpb):
        rows = slice(c * chunk, (c + 1) * chunk)
        for j in range(hq):
            k16 = k_ref[0, rows, j * HEAD:(j + 1) * HEAD]
            q16 = q_ref[0, rows, j * HEAD:(j + 1) * HEAD]
            kqk[c, j] = lax.dot_general(jnp.concatenate([k16, q16], axis=0), k16, nt,
                                        preferred_element_type=F32)
    nmats, keys = [], []
    for c in range(cpb):
        rows = slice(c * chunk, (c + 1) * chunk)
        gcol = jnp.dot(tril, gc_ref[0, 0, rows, :], precision=hi, preferred_element_type=F32)
        grow = jnp.dot(gr_ref[0, 0, c], triu, precision=hi, preferred_element_type=F32)
        bcol = bc_ref[0, 0, rows, :]
        for hh in range(hb):
            j = hh // 2
            k = k_ref[0, rows, j * HEAD:(j + 1) * HEAD].astype(F32)
            q = q_ref[0, rows, j * HEAD:(j + 1) * HEAD].astype(F32)
            v = v_ref[0, rows, hh * HEAD:(hh + 1) * HEAD].astype(F32)
            gc = gcol[:, hh:hh + 1]
            gr = grow[hh:hh + 1, :]
            beta = bcol[:, hh:hh + 1]
            decay = jnp.exp(jnp.where(causal, gc - gr, -jnp.inf))
            egc = jnp.exp(gc)
            g_last = gc[chunk - 1:chunk, :]
            kbeta = k * beta
            kk = kqk[c, j][:chunk]
            qk = kqk[c, j][chunk:]
            nmats.append(jnp.where(strict, (kk * beta) * decay, 0.0))
            keys.append((c, hh))
            pre[c, hh] = dict(
                rhs=jnp.concatenate([v * beta, kbeta * egc], axis=1).astype(BF16),
                qg=(q * egc).astype(BF16),
                intra=(qk * decay).astype(BF16),
                kd=(k * jnp.exp(g_last - gc)).astype(BF16),
                sdec=jnp.exp(g_last))
    tinv = _tri_inverse_all(nmats, ri, ci, chunk)
    for key, t in zip(keys, tinv):
        pre[key]['sol'] = jnp.dot(t.astype(BF16), pre[key]['rhs'], preferred_element_type=F32)

    for c in range(cpb):
        rows = slice(c * chunk, (c + 1) * chunk)
        ws = []
        for hh in range(hb):
            d = pre[c, hh]
            wq = jnp.concatenate([d['sol'][:, HEAD:].astype(BF16), d['qg']], axis=0)
            ws.append(jnp.dot(wq, s_ref[hh].astype(BF16), preferred_element_type=F32))
        for hh in range(hb):
            d = pre[c, hh]
            v16 = (d['sol'][:, :HEAD] - ws[hh][:chunk]).astype(BF16)
            o = ws[hh][chunk:] + jnp.dot(d['intra'], v16, preferred_element_type=F32)
            s_ref[hh] = s_ref[hh] * d['sdec'] + lax.dot_general(d['kd'], v16, tn,
                                                                preferred_element_type=F32)
            o = o * lax.rsqrt(jnp.mean(o * o, axis=-1, keepdims=True) + RMS_EPS) * nw_ref[...]
            zz = z_ref[0, rows, hh * HEAD:(hh + 1) * HEAD]
            o_ref[0, rows, hh * HEAD:(hh + 1) * HEAD] = (o * (zz * jax.nn.sigmoid(zz))).astype(BF16)

    @pl.when(c_idx == pl.num_programs(2) - 1)
    def _():
        sf_ref[0] = s_ref[...]


def _delta_call(qkv, z, g, beta, s0, norm_w, chunk, cpb, hb):
    n_b, t_len, _ = qkv.shape
    nhg = V_HEADS // hb
    tc = chunk * cpb
    n_chunks = t_len // chunk
    g_col = g.reshape(n_b, t_len, nhg, hb).transpose(0, 2, 1, 3)
    b_col = beta.reshape(n_b, t_len, nhg, hb).transpose(0, 2, 1, 3)
    g_row = g.reshape(n_b, n_chunks, chunk, nhg, hb).transpose(0, 3, 1, 4, 2)
    hq = hb // 2
    q_off = INPROJ_TN // (hq * HEAD)
    k_off = (INPROJ_TN + KEY_DIM) // (hq * HEAD)
    v_off = (INPROJ_TN + 2 * KEY_DIM) // (hb * HEAD)
    assert INPROJ_TN % (hb * HEAD) == 0 and KEY_DIM % (hb * HEAD) == 0, hb
    return pl.pallas_call(
        functools.partial(_delta_body, chunk=chunk, cpb=cpb, hb=hb),
        grid=(n_b, nhg, t_len // tc),
        in_specs=[pl.BlockSpec((1, tc, hq * HEAD), lambda b, h, c: (b, c, q_off + h)),
                  pl.BlockSpec((1, tc, hq * HEAD), lambda b, h, c: (b, c, k_off + h)),
                  pl.BlockSpec((1, tc, hb * HEAD), lambda b, h, c: (b, c, v_off + h)),
                  pl.BlockSpec((1, tc, hb * HEAD), lambda b, h, c: (b, c, h)),
                  pl.BlockSpec((1, 1, tc, hb), lambda b, h, c: (b, h, c, 0)),
                  pl.BlockSpec((1, 1, cpb, hb, chunk), lambda b, h, c: (b, h, c, 0, 0)),
                  pl.BlockSpec((1, 1, tc, hb), lambda b, h, c: (b, h, c, 0)),
                  pl.BlockSpec((1, hb, HEAD, HEAD), lambda b, h, c: (b, h, 0, 0)),
                  pl.BlockSpec((1, HEAD), lambda b, h, c: (0, 0))],
        out_specs=[pl.BlockSpec((1, tc, hb * HEAD), lambda b, h, c: (b, c, h)),
                   pl.BlockSpec((1, hb, HEAD, HEAD), lambda b, h, c: (b, h, 0, 0))],
        out_shape=[jax.ShapeDtypeStruct((n_b, t_len, VAL_DIM), BF16),
                   jax.ShapeDtypeStruct((n_b, V_HEADS, HEAD, HEAD), F32)],
        scratch_shapes=[pltpu.VMEM((hb, HEAD, HEAD), F32)],
        compiler_params=_cparams(("arbitrary", "arbitrary", "arbitrary")),
        name="gated_delta_rule",
    )(qkv, qkv, qkv, z, g_col, g_row, b_col, s0, norm_w)


def _tile(m, pref):
    t = min(m, pref)
    assert m % t == 0, (m, t)
    return t


def _trunk(x, s5_re0, s5_im0, conv0, s0, p):
    n_b, t_len, _ = x.shape
    m = n_b * t_len
    tm = _tile(m, 512)
    tm_wide = _tile(m, 1024)

    tt = _tile(t_len, 256)
    g, sr, si = _s5_call(x, p['nm0'], p['s5_d'], p['wb'], p['wc'], p['a_re'], p['a_im'],
                         s5_re0.reshape(n_b, 1, S5_CH), s5_im0.reshape(n_b, 1, S5_CH), tt)
    x2 = x.reshape(m, D_MODEL)
    x2 = _glu_call(g.reshape(m, D_MODEL), p['w_glu'], x2, tm_wide, 512)
    x2 = _ffn_call(x2, p['nf0'], p['w_up0'], p['w_down0'], p['nfin'], tm, 1024, False)

    qkv, z, gg, beta, tails = _inproj_call(
        x2, p['nm1'], p['w_qkvz'], p['w_ba'], conv0, p['conv_w'], p['a_log'], p['dt_bias'],
        t_len, tm)
    tpb = t_len // min(t_len, tm)
    new_conv = tails[tpb - 1::tpb]
    chunk = min(t_len, 64)
    cpb = _tile(t_len // chunk, 4)
    og, s_fin = _delta_call(qkv.reshape(n_b, t_len, -1), z.reshape(n_b, t_len, VAL_DIM),
                            gg.reshape(n_b, t_len, V_HEADS), beta.reshape(n_b, t_len, V_HEADS),
                            s0, p['gdn_norm_w'], chunk, cpb, 8)
    x2 = _matmul_res_call(og.reshape(m, VAL_DIM), p['w_out'], x2, tm_wide, 512)
    y = _ffn_call(x2, p['nf1'], p['w_up1'], p['w_down1'], p['nfin'], tm, 1024, True)

    return (y.reshape(n_b, t_len, D_MODEL),
            sr.reshape(1, n_b, S5_GROUPS, S5_STATE), si.reshape(1, n_b, S5_GROUPS, S5_STATE),
            new_conv[None], s_fin[None])


def kernel(x_prompt, x_sample, state_s5_re, state_s5_im, state_gdn_conv, state_gdn_s,
           norm_mix, norm_ffn, norm_final,
           s5_lam_re, s5_lam_im, s5_log_dt, s5_b_re, s5_b_im, s5_c_re, s5_c_im,
           s5_d, s5_w_glu,
           gdn_w_in, gdn_conv_w, gdn_a_log, gdn_dt_bias, gdn_norm_w, gdn_w_out,
           ffn_w_up, ffn_w_down):
    wb, wc, a_re, a_im = _s5_params(s5_lam_re[0], s5_lam_im[0], s5_log_dt[0], s5_b_re[0], s5_b_im[0],
                              s5_c_re[0], s5_c_im[0])
    w_in = gdn_w_in[0]
    ba = w_in[:, CONV_DIM + VAL_DIM:]
    w_ba = jnp.concatenate([ba, jnp.zeros((D_MODEL, LANES - ba.shape[1]), w_in.dtype)], axis=1)
    row = lambda a: a.reshape(1, -1).astype(F32)
    p = dict(
        nm0=row(norm_mix[0]), nm1=row(norm_mix[1]), nf0=row(norm_ffn[0]), nf1=row(norm_ffn[1]),
        nfin=row(norm_final), s5_d=row(s5_d[0]), wb=wb, wc=wc, a_re=a_re, a_im=a_im,
        w_glu=s5_w_glu[0].astype(BF16),
        w_up0=ffn_w_up[0].astype(BF16), w_down0=ffn_w_down[0].astype(BF16),
        w_up1=ffn_w_up[1].astype(BF16), w_down1=ffn_w_down[1].astype(BF16),
        w_qkvz=w_in[:, :CONV_DIM + VAL_DIM].astype(BF16), w_ba=w_ba.astype(BF16),
        conv_w=gdn_conv_w[0].astype(F32), a_log=row(gdn_a_log[0]), dt_bias=row(gdn_dt_bias[0]),
        gdn_norm_w=row(gdn_norm_w[0]), w_out=gdn_w_out[0].astype(BF16),
    )
    n_bp = x_prompt.shape[0]
    z_state = jnp.zeros((n_bp, S5_GROUPS, S5_STATE), F32)
    z_conv = jnp.zeros((n_bp, CONV_W - 1, CONV_DIM), F32)
    z_s = jnp.zeros((n_bp, V_HEADS, HEAD, HEAD), F32)

    yp, p_re, p_im, p_conv, p_s = _trunk(x_prompt, z_state, z_state, z_conv, z_s, p)
    ys, s_re, s_im, s_conv, s_s = _trunk(x_sample, state_s5_re[0], state_s5_im[0],
                                         state_gdn_conv[0], state_gdn_s[0], p)
    return (yp, ys, p_re, p_im, p_conv, p_s, s_re, s_im, s_conv, s_s)
```

```python
import functools
import math

import jax
import jax.numpy as jnp
from jax import lax
from jax.experimental import pallas as pl
from jax.experimental.pallas import tpu as pltpu

F32 = jnp.float32
BF16 = jnp.bfloat16

RMS_EPS = 1e-6
L2_EPS = 1e-6
D_MODEL = 2048
S5_GROUP = 16
S5_STATE = 64
S5_GROUPS = D_MODEL // S5_GROUP
S5_CH = S5_GROUPS * S5_STATE
HEAD = 128
QK_HEADS = D_MODEL // HEAD
V_HEADS = 2 * QK_HEADS
KEY_DIM = QK_HEADS * HEAD
VAL_DIM = V_HEADS * HEAD
CONV_DIM = 2 * KEY_DIM + VAL_DIM
CONV_W = 4

SUBLANES = 8
LANES = 128
MXU_K = 256
S5_KB = D_MODEL // MXU_K
S5_CB = S5_CH // S5_KB
VMEM_LIMIT = 56 * 1024 * 1024


def _cparams(sem):
    return pltpu.CompilerParams(dimension_semantics=sem, vmem_limit_bytes=VMEM_LIMIT)


def _rms(x, w):
    return x * lax.rsqrt(jnp.mean(x * x, axis=-1, keepdims=True) + RMS_EPS) * w


def _s5_body(x_ref, nw_ref, d_ref, wb_ref, wc_ref, coef_ref, perm_ref, permt_ref, s0r_ref, s0i_ref,
             g_ref, sr_ref, si_ref, car_ref, h_ref, bu_ref, st_ref, *, tt, multi_seq):
    t = pl.program_id(1)
    seg = tt // SUBLANES

    @pl.when(t == 0)
    def _():
        car_ref[0] = jnp.broadcast_to(s0r_ref[0], (SUBLANES, S5_CH))
        car_ref[1] = jnp.broadcast_to(s0i_ref[0], (SUBLANES, S5_CH))

    h = _rms(x_ref[0], nw_ref[...])
    h_ref[...] = h
    hp = jnp.dot(perm_ref[...], h.astype(BF16), preferred_element_type=F32).astype(BF16)
    row0 = lax.broadcasted_iota(jnp.int32, (SUBLANES, S5_CB), 0) == 0

    def step(ar, ai, sr, si, xr, xi):
        return ar * sr - ai * si + xr, ar * si + ai * sr + xi

    def project_in(kb):
        ks = slice(kb * MXU_K, (kb + 1) * MXU_K)
        bu_ref[kb % 2] = jnp.dot(hp[:, ks], wb_ref[kb], preferred_element_type=F32)

    def project_out(kb):
        ks = slice(kb * MXU_K, (kb + 1) * MXU_K)
        yp = jnp.dot(st_ref[kb % 2].astype(BF16), wc_ref[kb], preferred_element_type=F32)
        y_hi = yp.astype(BF16)
        y_lo = (yp - y_hi.astype(F32)).astype(BF16)
        y2 = jnp.dot(permt_ref[...], jnp.concatenate([y_hi, y_lo], axis=1),
                     preferred_element_type=F32)
        y = y2[:, :MXU_K] + y2[:, MXU_K:] + d_ref[:, ks] * h_ref[:, ks]
        g_ref[0, :, ks] = jax.nn.gelu(y).astype(BF16)

    def scan(kb):
        cs = slice(kb * S5_CB, (kb + 1) * S5_CB)
        buf = kb % 2

        def bu_rows(j):
            rows = slice(j * SUBLANES, (j + 1) * SUBLANES)
            return bu_ref[buf, rows, 0:S5_CB], bu_ref[buf, rows, S5_CB:2 * S5_CB]

        ar = coef_ref[4, 0, :, cs]
        ai = coef_ref[4, 1, :, cs]
        if multi_seq:
            sr = car_ref[0, :, cs]
            si = car_ref[1, :, cs]
            for j in range(seg):
                sr, si = step(ar, ai, sr, si, *bu_rows(j))
                rows = slice(j * SUBLANES, (j + 1) * SUBLANES)
                st_ref[buf, rows, 0:S5_CB] = sr
                st_ref[buf, rows, S5_CB:2 * S5_CB] = si
            car_ref[0, :, cs] = sr
            car_ref[1, :, cs] = si
            return
        er, ei = bu_rows(0)
        for j in range(1, seg):
            er, ei = step(ar, ai, er, ei, *bu_rows(j))
        for lvl, sh in enumerate((1, 2, 4)):
            cr_ = coef_ref[lvl, 0, :, cs]
            ci_ = coef_ref[lvl, 1, :, cs]
            pr = pltpu.roll(er, sh, 0)
            pi = pltpu.roll(ei, sh, 0)
            er, ei = er + cr_ * pr - ci_ * pi, ei + cr_ * pi + ci_ * pr
        c0r = car_ref[0, :, cs]
        c0i = car_ref[1, :, cs]
        cr_ = coef_ref[3, 0, :, cs]
        ci_ = coef_ref[3, 1, :, cs]
        fr = er + cr_ * c0r - ci_ * c0i
        fi = ei + cr_ * c0i + ci_ * c0r
        car_ref[0, :, cs] = jnp.broadcast_to(fr[SUBLANES - 1:SUBLANES], (SUBLANES, S5_CB))
        car_ref[1, :, cs] = jnp.broadcast_to(fi[SUBLANES - 1:SUBLANES], (SUBLANES, S5_CB))
        sr = jnp.where(row0, c0r, pltpu.roll(fr, 1, 0))
        si = jnp.where(row0, c0i, pltpu.roll(fi, 1, 0))
        for j in range(seg):
            sr, si = step(ar, ai, sr, si, *bu_rows(j))
            rows = slice(j * SUBLANES, (j + 1) * SUBLANES)
            st_ref[buf, rows, 0:S5_CB] = sr
            st_ref[buf, rows, S5_CB:2 * S5_CB] = si

    project_in(0)
    for kb in range(S5_KB):
        if kb + 1 < S5_KB:
            project_in(kb + 1)
        if kb >= 1:
            project_out(kb - 1)
        scan(kb)
    project_out(S5_KB - 1)

    @pl.when(t == pl.num_programs(1) - 1)
    def _():
        n_state = SUBLANES if multi_seq else 1
        sr_ref[0] = car_ref[0, 0:n_state, :]
        si_ref[0] = car_ref[1, 0:n_state, :]


def _s5_call(x, nw, d, wb, wc, a_re, a_im, s0r, s0i, tt, multi_seq):
    n_b, t_len, _ = x.shape
    seg = tt // SUBLANES
    n_state = s0r.shape[1]
    assert n_state == (SUBLANES if multi_seq else 1) and (not multi_seq or t_len == tt)
    coef = _s5_coef(a_re, a_im, seg)
    p_idx = jnp.arange(tt)
    src = (p_idx % SUBLANES) * seg + p_idx // SUBLANES
    perm = (src[:, None] == p_idx[None, :]).astype(BF16)
    const = lambda *shape: pl.BlockSpec(shape, lambda b, t: (0,) * len(shape))
    state = pl.BlockSpec((1, n_state, S5_CH), lambda b, t: (b, 0, 0))
    return pl.pallas_call(
        functools.partial(_s5_body, tt=tt, multi_seq=multi_seq),
        grid=(n_b, t_len // tt),
        in_specs=[pl.BlockSpec((1, tt, D_MODEL), lambda b, t: (b, t, 0)),
                  const(1, D_MODEL), const(1, D_MODEL),
                  const(S5_KB, MXU_K, 2 * S5_CB), const(S5_KB, 2 * S5_CB, MXU_K),
                  const(5, 2, SUBLANES, S5_CH), const(tt, tt), const(tt, tt), state, state],
        out_specs=[pl.BlockSpec((1, tt, D_MODEL), lambda b, t: (b, t, 0)), state, state],
        out_shape=[jax.ShapeDtypeStruct((n_b, t_len, D_MODEL), BF16),
                   jax.ShapeDtypeStruct((n_b, n_state, S5_CH), F32),
                   jax.ShapeDtypeStruct((n_b, n_state, S5_CH), F32)],
        scratch_shapes=[pltpu.VMEM((2, SUBLANES, S5_CH), F32),
                        pltpu.VMEM((tt, D_MODEL), F32),
                        pltpu.VMEM((2, tt, 2 * S5_CB), F32),
                        pltpu.VMEM((2, tt, 2 * S5_CB), F32)],
        compiler_params=_cparams(("arbitrary", "arbitrary")),
        name="s5_mixer",
    )(x, nw, d, wb, wc, coef, perm, perm.T, s0r, s0i)


def _s5_coef(a_re, a_im, seg):
    assert seg & (seg - 1) == 0, seg
    ar, ai = a_re, a_im
    for _ in range(int(math.log2(seg))):
        ar, ai = ar * ar - ai * ai, 2.0 * ar * ai
    pw_r, pw_i = [ar], [ai]
    for _ in range(SUBLANES - 1):
        pr, pi = pw_r[-1], pw_i[-1]
        pw_r.append(pr * ar - pi * ai)
        pw_i.append(pr * ai + pi * ar)
    row = jnp.arange(SUBLANES)[:, None]
    levels = []
    for sh in (1, 2, 4):
        m = (row >= sh).astype(F32)
        levels.append(jnp.stack([m * pw_r[sh - 1], m * pw_i[sh - 1]]))
    levels.append(jnp.stack([jnp.concatenate(pw_r, axis=0), jnp.concatenate(pw_i, axis=0)]))
    ones = jnp.ones((SUBLANES, 1), F32)
    levels.append(jnp.stack([ones * a_re, ones * a_im]))
    return jnp.stack(levels)


def _s5_params(lam_re, lam_im, log_dt, b_re, b_im, c_re, c_im):
    lr = jnp.minimum(lam_re.astype(F32), -1e-4)
    li = lam_im.astype(F32)
    dt = jnp.exp(log_dt.astype(F32))[:, None]
    mag = jnp.exp(lr * dt)
    ang = li * dt
    a_re = mag * jnp.cos(ang)
    a_im = mag * jnp.sin(ang)
    nr = a_re - 1.0
    den = lr * lr + li * li
    k_re = (nr * lr + a_im * li) / den
    k_im = (a_im * lr - nr * li) / den
    br = b_re.astype(F32)
    bi = b_im.astype(F32)
    bb_re = k_re[..., None] * br - k_im[..., None] * bi
    bb_im = k_re[..., None] * bi + k_im[..., None] * br

    gl = S5_GROUPS // S5_KB
    eye = jnp.eye(gl, dtype=F32)

    def blockdiag_in(bb):
        bbk = bb.reshape(S5_KB, gl, S5_STATE, S5_GROUP)
        w = jnp.einsum('kgph,gf->kghfp', bbk, eye)
        return w.reshape(S5_KB, gl * S5_GROUP, gl * S5_STATE)

    def blockdiag_out(c):
        ck = c.reshape(S5_KB, gl, S5_GROUP, S5_STATE)
        w = jnp.einsum('kghp,gf->kgpfh', ck, eye)
        return w.reshape(S5_KB, gl * S5_STATE, gl * S5_GROUP)

    wb = jnp.concatenate([blockdiag_in(bb_re), blockdiag_in(bb_im)], axis=2).astype(BF16)
    wc = jnp.concatenate([blockdiag_out(c_re.astype(F32)),
                          -blockdiag_out(c_im.astype(F32))], axis=1).astype(BF16)

    return wb, wc, a_re.reshape(1, S5_CH), a_im.reshape(1, S5_CH)


def _glu_body(g_ref, wa_ref, wg_ref, x_ref, o_ref):
    g = g_ref[...]
    za = jnp.dot(g, wa_ref[...], preferred_element_type=F32)
    zg = jnp.dot(g, wg_ref[...], preferred_element_type=F32)
    o_ref[...] = x_ref[...] + za * jax.nn.sigmoid(zg)


def _glu_call(g, w_glu, x, tm, tn):
    m = g.shape[0]
    nj = D_MODEL // tn
    return pl.pallas_call(
        _glu_body,
        grid=(m // tm, nj),
        in_specs=[pl.BlockSpec((tm, D_MODEL), lambda i, j: (i, 0)),
                  pl.BlockSpec((D_MODEL, tn), lambda i, j: (0, j)),
                  pl.BlockSpec((D_MODEL, tn), lambda i, j: (0, j + nj)),
                  pl.BlockSpec((tm, tn), lambda i, j: (i, j))],
        out_specs=pl.BlockSpec((tm, tn), lambda i, j: (i, j)),
        out_shape=jax.ShapeDtypeStruct((m, D_MODEL), F32),
        compiler_params=_cparams(("parallel", "arbitrary")),
        name="glu_residual",
    )(g, w_glu, w_glu, x)


def _ffn_body(x_ref, nw_ref, wu_ref, wd_ref, fw_ref, o_ref, h_ref, *, final_norm):
    f = pl.program_id(1)

    @pl.when(f == 0)
    def _():
        x = x_ref[...]
        h_ref[...] = _rms(x, nw_ref[...]).astype(BF16)
        o_ref[...] = x

    a = jnp.maximum(jnp.dot(h_ref[...], wu_ref[...], preferred_element_type=F32), 0.0)
    o_ref[...] += jnp.dot((a * a).astype(BF16), wd_ref[...], preferred_element_type=F32)

    if final_norm:
        @pl.when(f == pl.num_programs(1) - 1)
        def _():
            o_ref[...] = _rms(o_ref[...], fw_ref[...])


def _ffn_call(x, nw, w_up, w_down, layer, fw, tm, tf, final_norm):
    m = x.shape[0]
    d_ff = w_up.shape[2]
    return pl.pallas_call(
        functools.partial(_ffn_body, final_norm=final_norm),
        grid=(m // tm, d_ff // tf),
        in_specs=[pl.BlockSpec((tm, D_MODEL), lambda i, f: (i, 0)),
                  pl.BlockSpec((1, D_MODEL), lambda i, f: (0, 0)),
                  pl.BlockSpec((None, D_MODEL, tf), lambda i, f: (layer, 0, f)),
                  pl.BlockSpec((None, tf, D_MODEL), lambda i, f: (layer, f, 0)),
                  pl.BlockSpec((1, D_MODEL), lambda i, f: (0, 0))],
        out_specs=pl.BlockSpec((tm, D_MODEL), lambda i, f: (i, 0)),
        out_shape=jax.ShapeDtypeStruct((m, D_MODEL), F32),
        scratch_shapes=[pltpu.VMEM((tm, D_MODEL), BF16)],
        compiler_params=_cparams(("parallel", "arbitrary")),
        name="ffn_residual",
    )(x, nw, w_up, w_down, fw)


def _matmul_res_body(a_ref, w_ref, x_ref, o_ref):
    o_ref[...] = x_ref[...] + jnp.dot(a_ref[...], w_ref[...], preferred_element_type=F32)


def _matmul_res_call(a, w, x, tm, tn):
    m, k = a.shape
    n = w.shape[1]
    return pl.pallas_call(
        _matmul_res_body,
        grid=(m // tm, n // tn),
        in_specs=[pl.BlockSpec((tm, k), lambda i, j: (i, 0)),
                  pl.BlockSpec((k, tn), lambda i, j: (0, j)),
                  pl.BlockSpec((tm, tn), lambda i, j: (i, j))],
        out_specs=pl.BlockSpec((tm, tn), lambda i, j: (i, j)),
        out_shape=jax.ShapeDtypeStruct((m, n), F32),
        compiler_params=_cparams(("parallel", "arbitrary")),
        name="matmul_residual",
    )(a, w, x)


INPROJ_TN = 1024
N_CONV_TILES = CONV_DIM // INPROJ_TN
N_Z_TILES = VAL_DIM // INPROJ_TN
N_QK_TILES = 2 * KEY_DIM // INPROJ_TN
N_Q_TILES = KEY_DIM // INPROJ_TN
CONV_ROWS = 64


def _inproj_body(x_ref, nw_ref, w_ref, wba_ref, cprev_ref, cw_ref, alog_ref, dtb_ref,
                 qkv_ref, z_ref, g_ref, beta_ref, nc_ref, hs_ref, hist_ref, ra_ref, rb_ref,
                 *, tm, seq_rows, tiles_per_seq):
    n_seq = tm // seq_rows
    i = pl.program_id(0)
    j = pl.program_id(1)
    tn = INPROJ_TN
    n_sub = tn // MXU_K
    tail = CONV_W - 1
    pad = SUBLANES

    @pl.when(j == 0)
    def _():
        hs_ref[...] = _rms(x_ref[...], nw_ref[...]).astype(BF16)
        ba = jnp.dot(hs_ref[...], wba_ref[...], preferred_element_type=F32)
        zz = ba[:, V_HEADS:2 * V_HEADS] + dtb_ref[...]
        softplus = jnp.maximum(zz, 0.0) + jnp.log1p(jnp.exp(-jnp.abs(zz)))
        g_ref[...] = -jnp.exp(alog_ref[...]) * softplus
        beta_ref[...] = jax.nn.sigmoid(ba[:, 0:V_HEADS])

    @pl.when((i % tiles_per_seq == 0) & (j == 0))
    def _():
        hist_ref[...] = jnp.concatenate(
            [jnp.zeros((pad - tail, CONV_DIM), F32), cprev_ref[0]], axis=0)

    def project(r_ref, half, s):
        c0 = half * tn + s * MXU_K
        r_ref[s, pad:pad + tm, :] = jnp.dot(hs_ref[...], w_ref[:, c0:c0 + MXU_K],
                                            preferred_element_type=F32)

    def conv_rows(win, c0):
        acc = win[pad:] * cw_ref[tail:tail + 1, c0:c0 + MXU_K]
        for d in range(1, CONV_W):
            shifted = pltpu.roll(win, d, 0)[pad:]
            acc = acc + shifted * cw_ref[tail - d:tail - d + 1, c0:c0 + MXU_K]
        return acc

    def epilogue(r_ref, tile, out_half, s):
        c0 = tile * tn + s * MXU_K
        o0 = out_half * tn + s * MXU_K
        l2 = tile < N_QK_TILES
        scale = HEAD ** -0.5 if tile < N_Q_TILES else 1.0

        def finish(acc, rows):
            c = acc * jax.nn.sigmoid(acc)
            if l2:
                for hd in range(MXU_K // HEAD):
                    ch = c[:, hd * HEAD:(hd + 1) * HEAD]
                    ch = ch * (lax.rsqrt(jnp.sum(ch * ch, axis=-1, keepdims=True) + L2_EPS) * scale)
                    qkv_ref[rows, o0 + hd * HEAD:o0 + (hd + 1) * HEAD] = ch.astype(BF16)
            else:
                qkv_ref[rows, o0:o0 + MXU_K] = c.astype(BF16)

        r_ref[s, 0:pad, :] = hist_ref[:, c0:c0 + MXU_K]
        hist_ref[:, c0:c0 + MXU_K] = r_ref[s, tm:tm + pad, :]
        for sq in range(n_seq):
            end = pad + (sq + 1) * seq_rows
            nc_ref[sq, :, c0:c0 + MXU_K] = r_ref[s, end - tail:end, :]
        for r0 in range(0, tm, CONV_ROWS):
            rc = min(CONV_ROWS, tm - r0)
            finish(conv_rows(r_ref[s, r0:r0 + rc + pad, :], c0), slice(r0, r0 + rc))
        for sq in range(1, n_seq):
            r0 = sq * seq_rows
            win = jnp.concatenate([jnp.zeros((pad - tail, MXU_K), F32), cprev_ref[sq, :, c0:c0 + MXU_K],
                                   r_ref[s, pad + r0:pad + r0 + pad, :]], axis=0)
            finish(conv_rows(win, c0), slice(r0, r0 + pad))

    def project_z(half, s):
        c0 = half * tn + s * MXU_K
        z_ref[:, c0:c0 + MXU_K] = jnp.dot(hs_ref[...], w_ref[:, c0:c0 + MXU_K],
                                          preferred_element_type=F32).astype(BF16)

    def step_body(step):
        cur, prev = step % 2, (step - 1) % 2
        for half, r_ref in ((0, ra_ref), (1, rb_ref)):
            for s in range(n_sub):
                if step < n_conv_steps:
                    project(r_ref.at[cur], half, s)
                else:
                    project_z(half, s)
                if 0 < step <= n_conv_steps:
                    epilogue(r_ref.at[prev], 2 * (step - 1) + half, half, s)

    n_conv_steps = N_CONV_TILES // 2
    for step in range(n_conv_steps + 1):
        pl.when(j == step)(functools.partial(step_body, step))
    pl.when(j > n_conv_steps)(functools.partial(step_body, n_conv_steps + 1))


def _inproj_call(x, nw, w_qkvz, w_ba, conv_prev, conv_w, a_log, dt_bias, t_len, tm):
    m = x.shape[0]
    tn2 = 2 * INPROJ_TN
    n_cs = N_CONV_TILES // 2
    n_zs = N_Z_TILES // 2
    seq_rows = min(t_len, tm)
    n_seq = tm // seq_rows
    tpb = t_len // seq_rows
    return pl.pallas_call(
        functools.partial(_inproj_body, tm=tm, seq_rows=seq_rows, tiles_per_seq=tpb),
        grid=(m // tm, n_cs + n_zs),
        in_specs=[pl.BlockSpec((tm, D_MODEL), lambda i, j: (i, 0)),
                  pl.BlockSpec((1, D_MODEL), lambda i, j: (0, 0)),
                  pl.BlockSpec((D_MODEL, tn2), lambda i, j: (0, j)),
                  pl.BlockSpec((D_MODEL, LANES), lambda i, j: (0, 0)),
                  pl.BlockSpec((n_seq, CONV_W - 1, CONV_DIM), lambda i, j: (i // tpb, 0, 0)),
                  pl.BlockSpec((CONV_W, CONV_DIM), lambda i, j: (0, 0)),
                  pl.BlockSpec((1, V_HEADS), lambda i, j: (0, 0)),
                  pl.BlockSpec((1, V_HEADS), lambda i, j: (0, 0))],
        out_specs=[pl.BlockSpec((tm, tn2), lambda i, j: (i, jnp.clip(j - 1, 0, n_cs - 1))),
                   pl.BlockSpec((tm, tn2), lambda i, j: (i, jnp.maximum(j - n_cs, 0))),
                   pl.BlockSpec((tm, V_HEADS), lambda i, j: (i, 0)),
                   pl.BlockSpec((tm, V_HEADS), lambda i, j: (i, 0)),
                   pl.BlockSpec((n_seq, CONV_W - 1, CONV_DIM), lambda i, j: (i, 0, 0))],
        out_shape=[jax.ShapeDtypeStruct((m, CONV_DIM), BF16),
                   jax.ShapeDtypeStruct((m, VAL_DIM), BF16),
                   jax.ShapeDtypeStruct((m, V_HEADS), F32),
                   jax.ShapeDtypeStruct((m, V_HEADS), F32),
                   jax.ShapeDtypeStruct((m // seq_rows, CONV_W - 1, CONV_DIM), F32)],
        scratch_shapes=[pltpu.VMEM((tm, D_MODEL), BF16),
                        pltpu.VMEM((SUBLANES, CONV_DIM), F32),
                        pltpu.VMEM((2, INPROJ_TN // MXU_K, tm + SUBLANES, MXU_K), F32),
                        pltpu.VMEM((2, INPROJ_TN // MXU_K, tm + SUBLANES, MXU_K), F32)],
        compiler_params=_cparams(("arbitrary", "arbitrary")),
        name="gdn_inproj_conv",
    )(x, nw, w_qkvz, w_ba, conv_prev, conv_w, a_log, dt_bias)


INV_BASE = 16


def _mm16(a, b):
    return jnp.dot(a.astype(BF16), b.astype(BF16), preferred_element_type=F32)


def _tri_inverse_all(nmats, ri, ci, chunk):
    base = min(INV_BASE, chunk)
    sh = int(math.log2(base))
    same = (ri >> sh) == (ci >> sh)
    eye = (ri == ci).astype(F32)
    ms = [jnp.where(same, -n, 0.0) for n in nmats]
    ps = [eye + m for m in ms]
    for _ in range(sh - 1):
        ms = [_mm16(m, m) for m in ms]
        ps = [p + _mm16(p, m) for p, m in zip(ps, ms)]
    size = base
    while size < chunk:
        sh += 1
        size *= 2
        same_next = (ri >> sh) == (ci >> sh)
        sel = same_next & jnp.logical_not(same)
        ts = [_mm16(p, jnp.where(sel, n, 0.0)) for p, n in zip(ps, nmats)]
        ps = [p - _mm16(t, p) for p, t in zip(ps, ts)]
        same = same_next
    return ps


def _delta_body(q_ref, k_ref, v_ref, z_ref, gc_ref, gr_ref, bc_ref, s0_ref, nw_ref,
                o_ref, sf_ref, s_ref, *, chunk, cpb, hb):
    c_idx = pl.program_id(2)

    @pl.when(c_idx == 0)
    def _():
        s_ref[...] = s0_ref[0]

    ri = lax.broadcasted_iota(jnp.int32, (chunk, chunk), 0)
    ci = lax.broadcasted_iota(jnp.int32, (chunk, chunk), 1)
    causal = ri >= ci
    strict = ri > ci
    tril = causal.astype(F32)
    triu = (ri <= ci).astype(F32)
    hi = lax.Precision.HIGHEST
    nt = (((1,), (1,)), ((), ()))
    tn = (((0,), (0,)), ((), ()))
    hq = hb // 2

    kqk, pre = {}, {}
    for c in range(cpb):
        rows = slice(c * chunk, (c + 1) * chunk)
        for j in range(hq):
            k16 = k_ref[0, rows, j * HEAD:(j + 1) * HEAD]
            q16 = q_ref[0, rows, j * HEAD:(j + 1) * HEAD]
            kqk[c, j] = lax.dot_general(jnp.concatenate([k16, q16], axis=0), k16, nt,
                                        preferred_element_type=F32)
    nmats, keys = [], []
    for c in range(cpb):
        rows = slice(c * chunk, (c + 1) * chunk)
        gcol = jnp.dot(tril, gc_ref[0, 0, rows, :], precision=hi, preferred_element_type=F32)
        grow = jnp.dot(gr_ref[0, 0, c], triu, precision=hi, preferred_element_type=F32)
        bcol = bc_ref[0, 0, rows, :]
        for hh in range(hb):
            j = hh // 2
            k = k_ref[0, rows, j * HEAD:(j + 1) * HEAD].astype(F32)
            q = q_ref[0, rows, j * HEAD:(j + 1) * HEAD].astype(F32)
            v = v_ref[0, rows, hh * HEAD:(hh + 1) * HEAD].astype(F32)
            gc = gcol[:, hh:hh + 1]
            gr = grow[hh:hh + 1, :]
            beta = bcol[:, hh:hh + 1]
            decay = jnp.exp(jnp.where(causal, gc - gr, -jnp.inf))
            egc = jnp.exp(gc)
            g_last = gc[chunk - 1:chunk, :]
            kbeta = k * beta
            kk = kqk[c, j][:chunk]
            qk = kqk[c, j][chunk:]
            nmats.append(jnp.where(strict, (kk * beta) * decay, 0.0))
            keys.append((c, hh))
            pre[c, hh] = dict(
                rhs=jnp.concatenate([v * beta, kbeta * egc], axis=1).astype(BF16),
                qg=(q * egc).astype(BF16),
                intra=(qk * decay).astype(BF16),
                kd=(k * jnp.exp(g_last - gc)).astype(BF16),
                sdec=jnp.exp(g_last))
    tinv = _tri_inverse_all(nmats, ri, ci, chunk)
    for key, t in zip(keys, tinv):
        pre[key]['sol'] = jnp.dot(t.astype(BF16), pre[key]['rhs'], preferred_element_type=F32)

    for c in range(cpb):
        rows = slice(c * chunk, (c + 1) * chunk)
        ws = []
        for hh in range(hb):
            d = pre[c, hh]
            wq = jnp.concatenate([d['sol'][:, HEAD:].astype(BF16), d['qg']], axis=0)
            ws.append(jnp.dot(wq, s_ref[hh].astype(BF16), preferred_element_type=F32))
        for hh in range(hb):
            d = pre[c, hh]
            v16 = (d['sol'][:, :HEAD] - ws[hh][:chunk]).astype(BF16)
            o = ws[hh][chunk:] + jnp.dot(d['intra'], v16, preferred_element_type=F32)
            s_ref[hh] = s_ref[hh] * d['sdec'] + lax.dot_general(d['kd'], v16, tn,
                                                                preferred_element_type=F32)
            o = o * lax.rsqrt(jnp.mean(o * o, axis=-1, keepdims=True) + RMS_EPS) * nw_ref[...]
            zz = z_ref[0, rows, hh * HEAD:(hh + 1) * HEAD]
            o_ref[0, rows, hh * HEAD:(hh + 1) * HEAD] = (o * (zz * jax.nn.sigmoid(zz))).astype(BF16)

    @pl.when(c_idx == pl.num_programs(2) - 1)
    def _():
        sf_ref[0] = s_ref[...]


def _delta_call(qkv, z, g, beta, s0, norm_w, chunk, cpb, hb):
    n_b, t_len, _ = qkv.shape
    nhg = V_HEADS // hb
    tc = chunk * cpb
    n_chunks = t_len // chunk
    g_col = g.reshape(n_b, t_len, nhg, hb).transpose(0, 2, 1, 3)
    b_col = beta.reshape(n_b, t_len, nhg, hb).transpose(0, 2, 1, 3)
    g_row = g.reshape(n_b, n_chunks, chunk, nhg, hb).transpose(0, 3, 1, 4, 2)
    hq = hb // 2
    assert KEY_DIM % (hb * HEAD) == 0, hb
    q_off = 0
    k_off = KEY_DIM // (hq * HEAD)
    v_off = 2 * KEY_DIM // (hb * HEAD)
    return pl.pallas_call(
        functools.partial(_delta_body, chunk=chunk, cpb=cpb, hb=hb),
        grid=(n_b, nhg, t_len // tc),
        in_specs=[pl.BlockSpec((1, tc, hq * HEAD), lambda b, h, c: (b, c, q_off + h)),
                  pl.BlockSpec((1, tc, hq * HEAD), lambda b, h, c: (b, c, k_off + h)),
                  pl.BlockSpec((1, tc, hb * HEAD), lambda b, h, c: (b, c, v_off + h)),
                  pl.BlockSpec((1, tc, hb * HEAD), lambda b, h, c: (b, c, h)),
                  pl.BlockSpec((1, 1, tc, hb), lambda b, h, c: (b, h, c, 0)),
                  pl.BlockSpec((1, 1, cpb, hb, chunk), lambda b, h, c: (b, h, c, 0, 0)),
                  pl.BlockSpec((1, 1, tc, hb), lambda b, h, c: (b, h, c, 0)),
                  pl.BlockSpec((1, hb, HEAD, HEAD), lambda b, h, c: (b, h, 0, 0)),
                  pl.BlockSpec((1, HEAD), lambda b, h, c: (0, 0))],
        out_specs=[pl.BlockSpec((1, tc, hb * HEAD), lambda b, h, c: (b, c, h)),
                   pl.BlockSpec((1, hb, HEAD, HEAD), lambda b, h, c: (b, h, 0, 0))],
        out_shape=[jax.ShapeDtypeStruct((n_b, t_len, VAL_DIM), BF16),
                   jax.ShapeDtypeStruct((n_b, V_HEADS, HEAD, HEAD), F32)],
        scratch_shapes=[pltpu.VMEM((hb, HEAD, HEAD), F32)],
        compiler_params=_cparams(("arbitrary", "arbitrary", "arbitrary")),
        name="gated_delta_rule",
    )(qkv, qkv, qkv, z, g_col, g_row, b_col, s0, norm_w)


def _tile(m, pref):
    t = min(m, pref)
    assert m % t == 0, (m, t)
    return t


def _trunk(x, s5_re0, s5_im0, conv0, s0, p):
    n_b, t_len, _ = x.shape
    m = n_b * t_len
    tm = _tile(m, 512)
    tm_wide = _tile(m, 1024)

    tt = _tile(t_len, 256)
    multi_seq = SUBLANES * t_len <= 256 and n_b % SUBLANES == 0
    n_blk, n_state = (n_b // SUBLANES, SUBLANES) if multi_seq else (n_b, 1)
    if multi_seq:
        tt = SUBLANES * t_len
    g, sr, si = _s5_call(x.reshape(n_blk, n_state * t_len, D_MODEL), p['nm0'], p['s5_d'], p['wb'], p['wc'],
                         p['a_re'], p['a_im'], s5_re0.reshape(n_blk, n_state, S5_CH),
                         s5_im0.reshape(n_blk, n_state, S5_CH), tt, multi_seq)
    x2 = x.reshape(m, D_MODEL)
    x2 = _glu_call(g.reshape(m, D_MODEL), p['w_glu'], x2, tm_wide, 512)
    x2 = _ffn_call(x2, p['nf0'], p['w_up'], p['w_down'], 0, p['nfin'], tm_wide, 512, False)

    qkv, z, gg, beta, tails = _inproj_call(
        x2, p['nm1'], p['w_qkvz'], p['w_ba'], conv0, p['conv_w'], p['a_log'], p['dt_bias'],
        t_len, tm)
    tpb = t_len // min(t_len, tm)
    new_conv = tails[tpb - 1::tpb]
    chunk = min(t_len, 64)
    cpb = _tile(t_len // chunk, 4)
    hb = 8 if cpb > 1 else 16
    og, s_fin = _delta_call(qkv.reshape(n_b, t_len, -1), z.reshape(n_b, t_len, VAL_DIM),
                            gg.reshape(n_b, t_len, V_HEADS), beta.reshape(n_b, t_len, V_HEADS),
                            s0, p['gdn_norm_w'], chunk, cpb, hb)
    x2 = _matmul_res_call(og.reshape(m, VAL_DIM), p['w_out'], x2, tm_wide, 512)
    y = _ffn_call(x2, p['nf1'], p['w_up'], p['w_down'], 1, p['nfin'], tm_wide, 512, True)

    return (y.reshape(n_b, t_len, D_MODEL),
            sr.reshape(1, n_b, S5_GROUPS, S5_STATE), si.reshape(1, n_b, S5_GROUPS, S5_STATE),
            new_conv[None], s_fin[None])


def kernel(x_prompt, x_sample, state_s5_re, state_s5_im, state_gdn_conv, state_gdn_s,
           norm_mix, norm_ffn, norm_final,
           s5_lam_re, s5_lam_im, s5_log_dt, s5_b_re, s5_b_im, s5_c_re, s5_c_im,
           s5_d, s5_w_glu,
           gdn_w_in, gdn_conv_w, gdn_a_log, gdn_dt_bias, gdn_norm_w, gdn_w_out,
           ffn_w_up, ffn_w_down):
    wb, wc, a_re, a_im = _s5_params(s5_lam_re[0], s5_lam_im[0], s5_log_dt[0], s5_b_re[0], s5_b_im[0],
                              s5_c_re[0], s5_c_im[0])
    w_in = gdn_w_in[0]
    ba = w_in[:, CONV_DIM + VAL_DIM:]
    w_ba = jnp.concatenate([ba, jnp.zeros((D_MODEL, LANES - ba.shape[1]), w_in.dtype)], axis=1)
    row = lambda a: a.reshape(1, -1).astype(F32)
    p = dict(
        nm0=row(norm_mix[0]), nm1=row(norm_mix[1]), nf0=row(norm_ffn[0]), nf1=row(norm_ffn[1]),
        nfin=row(norm_final), s5_d=row(s5_d[0]), wb=wb, wc=wc, a_re=a_re, a_im=a_im,
        w_glu=s5_w_glu[0].astype(BF16),
        w_up=ffn_w_up.astype(BF16), w_down=ffn_w_down.astype(BF16),
        w_qkvz=w_in.astype(BF16), w_ba=w_ba.astype(BF16),
        conv_w=gdn_conv_w[0].astype(F32), a_log=row(gdn_a_log[0]), dt_bias=row(gdn_dt_bias[0]),
        gdn_norm_w=row(gdn_norm_w[0]), w_out=gdn_w_out[0].astype(BF16),
    )
    n_bp = x_prompt.shape[0]
    z_state = jnp.zeros((n_bp, S5_GROUPS, S5_STATE), F32)
    z_conv = jnp.zeros((n_bp, CONV_W - 1, CONV_DIM), F32)
    z_s = jnp.zeros((n_bp, V_HEADS, HEAD, HEAD), F32)

    yp, p_re, p_im, p_conv, p_s = _trunk(x_prompt, z_state, z_state, z_conv, z_s, p)
    ys, s_re, s_im, s_conv, s_s = _trunk(x_sample, state_s5_re[0], state_s5_im[0],
                                         state_gdn_conv[0], state_gdn_s[0], p)
    return (yp, ys, p_re, p_im, p_conv, p_s, s_re, s_im, s_conv, s_s)
```

```python
import functools
import math

import jax
import jax.numpy as jnp
from jax import lax
from jax.experimental import pallas as pl
from jax.experimental.pallas import tpu as pltpu

F32 = jnp.float32
BF16 = jnp.bfloat16

RMS_EPS = 1e-6
L2_EPS = 1e-6
D_MODEL = 2048
S5_GROUP = 16
S5_STATE = 64
S5_GROUPS = D_MODEL // S5_GROUP
S5_CH = S5_GROUPS * S5_STATE
HEAD = 128
QK_HEADS = D_MODEL // HEAD
V_HEADS = 2 * QK_HEADS
KEY_DIM = QK_HEADS * HEAD
VAL_DIM = V_HEADS * HEAD
CONV_DIM = 2 * KEY_DIM + VAL_DIM
CONV_W = 4

SUBLANES = 8
LANES = 128
MXU_K = 256
S5_KB = D_MODEL // MXU_K
S5_CB = S5_CH // S5_KB
VMEM_LIMIT = 56 * 1024 * 1024


def _cparams(sem):
    return pltpu.CompilerParams(dimension_semantics=sem, vmem_limit_bytes=VMEM_LIMIT)


def _rms(x, w):
    return x * lax.rsqrt(jnp.mean(x * x, axis=-1, keepdims=True) + RMS_EPS) * w


def _s5_body(x_ref, nw_ref, d_ref, wb_ref, wc_ref, coef_ref, perm_ref, permt_ref, s0r_ref, s0i_ref,
             g_ref, sr_ref, si_ref, car_ref, h_ref, bu_ref, st_ref, *, tt, multi_seq):
    t = pl.program_id(1)
    seg = tt // SUBLANES

    @pl.when(t == 0)
    def _():
        car_ref[0] = jnp.broadcast_to(s0r_ref[0], (SUBLANES, S5_CH))
        car_ref[1] = jnp.broadcast_to(s0i_ref[0], (SUBLANES, S5_CH))

    h = _rms(x_ref[0], nw_ref[...])
    h_ref[...] = h
    hp = jnp.dot(perm_ref[...], h.astype(BF16), preferred_element_type=F32).astype(BF16)
    row0 = lax.broadcasted_iota(jnp.int32, (SUBLANES, S5_CB), 0) == 0

    def step(ar, ai, sr, si, xr, xi):
        return ar * sr - ai * si + xr, ar * si + ai * sr + xi

    def project_in(kb):
        ks = slice(kb * MXU_K, (kb + 1) * MXU_K)
        bu_ref[kb % 2] = jnp.dot(hp[:, ks], wb_ref[kb], preferred_element_type=F32)

    def project_out(kb):
        ks = slice(kb * MXU_K, (kb + 1) * MXU_K)
        yp = jnp.dot(st_ref[kb % 2].astype(BF16), wc_ref[kb], preferred_element_type=F32)
        y_hi = yp.astype(BF16)
        y_lo = (yp - y_hi.astype(F32)).astype(BF16)
        y2 = jnp.dot(permt_ref[...], jnp.concatenate([y_hi, y_lo], axis=1),
                     preferred_element_type=F32)
        y = y2[:, :MXU_K] + y2[:, MXU_K:] + d_ref[:, ks] * h_ref[:, ks]
        g_ref[0, :, ks] = jax.nn.gelu(y).astype(BF16)

    def scan(kb):
        cs = slice(kb * S5_CB, (kb + 1) * S5_CB)
        buf = kb % 2

        def bu_rows(j):
            rows = slice(j * SUBLANES, (j + 1) * SUBLANES)
            return bu_ref[buf, rows, 0:S5_CB], bu_ref[buf, rows, S5_CB:2 * S5_CB]

        ar = coef_ref[4, 0, :, cs]
        ai = coef_ref[4, 1, :, cs]
        if multi_seq:
            sr = car_ref[0, :, cs]
            si = car_ref[1, :, cs]
            for j in range(seg):
                sr, si = step(ar, ai, sr, si, *bu_rows(j))
                rows = slice(j * SUBLANES, (j + 1) * SUBLANES)
                st_ref[buf, rows, 0:S5_CB] = sr
                st_ref[buf, rows, S5_CB:2 * S5_CB] = si
            car_ref[0, :, cs] = sr
            car_ref[1, :, cs] = si
            return
        er, ei = bu_rows(0)
        for j in range(1, seg):
            er, ei = step(ar, ai, er, ei, *bu_rows(j))
        for lvl, sh in enumerate((1, 2, 4)):
            cr_ = coef_ref[lvl, 0, :, cs]
            ci_ = coef_ref[lvl, 1, :, cs]
            pr = pltpu.roll(er, sh, 0)
            pi = pltpu.roll(ei, sh, 0)
            er, ei = er + cr_ * pr - ci_ * pi, ei + cr_ * pi + ci_ * pr
        c0r = car_ref[0, :, cs]
        c0i = car_ref[1, :, cs]
        cr_ = coef_ref[3, 0, :, cs]
        ci_ = coef_ref[3, 1, :, cs]
        fr = er + cr_ * c0r - ci_ * c0i
        fi = ei + cr_ * c0i + ci_ * c0r
        car_ref[0, :, cs] = jnp.broadcast_to(fr[SUBLANES - 1:SUBLANES], (SUBLANES, S5_CB))
        car_ref[1, :, cs] = jnp.broadcast_to(fi[SUBLANES - 1:SUBLANES], (SUBLANES, S5_CB))
        sr = jnp.where(row0, c0r, pltpu.roll(fr, 1, 0))
        si = jnp.where(row0, c0i, pltpu.roll(fi, 1, 0))
        for j in range(seg):
            sr, si = step(ar, ai, sr, si, *bu_rows(j))
            rows = slice(j * SUBLANES, (j + 1) * SUBLANES)
            st_ref[buf, rows, 0:S5_CB] = sr
            st_ref[buf, rows, S5_CB:2 * S5_CB] = si

    project_in(0)
    for kb in range(S5_KB):
        if kb + 1 < S5_KB:
            project_in(kb + 1)
        if kb >= 1:
            project_out(kb - 1)
        scan(kb)
    project_out(S5_KB - 1)

    @pl.when(t == pl.num_programs(1) - 1)
    def _():
        n_state = SUBLANES if multi_seq else 1
        sr_ref[0] = car_ref[0, 0:n_state, :]
        si_ref[0] = car_ref[1, 0:n_state, :]


def _s5_call(x, nw, d, wb, wc, a_re, a_im, s0r, s0i, tt, multi_seq):
    n_b, t_len, _ = x.shape
    seg = tt // SUBLANES
    n_state = s0r.shape[1]
    assert n_state == (SUBLANES if multi_seq else 1) and (not multi_seq or t_len == tt)
    coef = _s5_coef(a_re, a_im, seg)
    p_idx = jnp.arange(tt)
    src = (p_idx % SUBLANES) * seg + p_idx // SUBLANES
    perm = (src[:, None] == p_idx[None, :]).astype(BF16)
    const = lambda *shape: pl.BlockSpec(shape, lambda b, t: (0,) * len(shape))
    state = pl.BlockSpec((1, n_state, S5_CH), lambda b, t: (b, 0, 0))
    return pl.pallas_call(
        functools.partial(_s5_body, tt=tt, multi_seq=multi_seq),
        grid=(n_b, t_len // tt),
        in_specs=[pl.BlockSpec((1, tt, D_MODEL), lambda b, t: (b, t, 0)),
                  const(1, D_MODEL), const(1, D_MODEL),
                  const(S5_KB, MXU_K, 2 * S5_CB), const(S5_KB, 2 * S5_CB, MXU_K),
                  const(5, 2, SUBLANES, S5_CH), const(tt, tt), const(tt, tt), state, state],
        out_specs=[pl.BlockSpec((1, tt, D_MODEL), lambda b, t: (b, t, 0)), state, state],
        out_shape=[jax.ShapeDtypeStruct((n_b, t_len, D_MODEL), BF16),
                   jax.ShapeDtypeStruct((n_b, n_state, S5_CH), F32),
                   jax.ShapeDtypeStruct((n_b, n_state, S5_CH), F32)],
        scratch_shapes=[pltpu.VMEM((2, SUBLANES, S5_CH), F32),
                        pltpu.VMEM((tt, D_MODEL), F32),
                        pltpu.VMEM((2, tt, 2 * S5_CB), F32),
                        pltpu.VMEM((2, tt, 2 * S5_CB), F32)],
        compiler_params=_cparams(("arbitrary", "arbitrary")),
        name="s5_mixer",
    )(x, nw, d, wb, wc, coef, perm, perm.T, s0r, s0i)


def _s5_coef(a_re, a_im, seg):
    assert seg & (seg - 1) == 0, seg
    ar, ai = a_re, a_im
    for _ in range(int(math.log2(seg))):
        ar, ai = ar * ar - ai * ai, 2.0 * ar * ai
    pw_r, pw_i = [ar], [ai]
    for _ in range(SUBLANES - 1):
        pr, pi = pw_r[-1], pw_i[-1]
        pw_r.append(pr * ar - pi * ai)
        pw_i.append(pr * ai + pi * ar)
    row = jnp.arange(SUBLANES)[:, None]
    levels = []
    for sh in (1, 2, 4):
        m = (row >= sh).astype(F32)
        levels.append(jnp.stack([m * pw_r[sh - 1], m * pw_i[sh - 1]]))
    levels.append(jnp.stack([jnp.concatenate(pw_r, axis=0), jnp.concatenate(pw_i, axis=0)]))
    ones = jnp.ones((SUBLANES, 1), F32)
    levels.append(jnp.stack([ones * a_re, ones * a_im]))
    return jnp.stack(levels)


def _s5_params(lam_re, lam_im, log_dt, b_re, b_im, c_re, c_im):
    lr = jnp.minimum(lam_re.astype(F32), -1e-4)
    li = lam_im.astype(F32)
    dt = jnp.exp(log_dt.astype(F32))[:, None]
    mag = jnp.exp(lr * dt)
    ang = li * dt
    a_re = mag * jnp.cos(ang)
    a_im = mag * jnp.sin(ang)
    nr = a_re - 1.0
    den = lr * lr + li * li
    k_re = (nr * lr + a_im * li) / den
    k_im = (a_im * lr - nr * li) / den
    br = b_re.astype(F32)
    bi = b_im.astype(F32)
    bb_re = k_re[..., None] * br - k_im[..., None] * bi
    bb_im = k_re[..., None] * bi + k_im[..., None] * br

    gl = S5_GROUPS // S5_KB
    eye = jnp.eye(gl, dtype=F32)

    def blockdiag_in(bb):
        bbk = bb.reshape(S5_KB, gl, S5_STATE, S5_GROUP)
        w = jnp.einsum('kgph,gf->kghfp', bbk, eye)
        return w.reshape(S5_KB, gl * S5_GROUP, gl * S5_STATE)

    def blockdiag_out(c):
        ck = c.reshape(S5_KB, gl, S5_GROUP, S5_STATE)
        w = jnp.einsum('kghp,gf->kgpfh', ck, eye)
        return w.reshape(S5_KB, gl * S5_STATE, gl * S5_GROUP)

    wb = jnp.concatenate([blockdiag_in(bb_re), blockdiag_in(bb_im)], axis=2).astype(BF16)
    wc = jnp.concatenate([blockdiag_out(c_re.astype(F32)),
                          -blockdiag_out(c_im.astype(F32))], axis=1).astype(BF16)

    return wb, wc, a_re.reshape(1, S5_CH), a_im.reshape(1, S5_CH)


def _glu_body(g_ref, wa_ref, wg_ref, x_ref, o_ref):
    g = g_ref[...]
    za = jnp.dot(g, wa_ref[...], preferred_element_type=F32)
    zg = jnp.dot(g, wg_ref[...], preferred_element_type=F32)
    o_ref[...] = x_ref[...] + za * jax.nn.sigmoid(zg)


def _glu_call(g, w_glu, x, tm, tn):
    m = g.shape[0]
    nj = D_MODEL // tn
    return pl.pallas_call(
        _glu_body,
        grid=(m // tm, nj),
        in_specs=[pl.BlockSpec((tm, D_MODEL), lambda i, j: (i, 0)),
                  pl.BlockSpec((D_MODEL, tn), lambda i, j: (0, j)),
                  pl.BlockSpec((D_MODEL, tn), lambda i, j: (0, j + nj)),
                  pl.BlockSpec((tm, tn), lambda i, j: (i, j))],
        out_specs=pl.BlockSpec((tm, tn), lambda i, j: (i, j)),
        out_shape=jax.ShapeDtypeStruct((m, D_MODEL), F32),
        compiler_params=_cparams(("parallel", "arbitrary")),
        name="glu_residual",
    )(g, w_glu, w_glu, x)


def _ffn_body(x_ref, nw_ref, wu_ref, wd_ref, fw_ref, o_ref, h_ref, *, final_norm):
    f = pl.program_id(1)

    @pl.when(f == 0)
    def _():
        x = x_ref[...]
        h_ref[...] = _rms(x, nw_ref[...]).astype(BF16)
        o_ref[...] = x

    a = jnp.maximum(jnp.dot(h_ref[...], wu_ref[...], preferred_element_type=F32), 0.0)
    o_ref[...] += jnp.dot((a * a).astype(BF16), wd_ref[...], preferred_element_type=F32)

    if final_norm:
        @pl.when(f == pl.num_programs(1) - 1)
        def _():
            o_ref[...] = _rms(o_ref[...], fw_ref[...])


def _ffn_call(x, nw, w_up, w_down, layer, fw, tm, tf, final_norm):
    m = x.shape[0]
    d_ff = w_up.shape[2]
    return pl.pallas_call(
        functools.partial(_ffn_body, final_norm=final_norm),
        grid=(m // tm, d_ff // tf),
        in_specs=[pl.BlockSpec((tm, D_MODEL), lambda i, f: (i, 0)),
                  pl.BlockSpec((1, D_MODEL), lambda i, f: (0, 0)),
                  pl.BlockSpec((None, D_MODEL, tf), lambda i, f: (layer, 0, f)),
                  pl.BlockSpec((None, tf, D_MODEL), lambda i, f: (layer, f, 0)),
                  pl.BlockSpec((1, D_MODEL), lambda i, f: (0, 0))],
        out_specs=pl.BlockSpec((tm, D_MODEL), lambda i, f: (i, 0)),
        out_shape=jax.ShapeDtypeStruct((m, D_MODEL), F32),
        scratch_shapes=[pltpu.VMEM((tm, D_MODEL), BF16)],
        compiler_params=_cparams(("parallel", "arbitrary")),
        name="ffn_residual",
    )(x, nw, w_up, w_down, fw)


def _matmul_res_body(a_ref, w_ref, x_ref, o_ref):
    o_ref[...] = x_ref[...] + jnp.dot(a_ref[...], w_ref[...], preferred_element_type=F32)


def _matmul_res_call(a, w, x, tm, tn):
    m, k = a.shape
    n = w.shape[1]
    return pl.pallas_call(
        _matmul_res_body,
        grid=(m // tm, n // tn),
        in_specs=[pl.BlockSpec((tm, k), lambda i, j: (i, 0)),
                  pl.BlockSpec((k, tn), lambda i, j: (0, j)),
                  pl.BlockSpec((tm, tn), lambda i, j: (i, j))],
        out_specs=pl.BlockSpec((tm, tn), lambda i, j: (i, j)),
        out_shape=jax.ShapeDtypeStruct((m, n), F32),
        compiler_params=_cparams(("parallel", "arbitrary")),
        name="matmul_residual",
    )(a, w, x)


INPROJ_TN = 1024
N_CONV_TILES = CONV_DIM // INPROJ_TN
N_Z_TILES = VAL_DIM // INPROJ_TN
N_QK_TILES = 2 * KEY_DIM // INPROJ_TN
N_Q_TILES = KEY_DIM // INPROJ_TN
CONV_ROWS = 64


def _inproj_body(x_ref, nw_ref, w_ref, wba_ref, cprev_ref, cw_ref, alog_ref, dtb_ref,
                 qkv_ref, z_ref, g_ref, beta_ref, nc_ref, hs_ref, hist_ref, ra_ref, rb_ref,
                 *, tm, seq_rows, tiles_per_seq, gate_group):
    n_seq = tm // seq_rows
    i = pl.program_id(0)
    j = pl.program_id(1)
    tn = INPROJ_TN
    n_sub = tn // MXU_K
    tail = CONV_W - 1
    pad = SUBLANES

    @pl.when(j == 0)
    def _():
        hs_ref[...] = _rms(x_ref[...], nw_ref[...]).astype(BF16)
        ba = jnp.dot(hs_ref[...], wba_ref[...], preferred_element_type=F32)
        zz = ba[:, V_HEADS:2 * V_HEADS] + dtb_ref[...]
        softplus = jnp.maximum(zz, 0.0) + jnp.log1p(jnp.exp(-jnp.abs(zz)))
        g_all = -jnp.exp(alog_ref[...]) * softplus
        beta_all = jax.nn.sigmoid(ba[:, 0:V_HEADS])
        for gi in range(V_HEADS // gate_group):
            g_ref[gi] = g_all[:, gi * gate_group:(gi + 1) * gate_group]
            beta_ref[gi] = beta_all[:, gi * gate_group:(gi + 1) * gate_group]

    @pl.when((i % tiles_per_seq == 0) & (j == 0))
    def _():
        hist_ref[...] = jnp.concatenate(
            [jnp.zeros((pad - tail, CONV_DIM), F32), cprev_ref[0]], axis=0)

    def project(r_ref, half, s):
        c0 = half * tn + s * MXU_K
        r_ref[s, pad:pad + tm, :] = jnp.dot(hs_ref[...], w_ref[:, c0:c0 + MXU_K],
                                            preferred_element_type=F32)

    def conv_rows(win, c0):
        acc = win[pad:] * cw_ref[tail:tail + 1, c0:c0 + MXU_K]
        for d in range(1, CONV_W):
            shifted = pltpu.roll(win, d, 0)[pad:]
            acc = acc + shifted * cw_ref[tail - d:tail - d + 1, c0:c0 + MXU_K]
        return acc

    def epilogue(r_ref, tile, out_half, s):
        c0 = tile * tn + s * MXU_K
        o0 = out_half * tn + s * MXU_K
        l2 = tile < N_QK_TILES
        scale = HEAD ** -0.5 if tile < N_Q_TILES else 1.0

        def finish(acc, rows):
            c = acc * jax.nn.sigmoid(acc)
            if l2:
                for hd in range(MXU_K // HEAD):
                    ch = c[:, hd * HEAD:(hd + 1) * HEAD]
                    ch = ch * (lax.rsqrt(jnp.sum(ch * ch, axis=-1, keepdims=True) + L2_EPS) * scale)
                    qkv_ref[rows, o0 + hd * HEAD:o0 + (hd + 1) * HEAD] = ch.astype(BF16)
            else:
                qkv_ref[rows, o0:o0 + MXU_K] = c.astype(BF16)

        r_ref[s, 0:pad, :] = hist_ref[:, c0:c0 + MXU_K]
        hist_ref[:, c0:c0 + MXU_K] = r_ref[s, tm:tm + pad, :]
        for sq in range(n_seq):
            end = pad + (sq + 1) * seq_rows
            nc_ref[sq, :, c0:c0 + MXU_K] = r_ref[s, end - tail:end, :]
        for r0 in range(0, tm, CONV_ROWS):
            rc = min(CONV_ROWS, tm - r0)
            finish(conv_rows(r_ref[s, r0:r0 + rc + pad, :], c0), slice(r0, r0 + rc))
        for sq in range(1, n_seq):
            r0 = sq * seq_rows
            win = jnp.concatenate([jnp.zeros((pad - tail, MXU_K), F32), cprev_ref[sq, :, c0:c0 + MXU_K],
                                   r_ref[s, pad + r0:pad + r0 + pad, :]], axis=0)
            finish(conv_rows(win, c0), slice(r0, r0 + pad))

    def project_z(half, s):
        c0 = half * tn + s * MXU_K
        z_ref[:, c0:c0 + MXU_K] = jnp.dot(hs_ref[...], w_ref[:, c0:c0 + MXU_K],
                                          preferred_element_type=F32).astype(BF16)

    def step_body(step):
        cur, prev = step % 2, (step - 1) % 2
        for half, r_ref in ((0, ra_ref), (1, rb_ref)):
            for s in range(n_sub):
                if step < n_conv_steps:
                    project(r_ref.at[cur], half, s)
                else:
                    project_z(half, s)
                if 0 < step <= n_conv_steps:
                    epilogue(r_ref.at[prev], 2 * (step - 1) + half, half, s)

    n_conv_steps = N_CONV_TILES // 2
    for step in range(n_conv_steps + 1):
        pl.when(j == step)(functools.partial(step_body, step))
    pl.when(j > n_conv_steps)(functools.partial(step_body, n_conv_steps + 1))


def _inproj_call(x, nw, w_qkvz, w_ba, conv_prev, conv_w, a_log, dt_bias, t_len, tm, gate_group):
    m = x.shape[0]
    n_grp = V_HEADS // gate_group
    tn2 = 2 * INPROJ_TN
    n_cs = N_CONV_TILES // 2
    n_zs = N_Z_TILES // 2
    seq_rows = min(t_len, tm)
    n_seq = tm // seq_rows
    tpb = t_len // seq_rows
    return pl.pallas_call(
        functools.partial(_inproj_body, tm=tm, seq_rows=seq_rows, tiles_per_seq=tpb,
                          gate_group=gate_group),
        grid=(m // tm, n_cs + n_zs),
        in_specs=[pl.BlockSpec((tm, D_MODEL), lambda i, j: (i, 0)),
                  pl.BlockSpec((1, D_MODEL), lambda i, j: (0, 0)),
                  pl.BlockSpec((D_MODEL, tn2), lambda i, j: (0, j)),
                  pl.BlockSpec((D_MODEL, LANES), lambda i, j: (0, 0)),
                  pl.BlockSpec((n_seq, CONV_W - 1, CONV_DIM), lambda i, j: (i // tpb, 0, 0)),
                  pl.BlockSpec((CONV_W, CONV_DIM), lambda i, j: (0, 0)),
                  pl.BlockSpec((1, V_HEADS), lambda i, j: (0, 0)),
                  pl.BlockSpec((1, V_HEADS), lambda i, j: (0, 0))],
        out_specs=[pl.BlockSpec((tm, tn2), lambda i, j: (i, jnp.clip(j - 1, 0, n_cs - 1))),
                   pl.BlockSpec((tm, tn2), lambda i, j: (i, jnp.maximum(j - n_cs, 0))),
                   pl.BlockSpec((n_grp, tm, gate_group), lambda i, j: (0, i, 0)),
                   pl.BlockSpec((n_grp, tm, gate_group), lambda i, j: (0, i, 0)),
                   pl.BlockSpec((n_seq, CONV_W - 1, CONV_DIM), lambda i, j: (i, 0, 0))],
        out_shape=[jax.ShapeDtypeStruct((m, CONV_DIM), BF16),
                   jax.ShapeDtypeStruct((m, VAL_DIM), BF16),
                   jax.ShapeDtypeStruct((n_grp, m, gate_group), F32),
                   jax.ShapeDtypeStruct((n_grp, m, gate_group), F32),
                   jax.ShapeDtypeStruct((m // seq_rows, CONV_W - 1, CONV_DIM), F32)],
        scratch_shapes=[pltpu.VMEM((tm, D_MODEL), BF16),
                        pltpu.VMEM((SUBLANES, CONV_DIM), F32),
                        pltpu.VMEM((2, INPROJ_TN // MXU_K, tm + SUBLANES, MXU_K), F32),
                        pltpu.VMEM((2, INPROJ_TN // MXU_K, tm + SUBLANES, MXU_K), F32)],
        compiler_params=_cparams(("arbitrary", "arbitrary")),
        name="gdn_inproj_conv",
    )(x, nw, w_qkvz, w_ba, conv_prev, conv_w, a_log, dt_bias)


INV_BASE = 16


def _mm16(a, b):
    return jnp.dot(a.astype(BF16), b.astype(BF16), preferred_element_type=F32)


def _tri_inverse_all(nmats, ri, ci, chunk):
    base = min(INV_BASE, chunk)
    sh = int(math.log2(base))
    same = (ri >> sh) == (ci >> sh)
    eye = (ri == ci).astype(F32)
    ms = [jnp.where(same, -n, 0.0) for n in nmats]
    ps = [eye + m for m in ms]
    for _ in range(sh - 1):
        ms = [_mm16(m, m) for m in ms]
        ps = [p + _mm16(p, m) for p, m in zip(ps, ms)]
    size = base
    while size < chunk:
        sh += 1
        size *= 2
        same_next = (ri >> sh) == (ci >> sh)
        sel = same_next & jnp.logical_not(same)
        ts = [_mm16(p, jnp.where(sel, n, 0.0)) for p, n in zip(ps, nmats)]
        ps = [p - _mm16(t, p) for p, t in zip(ps, ts)]
        same = same_next
    return ps


def _delta_body(q_ref, k_ref, v_ref, z_ref, gc_ref, bc_ref, s0_ref, nw_ref,
                o_ref, sf_ref, s_ref, *, chunk, cpb, hb):
    c_idx = pl.program_id(2)

    @pl.when(c_idx == 0)
    def _():
        s_ref[...] = s0_ref[0]

    ri = lax.broadcasted_iota(jnp.int32, (chunk, chunk), 0)
    ci = lax.broadcasted_iota(jnp.int32, (chunk, chunk), 1)
    causal = ri >= ci
    strict = ri > ci
    tril = causal.astype(F32)
    triu = (ri <= ci).astype(F32)
    hi = lax.Precision.HIGHEST
    nt = (((1,), (1,)), ((), ()))
    tn = (((0,), (0,)), ((), ()))
    hq = hb // 2

    kqk, pre = {}, {}
    for c in range(cpb):
        rows = slice(c * chunk, (c + 1) * chunk)
        for j in range(hq):
            k16 = k_ref[0, rows, j * HEAD:(j + 1) * HEAD]
            q16 = q_ref[0, rows, j * HEAD:(j + 1) * HEAD]
            kqk[c, j] = lax.dot_general(jnp.concatenate([k16, q16], axis=0), k16, nt,
                                        preferred_element_type=F32)
    nmats, keys = [], []
    for c in range(cpb):
        rows = slice(c * chunk, (c + 1) * chunk)
        g_raw = gc_ref[0, rows, :]
        gcol = jnp.dot(tril, g_raw, precision=hi, preferred_element_type=F32)
        grow = lax.dot_general(g_raw, triu, tn, precision=hi, preferred_element_type=F32)
        bcol = bc_ref[0, rows, :]
        for hh in range(hb):
            j = hh // 2
            k = k_ref[0, rows, j * HEAD:(j + 1) * HEAD].astype(F32)
            q = q_ref[0, rows, j * HEAD:(j + 1) * HEAD].astype(F32)
            v = v_ref[0, rows, hh * HEAD:(hh + 1) * HEAD].astype(F32)
            gc = gcol[:, hh:hh + 1]
            gr = grow[hh:hh + 1, :]
            beta = bcol[:, hh:hh + 1]
            decay = jnp.exp(jnp.where(causal, gc - gr, -jnp.inf))
            egc = jnp.exp(gc)
            g_last = gc[chunk - 1:chunk, :]
            kbeta = k * beta
            kk = kqk[c, j][:chunk]
            qk = kqk[c, j][chunk:]
            nmats.append(jnp.where(strict, (kk * beta) * decay, 0.0))
            keys.append((c, hh))
            pre[c, hh] = dict(
                rhs=jnp.concatenate([v * beta, kbeta * egc], axis=1).astype(BF16),
                qg=(q * egc).astype(BF16),
                intra=(qk * decay).astype(BF16),
                kd=(k * jnp.exp(g_last - gc)).astype(BF16),
                sdec=jnp.exp(g_last))
    tinv = _tri_inverse_all(nmats, ri, ci, chunk)
    for key, t in zip(keys, tinv):
        pre[key]['sol'] = jnp.dot(t.astype(BF16), pre[key]['rhs'], preferred_element_type=F32)

    for c in range(cpb):
        rows = slice(c * chunk, (c + 1) * chunk)
        ws = []
        for hh in range(hb):
            d = pre[c, hh]
            wq = jnp.concatenate([d['sol'][:, HEAD:].astype(BF16), d['qg']], axis=0)
            ws.append(jnp.dot(wq, s_ref[hh].astype(BF16), preferred_element_type=F32))
        for hh in range(hb):
            d = pre[c, hh]
            v16 = (d['sol'][:, :HEAD] - ws[hh][:chunk]).astype(BF16)
            o = ws[hh][chunk:] + jnp.dot(d['intra'], v16, preferred_element_type=F32)
            s_ref[hh] = s_ref[hh] * d['sdec'] + lax.dot_general(d['kd'], v16, tn,
                                                                preferred_element_type=F32)
            o = o * lax.rsqrt(jnp.mean(o * o, axis=-1, keepdims=True) + RMS_EPS) * nw_ref[...]
            zz = z_ref[0, rows, hh * HEAD:(hh + 1) * HEAD]
            o_ref[0, rows, hh * HEAD:(hh + 1) * HEAD] = (o * (zz * jax.nn.sigmoid(zz))).astype(BF16)

    @pl.when(c_idx == pl.num_programs(2) - 1)
    def _():
        sf_ref[0] = s_ref[...]


def _delta_call(qkv, z, g, beta, s0, norm_w, chunk, cpb, hb):
    n_b, t_len, _ = qkv.shape
    nhg = V_HEADS // hb
    tc = chunk * cpb
    n_tc = t_len // tc
    assert g.shape == (nhg, n_b * t_len, hb), g.shape
    hq = hb // 2
    assert KEY_DIM % (hb * HEAD) == 0, hb
    q_off = 0
    k_off = KEY_DIM // (hq * HEAD)
    v_off = 2 * KEY_DIM // (hb * HEAD)
    return pl.pallas_call(
        functools.partial(_delta_body, chunk=chunk, cpb=cpb, hb=hb),
        grid=(n_b, nhg, t_len // tc),
        in_specs=[pl.BlockSpec((1, tc, hq * HEAD), lambda b, h, c: (b, c, q_off + h)),
                  pl.BlockSpec((1, tc, hq * HEAD), lambda b, h, c: (b, c, k_off + h)),
                  pl.BlockSpec((1, tc, hb * HEAD), lambda b, h, c: (b, c, v_off + h)),
                  pl.BlockSpec((1, tc, hb * HEAD), lambda b, h, c: (b, c, h)),
                  pl.BlockSpec((1, tc, hb), lambda b, h, c: (h, b * n_tc + c, 0)),
                  pl.BlockSpec((1, tc, hb), lambda b, h, c: (h, b * n_tc + c, 0)),
                  pl.BlockSpec((1, hb, HEAD, HEAD), lambda b, h, c: (b, h, 0, 0)),
                  pl.BlockSpec((1, HEAD), lambda b, h, c: (0, 0))],
        out_specs=[pl.BlockSpec((1, tc, hb * HEAD), lambda b, h, c: (b, c, h)),
                   pl.BlockSpec((1, hb, HEAD, HEAD), lambda b, h, c: (b, h, 0, 0))],
        out_shape=[jax.ShapeDtypeStruct((n_b, t_len, VAL_DIM), BF16),
                   jax.ShapeDtypeStruct((n_b, V_HEADS, HEAD, HEAD), F32)],
        scratch_shapes=[pltpu.VMEM((hb, HEAD, HEAD), F32)],
        compiler_params=_cparams(("arbitrary", "arbitrary", "arbitrary")),
        name="gated_delta_rule",
    )(qkv, qkv, qkv, z, g, beta, s0, norm_w)


def _tile(m, pref):
    t = min(m, pref)
    assert m % t == 0, (m, t)
    return t


def _trunk(x, s5_re0, s5_im0, conv0, s0, p):
    n_b, t_len, _ = x.shape
    m = n_b * t_len
    tm = _tile(m, 512)
    tm_wide = _tile(m, 1024)

    tt = _tile(t_len, 256)
    multi_seq = SUBLANES * t_len <= 256 and n_b % SUBLANES == 0
    n_blk, n_state = (n_b // SUBLANES, SUBLANES) if multi_seq else (n_b, 1)
    if multi_seq:
        tt = SUBLANES * t_len
    g, sr, si = _s5_call(x.reshape(n_blk, n_state * t_len, D_MODEL), p['nm0'], p['s5_d'], p['wb'], p['wc'],
                         p['a_re'], p['a_im'], s5_re0.reshape(n_blk, n_state, S5_CH),
                         s5_im0.reshape(n_blk, n_state, S5_CH), tt, multi_seq)
    x2 = x.reshape(m, D_MODEL)
    x2 = _glu_call(g.reshape(m, D_MODEL), p['w_glu'], x2, tm_wide, 512)
    x2 = _ffn_call(x2, p['nf0'], p['w_up'], p['w_down'], 0, p['nfin'], tm_wide, 512, False)

    chunk = min(t_len, 64)
    cpb = _tile(t_len // chunk, 4)
    hb = 8 if cpb > 1 else 16
    qkv, z, gg, beta, tails = _inproj_call(
        x2, p['nm1'], p['w_qkvz'], p['w_ba'], conv0, p['conv_w'], p['a_log'], p['dt_bias'],
        t_len, tm, hb)
    tpb = t_len // min(t_len, tm)
    new_conv = tails[tpb - 1::tpb]
    og, s_fin = _delta_call(qkv.reshape(n_b, t_len, -1), z.reshape(n_b, t_len, VAL_DIM),
                            gg, beta, s0, p['gdn_norm_w'], chunk, cpb, hb)
    x2 = _matmul_res_call(og.reshape(m, VAL_DIM), p['w_out'], x2, tm_wide, 512)
    y = _ffn_call(x2, p['nf1'], p['w_up'], p['w_down'], 1, p['nfin'], tm_wide, 512, True)

    return (y.reshape(n_b, t_len, D_MODEL),
            sr.reshape(1, n_b, S5_GROUPS, S5_STATE), si.reshape(1, n_b, S5_GROUPS, S5_STATE),
            new_conv[None], s_fin[None])


def kernel(x_prompt, x_sample, state_s5_re, state_s5_im, state_gdn_conv, state_gdn_s,
           norm_mix, norm_ffn, norm_final,
           s5_lam_re, s5_lam_im, s5_log_dt, s5_b_re, s5_b_im, s5_c_re, s5_c_im,
           s5_d, s5_w_glu,
           gdn_w_in, gdn_conv_w, gdn_a_log, gdn_dt_bias, gdn_norm_w, gdn_w_out,
           ffn_w_up, ffn_w_down):
    wb, wc, a_re, a_im = _s5_params(s5_lam_re[0], s5_lam_im[0], s5_log_dt[0], s5_b_re[0], s5_b_im[0],
                              s5_c_re[0], s5_c_im[0])
    w_in = gdn_w_in[0]
    ba = w_in[:, CONV_DIM + VAL_DIM:]
    w_ba = jnp.concatenate([ba, jnp.zeros((D_MODEL, LANES - ba.shape[1]), w_in.dtype)], axis=1)
    row = lambda a: a.reshape(1, -1).astype(F32)
    p = dict(
        nm0=row(norm_mix[0]), nm1=row(norm_mix[1]), nf0=row(norm_ffn[0]), nf1=row(norm_ffn[1]),
        nfin=row(norm_final), s5_d=row(s5_d[0]), wb=wb, wc=wc, a_re=a_re, a_im=a_im,
        w_glu=s5_w_glu[0].astype(BF16),
        w_up=ffn_w_up.astype(BF16), w_down=ffn_w_down.astype(BF16),
        w_qkvz=w_in.astype(BF16), w_ba=w_ba.astype(BF16),
        conv_w=gdn_conv_w[0].astype(F32), a_log=row(gdn_a_log[0]), dt_bias=row(gdn_dt_bias[0]),
        gdn_norm_w=row(gdn_norm_w[0]), w_out=gdn_w_out[0].astype(BF16),
    )
    n_bp = x_prompt.shape[0]
    z_state = jnp.zeros((n_bp, S5_GROUPS, S5_STATE), F32)
    z_conv = jnp.zeros((n_bp, CONV_W - 1, CONV_DIM), F32)
    z_s = jnp.zeros((n_bp, V_HEADS, HEAD, HEAD), F32)

    yp, p_re, p_im, p_conv, p_s = _trunk(x_prompt, z_state, z_state, z_conv, z_s, p)
    ys, s_re, s_im, s_conv, s_s = _trunk(x_sample, state_s5_re[0], state_s5_im[0],
                                         state_gdn_conv[0], state_gdn_s[0], p)
    return (yp, ys, p_re, p_im, p_conv, p_s, s_re, s_im, s_conv, s_s)
```

```python
import functools
import math

import jax
import jax.numpy as jnp
from jax import lax
from jax.experimental import pallas as pl
from jax.experimental.pallas import tpu as pltpu

F32 = jnp.float32
BF16 = jnp.bfloat16

RMS_EPS = 1e-6
L2_EPS = 1e-6
D_MODEL = 2048
S5_GROUP = 16
S5_STATE = 64
S5_GROUPS = D_MODEL // S5_GROUP
S5_CH = S5_GROUPS * S5_STATE
HEAD = 128
QK_HEADS = D_MODEL // HEAD
V_HEADS = 2 * QK_HEADS
KEY_DIM = QK_HEADS * HEAD
VAL_DIM = V_HEADS * HEAD
CONV_DIM = 2 * KEY_DIM + VAL_DIM
CONV_W = 4

SUBLANES = 8
LANES = 128
MXU_K = 256
S5_KB = D_MODEL // MXU_K
S5_CB = S5_CH // S5_KB
S5_SCAN_W = S5_CB
VMEM_LIMIT = 56 * 1024 * 1024


def _cparams(sem):
    return pltpu.CompilerParams(dimension_semantics=sem, vmem_limit_bytes=VMEM_LIMIT)


def _rms(x, w):
    return x * lax.rsqrt(jnp.mean(x * x, axis=-1, keepdims=True) + RMS_EPS) * w


def _s5_body(x_ref, nw_ref, d_ref, wb_ref, wc_ref, coef_ref, perm_ref, permt_ref, s0r_ref, s0i_ref,
             g_ref, sr_ref, si_ref, car_ref, h_ref, bu_ref, st_ref, *, tt, multi_seq):
    t = pl.program_id(1)
    seg = tt // SUBLANES

    @pl.when(t == 0)
    def _():
        car_ref[0] = jnp.broadcast_to(s0r_ref[0], (SUBLANES, S5_CH))
        car_ref[1] = jnp.broadcast_to(s0i_ref[0], (SUBLANES, S5_CH))

    h = _rms(x_ref[0], nw_ref[...])
    h_ref[...] = h
    hp = jnp.dot(perm_ref[...], h.astype(BF16), preferred_element_type=F32).astype(BF16)
    row0 = lax.broadcasted_iota(jnp.int32, (SUBLANES, S5_SCAN_W), 0) == 0

    def step(ar, ai, sr, si, xr, xi):
        return ar * sr - ai * si + xr, ar * si + ai * sr + xi

    def project_in(kb):
        ks = slice(kb * MXU_K, (kb + 1) * MXU_K)
        bu_ref[kb % 2] = jnp.dot(hp[:, ks], wb_ref[kb], preferred_element_type=F32)

    def project_out(kb):
        ks = slice(kb * MXU_K, (kb + 1) * MXU_K)
        yp = jnp.dot(st_ref[kb % 2].astype(BF16), wc_ref[kb], preferred_element_type=F32)
        y_hi = yp.astype(BF16)
        y_lo = (yp - y_hi.astype(F32)).astype(BF16)
        y2 = jnp.dot(permt_ref[...], jnp.concatenate([y_hi, y_lo], axis=1),
                     preferred_element_type=F32)
        y = y2[:, :MXU_K] + y2[:, MXU_K:] + d_ref[:, ks] * h_ref[:, ks]
        g_ref[0, :, ks] = jax.nn.gelu(y).astype(BF16)

    def scan(kb, part):
        lo = part * S5_SCAN_W
        re_cols = slice(lo, lo + S5_SCAN_W)
        im_cols = slice(S5_CB + lo, S5_CB + lo + S5_SCAN_W)
        cs = slice(kb * S5_CB + lo, kb * S5_CB + lo + S5_SCAN_W)
        buf = kb % 2

        def bu_rows(j):
            rows = slice(j * SUBLANES, (j + 1) * SUBLANES)
            return bu_ref[buf, rows, re_cols], bu_ref[buf, rows, im_cols]

        ar = coef_ref[4, 0, :, cs]
        ai = coef_ref[4, 1, :, cs]
        if multi_seq:
            sr = car_ref[0, :, cs]
            si = car_ref[1, :, cs]
            for j in range(seg):
                sr, si = step(ar, ai, sr, si, *bu_rows(j))
                rows = slice(j * SUBLANES, (j + 1) * SUBLANES)
                st_ref[buf, rows, re_cols] = sr
                st_ref[buf, rows, im_cols] = si
            car_ref[0, :, cs] = sr
            car_ref[1, :, cs] = si
            return
        er, ei = bu_rows(0)
        for j in range(1, seg):
            er, ei = step(ar, ai, er, ei, *bu_rows(j))
        for lvl, sh in enumerate((1, 2, 4)):
            cr_ = coef_ref[lvl, 0, :, cs]
            ci_ = coef_ref[lvl, 1, :, cs]
            pr = pltpu.roll(er, sh, 0)
            pi = pltpu.roll(ei, sh, 0)
            er, ei = er + cr_ * pr - ci_ * pi, ei + cr_ * pi + ci_ * pr
        c0r = car_ref[0, :, cs]
        c0i = car_ref[1, :, cs]
        cr_ = coef_ref[3, 0, :, cs]
        ci_ = coef_ref[3, 1, :, cs]
        fr = er + cr_ * c0r - ci_ * c0i
        fi = ei + cr_ * c0i + ci_ * c0r
        car_ref[0, :, cs] = jnp.broadcast_to(fr[SUBLANES - 1:SUBLANES], (SUBLANES, S5_SCAN_W))
        car_ref[1, :, cs] = jnp.broadcast_to(fi[SUBLANES - 1:SUBLANES], (SUBLANES, S5_SCAN_W))
        sr = jnp.where(row0, c0r, pltpu.roll(fr, 1, 0))
        si = jnp.where(row0, c0i, pltpu.roll(fi, 1, 0))
        for j in range(seg):
            sr, si = step(ar, ai, sr, si, *bu_rows(j))
            rows = slice(j * SUBLANES, (j + 1) * SUBLANES)
            st_ref[buf, rows, re_cols] = sr
            st_ref[buf, rows, im_cols] = si

    project_in(0)
    for kb in range(S5_KB):
        if kb + 1 < S5_KB:
            project_in(kb + 1)
        if kb >= 1:
            project_out(kb - 1)
        for part in range(S5_CB // S5_SCAN_W):
            scan(kb, part)
    project_out(S5_KB - 1)

    @pl.when(t == pl.num_programs(1) - 1)
    def _():
        n_state = SUBLANES if multi_seq else 1
        sr_ref[0] = car_ref[0, 0:n_state, :]
        si_ref[0] = car_ref[1, 0:n_state, :]


def _s5_call(x, nw, d, wb, wc, a_re, a_im, s0r, s0i, tt, multi_seq):
    n_b, t_len, _ = x.shape
    seg = tt // SUBLANES
    n_state = s0r.shape[1]
    assert n_state == (SUBLANES if multi_seq else 1) and (not multi_seq or t_len == tt)
    coef = _s5_coef(a_re, a_im, seg)
    p_idx = jnp.arange(tt)
    src = (p_idx % SUBLANES) * seg + p_idx // SUBLANES
    perm = (src[:, None] == p_idx[None, :]).astype(BF16)
    const = lambda *shape: pl.BlockSpec(shape, lambda b, t: (0,) * len(shape))
    state = pl.BlockSpec((1, n_state, S5_CH), lambda b, t: (b, 0, 0))
    return pl.pallas_call(
        functools.partial(_s5_body, tt=tt, multi_seq=multi_seq),
        grid=(n_b, t_len // tt),
        in_specs=[pl.BlockSpec((1, tt, D_MODEL), lambda b, t: (b, t, 0)),
                  const(1, D_MODEL), const(1, D_MODEL),
                  const(S5_KB, MXU_K, 2 * S5_CB), const(S5_KB, 2 * S5_CB, MXU_K),
                  const(5, 2, SUBLANES, S5_CH), const(tt, tt), const(tt, tt), state, state],
        out_specs=[pl.BlockSpec((1, tt, D_MODEL), lambda b, t: (b, t, 0)), state, state],
        out_shape=[jax.ShapeDtypeStruct((n_b, t_len, D_MODEL), BF16),
                   jax.ShapeDtypeStruct((n_b, n_state, S5_CH), F32),
                   jax.ShapeDtypeStruct((n_b, n_state, S5_CH), F32)],
        scratch_shapes=[pltpu.VMEM((2, SUBLANES, S5_CH), F32),
                        pltpu.VMEM((tt, D_MODEL), F32),
                        pltpu.VMEM((2, tt, 2 * S5_CB), F32),
                        pltpu.VMEM((2, tt, 2 * S5_CB), F32)],
        compiler_params=_cparams(("arbitrary", "arbitrary")),
        name="s5_mixer",
    )(x, nw, d, wb, wc, coef, perm, perm.T, s0r, s0i)


def _s5_coef(a_re, a_im, seg):
    assert seg & (seg - 1) == 0, seg
    ar, ai = a_re, a_im
    for _ in range(int(math.log2(seg))):
        ar, ai = ar * ar - ai * ai, 2.0 * ar * ai
    pw_r, pw_i = [ar], [ai]
    for _ in range(SUBLANES - 1):
        pr, pi = pw_r[-1], pw_i[-1]
        pw_r.append(pr * ar - pi * ai)
        pw_i.append(pr * ai + pi * ar)
    row = jnp.arange(SUBLANES)[:, None]
    levels = []
    for sh in (1, 2, 4):
        m = (row >= sh).astype(F32)
        levels.append(jnp.stack([m * pw_r[sh - 1], m * pw_i[sh - 1]]))
    levels.append(jnp.stack([jnp.concatenate(pw_r, axis=0), jnp.concatenate(pw_i, axis=0)]))
    ones = jnp.ones((SUBLANES, 1), F32)
    levels.append(jnp.stack([ones * a_re, ones * a_im]))
    return jnp.stack(levels)


def _s5_params(lam_re, lam_im, log_dt, b_re, b_im, c_re, c_im):
    lr = jnp.minimum(lam_re.astype(F32), -1e-4)
    li = lam_im.astype(F32)
    dt = jnp.exp(log_dt.astype(F32))[:, None]
    mag = jnp.exp(lr * dt)
    ang = li * dt
    a_re = mag * jnp.cos(ang)
    a_im = mag * jnp.sin(ang)
    nr = a_re - 1.0
    den = lr * lr + li * li
    k_re = (nr * lr + a_im * li) / den
    k_im = (a_im * lr - nr * li) / den
    br = b_re.astype(F32)
    bi = b_im.astype(F32)
    bb_re = k_re[..., None] * br - k_im[..., None] * bi
    bb_im = k_re[..., None] * bi + k_im[..., None] * br

    gl = S5_GROUPS // S5_KB
    eye = jnp.eye(gl, dtype=F32)

    def blockdiag_in(bb):
        bbk = bb.reshape(S5_KB, gl, S5_STATE, S5_GROUP)
        w = jnp.einsum('kgph,gf->kghfp', bbk, eye)
        return w.reshape(S5_KB, gl * S5_GROUP, gl * S5_STATE)

    def blockdiag_out(c):
        ck = c.reshape(S5_KB, gl, S5_GROUP, S5_STATE)
        w = jnp.einsum('kghp,gf->kgpfh', ck, eye)
        return w.reshape(S5_KB, gl * S5_STATE, gl * S5_GROUP)

    wb = jnp.concatenate([blockdiag_in(bb_re), blockdiag_in(bb_im)], axis=2).astype(BF16)
    wc = jnp.concatenate([blockdiag_out(c_re.astype(F32)),
                          -blockdiag_out(c_im.astype(F32))], axis=1).astype(BF16)

    return wb, wc, a_re.reshape(1, S5_CH), a_im.reshape(1, S5_CH)


def _glu_body(g_ref, wa_ref, wg_ref, x_ref, o_ref):
    g = g_ref[...]
    za = jnp.dot(g, wa_ref[...], preferred_element_type=F32)
    zg = jnp.dot(g, wg_ref[...], preferred_element_type=F32)
    o_ref[...] = x_ref[...] + za * jax.nn.sigmoid(zg)


def _glu_call(g, w_glu, x, tm, tn):
    m = g.shape[0]
    nj = D_MODEL // tn
    return pl.pallas_call(
        _glu_body,
        grid=(m // tm, nj),
        in_specs=[pl.BlockSpec((tm, D_MODEL), lambda i, j: (i, 0)),
                  pl.BlockSpec((D_MODEL, tn), lambda i, j: (0, j)),
                  pl.BlockSpec((D_MODEL, tn), lambda i, j: (0, j + nj)),
                  pl.BlockSpec((tm, tn), lambda i, j: (i, j))],
        out_specs=pl.BlockSpec((tm, tn), lambda i, j: (i, j)),
        out_shape=jax.ShapeDtypeStruct((m, D_MODEL), F32),
        compiler_params=_cparams(("parallel", "arbitrary")),
        name="glu_residual",
    )(g, w_glu, w_glu, x)


def _ffn_body(x_ref, nw_ref, wu_ref, wd_ref, fw_ref, o_ref, h_ref, *, final_norm):
    f = pl.program_id(1)

    @pl.when(f == 0)
    def _():
        x = x_ref[...]
        h_ref[...] = _rms(x, nw_ref[...]).astype(BF16)
        o_ref[...] = x

    a = jnp.maximum(jnp.dot(h_ref[...], wu_ref[...], preferred_element_type=F32), 0.0)
    o_ref[...] += jnp.dot((a * a).astype(BF16), wd_ref[...], preferred_element_type=F32)

    if final_norm:
        @pl.when(f == pl.num_programs(1) - 1)
        def _():
            o_ref[...] = _rms(o_ref[...], fw_ref[...])


def _ffn_call(x, nw, w_up, w_down, layer, fw, tm, tf, final_norm):
    m = x.shape[0]
    d_ff = w_up.shape[2]
    return pl.pallas_call(
        functools.partial(_ffn_body, final_norm=final_norm),
        grid=(m // tm, d_ff // tf),
        in_specs=[pl.BlockSpec((tm, D_MODEL), lambda i, f: (i, 0)),
                  pl.BlockSpec((1, D_MODEL), lambda i, f: (0, 0)),
                  pl.BlockSpec((None, D_MODEL, tf), lambda i, f: (layer, 0, f)),
                  pl.BlockSpec((None, tf, D_MODEL), lambda i, f: (layer, f, 0)),
                  pl.BlockSpec((1, D_MODEL), lambda i, f: (0, 0))],
        out_specs=pl.BlockSpec((tm, D_MODEL), lambda i, f: (i, 0)),
        out_shape=jax.ShapeDtypeStruct((m, D_MODEL), F32),
        scratch_shapes=[pltpu.VMEM((tm, D_MODEL), BF16)],
        compiler_params=_cparams(("parallel", "arbitrary")),
        name="ffn_residual",
    )(x, nw, w_up, w_down, fw)


def _matmul_res_body(a_ref, w_ref, x_ref, o_ref):
    o_ref[...] = x_ref[...] + jnp.dot(a_ref[...], w_ref[...], preferred_element_type=F32)


def _matmul_res_call(a, w, x, tm, tn):
    m, k = a.shape
    n = w.shape[1]
    return pl.pallas_call(
        _matmul_res_body,
        grid=(m // tm, n // tn),
        in_specs=[pl.BlockSpec((tm, k), lambda i, j: (i, 0)),
                  pl.BlockSpec((k, tn), lambda i, j: (0, j)),
                  pl.BlockSpec((tm, tn), lambda i, j: (i, j))],
        out_specs=pl.BlockSpec((tm, tn), lambda i, j: (i, j)),
        out_shape=jax.ShapeDtypeStruct((m, n), F32),
        compiler_params=_cparams(("parallel", "arbitrary")),
        name="matmul_residual",
    )(a, w, x)


INPROJ_TN = 1024
N_CONV_TILES = CONV_DIM // INPROJ_TN
N_Z_TILES = VAL_DIM // INPROJ_TN
N_QK_TILES = 2 * KEY_DIM // INPROJ_TN
N_Q_TILES = KEY_DIM // INPROJ_TN
CONV_ROWS = 64


def _inproj_body(x_ref, nw_ref, w_ref, wba_ref, cprev_ref, cw_ref, alog_ref, dtb_ref,
                 qkv_ref, z_ref, g_ref, beta_ref, nc_ref, hs_ref, hist_ref, ra_ref, rb_ref,
                 *, tm, seq_rows, tiles_per_seq, gate_group):
    n_seq = tm // seq_rows
    i = pl.program_id(0)
    j = pl.program_id(1)
    tn = INPROJ_TN
    n_sub = tn // MXU_K
    tail = CONV_W - 1
    pad = SUBLANES

    @pl.when(j == 0)
    def _():
        hs_ref[...] = _rms(x_ref[...], nw_ref[...]).astype(BF16)
        ba = jnp.dot(hs_ref[...], wba_ref[...], preferred_element_type=F32)
        zz = ba[:, V_HEADS:2 * V_HEADS] + dtb_ref[...]
        softplus = jnp.maximum(zz, 0.0) + jnp.log1p(jnp.exp(-jnp.abs(zz)))
        g_all = -jnp.exp(alog_ref[...]) * softplus
        beta_all = jax.nn.sigmoid(ba[:, 0:V_HEADS])
        for gi in range(V_HEADS // gate_group):
            g_ref[gi] = g_all[:, gi * gate_group:(gi + 1) * gate_group]
            beta_ref[gi] = beta_all[:, gi * gate_group:(gi + 1) * gate_group]

    @pl.when((i % tiles_per_seq == 0) & (j == 0))
    def _():
        hist_ref[...] = jnp.concatenate(
            [jnp.zeros((pad - tail, CONV_DIM), F32), cprev_ref[0]], axis=0)

    def project(r_ref, half, s):
        c0 = half * tn + s * MXU_K
        r_ref[s, pad:pad + tm, :] = jnp.dot(hs_ref[...], w_ref[:, c0:c0 + MXU_K],
                                            preferred_element_type=F32)

    def conv_rows(win, c0):
        acc = win[pad:] * cw_ref[tail:tail + 1, c0:c0 + MXU_K]
        for d in range(1, CONV_W):
            shifted = pltpu.roll(win, d, 0)[pad:]
            acc = acc + shifted * cw_ref[tail - d:tail - d + 1, c0:c0 + MXU_K]
        return acc

    def epilogue(r_ref, tile, out_half, s):
        c0 = tile * tn + s * MXU_K
        o0 = out_half * tn + s * MXU_K
        l2 = tile < N_QK_TILES
        scale = HEAD ** -0.5 if tile < N_Q_TILES else 1.0

        def finish(acc, rows):
            c = acc * jax.nn.sigmoid(acc)
            if l2:
                for hd in range(MXU_K // HEAD):
                    ch = c[:, hd * HEAD:(hd + 1) * HEAD]
                    ch = ch * (lax.rsqrt(jnp.sum(ch * ch, axis=-1, keepdims=True) + L2_EPS) * scale)
                    qkv_ref[rows, o0 + hd * HEAD:o0 + (hd + 1) * HEAD] = ch.astype(BF16)
            else:
                qkv_ref[rows, o0:o0 + MXU_K] = c.astype(BF16)

        r_ref[s, 0:pad, :] = hist_ref[:, c0:c0 + MXU_K]
        hist_ref[:, c0:c0 + MXU_K] = r_ref[s, tm:tm + pad, :]
        for sq in range(n_seq):
            end = pad + (sq + 1) * seq_rows
            nc_ref[sq, :, c0:c0 + MXU_K] = r_ref[s, end - tail:end, :]
        for r0 in range(0, tm, CONV_ROWS):
            rc = min(CONV_ROWS, tm - r0)
            finish(conv_rows(r_ref[s, r0:r0 + rc + pad, :], c0), slice(r0, r0 + rc))
        for sq in range(1, n_seq):
            r0 = sq * seq_rows
            win = jnp.concatenate([jnp.zeros((pad - tail, MXU_K), F32), cprev_ref[sq, :, c0:c0 + MXU_K],
                                   r_ref[s, pad + r0:pad + r0 + pad, :]], axis=0)
            finish(conv_rows(win, c0), slice(r0, r0 + pad))

    def project_z(half, s):
        c0 = half * tn + s * MXU_K
        z_ref[:, c0:c0 + MXU_K] = jnp.dot(hs_ref[...], w_ref[:, c0:c0 + MXU_K],
                                          preferred_element_type=F32).astype(BF16)

    def step_body(step):
        cur, prev = step % 2, (step - 1) % 2
        for half, r_ref in ((0, ra_ref), (1, rb_ref)):
            for s in range(n_sub):
                if step < n_conv_steps:
                    project(r_ref.at[cur], half, s)
                else:
                    project_z(half, s)
                if 0 < step <= n_conv_steps:
                    epilogue(r_ref.at[prev], 2 * (step - 1) + half, half, s)

    n_conv_steps = N_CONV_TILES // 2
    for step in range(n_conv_steps + 1):
        pl.when(j == step)(functools.partial(step_body, step))
    pl.when(j > n_conv_steps)(functools.partial(step_body, n_conv_steps + 1))


def _inproj_call(x, nw, w_qkvz, w_ba, conv_prev, conv_w, a_log, dt_bias, t_len, tm, gate_group):
    m = x.shape[0]
    n_grp = V_HEADS // gate_group
    tn2 = 2 * INPROJ_TN
    n_cs = N_CONV_TILES // 2
    n_zs = N_Z_TILES // 2
    seq_rows = min(t_len, tm)
    n_seq = tm // seq_rows
    tpb = t_len // seq_rows
    return pl.pallas_call(
        functools.partial(_inproj_body, tm=tm, seq_rows=seq_rows, tiles_per_seq=tpb,
                          gate_group=gate_group),
        grid=(m // tm, n_cs + n_zs),
        in_specs=[pl.BlockSpec((tm, D_MODEL), lambda i, j: (i, 0)),
                  pl.BlockSpec((1, D_MODEL), lambda i, j: (0, 0)),
                  pl.BlockSpec((D_MODEL, tn2), lambda i, j: (0, j)),
                  pl.BlockSpec((D_MODEL, LANES), lambda i, j: (0, 0)),
                  pl.BlockSpec((n_seq, CONV_W - 1, CONV_DIM), lambda i, j: (i // tpb, 0, 0)),
                  pl.BlockSpec((CONV_W, CONV_DIM), lambda i, j: (0, 0)),
                  pl.BlockSpec((1, V_HEADS), lambda i, j: (0, 0)),
                  pl.BlockSpec((1, V_HEADS), lambda i, j: (0, 0))],
        out_specs=[pl.BlockSpec((tm, tn2), lambda i, j: (i, jnp.clip(j - 1, 0, n_cs - 1))),
                   pl.BlockSpec((tm, tn2), lambda i, j: (i, jnp.maximum(j - n_cs, 0))),
                   pl.BlockSpec((n_grp, tm, gate_group), lambda i, j: (0, i, 0)),
                   pl.BlockSpec((n_grp, tm, gate_group), lambda i, j: (0, i, 0)),
                   pl.BlockSpec((n_seq, CONV_W - 1, CONV_DIM), lambda i, j: (i, 0, 0))],
        out_shape=[jax.ShapeDtypeStruct((m, CONV_DIM), BF16),
                   jax.ShapeDtypeStruct((m, VAL_DIM), BF16),
                   jax.ShapeDtypeStruct((n_grp, m, gate_group), F32),
                   jax.ShapeDtypeStruct((n_grp, m, gate_group), F32),
                   jax.ShapeDtypeStruct((m // seq_rows, CONV_W - 1, CONV_DIM), F32)],
        scratch_shapes=[pltpu.VMEM((tm, D_MODEL), BF16),
                        pltpu.VMEM((SUBLANES, CONV_DIM), F32),
                        pltpu.VMEM((2, INPROJ_TN // MXU_K, tm + SUBLANES, MXU_K), F32),
                        pltpu.VMEM((2, INPROJ_TN // MXU_K, tm + SUBLANES, MXU_K), F32)],
        compiler_params=_cparams(("arbitrary", "arbitrary")),
        name="gdn_inproj_conv",
    )(x, nw, w_qkvz, w_ba, conv_prev, conv_w, a_log, dt_bias)


INV_BASE = 16


def _mm16(a, b):
    return jnp.dot(a.astype(BF16), b.astype(BF16), preferred_element_type=F32)


def _tri_inverse_all(nmats, ri, ci, chunk):
    base = min(INV_BASE, chunk)
    sh = int(math.log2(base))
    same = (ri >> sh) == (ci >> sh)
    eye = (ri == ci).astype(F32)
    ms = [jnp.where(same, -n, 0.0) for n in nmats]
    ps = [eye + m for m in ms]
    for _ in range(sh - 1):
        ms = [_mm16(m, m) for m in ms]
        ps = [p + _mm16(p, m) for p, m in zip(ps, ms)]
    size = base
    while size < chunk:
        sh += 1
        size *= 2
        same_next = (ri >> sh) == (ci >> sh)
        sel = same_next & jnp.logical_not(same)
        ts = [_mm16(p, jnp.where(sel, n, 0.0)) for p, n in zip(ps, nmats)]
        ps = [p - _mm16(t, p) for p, t in zip(ps, ts)]
        same = same_next
    return ps


def _delta_body(q_ref, k_ref, v_ref, z_ref, gc_ref, bc_ref, s0_ref, nw_ref,
                o_ref, sf_ref, s_ref, *, chunk, cpb, hb):
    c_idx = pl.program_id(2)

    @pl.when(c_idx == 0)
    def _():
        s_ref[...] = s0_ref[0]

    ri = lax.broadcasted_iota(jnp.int32, (chunk, chunk), 0)
    ci = lax.broadcasted_iota(jnp.int32, (chunk, chunk), 1)
    causal = ri >= ci
    strict = ri > ci
    tril = causal.astype(F32)
    triu = (ri <= ci).astype(F32)
    hi = lax.Precision.HIGHEST
    nt = (((1,), (1,)), ((), ()))
    tn = (((0,), (0,)), ((), ()))
    hq = hb // 2

    kqk, pre = {}, {}
    for c in range(cpb):
        rows = slice(c * chunk, (c + 1) * chunk)
        for j in range(hq):
            k16 = k_ref[0, rows, j * HEAD:(j + 1) * HEAD]
            q16 = q_ref[0, rows, j * HEAD:(j + 1) * HEAD]
            kqk[c, j] = lax.dot_general(jnp.concatenate([k16, q16], axis=0), k16, nt,
                                        preferred_element_type=F32)
    nmats, keys = [], []
    for c in range(cpb):
        rows = slice(c * chunk, (c + 1) * chunk)
        g_raw = gc_ref[0, rows, :]
        gcol = jnp.dot(tril, g_raw, precision=hi, preferred_element_type=F32)
        grow = lax.dot_general(g_raw, triu, tn, precision=hi, preferred_element_type=F32)
        bcol = bc_ref[0, rows, :]
        for hh in range(hb):
            j = hh // 2
            k = k_ref[0, rows, j * HEAD:(j + 1) * HEAD].astype(F32)
            q = q_ref[0, rows, j * HEAD:(j + 1) * HEAD].astype(F32)
            v = v_ref[0, rows, hh * HEAD:(hh + 1) * HEAD].astype(F32)
            gc = gcol[:, hh:hh + 1]
            gr = grow[hh:hh + 1, :]
            beta = bcol[:, hh:hh + 1]
            decay = jnp.exp(jnp.where(causal, gc - gr, -jnp.inf))
            egc = jnp.exp(gc)
            g_last = gc[chunk - 1:chunk, :]
            kbeta = k * beta
            kk = kqk[c, j][:chunk]
            qk = kqk[c, j][chunk:]
            nmats.append(jnp.where(strict, (kk * beta) * decay, 0.0))
            keys.append((c, hh))
            pre[c, hh] = dict(
                rhs=jnp.concatenate([v * beta, kbeta * egc], axis=1).astype(BF16),
                qg=(q * egc).astype(BF16),
                intra=(qk * decay).astype(BF16),
                kd=(k * jnp.exp(g_last - gc)).astype(BF16),
                sdec=jnp.exp(g_last))
    tinv = _tri_inverse_all(nmats, ri, ci, chunk)
    for key, t in zip(keys, tinv):
        pre[key]['sol'] = jnp.dot(t.astype(BF16), pre[key]['rhs'], preferred_element_type=F32)

    for c in range(cpb):
        rows = slice(c * chunk, (c + 1) * chunk)
        ws = []
        for hh in range(hb):
            d = pre[c, hh]
            wq = jnp.concatenate([d['sol'][:, HEAD:].astype(BF16), d['qg']], axis=0)
            ws.append(jnp.dot(wq, s_ref[hh].astype(BF16), preferred_element_type=F32))
        for hh in range(hb):
            d = pre[c, hh]
            v16 = (d['sol'][:, :HEAD] - ws[hh][:chunk]).astype(BF16)
            o = ws[hh][chunk:] + jnp.dot(d['intra'], v16, preferred_element_type=F32)
            s_ref[hh] = s_ref[hh] * d['sdec'] + lax.dot_general(d['kd'], v16, tn,
                                                                preferred_element_type=F32)
            o = o * lax.rsqrt(jnp.mean(o * o, axis=-1, keepdims=True) + RMS_EPS) * nw_ref[...]
            zz = z_ref[0, rows, hh * HEAD:(hh + 1) * HEAD]
            o_ref[0, rows, hh * HEAD:(hh + 1) * HEAD] = (o * (zz * jax.nn.sigmoid(zz))).astype(BF16)

    @pl.when(c_idx == pl.num_programs(2) - 1)
    def _():
        sf_ref[0] = s_ref[...]


def _delta_call(qkv, z, g, beta, s0, norm_w, chunk, cpb, hb):
    n_b, t_len, _ = qkv.shape
    nhg = V_HEADS // hb
    tc = chunk * cpb
    n_tc = t_len // tc
    assert g.shape == (nhg, n_b * t_len, hb), g.shape
    hq = hb // 2
    assert KEY_DIM % (hb * HEAD) == 0, hb
    q_off = 0
    k_off = KEY_DIM // (hq * HEAD)
    v_off = 2 * KEY_DIM // (hb * HEAD)
    return pl.pallas_call(
        functools.partial(_delta_body, chunk=chunk, cpb=cpb, hb=hb),
        grid=(n_b, nhg, t_len // tc),
        in_specs=[pl.BlockSpec((1, tc, hq * HEAD), lambda b, h, c: (b, c, q_off + h)),
                  pl.BlockSpec((1, tc, hq * HEAD), lambda b, h, c: (b, c, k_off + h)),
                  pl.BlockSpec((1, tc, hb * HEAD), lambda b, h, c: (b, c, v_off + h)),
                  pl.BlockSpec((1, tc, hb * HEAD), lambda b, h, c: (b, c, h)),
                  pl.BlockSpec((1, tc, hb), lambda b, h, c: (h, b * n_tc + c, 0)),
                  pl.BlockSpec((1, tc, hb), lambda b, h, c: (h, b * n_tc + c, 0)),
                  pl.BlockSpec((1, hb, HEAD, HEAD), lambda b, h, c: (b, h, 0, 0)),
                  pl.BlockSpec((1, HEAD), lambda b, h, c: (0, 0))],
        out_specs=[pl.BlockSpec((1, tc, hb * HEAD), lambda b, h, c: (b, c, h)),
                   pl.BlockSpec((1, hb, HEAD, HEAD), lambda b, h, c: (b, h, 0, 0))],
        out_shape=[jax.ShapeDtypeStruct((n_b, t_len, VAL_DIM), BF16),
                   jax.ShapeDtypeStruct((n_b, V_HEADS, HEAD, HEAD), F32)],
        scratch_shapes=[pltpu.VMEM((hb, HEAD, HEAD), F32)],
        compiler_params=_cparams(("arbitrary", "arbitrary", "arbitrary")),
        name="gated_delta_rule",
    )(qkv, qkv, qkv, z, g, beta, s0, norm_w)


def _tile(m, pref):
    t = min(m, pref)
    assert m % t == 0, (m, t)
    return t


def _trunk(x, s5_re0, s5_im0, conv0, s0, p):
    n_b, t_len, _ = x.shape
    m = n_b * t_len
    tm = _tile(m, 512)
    tm_wide = _tile(m, 1024)

    tt = _tile(t_len, 256)
    multi_seq = SUBLANES * t_len <= 256 and n_b % SUBLANES == 0
    n_blk, n_state = (n_b // SUBLANES, SUBLANES) if multi_seq else (n_b, 1)
    if multi_seq:
        tt = SUBLANES * t_len
    g, sr, si = _s5_call(x.reshape(n_blk, n_state * t_len, D_MODEL), p['nm0'], p['s5_d'], p['wb'], p['wc'],
                         p['a_re'], p['a_im'], s5_re0.reshape(n_blk, n_state, S5_CH),
                         s5_im0.reshape(n_blk, n_state, S5_CH), tt, multi_seq)
    x2 = x.reshape(m, D_MODEL)
    x2 = _glu_call(g.reshape(m, D_MODEL), p['w_glu'], x2, tm_wide, 1024)
    x2 = _ffn_call(x2, p['nf0'], p['w_up'], p['w_down'], 0, p['nfin'], tm_wide, 512, False)

    chunk = min(t_len, 64)
    cpb = _tile(t_len // chunk, 4)
    hb = 8 if cpb > 1 else 16
    qkv, z, gg, beta, tails = _inproj_call(
        x2, p['nm1'], p['w_qkvz'], p['w_ba'], conv0, p['conv_w'], p['a_log'], p['dt_bias'],
        t_len, tm, hb)
    tpb = t_len // min(t_len, tm)
    new_conv = tails[tpb - 1::tpb]
    og, s_fin = _delta_call(qkv.reshape(n_b, t_len, -1), z.reshape(n_b, t_len, VAL_DIM),
                            gg, beta, s0, p['gdn_norm_w'], chunk, cpb, hb)
    x2 = _matmul_res_call(og.reshape(m, VAL_DIM), p['w_out'], x2, tm_wide, 1024)
    y = _ffn_call(x2, p['nf1'], p['w_up'], p['w_down'], 1, p['nfin'], tm_wide, 512, True)

    return (y.reshape(n_b, t_len, D_MODEL),
            sr.reshape(1, n_b, S5_GROUPS, S5_STATE), si.reshape(1, n_b, S5_GROUPS, S5_STATE),
            new_conv[None], s_fin[None])


def kernel(x_prompt, x_sample, state_s5_re, state_s5_im, state_gdn_conv, state_gdn_s,
           norm_mix, norm_ffn, norm_final,
           s5_lam_re, s5_lam_im, s5_log_dt, s5_b_re, s5_b_im, s5_c_re, s5_c_im,
           s5_d, s5_w_glu,
           gdn_w_in, gdn_conv_w, gdn_a_log, gdn_dt_bias, gdn_norm_w, gdn_w_out,
           ffn_w_up, ffn_w_down):
    wb, wc, a_re, a_im = _s5_params(s5_lam_re[0], s5_lam_im[0], s5_log_dt[0], s5_b_re[0], s5_b_im[0],
                              s5_c_re[0], s5_c_im[0])
    w_in = gdn_w_in[0]
    ba = w_in[:, CONV_DIM + VAL_DIM:]
    w_ba = jnp.concatenate([ba, jnp.zeros((D_MODEL, LANES - ba.shape[1]), w_in.dtype)], axis=1)
    row = lambda a: a.reshape(1, -1).astype(F32)
    p = dict(
        nm0=row(norm_mix[0]), nm1=row(norm_mix[1]), nf0=row(norm_ffn[0]), nf1=row(norm_ffn[1]),
        nfin=row(norm_final), s5_d=row(s5_d[0]), wb=wb, wc=wc, a_re=a_re, a_im=a_im,
        w_glu=s5_w_glu[0].astype(BF16),
        w_up=ffn_w_up.astype(BF16), w_down=ffn_w_down.astype(BF16),
        w_qkvz=w_in.astype(BF16), w_ba=w_ba.astype(BF16),
        conv_w=gdn_conv_w[0].astype(F32), a_log=row(gdn_a_log[0]), dt_bias=row(gdn_dt_bias[0]),
        gdn_norm_w=row(gdn_norm_w[0]), w_out=gdn_w_out[0].astype(BF16),
    )
    n_bp = x_prompt.shape[0]
    z_state = jnp.zeros((n_bp, S5_GROUPS, S5_STATE), F32)
    z_conv = jnp.zeros((n_bp, CONV_W - 1, CONV_DIM), F32)
    z_s = jnp.zeros((n_bp, V_HEADS, HEAD, HEAD), F32)

    yp, p_re, p_im, p_conv, p_s = _trunk(x_prompt, z_state, z_state, z_conv, z_s, p)
    ys, s_re, s_im, s_conv, s_s = _trunk(x_sample, state_s5_re[0], state_s5_im[0],
                                         state_gdn_conv[0], state_gdn_s[0], p)
    return (yp, ys, p_re, p_im, p_conv, p_s, s_re, s_im, s_conv, s_s)
```

```python
import functools
import math

import jax
import jax.numpy as jnp
from jax import lax
from jax.experimental import pallas as pl
from jax.experimental.pallas import tpu as pltpu

F32 = jnp.float32
BF16 = jnp.bfloat16

RMS_EPS = 1e-6
L2_EPS = 1e-6
D_MODEL = 2048
S5_GROUP = 16
S5_STATE = 64
S5_GROUPS = D_MODEL // S5_GROUP
S5_CH = S5_GROUPS * S5_STATE
HEAD = 128
QK_HEADS = D_MODEL // HEAD
V_HEADS = 2 * QK_HEADS
KEY_DIM = QK_HEADS * HEAD
VAL_DIM = V_HEADS * HEAD
CONV_DIM = 2 * KEY_DIM + VAL_DIM
CONV_W = 4

SUBLANES = 8
LANES = 128
MXU_K = 256
S5_KB = D_MODEL // MXU_K
S5_CB = S5_CH // S5_KB
S5_SCAN_W = S5_CB
VMEM_LIMIT = 56 * 1024 * 1024


def _cparams(sem):
    return pltpu.CompilerParams(dimension_semantics=sem, vmem_limit_bytes=VMEM_LIMIT)


def _rms(x, w):
    return x * lax.rsqrt(jnp.mean(x * x, axis=-1, keepdims=True) + RMS_EPS) * w


def _s5_body(x_ref, nw_ref, d_ref, wb_ref, wc_ref, coef_ref, perm_ref, permt_ref, s0r_ref, s0i_ref,
             g_ref, sr_ref, si_ref, car_ref, h_ref, bu_ref, st_ref, *, tt, multi_seq):
    t = pl.program_id(1)
    seg = tt // SUBLANES

    @pl.when(t == 0)
    def _():
        car_ref[0] = jnp.broadcast_to(s0r_ref[0], (SUBLANES, S5_CH))
        car_ref[1] = jnp.broadcast_to(s0i_ref[0], (SUBLANES, S5_CH))

    h = _rms(x_ref[0], nw_ref[...])
    h_ref[...] = h
    hp = jnp.dot(perm_ref[...], h.astype(BF16), preferred_element_type=F32).astype(BF16)
    row0 = lax.broadcasted_iota(jnp.int32, (SUBLANES, S5_SCAN_W), 0) == 0

    def step(ar, ai, sr, si, xr, xi):
        return ar * sr - ai * si + xr, ar * si + ai * sr + xi

    def project_in(kb):
        ks = slice(kb * MXU_K, (kb + 1) * MXU_K)
        bu_ref[kb % 2] = jnp.dot(hp[:, ks], wb_ref[kb], preferred_element_type=F32)

    def project_out(kb):
        ks = slice(kb * MXU_K, (kb + 1) * MXU_K)
        yp = jnp.dot(st_ref[kb % 2].astype(BF16), wc_ref[kb], preferred_element_type=F32)
        y_hi = yp.astype(BF16)
        y_lo = (yp - y_hi.astype(F32)).astype(BF16)
        y2 = jnp.dot(permt_ref[...], jnp.concatenate([y_hi, y_lo], axis=1),
                     preferred_element_type=F32)
        y = y2[:, :MXU_K] + y2[:, MXU_K:] + d_ref[:, ks] * h_ref[:, ks]
        g_ref[0, :, ks] = jax.nn.gelu(y).astype(BF16)

    def scan(kb, part):
        lo = part * S5_SCAN_W
        re_cols = slice(lo, lo + S5_SCAN_W)
        im_cols = slice(S5_CB + lo, S5_CB + lo + S5_SCAN_W)
        cs = slice(kb * S5_CB + lo, kb * S5_CB + lo + S5_SCAN_W)
        buf = kb % 2

        def bu_rows(j):
            rows = slice(j * SUBLANES, (j + 1) * SUBLANES)
            return bu_ref[buf, rows, re_cols], bu_ref[buf, rows, im_cols]

        ar = coef_ref[4, 0, :, cs]
        ai = coef_ref[4, 1, :, cs]
        if multi_seq:
            sr = car_ref[0, :, cs]
            si = car_ref[1, :, cs]
            for j in range(seg):
                sr, si = step(ar, ai, sr, si, *bu_rows(j))
                rows = slice(j * SUBLANES, (j + 1) * SUBLANES)
                st_ref[buf, rows, re_cols] = sr
                st_ref[buf, rows, im_cols] = si
            car_ref[0, :, cs] = sr
            car_ref[1, :, cs] = si
            return
        er, ei = bu_rows(0)
        for j in range(1, seg):
            er, ei = step(ar, ai, er, ei, *bu_rows(j))
        for lvl, sh in enumerate((1, 2, 4)):
            cr_ = coef_ref[lvl, 0, :, cs]
            ci_ = coef_ref[lvl, 1, :, cs]
            pr = pltpu.roll(er, sh, 0)
            pi = pltpu.roll(ei, sh, 0)
            er, ei = er + cr_ * pr - ci_ * pi, ei + cr_ * pi + ci_ * pr
        c0r = car_ref[0, :, cs]
        c0i = car_ref[1, :, cs]
        cr_ = coef_ref[3, 0, :, cs]
        ci_ = coef_ref[3, 1, :, cs]
        fr = er + cr_ * c0r - ci_ * c0i
        fi = ei + cr_ * c0i + ci_ * c0r
        car_ref[0, :, cs] = jnp.broadcast_to(fr[SUBLANES - 1:SUBLANES], (SUBLANES, S5_SCAN_W))
        car_ref[1, :, cs] = jnp.broadcast_to(fi[SUBLANES - 1:SUBLANES], (SUBLANES, S5_SCAN_W))
        sr = jnp.where(row0, c0r, pltpu.roll(fr, 1, 0))
        si = jnp.where(row0, c0i, pltpu.roll(fi, 1, 0))
        for j in range(seg):
            sr, si = step(ar, ai, sr, si, *bu_rows(j))
            rows = slice(j * SUBLANES, (j + 1) * SUBLANES)
            st_ref[buf, rows, re_cols] = sr
            st_ref[buf, rows, im_cols] = si

    project_in(0)
    for kb in range(S5_KB):
        if kb + 1 < S5_KB:
            project_in(kb + 1)
        if kb >= 1:
            project_out(kb - 1)
        for part in range(S5_CB // S5_SCAN_W):
            scan(kb, part)
    project_out(S5_KB - 1)

    @pl.when(t == pl.num_programs(1) - 1)
    def _():
        n_state = SUBLANES if multi_seq else 1
        sr_ref[0] = car_ref[0, 0:n_state, :]
        si_ref[0] = car_ref[1, 0:n_state, :]


def _s5_call(x, nw, d, wb, wc, a_re, a_im, s0r, s0i, tt, multi_seq):
    n_b, t_len, _ = x.shape
    seg = tt // SUBLANES
    n_state = s0r.shape[1]
    assert n_state == (SUBLANES if multi_seq else 1) and (not multi_seq or t_len == tt)
    coef = _s5_coef(a_re, a_im, seg)
    p_idx = jnp.arange(tt)
    src = (p_idx % SUBLANES) * seg + p_idx // SUBLANES
    perm = (src[:, None] == p_idx[None, :]).astype(BF16)
    const = lambda *shape: pl.BlockSpec(shape, lambda b, t: (0,) * len(shape))
    state = pl.BlockSpec((1, n_state, S5_CH), lambda b, t: (b, 0, 0))
    return pl.pallas_call(
        functools.partial(_s5_body, tt=tt, multi_seq=multi_seq),
        grid=(n_b, t_len // tt),
        in_specs=[pl.BlockSpec((1, tt, D_MODEL), lambda b, t: (b, t, 0)),
                  const(1, D_MODEL), const(1, D_MODEL),
                  const(S5_KB, MXU_K, 2 * S5_CB), const(S5_KB, 2 * S5_CB, MXU_K),
                  const(5, 2, SUBLANES, S5_CH), const(tt, tt), const(tt, tt), state, state],
        out_specs=[pl.BlockSpec((1, tt, D_MODEL), lambda b, t: (b, t, 0)), state, state],
        out_shape=[jax.ShapeDtypeStruct((n_b, t_len, D_MODEL), BF16),
                   jax.ShapeDtypeStruct((n_b, n_state, S5_CH), F32),
                   jax.ShapeDtypeStruct((n_b, n_state, S5_CH), F32)],
        scratch_shapes=[pltpu.VMEM((2, SUBLANES, S5_CH), F32),
                        pltpu.VMEM((tt, D_MODEL), F32),
                        pltpu.VMEM((2, tt, 2 * S5_CB), F32),
                        pltpu.VMEM((2, tt, 2 * S5_CB), F32)],
        compiler_params=_cparams(("arbitrary", "arbitrary")),
        name="s5_mixer",
    )(x, nw, d, wb, wc, coef, perm, perm.T, s0r, s0i)


def _s5_coef(a_re, a_im, seg):
    assert seg & (seg - 1) == 0, seg
    ar, ai = a_re, a_im
    for _ in range(int(math.log2(seg))):
        ar, ai = ar * ar - ai * ai, 2.0 * ar * ai
    pw_r, pw_i = [ar], [ai]
    for _ in range(SUBLANES - 1):
        pr, pi = pw_r[-1], pw_i[-1]
        pw_r.append(pr * ar - pi * ai)
        pw_i.append(pr * ai + pi * ar)
    row = jnp.arange(SUBLANES)[:, None]
    levels = []
    for sh in (1, 2, 4):
        m = (row >= sh).astype(F32)
        levels.append(jnp.stack([m * pw_r[sh - 1], m * pw_i[sh - 1]]))
    levels.append(jnp.stack([jnp.concatenate(pw_r, axis=0), jnp.concatenate(pw_i, axis=0)]))
    ones = jnp.ones((SUBLANES, 1), F32)
    levels.append(jnp.stack([ones * a_re, ones * a_im]))
    return jnp.stack(levels)


def _s5_params(lam_re, lam_im, log_dt, b_re, b_im, c_re, c_im):
    lr = jnp.minimum(lam_re.astype(F32), -1e-4)
    li = lam_im.astype(F32)
    dt = jnp.exp(log_dt.astype(F32))[:, None]
    mag = jnp.exp(lr * dt)
    ang = li * dt
    a_re = mag * jnp.cos(ang)
    a_im = mag * jnp.sin(ang)
    nr = a_re - 1.0
    den = lr * lr + li * li
    k_re = (nr * lr + a_im * li) / den
    k_im = (a_im * lr - nr * li) / den
    br = b_re.astype(F32)
    bi = b_im.astype(F32)
    bb_re = k_re[..., None] * br - k_im[..., None] * bi
    bb_im = k_re[..., None] * bi + k_im[..., None] * br

    gl = S5_GROUPS // S5_KB
    eye = jnp.eye(gl, dtype=F32)

    def blockdiag_in(bb):
        bbk = bb.reshape(S5_KB, gl, S5_STATE, S5_GROUP)
        w = jnp.einsum('kgph,gf->kghfp', bbk, eye)
        return w.reshape(S5_KB, gl * S5_GROUP, gl * S5_STATE)

    def blockdiag_out(c):
        ck = c.reshape(S5_KB, gl, S5_GROUP, S5_STATE)
        w = jnp.einsum('kghp,gf->kgpfh', ck, eye)
        return w.reshape(S5_KB, gl * S5_STATE, gl * S5_GROUP)

    wb = jnp.concatenate([blockdiag_in(bb_re), blockdiag_in(bb_im)], axis=2).astype(BF16)
    wc = jnp.concatenate([blockdiag_out(c_re.astype(F32)),
                          -blockdiag_out(c_im.astype(F32))], axis=1).astype(BF16)

    return wb, wc, a_re.reshape(1, S5_CH), a_im.reshape(1, S5_CH)


def _glu_body(g_ref, wa_ref, wg_ref, x_ref, o_ref):
    g = g_ref[...]
    za = jnp.dot(g, wa_ref[...], preferred_element_type=F32)
    zg = jnp.dot(g, wg_ref[...], preferred_element_type=F32)
    o_ref[...] = x_ref[...] + za * jax.nn.sigmoid(zg)


def _glu_call(g, w_glu, x, tm, tn):
    m = g.shape[0]
    nj = D_MODEL // tn
    return pl.pallas_call(
        _glu_body,
        grid=(m // tm, nj),
        in_specs=[pl.BlockSpec((tm, D_MODEL), lambda i, j: (i, 0)),
                  pl.BlockSpec((D_MODEL, tn), lambda i, j: (0, j)),
                  pl.BlockSpec((D_MODEL, tn), lambda i, j: (0, j + nj)),
                  pl.BlockSpec((tm, tn), lambda i, j: (i, j))],
        out_specs=pl.BlockSpec((tm, tn), lambda i, j: (i, j)),
        out_shape=jax.ShapeDtypeStruct((m, D_MODEL), F32),
        compiler_params=_cparams(("parallel", "arbitrary")),
        name="glu_residual",
    )(g, w_glu, w_glu, x)


def _ffn_body(x_ref, nw_ref, wu_ref, wd_ref, fw_ref, o_ref, h_ref, *, final_norm):
    f = pl.program_id(1)

    @pl.when(f == 0)
    def _():
        x = x_ref[...]
        h_ref[...] = _rms(x, nw_ref[...]).astype(BF16)
        o_ref[...] = x

    a = jnp.maximum(jnp.dot(h_ref[...], wu_ref[...], preferred_element_type=F32), 0.0)
    o_ref[...] += jnp.dot((a * a).astype(BF16), wd_ref[...], preferred_element_type=F32)

    if final_norm:
        @pl.when(f == pl.num_programs(1) - 1)
        def _():
            o_ref[...] = _rms(o_ref[...], fw_ref[...])


def _ffn_call(x, nw, w_up, w_down, layer, fw, tm, tf, final_norm):
    m = x.shape[0]
    d_ff = w_up.shape[2]
    return pl.pallas_call(
        functools.partial(_ffn_body, final_norm=final_norm),
        grid=(m // tm, d_ff // tf),
        in_specs=[pl.BlockSpec((tm, D_MODEL), lambda i, f: (i, 0)),
                  pl.BlockSpec((1, D_MODEL), lambda i, f: (0, 0)),
                  pl.BlockSpec((None, D_MODEL, tf), lambda i, f: (layer, 0, f)),
                  pl.BlockSpec((None, tf, D_MODEL), lambda i, f: (layer, f, 0)),
                  pl.BlockSpec((1, D_MODEL), lambda i, f: (0, 0))],
        out_specs=pl.BlockSpec((tm, D_MODEL), lambda i, f: (i, 0)),
        out_shape=jax.ShapeDtypeStruct((m, D_MODEL), F32),
        scratch_shapes=[pltpu.VMEM((tm, D_MODEL), BF16)],
        compiler_params=_cparams(("parallel", "arbitrary")),
        name="ffn_residual",
    )(x, nw, w_up, w_down, fw)


def _matmul_res_body(a_ref, w_ref, x_ref, o_ref):
    o_ref[...] = x_ref[...] + jnp.dot(a_ref[...], w_ref[...], preferred_element_type=F32)


def _matmul_res_call(a, w, x, tm, tn):
    m, k = a.shape
    n = w.shape[1]
    return pl.pallas_call(
        _matmul_res_body,
        grid=(m // tm, n // tn),
        in_specs=[pl.BlockSpec((tm, k), lambda i, j: (i, 0)),
                  pl.BlockSpec((k, tn), lambda i, j: (0, j)),
                  pl.BlockSpec((tm, tn), lambda i, j: (i, j))],
        out_specs=pl.BlockSpec((tm, tn), lambda i, j: (i, j)),
        out_shape=jax.ShapeDtypeStruct((m, n), F32),
        compiler_params=_cparams(("parallel", "arbitrary")),
        name="matmul_residual",
    )(a, w, x)


INPROJ_TN = 1024
N_CONV_TILES = CONV_DIM // INPROJ_TN
N_Z_TILES = VAL_DIM // INPROJ_TN
N_QK_TILES = 2 * KEY_DIM // INPROJ_TN
N_Q_TILES = KEY_DIM // INPROJ_TN
CONV_ROWS = 64


def _inproj_body(x_ref, nw_ref, w_ref, wba_ref, cprev_ref, cw_ref, alog_ref, dtb_ref,
                 qkv_ref, z_ref, g_ref, beta_ref, nc_ref, hs_ref, hist_ref, ra_ref, rb_ref,
                 *, tm, seq_rows, tiles_per_seq, gate_group):
    n_seq = tm // seq_rows
    i = pl.program_id(0)
    j = pl.program_id(1)
    tn = INPROJ_TN
    n_sub = tn // MXU_K
    tail = CONV_W - 1
    pad = SUBLANES

    @pl.when(j == 0)
    def _():
        hs_ref[...] = _rms(x_ref[...], nw_ref[...]).astype(BF16)
        ba = jnp.dot(hs_ref[...], wba_ref[...], preferred_element_type=F32)
        zz = ba[:, V_HEADS:2 * V_HEADS] + dtb_ref[...]
        softplus = jnp.maximum(zz, 0.0) + jnp.log1p(jnp.exp(-jnp.abs(zz)))
        g_all = -jnp.exp(alog_ref[...]) * softplus
        beta_all = jax.nn.sigmoid(ba[:, 0:V_HEADS])
        for gi in range(V_HEADS // gate_group):
            g_ref[gi] = g_all[:, gi * gate_group:(gi + 1) * gate_group]
            beta_ref[gi] = beta_all[:, gi * gate_group:(gi + 1) * gate_group]

    @pl.when((i % tiles_per_seq == 0) & (j == 0))
    def _():
        hist_ref[...] = jnp.concatenate(
            [jnp.zeros((pad - tail, CONV_DIM), F32), cprev_ref[0]], axis=0)

    def project(r_ref, half, s):
        c0 = half * tn + s * MXU_K
        r_ref[s, pad:pad + tm, :] = jnp.dot(hs_ref[...], w_ref[:, c0:c0 + MXU_K],
                                            preferred_element_type=F32)

    def conv_rows(win, c0):
        acc = win[pad:] * cw_ref[tail:tail + 1, c0:c0 + MXU_K]
        for d in range(1, CONV_W):
            shifted = pltpu.roll(win, d, 0)[pad:]
            acc = acc + shifted * cw_ref[tail - d:tail - d + 1, c0:c0 + MXU_K]
        return acc

    def epilogue(r_ref, tile, out_half, s):
        c0 = tile * tn + s * MXU_K
        o0 = out_half * tn + s * MXU_K
        l2 = tile < N_QK_TILES
        scale = HEAD ** -0.5 if tile < N_Q_TILES else 1.0

        def finish(acc, rows):
            c = acc * jax.nn.sigmoid(acc)
            if l2:
                for hd in range(MXU_K // HEAD):
                    ch = c[:, hd * HEAD:(hd + 1) * HEAD]
                    ch = ch * (lax.rsqrt(jnp.sum(ch * ch, axis=-1, keepdims=True) + L2_EPS) * scale)
                    qkv_ref[rows, o0 + hd * HEAD:o0 + (hd + 1) * HEAD] = ch.astype(BF16)
            else:
                qkv_ref[rows, o0:o0 + MXU_K] = c.astype(BF16)

        r_ref[s, 0:pad, :] = hist_ref[:, c0:c0 + MXU_K]
        hist_ref[:, c0:c0 + MXU_K] = r_ref[s, tm:tm + pad, :]
        for sq in range(n_seq):
            end = pad + (sq + 1) * seq_rows
            nc_ref[sq, :, c0:c0 + MXU_K] = r_ref[s, end - tail:end, :]
        for r0 in range(0, tm, CONV_ROWS):
            rc = min(CONV_ROWS, tm - r0)
            finish(conv_rows(r_ref[s, r0:r0 + rc + pad, :], c0), slice(r0, r0 + rc))
        for sq in range(1, n_seq):
            r0 = sq * seq_rows
            win = jnp.concatenate([jnp.zeros((pad - tail, MXU_K), F32), cprev_ref[sq, :, c0:c0 + MXU_K],
                                   r_ref[s, pad + r0:pad + r0 + pad, :]], axis=0)
            finish(conv_rows(win, c0), slice(r0, r0 + pad))

    def project_z(half, s):
        c0 = half * tn + s * MXU_K
        zz = jnp.dot(hs_ref[...], w_ref[:, c0:c0 + MXU_K], preferred_element_type=F32)
        z_ref[:, c0:c0 + MXU_K] = (zz * jax.nn.sigmoid(zz)).astype(BF16)

    def step_body(step):
        cur, prev = step % 2, (step - 1) % 2
        for half, r_ref in ((0, ra_ref), (1, rb_ref)):
            for s in range(n_sub):
                if step < n_conv_steps:
                    project(r_ref.at[cur], half, s)
                else:
                    project_z(half, s)
                if 0 < step <= n_conv_steps:
                    epilogue(r_ref.at[prev], 2 * (step - 1) + half, half, s)

    n_conv_steps = N_CONV_TILES // 2
    for step in range(n_conv_steps + 1):
        pl.when(j == step)(functools.partial(step_body, step))
    pl.when(j > n_conv_steps)(functools.partial(step_body, n_conv_steps + 1))


def _inproj_call(x, nw, w_qkvz, w_ba, conv_prev, conv_w, a_log, dt_bias, t_len, tm, gate_group):
    m = x.shape[0]
    n_grp = V_HEADS // gate_group
    tn2 = 2 * INPROJ_TN
    n_cs = N_CONV_TILES // 2
    n_zs = N_Z_TILES // 2
    seq_rows = min(t_len, tm)
    n_seq = tm // seq_rows
    tpb = t_len // seq_rows
    return pl.pallas_call(
        functools.partial(_inproj_body, tm=tm, seq_rows=seq_rows, tiles_per_seq=tpb,
                          gate_group=gate_group),
        grid=(m // tm, n_cs + n_zs),
        in_specs=[pl.BlockSpec((tm, D_MODEL), lambda i, j: (i, 0)),
                  pl.BlockSpec((1, D_MODEL), lambda i, j: (0, 0)),
                  pl.BlockSpec((D_MODEL, tn2), lambda i, j: (0, j)),
                  pl.BlockSpec((D_MODEL, LANES), lambda i, j: (0, 0)),
                  pl.BlockSpec((n_seq, CONV_W - 1, CONV_DIM), lambda i, j: (i // tpb, 0, 0)),
                  pl.BlockSpec((CONV_W, CONV_DIM), lambda i, j: (0, 0)),
                  pl.BlockSpec((1, V_HEADS), lambda i, j: (0, 0)),
                  pl.BlockSpec((1, V_HEADS), lambda i, j: (0, 0))],
        out_specs=[pl.BlockSpec((tm, tn2), lambda i, j: (i, jnp.clip(j - 1, 0, n_cs - 1))),
                   pl.BlockSpec((tm, tn2), lambda i, j: (i, jnp.maximum(j - n_cs, 0))),
                   pl.BlockSpec((n_grp, tm, gate_group), lambda i, j: (0, i, 0)),
                   pl.BlockSpec((n_grp, tm, gate_group), lambda i, j: (0, i, 0)),
                   pl.BlockSpec((n_seq, CONV_W - 1, CONV_DIM), lambda i, j: (i, 0, 0))],
        out_shape=[jax.ShapeDtypeStruct((m, CONV_DIM), BF16),
                   jax.ShapeDtypeStruct((m, VAL_DIM), BF16),
                   jax.ShapeDtypeStruct((n_grp, m, gate_group), F32),
                   jax.ShapeDtypeStruct((n_grp, m, gate_group), F32),
                   jax.ShapeDtypeStruct((m // seq_rows, CONV_W - 1, CONV_DIM), F32)],
        scratch_shapes=[pltpu.VMEM((tm, D_MODEL), BF16),
                        pltpu.VMEM((SUBLANES, CONV_DIM), F32),
                        pltpu.VMEM((2, INPROJ_TN // MXU_K, tm + SUBLANES, MXU_K), F32),
                        pltpu.VMEM((2, INPROJ_TN // MXU_K, tm + SUBLANES, MXU_K), F32)],
        compiler_params=_cparams(("arbitrary", "arbitrary")),
        name="gdn_inproj_conv",
    )(x, nw, w_qkvz, w_ba, conv_prev, conv_w, a_log, dt_bias)


INV_BASE = 16


def _mm16(a, b):
    return jnp.dot(a.astype(BF16), b.astype(BF16), preferred_element_type=F32)


def _tri_inverse_all(nmats, ri, ci, chunk):
    base = min(INV_BASE, chunk)
    sh = int(math.log2(base))
    same = (ri >> sh) == (ci >> sh)
    eye = (ri == ci).astype(F32)
    ms = [jnp.where(same, -n, 0.0) for n in nmats]
    ps = [eye + m for m in ms]
    for _ in range(sh - 1):
        ms = [_mm16(m, m) for m in ms]
        ps = [p + _mm16(p, m) for p, m in zip(ps, ms)]
    size = base
    while size < chunk:
        sh += 1
        size *= 2
        same_next = (ri >> sh) == (ci >> sh)
        sel = same_next & jnp.logical_not(same)
        ts = [_mm16(p, jnp.where(sel, n, 0.0)) for p, n in zip(ps, nmats)]
        ps = [p - _mm16(t, p) for p, t in zip(ps, ts)]
        same = same_next
    return ps


def _delta_body(q_ref, k_ref, v_ref, z_ref, gc_ref, bc_ref, s0_ref, nw_ref,
                o_ref, sf_ref, s_ref, *, chunk, cpb, hb):
    c_idx = pl.program_id(2)

    @pl.when(c_idx == 0)
    def _():
        s_ref[...] = s0_ref[0]

    ri = lax.broadcasted_iota(jnp.int32, (chunk, chunk), 0)
    ci = lax.broadcasted_iota(jnp.int32, (chunk, chunk), 1)
    causal = ri >= ci
    strict = ri > ci
    tril = causal.astype(F32)
    triu = (ri <= ci).astype(F32)
    eye = (ri == ci).astype(F32)
    hi = lax.Precision.HIGHEST
    nt = (((1,), (1,)), ((), ()))
    tn = (((0,), (0,)), ((), ()))
    hq = hb // 2

    kqk, pre = {}, {}
    for c in range(cpb):
        rows = slice(c * chunk, (c + 1) * chunk)
        for j in range(hq):
            k16 = k_ref[0, rows, j * HEAD:(j + 1) * HEAD]
            q16 = q_ref[0, rows, j * HEAD:(j + 1) * HEAD]
            kqk[c, j] = lax.dot_general(jnp.concatenate([k16, q16], axis=0), k16, nt,
                                        preferred_element_type=F32)
    nmats, keys = [], []
    for c in range(cpb):
        rows = slice(c * chunk, (c + 1) * chunk)
        g_raw = gc_ref[0, rows, :]
        gcol = jnp.dot(tril, g_raw, precision=hi, preferred_element_type=F32)
        grow = lax.dot_general(g_raw, triu, tn, precision=hi, preferred_element_type=F32)
        bcol = bc_ref[0, rows, :]
        brow = lax.dot_general(bcol, eye, tn, precision=hi, preferred_element_type=F32)
        for hh in range(hb):
            j = hh // 2
            k = k_ref[0, rows, j * HEAD:(j + 1) * HEAD].astype(F32)
            q = q_ref[0, rows, j * HEAD:(j + 1) * HEAD].astype(F32)
            v16 = v_ref[0, rows, hh * HEAD:(hh + 1) * HEAD]
            gc = gcol[:, hh:hh + 1]
            gr = grow[hh:hh + 1, :]
            beta = bcol[:, hh:hh + 1]
            decay = jnp.exp(jnp.where(causal, gc - gr, -jnp.inf))
            egc = jnp.exp(gc)
            g_last = gc[chunk - 1:chunk, :]
            kk = kqk[c, j][:chunk]
            qk = kqk[c, j][chunk:]
            nmats.append(jnp.where(strict, (kk * beta) * decay, 0.0))
            keys.append((c, hh))
            pre[c, hh] = dict(
                rhs=jnp.concatenate([v16, (k * egc).astype(BF16)], axis=1),
                brow=brow[hh:hh + 1, :],
                qg=(q * egc).astype(BF16),
                intra=(qk * decay).astype(BF16),
                kd=(k * jnp.exp(g_last - gc)).astype(BF16),
                sdec=jnp.exp(g_last))
    tinv = _tri_inverse_all(nmats, ri, ci, chunk)
    for key, t in zip(keys, tinv):
        pre[key]['sol'] = jnp.dot((t * pre[key]['brow']).astype(BF16), pre[key]['rhs'],
                                  preferred_element_type=F32)

    for c in range(cpb):
        rows = slice(c * chunk, (c + 1) * chunk)
        ws = []
        for hh in range(hb):
            d = pre[c, hh]
            wq = jnp.concatenate([d['sol'][:, HEAD:].astype(BF16), d['qg']], axis=0)
            ws.append(jnp.dot(wq, s_ref[hh].astype(BF16), preferred_element_type=F32))
        for hh in range(hb):
            d = pre[c, hh]
            v16 = (d['sol'][:, :HEAD] - ws[hh][:chunk]).astype(BF16)
            o = ws[hh][chunk:] + jnp.dot(d['intra'], v16, preferred_element_type=F32)
            s_ref[hh] = s_ref[hh] * d['sdec'] + lax.dot_general(d['kd'], v16, tn,
                                                                preferred_element_type=F32)
            o = o * lax.rsqrt(jnp.mean(o * o, axis=-1, keepdims=True) + RMS_EPS) * nw_ref[...]
            gate = z_ref[0, rows, hh * HEAD:(hh + 1) * HEAD]
            o_ref[0, rows, hh * HEAD:(hh + 1) * HEAD] = (o * gate).astype(BF16)

    @pl.when(c_idx == pl.num_programs(2) - 1)
    def _():
        sf_ref[0] = s_ref[...]


def _delta_call(qkv, z, g, beta, s0, norm_w, chunk, cpb, hb):
    n_b, t_len, _ = qkv.shape
    nhg = V_HEADS // hb
    tc = chunk * cpb
    n_tc = t_len // tc
    assert g.shape == (nhg, n_b * t_len, hb), g.shape
    hq = hb // 2
    assert KEY_DIM % (hb * HEAD) == 0, hb
    q_off = 0
    k_off = KEY_DIM // (hq * HEAD)
    v_off = 2 * KEY_DIM // (hb * HEAD)
    return pl.pallas_call(
        functools.partial(_delta_body, chunk=chunk, cpb=cpb, hb=hb),
        grid=(n_b, nhg, t_len // tc),
        in_specs=[pl.BlockSpec((1, tc, hq * HEAD), lambda b, h, c: (b, c, q_off + h)),
                  pl.BlockSpec((1, tc, hq * HEAD), lambda b, h, c: (b, c, k_off + h)),
                  pl.BlockSpec((1, tc, hb * HEAD), lambda b, h, c: (b, c, v_off + h)),
                  pl.BlockSpec((1, tc, hb * HEAD), lambda b, h, c: (b, c, h)),
                  pl.BlockSpec((1, tc, hb), lambda b, h, c: (h, b * n_tc + c, 0)),
                  pl.BlockSpec((1, tc, hb), lambda b, h, c: (h, b * n_tc + c, 0)),
                  pl.BlockSpec((1, hb, HEAD, HEAD), lambda b, h, c: (b, h, 0, 0)),
                  pl.BlockSpec((1, HEAD), lambda b, h, c: (0, 0))],
        out_specs=[pl.BlockSpec((1, tc, hb * HEAD), lambda b, h, c: (b, c, h)),
                   pl.BlockSpec((1, hb, HEAD, HEAD), lambda b, h, c: (b, h, 0, 0))],
        out_shape=[jax.ShapeDtypeStruct((n_b, t_len, VAL_DIM), BF16),
                   jax.ShapeDtypeStruct((n_b, V_HEADS, HEAD, HEAD), F32)],
        scratch_shapes=[pltpu.VMEM((hb, HEAD, HEAD), F32)],
        compiler_params=_cparams(("arbitrary", "arbitrary", "arbitrary")),
        name="gated_delta_rule",
    )(qkv, qkv, qkv, z, g, beta, s0, norm_w)


def _tile(m, pref):
    t = min(m, pref)
    assert m % t == 0, (m, t)
    return t


def _trunk(x, s5_re0, s5_im0, conv0, s0, p):
    n_b, t_len, _ = x.shape
    m = n_b * t_len
    tm = _tile(m, 512)
    tm_wide = _tile(m, 1024)

    tt = _tile(t_len, 256)
    multi_seq = SUBLANES * t_len <= 256 and n_b % SUBLANES == 0
    n_blk, n_state = (n_b // SUBLANES, SUBLANES) if multi_seq else (n_b, 1)
    if multi_seq:
        tt = SUBLANES * t_len
    g, sr, si = _s5_call(x.reshape(n_blk, n_state * t_len, D_MODEL), p['nm0'], p['s5_d'], p['wb'], p['wc'],
                         p['a_re'], p['a_im'], s5_re0.reshape(n_blk, n_state, S5_CH),
                         s5_im0.reshape(n_blk, n_state, S5_CH), tt, multi_seq)
    x2 = x.reshape(m, D_MODEL)
    x2 = _glu_call(g.reshape(m, D_MODEL), p['w_glu'], x2, tm_wide, 1024)
    x2 = _ffn_call(x2, p['nf0'], p['w_up'], p['w_down'], 0, p['nfin'], tm_wide, 512, False)

    chunk = min(t_len, 64)
    cpb = _tile(t_len // chunk, 4)
    hb = 8 if cpb > 1 else 16
    qkv, z, gg, beta, tails = _inproj_call(
        x2, p['nm1'], p['w_qkvz'], p['w_ba'], conv0, p['conv_w'], p['a_log'], p['dt_bias'],
        t_len, tm, hb)
    tpb = t_len // min(t_len, tm)
    new_conv = tails[tpb - 1::tpb]
    og, s_fin = _delta_call(qkv.reshape(n_b, t_len, -1), z.reshape(n_b, t_len, VAL_DIM),
                            gg, beta, s0, p['gdn_norm_w'], chunk, cpb, hb)
    x2 = _matmul_res_call(og.reshape(m, VAL_DIM), p['w_out'], x2, tm_wide, 1024)
    y = _ffn_call(x2, p['nf1'], p['w_up'], p['w_down'], 1, p['nfin'], tm_wide, 512, True)

    return (y.reshape(n_b, t_len, D_MODEL),
            sr.reshape(1, n_b, S5_GROUPS, S5_STATE), si.reshape(1, n_b, S5_GROUPS, S5_STATE),
            new_conv[None], s_fin[None])


def kernel(x_prompt, x_sample, state_s5_re, state_s5_im, state_gdn_conv, state_gdn_s,
           norm_mix, norm_ffn, norm_final,
           s5_lam_re, s5_lam_im, s5_log_dt, s5_b_re, s5_b_im, s5_c_re, s5_c_im,
           s5_d, s5_w_glu,
           gdn_w_in, gdn_conv_w, gdn_a_log, gdn_dt_bias, gdn_norm_w, gdn_w_out,
           ffn_w_up, ffn_w_down):
    wb, wc, a_re, a_im = _s5_params(s5_lam_re[0], s5_lam_im[0], s5_log_dt[0], s5_b_re[0], s5_b_im[0],
                              s5_c_re[0], s5_c_im[0])
    w_in = gdn_w_in[0]
    ba = w_in[:, CONV_DIM + VAL_DIM:]
    w_ba = jnp.concatenate([ba, jnp.zeros((D_MODEL, LANES - ba.shape[1]), w_in.dtype)], axis=1)
    row = lambda a: a.reshape(1, -1).astype(F32)
    p = dict(
        nm0=row(norm_mix[0]), nm1=row(norm_mix[1]), nf0=row(norm_ffn[0]), nf1=row(norm_ffn[1]),
        nfin=row(norm_final), s5_d=row(s5_d[0]), wb=wb, wc=wc, a_re=a_re, a_im=a_im,
        w_glu=s5_w_glu[0].astype(BF16),
        w_up=ffn_w_up.astype(BF16), w_down=ffn_w_down.astype(BF16),
        w_qkvz=w_in.astype(BF16), w_ba=w_ba.astype(BF16),
        conv_w=gdn_conv_w[0].astype(F32), a_log=row(gdn_a_log[0]), dt_bias=row(gdn_dt_bias[0]),
        gdn_norm_w=row(gdn_norm_w[0]), w_out=gdn_w_out[0].astype(BF16),
    )
    n_bp = x_prompt.shape[0]
    z_state = jnp.zeros((n_bp, S5_GROUPS, S5_STATE), F32)
    z_conv = jnp.zeros((n_bp, CONV_W - 1, CONV_DIM), F32)
    z_s = jnp.zeros((n_bp, V_HEADS, HEAD, HEAD), F32)

    yp, p_re, p_im, p_conv, p_s = _trunk(x_prompt, z_state, z_state, z_conv, z_s, p)
    ys, s_re, s_im, s_conv, s_s = _trunk(x_sample, state_s5_re[0], state_s5_im[0],
                                         state_gdn_conv[0], state_gdn_s[0], p)
    return (yp, ys, p_re, p_im, p_conv, p_s, s_re, s_im, s_conv, s_s)
```

```python
import functools
import math

import jax
import jax.numpy as jnp
from jax import lax
from jax.experimental import pallas as pl
from jax.experimental.pallas import tpu as pltpu

F32 = jnp.float32
BF16 = jnp.bfloat16

RMS_EPS = 1e-6
L2_EPS = 1e-6
D_MODEL = 2048
S5_GROUP = 16
S5_STATE = 64
S5_GROUPS = D_MODEL // S5_GROUP
S5_CH = S5_GROUPS * S5_STATE
HEAD = 128
QK_HEADS = D_MODEL // HEAD
V_HEADS = 2 * QK_HEADS
KEY_DIM = QK_HEADS * HEAD
VAL_DIM = V_HEADS * HEAD
CONV_DIM = 2 * KEY_DIM + VAL_DIM
CONV_W = 4

SUBLANES = 8
LANES = 128
MXU_K = 256
S5_KB = D_MODEL // MXU_K
S5_CB = S5_CH // S5_KB
S5_SCAN_W = S5_CB
VMEM_LIMIT = 56 * 1024 * 1024


def _cparams(sem):
    return pltpu.CompilerParams(dimension_semantics=sem, vmem_limit_bytes=VMEM_LIMIT)


def _rms(x, w):
    return x * lax.rsqrt(jnp.mean(x * x, axis=-1, keepdims=True) + RMS_EPS) * w


def _s5_body(x_ref, nw_ref, d_ref, wb_ref, wc_ref, coef_ref, perm_ref, permt_ref, s0r_ref, s0i_ref,
             g_ref, sr_ref, si_ref, car_ref, h_ref, bu_ref, st_ref, *, tt, multi_seq):
    t = pl.program_id(1)
    seg = tt // SUBLANES

    @pl.when(t == 0)
    def _():
        car_ref[0] = jnp.broadcast_to(s0r_ref[0], (SUBLANES, S5_CH))
        car_ref[1] = jnp.broadcast_to(s0i_ref[0], (SUBLANES, S5_CH))

    h = _rms(x_ref[0], nw_ref[...])
    h_ref[...] = h
    hp = jnp.dot(perm_ref[...], h.astype(BF16), preferred_element_type=F32).astype(BF16)
    row0 = lax.broadcasted_iota(jnp.int32, (SUBLANES, S5_SCAN_W), 0) == 0

    def step(ar, ai, sr, si, xr, xi):
        return ar * sr - ai * si + xr, ar * si + ai * sr + xi

    def project_in(kb):
        ks = slice(kb * MXU_K, (kb + 1) * MXU_K)
        bu_ref[kb % 2] = jnp.dot(hp[:, ks], wb_ref[kb], preferred_element_type=F32)

    def project_out(kb):
        ks = slice(kb * MXU_K, (kb + 1) * MXU_K)
        yp = jnp.dot(st_ref[kb % 2].astype(BF16), wc_ref[kb], preferred_element_type=F32)
        y_hi = yp.astype(BF16)
        y_lo = (yp - y_hi.astype(F32)).astype(BF16)
        y2 = jnp.dot(permt_ref[...], jnp.concatenate([y_hi, y_lo], axis=1),
                     preferred_element_type=F32)
        y = y2[:, :MXU_K] + y2[:, MXU_K:] + d_ref[:, ks] * h_ref[:, ks]
        g_ref[0, :, ks] = jax.nn.gelu(y).astype(BF16)

    def scan(kb, part):
        lo = part * S5_SCAN_W
        re_cols = slice(lo, lo + S5_SCAN_W)
        im_cols = slice(S5_CB + lo, S5_CB + lo + S5_SCAN_W)
        cs = slice(kb * S5_CB + lo, kb * S5_CB + lo + S5_SCAN_W)
        buf = kb % 2

        def bu_rows(j):
            rows = slice(j * SUBLANES, (j + 1) * SUBLANES)
            return bu_ref[buf, rows, re_cols], bu_ref[buf, rows, im_cols]

        ar = coef_ref[4, 0, :, cs]
        ai = coef_ref[4, 1, :, cs]
        if multi_seq:
            sr = car_ref[0, :, cs]
            si = car_ref[1, :, cs]
            for j in range(seg):
                sr, si = step(ar, ai, sr, si, *bu_rows(j))
                rows = slice(j * SUBLANES, (j + 1) * SUBLANES)
                st_ref[buf, rows, re_cols] = sr
                st_ref[buf, rows, im_cols] = si
            car_ref[0, :, cs] = sr
            car_ref[1, :, cs] = si
            return
        er, ei = bu_rows(0)
        for j in range(1, seg):
            er, ei = step(ar, ai, er, ei, *bu_rows(j))
        for lvl, sh in enumerate((1, 2, 4)):
            cr_ = coef_ref[lvl, 0, :, cs]
            ci_ = coef_ref[lvl, 1, :, cs]
            pr = pltpu.roll(er, sh, 0)
            pi = pltpu.roll(ei, sh, 0)
            er, ei = er + cr_ * pr - ci_ * pi, ei + cr_ * pi + ci_ * pr
        c0r = car_ref[0, :, cs]
        c0i = car_ref[1, :, cs]
        cr_ = coef_ref[3, 0, :, cs]
        ci_ = coef_ref[3, 1, :, cs]
        fr = er + cr_ * c0r - ci_ * c0i
        fi = ei + cr_ * c0i + ci_ * c0r
        car_ref[0, :, cs] = jnp.broadcast_to(fr[SUBLANES - 1:SUBLANES], (SUBLANES, S5_SCAN_W))
        car_ref[1, :, cs] = jnp.broadcast_to(fi[SUBLANES - 1:SUBLANES], (SUBLANES, S5_SCAN_W))
        sr = jnp.where(row0, c0r, pltpu.roll(fr, 1, 0))
        si = jnp.where(row0, c0i, pltpu.roll(fi, 1, 0))
        for j in range(seg):
            sr, si = step(ar, ai, sr, si, *bu_rows(j))
            rows = slice(j * SUBLANES, (j + 1) * SUBLANES)
            st_ref[buf, rows, re_cols] = sr
            st_ref[buf, rows, im_cols] = si

    project_in(0)
    for kb in range(S5_KB):
        if kb + 1 < S5_KB:
            project_in(kb + 1)
        if kb >= 1:
            project_out(kb - 1)
        for part in range(S5_CB // S5_SCAN_W):
            scan(kb, part)
    project_out(S5_KB - 1)

    @pl.when(t == pl.num_programs(1) - 1)
    def _():
        n_state = SUBLANES if multi_seq else 1
        sr_ref[0] = car_ref[0, 0:n_state, :]
        si_ref[0] = car_ref[1, 0:n_state, :]


def _s5_call(x, nw, d, wb, wc, a_re, a_im, s0r, s0i, tt, multi_seq):
    n_b, t_len, _ = x.shape
    seg = tt // SUBLANES
    n_state = s0r.shape[1]
    assert n_state == (SUBLANES if multi_seq else 1) and (not multi_seq or t_len == tt)
    coef = _s5_coef(a_re, a_im, seg)
    p_idx = jnp.arange(tt)
    src = (p_idx % SUBLANES) * seg + p_idx // SUBLANES
    perm = (src[:, None] == p_idx[None, :]).astype(BF16)
    const = lambda *shape: pl.BlockSpec(shape, lambda b, t: (0,) * len(shape))
    state = pl.BlockSpec((1, n_state, S5_CH), lambda b, t: (b, 0, 0))
    return pl.pallas_call(
        functools.partial(_s5_body, tt=tt, multi_seq=multi_seq),
        grid=(n_b, t_len // tt),
        in_specs=[pl.BlockSpec((1, tt, D_MODEL), lambda b, t: (b, t, 0)),
                  const(1, D_MODEL), const(1, D_MODEL),
                  const(S5_KB, MXU_K, 2 * S5_CB), const(S5_KB, 2 * S5_CB, MXU_K),
                  const(5, 2, SUBLANES, S5_CH), const(tt, tt), const(tt, tt), state, state],
        out_specs=[pl.BlockSpec((1, tt, D_MODEL), lambda b, t: (b, t, 0)), state, state],
        out_shape=[jax.ShapeDtypeStruct((n_b, t_len, D_MODEL), BF16),
                   jax.ShapeDtypeStruct((n_b, n_state, S5_CH), F32),
                   jax.ShapeDtypeStruct((n_b, n_state, S5_CH), F32)],
        scratch_shapes=[pltpu.VMEM((2, SUBLANES, S5_CH), F32),
                        pltpu.VMEM((tt, D_MODEL), F32),
                        pltpu.VMEM((2, tt, 2 * S5_CB), F32),
                        pltpu.VMEM((2, tt, 2 * S5_CB), F32)],
        compiler_params=_cparams(("arbitrary", "arbitrary")),
        name="s5_mixer",
    )(x, nw, d, wb, wc, coef, perm, perm.T, s0r, s0i)


def _s5_coef(a_re, a_im, seg):
    assert seg & (seg - 1) == 0, seg
    ar, ai = a_re, a_im
    for _ in range(int(math.log2(seg))):
        ar, ai = ar * ar - ai * ai, 2.0 * ar * ai
    pw_r, pw_i = [ar], [ai]
    for _ in range(SUBLANES - 1):
        pr, pi = pw_r[-1], pw_i[-1]
        pw_r.append(pr * ar - pi * ai)
        pw_i.append(pr * ai + pi * ar)
    row = jnp.arange(SUBLANES)[:, None]
    levels = []
    for sh in (1, 2, 4):
        m = (row >= sh).astype(F32)
        levels.append(jnp.stack([m * pw_r[sh - 1], m * pw_i[sh - 1]]))
    levels.append(jnp.stack([jnp.concatenate(pw_r, axis=0), jnp.concatenate(pw_i, axis=0)]))
    ones = jnp.ones((SUBLANES, 1), F32)
    levels.append(jnp.stack([ones * a_re, ones * a_im]))
    return jnp.stack(levels)


def _s5_params(lam_re, lam_im, log_dt, b_re, b_im, c_re, c_im):
    lr = jnp.minimum(lam_re.astype(F32), -1e-4)
    li = lam_im.astype(F32)
    dt = jnp.exp(log_dt.astype(F32))[:, None]
    mag = jnp.exp(lr * dt)
    ang = li * dt
    a_re = mag * jnp.cos(ang)
    a_im = mag * jnp.sin(ang)
    nr = a_re - 1.0
    den = lr * lr + li * li
    k_re = (nr * lr + a_im * li) / den
    k_im = (a_im * lr - nr * li) / den
    br = b_re.astype(F32)
    bi = b_im.astype(F32)
    bb_re = k_re[..., None] * br - k_im[..., None] * bi
    bb_im = k_re[..., None] * bi + k_im[..., None] * br

    gl = S5_GROUPS // S5_KB
    eye = jnp.eye(gl, dtype=F32)

    def blockdiag_in(bb):
        bbk = bb.reshape(S5_KB, gl, S5_STATE, S5_GROUP)
        w = jnp.einsum('kgph,gf->kghfp', bbk, eye)
        return w.reshape(S5_KB, gl * S5_GROUP, gl * S5_STATE)

    def blockdiag_out(c):
        ck = c.reshape(S5_KB, gl, S5_GROUP, S5_STATE)
        w = jnp.einsum('kghp,gf->kgpfh', ck, eye)
        return w.reshape(S5_KB, gl * S5_STATE, gl * S5_GROUP)

    wb = jnp.concatenate([blockdiag_in(bb_re), blockdiag_in(bb_im)], axis=2).astype(BF16)
    wc = jnp.concatenate([blockdiag_out(c_re.astype(F32)),
                          -blockdiag_out(c_im.astype(F32))], axis=1).astype(BF16)

    return wb, wc, a_re.reshape(1, S5_CH), a_im.reshape(1, S5_CH)


def _glu_body(g_ref, wa_ref, wg_ref, x_ref, o_ref):
    g = g_ref[...]
    za = jnp.dot(g, wa_ref[...], preferred_element_type=F32)
    zg = jnp.dot(g, wg_ref[...], preferred_element_type=F32)
    o_ref[...] = x_ref[...] + za * jax.nn.sigmoid(zg)


def _glu_call(g, w_glu, x, tm, tn):
    m = g.shape[0]
    nj = D_MODEL // tn
    return pl.pallas_call(
        _glu_body,
        grid=(m // tm, nj),
        in_specs=[pl.BlockSpec((tm, D_MODEL), lambda i, j: (i, 0)),
                  pl.BlockSpec((D_MODEL, tn), lambda i, j: (0, j)),
                  pl.BlockSpec((D_MODEL, tn), lambda i, j: (0, j + nj)),
                  pl.BlockSpec((tm, tn), lambda i, j: (i, j))],
        out_specs=pl.BlockSpec((tm, tn), lambda i, j: (i, j)),
        out_shape=jax.ShapeDtypeStruct((m, D_MODEL), F32),
        compiler_params=_cparams(("parallel", "arbitrary")),
        name="glu_residual",
    )(g, w_glu, w_glu, x)


def _ffn_body(x_ref, nw_ref, wu_ref, wd_ref, fw_ref, o_ref, h_ref, *, final_norm):
    f = pl.program_id(1)

    @pl.when(f == 0)
    def _():
        x = x_ref[...]
        h_ref[...] = _rms(x, nw_ref[...]).astype(BF16)
        o_ref[...] = x

    a = jnp.maximum(jnp.dot(h_ref[...], wu_ref[...], preferred_element_type=F32), 0.0)
    o_ref[...] += jnp.dot((a * a).astype(BF16), wd_ref[...], preferred_element_type=F32)

    if final_norm:
        @pl.when(f == pl.num_programs(1) - 1)
        def _():
            o_ref[...] = _rms(o_ref[...], fw_ref[...])


def _ffn_call(x, nw, w_up, w_down, layer, fw, tm, tf, final_norm):
    m = x.shape[0]
    d_ff = w_up.shape[2]
    return pl.pallas_call(
        functools.partial(_ffn_body, final_norm=final_norm),
        grid=(m // tm, d_ff // tf),
        in_specs=[pl.BlockSpec((tm, D_MODEL), lambda i, f: (i, 0)),
                  pl.BlockSpec((1, D_MODEL), lambda i, f: (0, 0)),
                  pl.BlockSpec((None, D_MODEL, tf), lambda i, f: (layer, 0, f)),
                  pl.BlockSpec((None, tf, D_MODEL), lambda i, f: (layer, f, 0)),
                  pl.BlockSpec((1, D_MODEL), lambda i, f: (0, 0))],
        out_specs=pl.BlockSpec((tm, D_MODEL), lambda i, f: (i, 0)),
        out_shape=jax.ShapeDtypeStruct((m, D_MODEL), F32),
        scratch_shapes=[pltpu.VMEM((tm, D_MODEL), BF16)],
        compiler_params=_cparams(("parallel", "arbitrary")),
        name="ffn_residual",
    )(x, nw, w_up, w_down, fw)


def _matmul_res_body(a_ref, w_ref, x_ref, o_ref):
    o_ref[...] = x_ref[...] + jnp.dot(a_ref[...], w_ref[...], preferred_element_type=F32)


def _matmul_res_call(a, w, x, tm, tn):
    m, k = a.shape
    n = w.shape[1]
    return pl.pallas_call(
        _matmul_res_body,
        grid=(m // tm, n // tn),
        in_specs=[pl.BlockSpec((tm, k), lambda i, j: (i, 0)),
                  pl.BlockSpec((k, tn), lambda i, j: (0, j)),
                  pl.BlockSpec((tm, tn), lambda i, j: (i, j))],
        out_specs=pl.BlockSpec((tm, tn), lambda i, j: (i, j)),
        out_shape=jax.ShapeDtypeStruct((m, n), F32),
        compiler_params=_cparams(("parallel", "arbitrary")),
        name="matmul_residual",
    )(a, w, x)


INPROJ_TN = 1024
N_CONV_TILES = CONV_DIM // INPROJ_TN
N_Z_TILES = VAL_DIM // INPROJ_TN
N_QK_TILES = 2 * KEY_DIM // INPROJ_TN
N_Q_TILES = KEY_DIM // INPROJ_TN
CONV_ROWS = 64


def _inproj_body(x_ref, nw_ref, w_ref, wba_ref, cprev_ref, cw_ref, alog_ref, dtb_ref,
                 qkv_ref, z_ref, g_ref, beta_ref, nc_ref, hs_ref, hist_ref, ra_ref, rb_ref,
                 *, tm, seq_rows, tiles_per_seq, gate_group):
    n_seq = tm // seq_rows
    i = pl.program_id(0)
    j = pl.program_id(1)
    tn = INPROJ_TN
    n_sub = tn // MXU_K
    tail = CONV_W - 1
    pad = SUBLANES

    @pl.when(j == 0)
    def _():
        hs_ref[...] = _rms(x_ref[...], nw_ref[...]).astype(BF16)
        ba = jnp.dot(hs_ref[...], wba_ref[...], preferred_element_type=F32)
        zz = ba[:, V_HEADS:2 * V_HEADS] + dtb_ref[...]
        softplus = jnp.maximum(zz, 0.0) + jnp.log1p(jnp.exp(-jnp.abs(zz)))
        g_all = -jnp.exp(alog_ref[...]) * softplus
        beta_all = jax.nn.sigmoid(ba[:, 0:V_HEADS])
        for gi in range(V_HEADS // gate_group):
            g_ref[gi] = g_all[:, gi * gate_group:(gi + 1) * gate_group]
            beta_ref[gi] = beta_all[:, gi * gate_group:(gi + 1) * gate_group]

    @pl.when((i % tiles_per_seq == 0) & (j == 0))
    def _():
        hist_ref[...] = jnp.concatenate(
            [jnp.zeros((pad - tail, CONV_DIM), F32), cprev_ref[0]], axis=0)

    def project(r_ref, half, s):
        c0 = half * tn + s * MXU_K
        r_ref[s, pad:pad + tm, :] = jnp.dot(hs_ref[...], w_ref[:, c0:c0 + MXU_K],
                                            preferred_element_type=F32)

    def conv_rows(win, c0):
        acc = win[pad:] * cw_ref[tail:tail + 1, c0:c0 + MXU_K]
        for d in range(1, CONV_W):
            shifted = pltpu.roll(win, d, 0)[pad:]
            acc = acc + shifted * cw_ref[tail - d:tail - d + 1, c0:c0 + MXU_K]
        return acc

    def epilogue(r_ref, tile, out_half, s):
        c0 = tile * tn + s * MXU_K
        o0 = out_half * tn + s * MXU_K
        l2 = tile < N_QK_TILES
        scale = HEAD ** -0.5 if tile < N_Q_TILES else 1.0

        def finish(acc, rows):
            c = acc * jax.nn.sigmoid(acc)
            if l2:
                for hd in range(MXU_K // HEAD):
                    ch = c[:, hd * HEAD:(hd + 1) * HEAD]
                    ch = ch * (lax.rsqrt(jnp.sum(ch * ch, axis=-1, keepdims=True) + L2_EPS) * scale)
                    qkv_ref[rows, o0 + hd * HEAD:o0 + (hd + 1) * HEAD] = ch.astype(BF16)
            else:
                qkv_ref[rows, o0:o0 + MXU_K] = c.astype(BF16)

        r_ref[s, 0:pad, :] = hist_ref[:, c0:c0 + MXU_K]
        hist_ref[:, c0:c0 + MXU_K] = r_ref[s, tm:tm + pad, :]
        for sq in range(n_seq):
            end = pad + (sq + 1) * seq_rows
            nc_ref[sq, :, c0:c0 + MXU_K] = r_ref[s, end - tail:end, :]
        for r0 in range(0, tm, CONV_ROWS):
            rc = min(CONV_ROWS, tm - r0)
            finish(conv_rows(r_ref[s, r0:r0 + rc + pad, :], c0), slice(r0, r0 + rc))
        for sq in range(1, n_seq):
            r0 = sq * seq_rows
            win = jnp.concatenate([jnp.zeros((pad - tail, MXU_K), F32), cprev_ref[sq, :, c0:c0 + MXU_K],
                                   r_ref[s, pad + r0:pad + r0 + pad, :]], axis=0)
            finish(conv_rows(win, c0), slice(r0, r0 + pad))

    def project_z(half, s):
        c0 = half * tn + s * MXU_K
        zz = jnp.dot(hs_ref[...], w_ref[:, c0:c0 + MXU_K], preferred_element_type=F32)
        z_ref[:, c0:c0 + MXU_K] = (zz * jax.nn.sigmoid(zz)).astype(BF16)

    def step_body(step):
        cur, prev = step % 2, (step - 1) % 2
        for half, r_ref in ((0, ra_ref), (1, rb_ref)):
            for s in range(n_sub):
                if step < n_conv_steps:
                    project(r_ref.at[cur], half, s)
                else:
                    project_z(half, s)
                if 0 < step <= n_conv_steps:
                    epilogue(r_ref.at[prev], 2 * (step - 1) + half, half, s)

    n_conv_steps = N_CONV_TILES // 2
    for step in range(n_conv_steps + 1):
        pl.when(j == step)(functools.partial(step_body, step))
    pl.when(j > n_conv_steps)(functools.partial(step_body, n_conv_steps + 1))


def _inproj_call(x, nw, w_qkvz, w_ba, conv_prev, conv_w, a_log, dt_bias, t_len, tm, gate_group):
    m = x.shape[0]
    n_grp = V_HEADS // gate_group
    tn2 = 2 * INPROJ_TN
    n_cs = N_CONV_TILES // 2
    n_zs = N_Z_TILES // 2
    seq_rows = min(t_len, tm)
    n_seq = tm // seq_rows
    tpb = t_len // seq_rows
    return pl.pallas_call(
        functools.partial(_inproj_body, tm=tm, seq_rows=seq_rows, tiles_per_seq=tpb,
                          gate_group=gate_group),
        grid=(m // tm, n_cs + n_zs),
        in_specs=[pl.BlockSpec((tm, D_MODEL), lambda i, j: (i, 0)),
                  pl.BlockSpec((1, D_MODEL), lambda i, j: (0, 0)),
                  pl.BlockSpec((D_MODEL, tn2), lambda i, j: (0, j)),
                  pl.BlockSpec((D_MODEL, LANES), lambda i, j: (0, 0)),
                  pl.BlockSpec((n_seq, CONV_W - 1, CONV_DIM), lambda i, j: (i // tpb, 0, 0)),
                  pl.BlockSpec((CONV_W, CONV_DIM), lambda i, j: (0, 0)),
                  pl.BlockSpec((1, V_HEADS), lambda i, j: (0, 0)),
                  pl.BlockSpec((1, V_HEADS), lambda i, j: (0, 0))],
        out_specs=[pl.BlockSpec((tm, tn2), lambda i, j: (i, jnp.clip(j - 1, 0, n_cs - 1))),
                   pl.BlockSpec((tm, tn2), lambda i, j: (i, jnp.maximum(j - n_cs, 0))),
                   pl.BlockSpec((n_grp, tm, gate_group), lambda i, j: (0, i, 0)),
                   pl.BlockSpec((n_grp, tm, gate_group), lambda i, j: (0, i, 0)),
                   pl.BlockSpec((n_seq, CONV_W - 1, CONV_DIM), lambda i, j: (i, 0, 0))],
        out_shape=[jax.ShapeDtypeStruct((m, CONV_DIM), BF16),
                   jax.ShapeDtypeStruct((m, VAL_DIM), BF16),
                   jax.ShapeDtypeStruct((n_grp, m, gate_group), F32),
                   jax.ShapeDtypeStruct((n_grp, m, gate_group), F32),
                   jax.ShapeDtypeStruct((m // seq_rows, CONV_W - 1, CONV_DIM), F32)],
        scratch_shapes=[pltpu.VMEM((tm, D_MODEL), BF16),
                        pltpu.VMEM((SUBLANES, CONV_DIM), F32),
                        pltpu.VMEM((2, INPROJ_TN // MXU_K, tm + SUBLANES, MXU_K), F32),
                        pltpu.VMEM((2, INPROJ_TN // MXU_K, tm + SUBLANES, MXU_K), F32)],
        compiler_params=_cparams(("arbitrary", "arbitrary")),
        name="gdn_inproj_conv",
    )(x, nw, w_qkvz, w_ba, conv_prev, conv_w, a_log, dt_bias)


INV_BASE = 16


def _mm16(a, b):
    return jnp.dot(a.astype(BF16), b.astype(BF16), preferred_element_type=F32)


def _tri_inverse_all(nmats, ri, ci, chunk):
    base = min(INV_BASE, chunk)
    sh = int(math.log2(base))
    same = (ri >> sh) == (ci >> sh)
    eye = (ri == ci).astype(F32)
    ms = [jnp.where(same, -n, 0.0) for n in nmats]
    ps = [eye + m for m in ms]
    for _ in range(sh - 1):
        ms = [_mm16(m, m) for m in ms]
        ps = [p + _mm16(p, m) for p, m in zip(ps, ms)]
    size = base
    while size < chunk:
        sh += 1
        size *= 2
        same_next = (ri >> sh) == (ci >> sh)
        sel = same_next & jnp.logical_not(same)
        ts = [_mm16(p, jnp.where(sel, n, 0.0)) for p, n in zip(ps, nmats)]
        ps = [p - _mm16(t, p) for p, t in zip(ps, ts)]
        same = same_next
    return ps


def _delta_body(q_ref, k_ref, v_ref, z_ref, gc_ref, bc_ref, s0_ref, nw_ref,
                o_ref, sf_ref, s_ref, *, chunk, cpb, hb):
    c_idx = pl.program_id(2)

    @pl.when(c_idx == 0)
    def _():
        s_ref[...] = s0_ref[0]

    ri = lax.broadcasted_iota(jnp.int32, (chunk, chunk), 0)
    ci = lax.broadcasted_iota(jnp.int32, (chunk, chunk), 1)
    causal = ri >= ci
    strict = ri > ci
    tril = causal.astype(F32)
    triu = (ri <= ci).astype(F32)
    eye = (ri == ci).astype(F32)
    hi = lax.Precision.HIGHEST
    nt = (((1,), (1,)), ((), ()))
    tn = (((0,), (0,)), ((), ()))
    hq = hb // 2

    kqk, pre = {}, {}
    for c in range(cpb):
        rows = slice(c * chunk, (c + 1) * chunk)
        for j in range(hq):
            k16 = k_ref[0, rows, j * HEAD:(j + 1) * HEAD]
            q16 = q_ref[0, rows, j * HEAD:(j + 1) * HEAD]
            kqk[c, j] = lax.dot_general(jnp.concatenate([k16, q16], axis=0), k16, nt,
                                        preferred_element_type=F32)
    nmats, keys = [], []
    for c in range(cpb):
        rows = slice(c * chunk, (c + 1) * chunk)
        g_raw = gc_ref[0, rows, :]
        gcol = jnp.dot(tril, g_raw, precision=hi, preferred_element_type=F32)
        grow = lax.dot_general(g_raw, triu, tn, precision=hi, preferred_element_type=F32)
        bcol = bc_ref[0, rows, :]
        brow = lax.dot_general(bcol, eye, tn, precision=hi, preferred_element_type=F32)
        for hh in range(hb):
            j = hh // 2
            k = k_ref[0, rows, j * HEAD:(j + 1) * HEAD].astype(F32)
            q = q_ref[0, rows, j * HEAD:(j + 1) * HEAD].astype(F32)
            v16 = v_ref[0, rows, hh * HEAD:(hh + 1) * HEAD]
            gc = gcol[:, hh:hh + 1]
            gr = grow[hh:hh + 1, :]
            beta = bcol[:, hh:hh + 1]
            decay = jnp.exp(jnp.where(causal, gc - gr, -jnp.inf))
            egc = jnp.exp(gc)
            g_last = gc[chunk - 1:chunk, :]
            kk = kqk[c, j][:chunk]
            qk = kqk[c, j][chunk:]
            nmats.append(jnp.where(strict, (kk * beta) * decay, 0.0))
            keys.append((c, hh))
            pre[c, hh] = dict(
                rhs=jnp.concatenate([v16, (k * egc).astype(BF16)], axis=1),
                brow=brow[hh:hh + 1, :],
                qg=(q * egc).astype(BF16),
                intra=(qk * decay).astype(BF16),
                kd=(k * jnp.exp(g_last - gc)).astype(BF16),
                sdec=jnp.exp(g_last))
    tinv = _tri_inverse_all(nmats, ri, ci, chunk)
    for key, t in zip(keys, tinv):
        pre[key]['sol'] = jnp.dot((t * pre[key]['brow']).astype(BF16), pre[key]['rhs'],
                                  preferred_element_type=F32)

    for c in range(cpb):
        rows = slice(c * chunk, (c + 1) * chunk)
        ws = []
        for hh in range(hb):
            d = pre[c, hh]
            wq = jnp.concatenate([d['sol'][:, HEAD:].astype(BF16), d['qg']], axis=0)
            ws.append(jnp.dot(wq, s_ref[hh].astype(BF16), preferred_element_type=F32))
        for hh in range(hb):
            d = pre[c, hh]
            v16 = (d['sol'][:, :HEAD] - ws[hh][:chunk]).astype(BF16)
            o = ws[hh][chunk:] + jnp.dot(d['intra'], v16, preferred_element_type=F32)
            s_ref[hh] = s_ref[hh] * d['sdec'] + lax.dot_general(d['kd'], v16, tn,
                                                                preferred_element_type=F32)
            o = o * lax.rsqrt(jnp.mean(o * o, axis=-1, keepdims=True) + RMS_EPS) * nw_ref[...]
            gate = z_ref[0, rows, hh * HEAD:(hh + 1) * HEAD]
            o_ref[0, rows, hh * HEAD:(hh + 1) * HEAD] = (o * gate).astype(BF16)

    @pl.when(c_idx == pl.num_programs(2) - 1)
    def _():
        sf_ref[0] = s_ref[...]


def _delta_call(qkv, z, g, beta, s0, norm_w, chunk, cpb, hb):
    n_b, t_len, _ = qkv.shape
    nhg = V_HEADS // hb
    tc = chunk * cpb
    n_tc = t_len // tc
    assert g.shape == (nhg, n_b * t_len, hb), g.shape
    hq = hb // 2
    assert KEY_DIM % (hb * HEAD) == 0, hb
    q_off = 0
    k_off = KEY_DIM // (hq * HEAD)
    v_off = 2 * KEY_DIM // (hb * HEAD)
    return pl.pallas_call(
        functools.partial(_delta_body, chunk=chunk, cpb=cpb, hb=hb),
        grid=(n_b, nhg, t_len // tc),
        in_specs=[pl.BlockSpec((1, tc, hq * HEAD), lambda b, h, c: (b, c, q_off + h)),
                  pl.BlockSpec((1, tc, hq * HEAD), lambda b, h, c: (b, c, k_off + h)),
                  pl.BlockSpec((1, tc, hb * HEAD), lambda b, h, c: (b, c, v_off + h)),
                  pl.BlockSpec((1, tc, hb * HEAD), lambda b, h, c: (b, c, h)),
                  pl.BlockSpec((1, tc, hb), lambda b, h, c: (h, b * n_tc + c, 0)),
                  pl.BlockSpec((1, tc, hb), lambda b, h, c: (h, b * n_tc + c, 0)),
                  pl.BlockSpec((1, hb, HEAD, HEAD), lambda b, h, c: (b, h, 0, 0)),
                  pl.BlockSpec((1, HEAD), lambda b, h, c: (0, 0))],
        out_specs=[pl.BlockSpec((1, tc, hb * HEAD), lambda b, h, c: (b, c, h)),
                   pl.BlockSpec((1, hb, HEAD, HEAD), lambda b, h, c: (b, h, 0, 0))],
        out_shape=[jax.ShapeDtypeStruct((n_b, t_len, VAL_DIM), BF16),
                   jax.ShapeDtypeStruct((n_b, V_HEADS, HEAD, HEAD), F32)],
        scratch_shapes=[pltpu.VMEM((hb, HEAD, HEAD), F32)],
        compiler_params=_cparams(("arbitrary", "arbitrary", "arbitrary")),
        name="gated_delta_rule",
    )(qkv, qkv, qkv, z, g, beta, s0, norm_w)


def _tile(m, pref):
    t = min(m, pref)
    assert m % t == 0, (m, t)
    return t


def _trunk(x, s5_re0, s5_im0, conv0, s0, p):
    n_b, t_len, _ = x.shape
    m = n_b * t_len
    tm = _tile(m, 512)
    tm_wide = _tile(m, 1024)

    tt = _tile(t_len, 256)
    multi_seq = SUBLANES * t_len <= 256 and n_b % SUBLANES == 0
    n_blk, n_state = (n_b // SUBLANES, SUBLANES) if multi_seq else (n_b, 1)
    if multi_seq:
        tt = SUBLANES * t_len
    g, sr, si = _s5_call(x.reshape(n_blk, n_state * t_len, D_MODEL), p['nm0'], p['s5_d'], p['wb'], p['wc'],
                         p['a_re'], p['a_im'], s5_re0.reshape(n_blk, n_state, S5_CH),
                         s5_im0.reshape(n_blk, n_state, S5_CH), tt, multi_seq)
    x2 = x.reshape(m, D_MODEL)
    x2 = _glu_call(g.reshape(m, D_MODEL), p['w_glu'], x2, tm_wide, 1024)
    x2 = _ffn_call(x2, p['nf0'], p['w_up'], p['w_down'], 0, p['nfin'], tm_wide, 512, False)

    chunk = min(t_len, 64)
    cpb = _tile(t_len // chunk, 8)
    hb = 8 if cpb > 1 else 16
    qkv, z, gg, beta, tails = _inproj_call(
        x2, p['nm1'], p['w_qkvz'], p['w_ba'], conv0, p['conv_w'], p['a_log'], p['dt_bias'],
        t_len, tm, hb)
    tpb = t_len // min(t_len, tm)
    new_conv = tails[tpb - 1::tpb]
    og, s_fin = _delta_call(qkv.reshape(n_b, t_len, -1), z.reshape(n_b, t_len, VAL_DIM),
                            gg, beta, s0, p['gdn_norm_w'], chunk, cpb, hb)
    x2 = _matmul_res_call(og.reshape(m, VAL_DIM), p['w_out'], x2, tm_wide, 1024)
    y = _ffn_call(x2, p['nf1'], p['w_up'], p['w_down'], 1, p['nfin'], tm_wide, 512, True)

    return (y.reshape(n_b, t_len, D_MODEL),
            sr.reshape(1, n_b, S5_GROUPS, S5_STATE), si.reshape(1, n_b, S5_GROUPS, S5_STATE),
            new_conv[None], s_fin[None])


def kernel(x_prompt, x_sample, state_s5_re, state_s5_im, state_gdn_conv, state_gdn_s,
           norm_mix, norm_ffn, norm_final,
           s5_lam_re, s5_lam_im, s5_log_dt, s5_b_re, s5_b_im, s5_c_re, s5_c_im,
           s5_d, s5_w_glu,
           gdn_w_in, gdn_conv_w, gdn_a_log, gdn_dt_bias, gdn_norm_w, gdn_w_out,
           ffn_w_up, ffn_w_down):
    wb, wc, a_re, a_im = _s5_params(s5_lam_re[0], s5_lam_im[0], s5_log_dt[0], s5_b_re[0], s5_b_im[0],
                              s5_c_re[0], s5_c_im[0])
    w_in = gdn_w_in[0]
    ba = w_in[:, CONV_DIM + VAL_DIM:]
    w_ba = jnp.concatenate([ba, jnp.zeros((D_MODEL, LANES - ba.shape[1]), w_in.dtype)], axis=1)
    row = lambda a: a.reshape(1, -1).astype(F32)
    p = dict(
        nm0=row(norm_mix[0]), nm1=row(norm_mix[1]), nf0=row(norm_ffn[0]), nf1=row(norm_ffn[1]),
        nfin=row(norm_final), s5_d=row(s5_d[0]), wb=wb, wc=wc, a_re=a_re, a_im=a_im,
        w_glu=s5_w_glu[0].astype(BF16),
        w_up=ffn_w_up.astype(BF16), w_down=ffn_w_down.astype(BF16),
        w_qkvz=w_in.astype(BF16), w_ba=w_ba.astype(BF16),
        conv_w=gdn_conv_w[0].astype(F32), a_log=row(gdn_a_log[0]), dt_bias=row(gdn_dt_bias[0]),
        gdn_norm_w=row(gdn_norm_w[0]), w_out=gdn_w_out[0].astype(BF16),
    )
    n_bp = x_prompt.shape[0]
    z_state = jnp.zeros((n_bp, S5_GROUPS, S5_STATE), F32)
    z_conv = jnp.zeros((n_bp, CONV_W - 1, CONV_DIM), F32)
    z_s = jnp.zeros((n_bp, V_HEADS, HEAD, HEAD), F32)

    yp, p_re, p_im, p_conv, p_s = _trunk(x_prompt, z_state, z_state, z_conv, z_s, p)
    ys, s_re, s_im, s_conv, s_s = _trunk(x_sample, state_s5_re[0], state_s5_im[0],
                                         state_gdn_conv[0], state_gdn_s[0], p)
    return (yp, ys, p_re, p_im, p_conv, p_s, s_re, s_im, s_conv, s_s)
```

```python
import functools
import math

import jax
import jax.numpy as jnp
from jax import lax
from jax.experimental import pallas as pl
from jax.experimental.pallas import tpu as pltpu

F32 = jnp.float32
BF16 = jnp.bfloat16

RMS_EPS = 1e-6
L2_EPS = 1e-6
D_MODEL = 2048
S5_GROUP = 16
S5_STATE = 64
S5_GROUPS = D_MODEL // S5_GROUP
S5_CH = S5_GROUPS * S5_STATE
HEAD = 128
QK_HEADS = D_MODEL // HEAD
V_HEADS = 2 * QK_HEADS
KEY_DIM = QK_HEADS * HEAD
VAL_DIM = V_HEADS * HEAD
CONV_DIM = 2 * KEY_DIM + VAL_DIM
CONV_W = 4

SUBLANES = 8
LANES = 128
MXU_K = 256
S5_KB = D_MODEL // MXU_K
S5_CB = S5_CH // S5_KB
S5_SCAN_W = S5_CB
VMEM_LIMIT = 60 * 1024 * 1024


def _cparams(sem):
    return pltpu.CompilerParams(dimension_semantics=sem, vmem_limit_bytes=VMEM_LIMIT)


def _rms(x, w):
    return x * lax.rsqrt(jnp.mean(x * x, axis=-1, keepdims=True) + RMS_EPS) * w


def _s5_body(x_ref, nw_ref, d_ref, wb_ref, wc_ref, coef_ref, perm_ref, permt_ref, s0r_ref, s0i_ref,
             g_ref, sr_ref, si_ref, car_ref, h_ref, bu_ref, st_ref, *, tt, multi_seq):
    t = pl.program_id(1)
    seg = tt // SUBLANES

    @pl.when(t == 0)
    def _():
        car_ref[0] = jnp.broadcast_to(s0r_ref[0], (SUBLANES, S5_CH))
        car_ref[1] = jnp.broadcast_to(s0i_ref[0], (SUBLANES, S5_CH))

    h = _rms(x_ref[0], nw_ref[...])
    h_ref[...] = h
    hp = jnp.dot(perm_ref[...], h.astype(BF16), preferred_element_type=F32).astype(BF16)
    row0 = lax.broadcasted_iota(jnp.int32, (SUBLANES, S5_SCAN_W), 0) == 0

    def step(ar, ai, sr, si, xr, xi):
        return ar * sr - ai * si + xr, ar * si + ai * sr + xi

    def project_in(kb):
        ks = slice(kb * MXU_K, (kb + 1) * MXU_K)
        bu_ref[kb % 2] = jnp.dot(hp[:, ks], wb_ref[kb], preferred_element_type=F32)

    def project_out(kb):
        ks = slice(kb * MXU_K, (kb + 1) * MXU_K)
        yp = jnp.dot(st_ref[kb % 2].astype(BF16), wc_ref[kb], preferred_element_type=F32)
        y_hi = yp.astype(BF16)
        y_lo = (yp - y_hi.astype(F32)).astype(BF16)
        y2 = jnp.dot(permt_ref[...], jnp.concatenate([y_hi, y_lo], axis=1),
                     preferred_element_type=F32)
        y = y2[:, :MXU_K] + y2[:, MXU_K:] + d_ref[:, ks] * h_ref[:, ks]
        g_ref[0, :, ks] = jax.nn.gelu(y).astype(BF16)

    def scan(kb, part):
        lo = part * S5_SCAN_W
        re_cols = slice(lo, lo + S5_SCAN_W)
        im_cols = slice(S5_CB + lo, S5_CB + lo + S5_SCAN_W)
        cs = slice(kb * S5_CB + lo, kb * S5_CB + lo + S5_SCAN_W)
        buf = kb % 2

        def bu_rows(j):
            rows = slice(j * SUBLANES, (j + 1) * SUBLANES)
            return bu_ref[buf, rows, re_cols], bu_ref[buf, rows, im_cols]

        ar = coef_ref[4, 0, :, cs]
        ai = coef_ref[4, 1, :, cs]
        if multi_seq:
            sr = car_ref[0, :, cs]
            si = car_ref[1, :, cs]
            for j in range(seg):
                sr, si = step(ar, ai, sr, si, *bu_rows(j))
                rows = slice(j * SUBLANES, (j + 1) * SUBLANES)
                st_ref[buf, rows, re_cols] = sr
                st_ref[buf, rows, im_cols] = si
            car_ref[0, :, cs] = sr
            car_ref[1, :, cs] = si
            return
        er, ei = bu_rows(0)
        for j in range(1, seg):
            er, ei = step(ar, ai, er, ei, *bu_rows(j))
        for lvl, sh in enumerate((1, 2, 4)):
            cr_ = coef_ref[lvl, 0, :, cs]
            ci_ = coef_ref[lvl, 1, :, cs]
            pr = pltpu.roll(er, sh, 0)
            pi = pltpu.roll(ei, sh, 0)
            er, ei = er + cr_ * pr - ci_ * pi, ei + cr_ * pi + ci_ * pr
        c0r = car_ref[0, :, cs]
        c0i = car_ref[1, :, cs]
        cr_ = coef_ref[3, 0, :, cs]
        ci_ = coef_ref[3, 1, :, cs]
        fr = er + cr_ * c0r - ci_ * c0i
        fi = ei + cr_ * c0i + ci_ * c0r
        car_ref[0, :, cs] = jnp.broadcast_to(fr[SUBLANES - 1:SUBLANES], (SUBLANES, S5_SCAN_W))
        car_ref[1, :, cs] = jnp.broadcast_to(fi[SUBLANES - 1:SUBLANES], (SUBLANES, S5_SCAN_W))
        sr = jnp.where(row0, c0r, pltpu.roll(fr, 1, 0))
        si = jnp.where(row0, c0i, pltpu.roll(fi, 1, 0))
        for j in range(seg):
            sr, si = step(ar, ai, sr, si, *bu_rows(j))
            rows = slice(j * SUBLANES, (j + 1) * SUBLANES)
            st_ref[buf, rows, re_cols] = sr
            st_ref[buf, rows, im_cols] = si

    project_in(0)
    for kb in range(S5_KB):
        if kb + 1 < S5_KB:
            project_in(kb + 1)
        if kb >= 1:
            project_out(kb - 1)
        for part in range(S5_CB // S5_SCAN_W):
            scan(kb, part)
    project_out(S5_KB - 1)

    @pl.when(t == pl.num_programs(1) - 1)
    def _():
        n_state = SUBLANES if multi_seq else 1
        sr_ref[0] = car_ref[0, 0:n_state, :]
        si_ref[0] = car_ref[1, 0:n_state, :]


def _s5_call(x, nw, d, wb, wc, a_re, a_im, s0r, s0i, tt, multi_seq):
    n_b, t_len, _ = x.shape
    seg = tt // SUBLANES
    n_state = s0r.shape[1]
    assert n_state == (SUBLANES if multi_seq else 1) and (not multi_seq or t_len == tt)
    coef = _s5_coef(a_re, a_im, seg)
    p_idx = jnp.arange(tt)
    src = (p_idx % SUBLANES) * seg + p_idx // SUBLANES
    perm = (src[:, None] == p_idx[None, :]).astype(BF16)
    const = lambda *shape: pl.BlockSpec(shape, lambda b, t: (0,) * len(shape))
    state = pl.BlockSpec((1, n_state, S5_CH), lambda b, t: (b, 0, 0))
    return pl.pallas_call(
        functools.partial(_s5_body, tt=tt, multi_seq=multi_seq),
        grid=(n_b, t_len // tt),
        in_specs=[pl.BlockSpec((1, tt, D_MODEL), lambda b, t: (b, t, 0)),
                  const(1, D_MODEL), const(1, D_MODEL),
                  const(S5_KB, MXU_K, 2 * S5_CB), const(S5_KB, 2 * S5_CB, MXU_K),
                  const(5, 2, SUBLANES, S5_CH), const(tt, tt), const(tt, tt), state, state],
        out_specs=[pl.BlockSpec((1, tt, D_MODEL), lambda b, t: (b, t, 0)), state, state],
        out_shape=[jax.ShapeDtypeStruct((n_b, t_len, D_MODEL), BF16),
                   jax.ShapeDtypeStruct((n_b, n_state, S5_CH), F32),
                   jax.ShapeDtypeStruct((n_b, n_state, S5_CH), F32)],
        scratch_shapes=[pltpu.VMEM((2, SUBLANES, S5_CH), F32),
                        pltpu.VMEM((tt, D_MODEL), F32),
                        pltpu.VMEM((2, tt, 2 * S5_CB), F32),
                        pltpu.VMEM((2, tt, 2 * S5_CB), F32)],
        compiler_params=_cparams(("arbitrary", "arbitrary")),
        name="s5_mixer",
    )(x, nw, d, wb, wc, coef, perm, perm.T, s0r, s0i)


def _s5_coef(a_re, a_im, seg):
    assert seg & (seg - 1) == 0, seg
    ar, ai = a_re, a_im
    for _ in range(int(math.log2(seg))):
        ar, ai = ar * ar - ai * ai, 2.0 * ar * ai
    pw_r, pw_i = [ar], [ai]
    for _ in range(SUBLANES - 1):
        pr, pi = pw_r[-1], pw_i[-1]
        pw_r.append(pr * ar - pi * ai)
        pw_i.append(pr * ai + pi * ar)
    row = jnp.arange(SUBLANES)[:, None]
    levels = []
    for sh in (1, 2, 4):
        m = (row >= sh).astype(F32)
        levels.append(jnp.stack([m * pw_r[sh - 1], m * pw_i[sh - 1]]))
    levels.append(jnp.stack([jnp.concatenate(pw_r, axis=0), jnp.concatenate(pw_i, axis=0)]))
    ones = jnp.ones((SUBLANES, 1), F32)
    levels.append(jnp.stack([ones * a_re, ones * a_im]))
    return jnp.stack(levels)


def _s5_params(lam_re, lam_im, log_dt, b_re, b_im, c_re, c_im):
    lr = jnp.minimum(lam_re.astype(F32), -1e-4)
    li = lam_im.astype(F32)
    dt = jnp.exp(log_dt.astype(F32))[:, None]
    mag = jnp.exp(lr * dt)
    ang = li * dt
    a_re = mag * jnp.cos(ang)
    a_im = mag * jnp.sin(ang)
    nr = a_re - 1.0
    den = lr * lr + li * li
    k_re = (nr * lr + a_im * li) / den
    k_im = (a_im * lr - nr * li) / den
    br = b_re.astype(F32)
    bi = b_im.astype(F32)
    bb_re = k_re[..., None] * br - k_im[..., None] * bi
    bb_im = k_re[..., None] * bi + k_im[..., None] * br

    gl = S5_GROUPS // S5_KB
    eye = jnp.eye(gl, dtype=F32)

    def blockdiag_in(bb):
        bbk = bb.reshape(S5_KB, gl, S5_STATE, S5_GROUP)
        w = jnp.einsum('kgph,gf->kghfp', bbk, eye)
        return w.reshape(S5_KB, gl * S5_GROUP, gl * S5_STATE)

    def blockdiag_out(c):
        ck = c.reshape(S5_KB, gl, S5_GROUP, S5_STATE)
        w = jnp.einsum('kghp,gf->kgpfh', ck, eye)
        return w.reshape(S5_KB, gl * S5_STATE, gl * S5_GROUP)

    wb = jnp.concatenate([blockdiag_in(bb_re), blockdiag_in(bb_im)], axis=2).astype(BF16)
    wc = jnp.concatenate([blockdiag_out(c_re.astype(F32)),
                          -blockdiag_out(c_im.astype(F32))], axis=1).astype(BF16)

    return wb, wc, a_re.reshape(1, S5_CH), a_im.reshape(1, S5_CH)


def _glu_body(g_ref, wa_ref, wg_ref, x_ref, o_ref):
    g = g_ref[...]
    za = jnp.dot(g, wa_ref[...], preferred_element_type=F32)
    zg = jnp.dot(g, wg_ref[...], preferred_element_type=F32)
    o_ref[...] = x_ref[...] + za * jax.nn.sigmoid(zg)


def _glu_call(g, w_glu, x, tm, tn):
    m = g.shape[0]
    nj = D_MODEL // tn
    return pl.pallas_call(
        _glu_body,
        grid=(m // tm, nj),
        in_specs=[pl.BlockSpec((tm, D_MODEL), lambda i, j: (i, 0)),
                  pl.BlockSpec((D_MODEL, tn), lambda i, j: (0, j)),
                  pl.BlockSpec((D_MODEL, tn), lambda i, j: (0, j + nj)),
                  pl.BlockSpec((tm, tn), lambda i, j: (i, j))],
        out_specs=pl.BlockSpec((tm, tn), lambda i, j: (i, j)),
        out_shape=jax.ShapeDtypeStruct((m, D_MODEL), F32),
        compiler_params=_cparams(("parallel", "arbitrary")),
        name="glu_residual",
    )(g, w_glu, w_glu, x)


def _ffn_body(x_ref, nw_ref, wu_ref, wd_ref, fw_ref, o_ref, h_ref, *, final_norm):
    f = pl.program_id(1)

    @pl.when(f == 0)
    def _():
        x = x_ref[...]
        h_ref[...] = _rms(x, nw_ref[...]).astype(BF16)
        o_ref[...] = x

    a = jnp.maximum(jnp.dot(h_ref[...], wu_ref[...], preferred_element_type=F32), 0.0)
    o_ref[...] += jnp.dot((a * a).astype(BF16), wd_ref[...], preferred_element_type=F32)

    if final_norm:
        @pl.when(f == pl.num_programs(1) - 1)
        def _():
            o_ref[...] = _rms(o_ref[...], fw_ref[...])


def _ffn_call(x, nw, w_up, w_down, layer, fw, tm, tf, final_norm):
    m = x.shape[0]
    d_ff = w_up.shape[2]
    return pl.pallas_call(
        functools.partial(_ffn_body, final_norm=final_norm),
        grid=(m // tm, d_ff // tf),
        in_specs=[pl.BlockSpec((tm, D_MODEL), lambda i, f: (i, 0)),
                  pl.BlockSpec((1, D_MODEL), lambda i, f: (0, 0)),
                  pl.BlockSpec((None, D_MODEL, tf), lambda i, f: (layer, 0, f)),
                  pl.BlockSpec((None, tf, D_MODEL), lambda i, f: (layer, f, 0)),
                  pl.BlockSpec((1, D_MODEL), lambda i, f: (0, 0))],
        out_specs=pl.BlockSpec((tm, D_MODEL), lambda i, f: (i, 0)),
        out_shape=jax.ShapeDtypeStruct((m, D_MODEL), F32),
        scratch_shapes=[pltpu.VMEM((tm, D_MODEL), BF16)],
        compiler_params=_cparams(("parallel", "arbitrary")),
        name="ffn_residual",
    )(x, nw, w_up, w_down, fw)


def _matmul_res_body(a_ref, w_ref, x_ref, o_ref):
    o_ref[...] = x_ref[...] + jnp.dot(a_ref[...], w_ref[...], preferred_element_type=F32)


def _matmul_res_call(a, w, x, tm, tn):
    m, k = a.shape
    n = w.shape[1]
    return pl.pallas_call(
        _matmul_res_body,
        grid=(m // tm, n // tn),
        in_specs=[pl.BlockSpec((tm, k), lambda i, j: (i, 0)),
                  pl.BlockSpec((k, tn), lambda i, j: (0, j)),
                  pl.BlockSpec((tm, tn), lambda i, j: (i, j))],
        out_specs=pl.BlockSpec((tm, tn), lambda i, j: (i, j)),
        out_shape=jax.ShapeDtypeStruct((m, n), F32),
        compiler_params=_cparams(("parallel", "arbitrary")),
        name="matmul_residual",
    )(a, w, x)


INPROJ_TN = 1024
N_CONV_TILES = CONV_DIM // INPROJ_TN
N_Z_TILES = VAL_DIM // INPROJ_TN
N_QK_TILES = 2 * KEY_DIM // INPROJ_TN
N_Q_TILES = KEY_DIM // INPROJ_TN
CONV_ROWS = 64


def _inproj_body(x_ref, nw_ref, w_ref, wba_ref, cprev_ref, cw_ref, alog_ref, dtb_ref,
                 qkv_ref, z_ref, g_ref, beta_ref, nc_ref, hs_ref, hist_ref, ra_ref, rb_ref,
                 *, tm, seq_rows, tiles_per_seq, gate_group):
    n_seq = tm // seq_rows
    i = pl.program_id(0)
    j = pl.program_id(1)
    tn = INPROJ_TN
    n_sub = tn // MXU_K
    tail = CONV_W - 1
    pad = SUBLANES

    @pl.when(j == 0)
    def _():
        hs_ref[...] = _rms(x_ref[...], nw_ref[...]).astype(BF16)
        ba = jnp.dot(hs_ref[...], wba_ref[...], preferred_element_type=F32)
        zz = ba[:, V_HEADS:2 * V_HEADS] + dtb_ref[...]
        softplus = jnp.maximum(zz, 0.0) + jnp.log1p(jnp.exp(-jnp.abs(zz)))
        g_all = -jnp.exp(alog_ref[...]) * softplus
        beta_all = jax.nn.sigmoid(ba[:, 0:V_HEADS])
        for gi in range(V_HEADS // gate_group):
            g_ref[gi] = g_all[:, gi * gate_group:(gi + 1) * gate_group]
            beta_ref[gi] = beta_all[:, gi * gate_group:(gi + 1) * gate_group]

    @pl.when((i % tiles_per_seq == 0) & (j == 0))
    def _():
        hist_ref[...] = jnp.concatenate(
            [jnp.zeros((pad - tail, CONV_DIM), F32), cprev_ref[0]], axis=0)

    def project(r_ref, half, s):
        c0 = half * tn + s * MXU_K
        r_ref[s, pad:pad + tm, :] = jnp.dot(hs_ref[...], w_ref[:, c0:c0 + MXU_K],
                                            preferred_element_type=F32)

    def conv_rows(win, c0):
        acc = win[pad:] * cw_ref[tail:tail + 1, c0:c0 + MXU_K]
        for d in range(1, CONV_W):
            shifted = pltpu.roll(win, d, 0)[pad:]
            acc = acc + shifted * cw_ref[tail - d:tail - d + 1, c0:c0 + MXU_K]
        return acc

    def epilogue(r_ref, tile, out_half, s):
        c0 = tile * tn + s * MXU_K
        o0 = out_half * tn + s * MXU_K
        l2 = tile < N_QK_TILES
        scale = HEAD ** -0.5 if tile < N_Q_TILES else 1.0

        def finish(acc, rows):
            c = acc * jax.nn.sigmoid(acc)
            if l2:
                for hd in range(MXU_K // HEAD):
                    ch = c[:, hd * HEAD:(hd + 1) * HEAD]
                    ch = ch * (lax.rsqrt(jnp.sum(ch * ch, axis=-1, keepdims=True) + L2_EPS) * scale)
                    qkv_ref[rows, o0 + hd * HEAD:o0 + (hd + 1) * HEAD] = ch.astype(BF16)
            else:
                qkv_ref[rows, o0:o0 + MXU_K] = c.astype(BF16)

        r_ref[s, 0:pad, :] = hist_ref[:, c0:c0 + MXU_K]
        hist_ref[:, c0:c0 + MXU_K] = r_ref[s, tm:tm + pad, :]
        for sq in range(n_seq):
            end = pad + (sq + 1) * seq_rows
            nc_ref[sq, :, c0:c0 + MXU_K] = r_ref[s, end - tail:end, :]
        for r0 in range(0, tm, CONV_ROWS):
            rc = min(CONV_ROWS, tm - r0)
            finish(conv_rows(r_ref[s, r0:r0 + rc + pad, :], c0), slice(r0, r0 + rc))
        for sq in range(1, n_seq):
            r0 = sq * seq_rows
            win = jnp.concatenate([jnp.zeros((pad - tail, MXU_K), F32), cprev_ref[sq, :, c0:c0 + MXU_K],
                                   r_ref[s, pad + r0:pad + r0 + pad, :]], axis=0)
            finish(conv_rows(win, c0), slice(r0, r0 + pad))

    def project_z(half, s):
        c0 = half * tn + s * MXU_K
        zz = jnp.dot(hs_ref[...], w_ref[:, c0:c0 + MXU_K], preferred_element_type=F32)
        z_ref[:, c0:c0 + MXU_K] = (zz * jax.nn.sigmoid(zz)).astype(BF16)

    def step_body(step):
        cur, prev = step % 2, (step - 1) % 2
        for half, r_ref in ((0, ra_ref), (1, rb_ref)):
            for s in range(n_sub):
                if step < n_conv_steps:
                    project(r_ref.at[cur], half, s)
                else:
                    project_z(half, s)
                if 0 < step <= n_conv_steps:
                    epilogue(r_ref.at[prev], 2 * (step - 1) + half, half, s)

    n_conv_steps = N_CONV_TILES // 2
    for step in range(n_conv_steps + 1):
        pl.when(j == step)(functools.partial(step_body, step))
    pl.when(j > n_conv_steps)(functools.partial(step_body, n_conv_steps + 1))


def _inproj_call(x, nw, w_qkvz, w_ba, conv_prev, conv_w, a_log, dt_bias, t_len, tm, gate_group):
    m = x.shape[0]
    n_grp = V_HEADS // gate_group
    tn2 = 2 * INPROJ_TN
    n_cs = N_CONV_TILES // 2
    n_zs = N_Z_TILES // 2
    seq_rows = min(t_len, tm)
    n_seq = tm // seq_rows
    tpb = t_len // seq_rows
    return pl.pallas_call(
        functools.partial(_inproj_body, tm=tm, seq_rows=seq_rows, tiles_per_seq=tpb,
                          gate_group=gate_group),
        grid=(m // tm, n_cs + n_zs),
        in_specs=[pl.BlockSpec((tm, D_MODEL), lambda i, j: (i, 0)),
                  pl.BlockSpec((1, D_MODEL), lambda i, j: (0, 0)),
                  pl.BlockSpec((D_MODEL, tn2), lambda i, j: (0, j)),
                  pl.BlockSpec((D_MODEL, LANES), lambda i, j: (0, 0)),
                  pl.BlockSpec((n_seq, CONV_W - 1, CONV_DIM), lambda i, j: (i // tpb, 0, 0)),
                  pl.BlockSpec((CONV_W, CONV_DIM), lambda i, j: (0, 0)),
                  pl.BlockSpec((1, V_HEADS), lambda i, j: (0, 0)),
                  pl.BlockSpec((1, V_HEADS), lambda i, j: (0, 0))],
        out_specs=[pl.BlockSpec((tm, tn2), lambda i, j: (i, jnp.clip(j - 1, 0, n_cs - 1))),
                   pl.BlockSpec((tm, tn2), lambda i, j: (i, jnp.maximum(j - n_cs, 0))),
                   pl.BlockSpec((n_grp, tm, gate_group), lambda i, j: (0, i, 0)),
                   pl.BlockSpec((n_grp, tm, gate_group), lambda i, j: (0, i, 0)),
                   pl.BlockSpec((n_seq, CONV_W - 1, CONV_DIM), lambda i, j: (i, 0, 0))],
        out_shape=[jax.ShapeDtypeStruct((m, CONV_DIM), BF16),
                   jax.ShapeDtypeStruct((m, VAL_DIM), BF16),
                   jax.ShapeDtypeStruct((n_grp, m, gate_group), F32),
                   jax.ShapeDtypeStruct((n_grp, m, gate_group), F32),
                   jax.ShapeDtypeStruct((m // seq_rows, CONV_W - 1, CONV_DIM), F32)],
        scratch_shapes=[pltpu.VMEM((tm, D_MODEL), BF16),
                        pltpu.VMEM((SUBLANES, CONV_DIM), F32),
                        pltpu.VMEM((2, INPROJ_TN // MXU_K, tm + SUBLANES, MXU_K), F32),
                        pltpu.VMEM((2, INPROJ_TN // MXU_K, tm + SUBLANES, MXU_K), F32)],
        compiler_params=_cparams(("arbitrary", "arbitrary")),
        name="gdn_inproj_conv",
    )(x, nw, w_qkvz, w_ba, conv_prev, conv_w, a_log, dt_bias)


INV_BASE = 16


def _mm16(a, b):
    return jnp.dot(a.astype(BF16), b.astype(BF16), preferred_element_type=F32)


def _tri_inverse_all(nmats, ri, ci, chunk):
    base = min(INV_BASE, chunk)
    sh = int(math.log2(base))
    same = (ri >> sh) == (ci >> sh)
    eye = (ri == ci).astype(F32)
    ms = [jnp.where(same, -n, 0.0) for n in nmats]
    ps = [eye + m for m in ms]
    for _ in range(sh - 1):
        ms = [_mm16(m, m) for m in ms]
        ps = [p + _mm16(p, m) for p, m in zip(ps, ms)]
    size = base
    while size < chunk:
        sh += 1
        size *= 2
        same_next = (ri >> sh) == (ci >> sh)
        sel = same_next & jnp.logical_not(same)
        ts = [_mm16(p, jnp.where(sel, n, 0.0)) for p, n in zip(ps, nmats)]
        ps = [p - _mm16(t, p) for p, t in zip(ps, ts)]
        same = same_next
    return ps


def _delta_body(q_ref, k_ref, v_ref, z_ref, gc_ref, bc_ref, s0_ref, nw_ref,
                o_ref, sf_ref, s_ref, *, chunk, cpb, hb):
    c_idx = pl.program_id(2)

    @pl.when(c_idx == 0)
    def _():
        s_ref[...] = s0_ref[0]

    ri = lax.broadcasted_iota(jnp.int32, (chunk, chunk), 0)
    ci = lax.broadcasted_iota(jnp.int32, (chunk, chunk), 1)
    causal = ri >= ci
    strict = ri > ci
    tril = causal.astype(F32)
    triu = (ri <= ci).astype(F32)
    eye = (ri == ci).astype(F32)
    hi = lax.Precision.HIGHEST
    nt = (((1,), (1,)), ((), ()))
    tn = (((0,), (0,)), ((), ()))
    hq = hb // 2

    kqk, pre = {}, {}
    for c in range(cpb):
        rows = slice(c * chunk, (c + 1) * chunk)
        for j in range(hq):
            k16 = k_ref[0, rows, j * HEAD:(j + 1) * HEAD]
            q16 = q_ref[0, rows, j * HEAD:(j + 1) * HEAD]
            kqk[c, j] = lax.dot_general(jnp.concatenate([k16, q16], axis=0), k16, nt,
                                        preferred_element_type=F32)
    nmats, keys = [], []
    for c in range(cpb):
        rows = slice(c * chunk, (c + 1) * chunk)
        g_raw = gc_ref[0, rows, :]
        gcol = jnp.dot(tril, g_raw, precision=hi, preferred_element_type=F32)
        grow = lax.dot_general(g_raw, triu, tn, precision=hi, preferred_element_type=F32)
        bcol = bc_ref[0, rows, :]
        brow = lax.dot_general(bcol, eye, tn, precision=hi, preferred_element_type=F32)
        for hh in range(hb):
            j = hh // 2
            k = k_ref[0, rows, j * HEAD:(j + 1) * HEAD].astype(F32)
            q = q_ref[0, rows, j * HEAD:(j + 1) * HEAD].astype(F32)
            v16 = v_ref[0, rows, hh * HEAD:(hh + 1) * HEAD]
            gc = gcol[:, hh:hh + 1]
            gr = grow[hh:hh + 1, :]
            beta = bcol[:, hh:hh + 1]
            decay = jnp.exp(jnp.where(causal, gc - gr, -jnp.inf))
            egc = jnp.exp(gc)
            g_last = gc[chunk - 1:chunk, :]
            kk = kqk[c, j][:chunk]
            qk = kqk[c, j][chunk:]
            nmats.append(jnp.where(strict, (kk * beta) * decay, 0.0))
            keys.append((c, hh))
            pre[c, hh] = dict(
                rhs=jnp.concatenate([v16, (k * egc).astype(BF16)], axis=1),
                brow=brow[hh:hh + 1, :],
                qg=(q * egc).astype(BF16),
                intra=(qk * decay).astype(BF16),
                kd=(k * jnp.exp(g_last - gc)).astype(BF16),
                sdec=jnp.exp(g_last))
    tinv = _tri_inverse_all(nmats, ri, ci, chunk)
    for key, t in zip(keys, tinv):
        pre[key]['sol'] = jnp.dot((t * pre[key]['brow']).astype(BF16), pre[key]['rhs'],
                                  preferred_element_type=F32)

    for c in range(cpb):
        rows = slice(c * chunk, (c + 1) * chunk)
        ws = []
        for hh in range(hb):
            d = pre[c, hh]
            wq = jnp.concatenate([d['sol'][:, HEAD:].astype(BF16), d['qg']], axis=0)
            ws.append(jnp.dot(wq, s_ref[hh].astype(BF16), preferred_element_type=F32))
        for hh in range(hb):
            d = pre[c, hh]
            v16 = (d['sol'][:, :HEAD] - ws[hh][:chunk]).astype(BF16)
            o = ws[hh][chunk:] + jnp.dot(d['intra'], v16, preferred_element_type=F32)
            s_ref[hh] = s_ref[hh] * d['sdec'] + lax.dot_general(d['kd'], v16, tn,
                                                                preferred_element_type=F32)
            o = o * lax.rsqrt(jnp.mean(o * o, axis=-1, keepdims=True) + RMS_EPS) * nw_ref[...]
            gate = z_ref[0, rows, hh * HEAD:(hh + 1) * HEAD]
            o_ref[0, rows, hh * HEAD:(hh + 1) * HEAD] = (o * gate).astype(BF16)

    @pl.when(c_idx == pl.num_programs(2) - 1)
    def _():
        sf_ref[0] = s_ref[...]


def _delta_call(qkv, z, g, beta, s0, norm_w, chunk, cpb, hb):
    n_b, t_len, _ = qkv.shape
    nhg = V_HEADS // hb
    tc = chunk * cpb
    n_tc = t_len // tc
    assert g.shape == (nhg, n_b * t_len, hb), g.shape
    hq = hb // 2
    assert KEY_DIM % (hb * HEAD) == 0, hb
    q_off = 0
    k_off = KEY_DIM // (hq * HEAD)
    v_off = 2 * KEY_DIM // (hb * HEAD)
    return pl.pallas_call(
        functools.partial(_delta_body, chunk=chunk, cpb=cpb, hb=hb),
        grid=(n_b, nhg, t_len // tc),
        in_specs=[pl.BlockSpec((1, tc, hq * HEAD), lambda b, h, c: (b, c, q_off + h)),
                  pl.BlockSpec((1, tc, hq * HEAD), lambda b, h, c: (b, c, k_off + h)),
                  pl.BlockSpec((1, tc, hb * HEAD), lambda b, h, c: (b, c, v_off + h)),
                  pl.BlockSpec((1, tc, hb * HEAD), lambda b, h, c: (b, c, h)),
                  pl.BlockSpec((1, tc, hb), lambda b, h, c: (h, b * n_tc + c, 0)),
                  pl.BlockSpec((1, tc, hb), lambda b, h, c: (h, b * n_tc + c, 0)),
                  pl.BlockSpec((1, hb, HEAD, HEAD), lambda b, h, c: (b, h, 0, 0)),
                  pl.BlockSpec((1, HEAD), lambda b, h, c: (0, 0))],
        out_specs=[pl.BlockSpec((1, tc, hb * HEAD), lambda b, h, c: (b, c, h)),
                   pl.BlockSpec((1, hb, HEAD, HEAD), lambda b, h, c: (b, h, 0, 0))],
        out_shape=[jax.ShapeDtypeStruct((n_b, t_len, VAL_DIM), BF16),
                   jax.ShapeDtypeStruct((n_b, V_HEADS, HEAD, HEAD), F32)],
        scratch_shapes=[pltpu.VMEM((hb, HEAD, HEAD), F32)],
        compiler_params=_cparams(("arbitrary", "arbitrary", "arbitrary")),
        name="gated_delta_rule",
    )(qkv, qkv, qkv, z, g, beta, s0, norm_w)


def _tile(m, pref):
    t = min(m, pref)
    assert m % t == 0, (m, t)
    return t


def _trunk(x, s5_re0, s5_im0, conv0, s0, p):
    n_b, t_len, _ = x.shape
    m = n_b * t_len
    tm = _tile(m, 512)
    tm_wide = _tile(m, 1024)

    tt = _tile(t_len, 256)
    multi_seq = SUBLANES * t_len <= 256 and n_b % SUBLANES == 0
    n_blk, n_state = (n_b // SUBLANES, SUBLANES) if multi_seq else (n_b, 1)
    if multi_seq:
        tt = SUBLANES * t_len
    g, sr, si = _s5_call(x.reshape(n_blk, n_state * t_len, D_MODEL), p['nm0'], p['s5_d'], p['wb'], p['wc'],
                         p['a_re'], p['a_im'], s5_re0.reshape(n_blk, n_state, S5_CH),
                         s5_im0.reshape(n_blk, n_state, S5_CH), tt, multi_seq)
    x2 = x.reshape(m, D_MODEL)
    x2 = _glu_call(g.reshape(m, D_MODEL), p['w_glu'], x2, tm_wide, 1024)
    x2 = _ffn_call(x2, p['nf0'], p['w_up'], p['w_down'], 0, p['nfin'], tm_wide, 1024, False)

    chunk = min(t_len, 64)
    cpb = _tile(t_len // chunk, 8)
    hb = 8 if cpb > 1 else 16
    qkv, z, gg, beta, tails = _inproj_call(
        x2, p['nm1'], p['w_qkvz'], p['w_ba'], conv0, p['conv_w'], p['a_log'], p['dt_bias'],
        t_len, tm, hb)
    tpb = t_len // min(t_len, tm)
    new_conv = tails[tpb - 1::tpb]
    og, s_fin = _delta_call(qkv.reshape(n_b, t_len, -1), z.reshape(n_b, t_len, VAL_DIM),
                            gg, beta, s0, p['gdn_norm_w'], chunk, cpb, hb)
    x2 = _matmul_res_call(og.reshape(m, VAL_DIM), p['w_out'], x2, tm_wide, 1024)
    y = _ffn_call(x2, p['nf1'], p['w_up'], p['w_down'], 1, p['nfin'], tm_wide, 1024, True)

    return (y.reshape(n_b, t_len, D_MODEL),
            sr.reshape(1, n_b, S5_GROUPS, S5_STATE), si.reshape(1, n_b, S5_GROUPS, S5_STATE),
            new_conv[None], s_fin[None])


def kernel(x_prompt, x_sample, state_s5_re, state_s5_im, state_gdn_conv, state_gdn_s,
           norm_mix, norm_ffn, norm_final,
           s5_lam_re, s5_lam_im, s5_log_dt, s5_b_re, s5_b_im, s5_c_re, s5_c_im,
           s5_d, s5_w_glu,
           gdn_w_in, gdn_conv_w, gdn_a_log, gdn_dt_bias, gdn_norm_w, gdn_w_out,
           ffn_w_up, ffn_w_down):
    wb, wc, a_re, a_im = _s5_params(s5_lam_re[0], s5_lam_im[0], s5_log_dt[0], s5_b_re[0], s5_b_im[0],
                              s5_c_re[0], s5_c_im[0])
    w_in = gdn_w_in[0]
    ba = w_in[:, CONV_DIM + VAL_DIM:]
    w_ba = jnp.concatenate([ba, jnp.zeros((D_MODEL, LANES - ba.shape[1]), w_in.dtype)], axis=1)
    row = lambda a: a.reshape(1, -1).astype(F32)
    p = dict(
        nm0=row(norm_mix[0]), nm1=row(norm_mix[1]), nf0=row(norm_ffn[0]), nf1=row(norm_ffn[1]),
        nfin=row(norm_final), s5_d=row(s5_d[0]), wb=wb, wc=wc, a_re=a_re, a_im=a_im,
        w_glu=s5_w_glu[0].astype(BF16),
        w_up=ffn_w_up.astype(BF16), w_down=ffn_w_down.astype(BF16),
        w_qkvz=w_in.astype(BF16), w_ba=w_ba.astype(BF16),
        conv_w=gdn_conv_w[0].astype(F32), a_log=row(gdn_a_log[0]), dt_bias=row(gdn_dt_bias[0]),
        gdn_norm_w=row(gdn_norm_w[0]), w_out=gdn_w_out[0].astype(BF16),
    )
    n_bp = x_prompt.shape[0]
    z_state = jnp.zeros((n_bp, S5_GROUPS, S5_STATE), F32)
    z_conv = jnp.zeros((n_bp, CONV_W - 1, CONV_DIM), F32)
    z_s = jnp.zeros((n_bp, V_HEADS, HEAD, HEAD), F32)

    yp, p_re, p_im, p_conv, p_s = _trunk(x_prompt, z_state, z_state, z_conv, z_s, p)
    ys, s_re, s_im, s_conv, s_s = _trunk(x_sample, state_s5_re[0], state_s5_im[0],
                                         state_gdn_conv[0], state_gdn_s[0], p)
    return (yp, ys, p_re, p_im, p_conv, p_s, s_re, s_im, s_conv, s_s)
```

```python
import functools
import math

import jax
import jax.numpy as jnp
from jax import lax
from jax.experimental import pallas as pl
from jax.experimental.pallas import tpu as pltpu

F32 = jnp.float32
BF16 = jnp.bfloat16

RMS_EPS = 1e-6
L2_EPS = 1e-6
D_MODEL = 2048
S5_GROUP = 16
S5_STATE = 64
S5_GROUPS = D_MODEL // S5_GROUP
S5_CH = S5_GROUPS * S5_STATE
HEAD = 128
QK_HEADS = D_MODEL // HEAD
V_HEADS = 2 * QK_HEADS
KEY_DIM = QK_HEADS * HEAD
VAL_DIM = V_HEADS * HEAD
CONV_DIM = 2 * KEY_DIM + VAL_DIM
CONV_W = 4

SUBLANES = 8
LANES = 128
MXU_K = 256
S5_KB = D_MODEL // MXU_K
S5_CB = S5_CH // S5_KB
S5_SCAN_W = S5_CB
VMEM_LIMIT = 60 * 1024 * 1024


def _cparams(sem):
    return pltpu.CompilerParams(dimension_semantics=sem, vmem_limit_bytes=VMEM_LIMIT)


def _rms(x, w):
    return x * lax.rsqrt(jnp.mean(x * x, axis=-1, keepdims=True) + RMS_EPS) * w


def _s5_body(x_ref, nw_ref, d_ref, wb_ref, wc_ref, coef_ref, perm_ref, permt_ref, s0r_ref, s0i_ref,
             g_ref, sr_ref, si_ref, car_ref, h_ref, bu_ref, st_ref, *, tt, multi_seq):
    t = pl.program_id(1)
    seg = tt // SUBLANES

    @pl.when(t == 0)
    def _():
        car_ref[0] = jnp.broadcast_to(s0r_ref[0], (SUBLANES, S5_CH))
        car_ref[1] = jnp.broadcast_to(s0i_ref[0], (SUBLANES, S5_CH))

    h = _rms(x_ref[0], nw_ref[...])
    h_ref[...] = h
    hp = jnp.dot(perm_ref[...], h.astype(BF16), preferred_element_type=F32).astype(BF16)
    row0 = lax.broadcasted_iota(jnp.int32, (SUBLANES, S5_SCAN_W), 0) == 0

    def step(ar, ai, sr, si, xr, xi):
        return ar * sr - ai * si + xr, ar * si + ai * sr + xi

    def project_in(kb):
        ks = slice(kb * MXU_K, (kb + 1) * MXU_K)
        bu_ref[kb % 2] = jnp.dot(hp[:, ks], wb_ref[kb], preferred_element_type=F32)

    def project_out(kb):
        ks = slice(kb * MXU_K, (kb + 1) * MXU_K)
        yp = jnp.dot(st_ref[kb % 2].astype(BF16), wc_ref[kb], preferred_element_type=F32)
        y_hi = yp.astype(BF16)
        y_lo = (yp - y_hi.astype(F32)).astype(BF16)
        y2 = jnp.dot(permt_ref[...], jnp.concatenate([y_hi, y_lo], axis=1),
                     preferred_element_type=F32)
        y = y2[:, :MXU_K] + y2[:, MXU_K:] + d_ref[:, ks] * h_ref[:, ks]
        g_ref[0, :, ks] = jax.nn.gelu(y).astype(BF16)

    def scan(kb, part):
        lo = part * S5_SCAN_W
        re_cols = slice(lo, lo + S5_SCAN_W)
        im_cols = slice(S5_CB + lo, S5_CB + lo + S5_SCAN_W)
        cs = slice(kb * S5_CB + lo, kb * S5_CB + lo + S5_SCAN_W)
        buf = kb % 2

        def bu_rows(j):
            rows = slice(j * SUBLANES, (j + 1) * SUBLANES)
            return bu_ref[buf, rows, re_cols], bu_ref[buf, rows, im_cols]

        ar = coef_ref[4, 0, :, cs]
        ai = coef_ref[4, 1, :, cs]
        if multi_seq:
            sr = car_ref[0, :, cs]
            si = car_ref[1, :, cs]
            for j in range(seg):
                sr, si = step(ar, ai, sr, si, *bu_rows(j))
                rows = slice(j * SUBLANES, (j + 1) * SUBLANES)
                st_ref[buf, rows, re_cols] = sr
                st_ref[buf, rows, im_cols] = si
            car_ref[0, :, cs] = sr
            car_ref[1, :, cs] = si
            return
        er, ei = bu_rows(0)
        for j in range(1, seg):
            er, ei = step(ar, ai, er, ei, *bu_rows(j))
        for lvl, sh in enumerate((1, 2, 4)):
            cr_ = coef_ref[lvl, 0, :, cs]
            ci_ = coef_ref[lvl, 1, :, cs]
            pr = pltpu.roll(er, sh, 0)
            pi = pltpu.roll(ei, sh, 0)
            er, ei = er + cr_ * pr - ci_ * pi, ei + cr_ * pi + ci_ * pr
        c0r = car_ref[0, :, cs]
        c0i = car_ref[1, :, cs]
        cr_ = coef_ref[3, 0, :, cs]
        ci_ = coef_ref[3, 1, :, cs]
        fr = er + cr_ * c0r - ci_ * c0i
        fi = ei + cr_ * c0i + ci_ * c0r
        car_ref[0, :, cs] = jnp.broadcast_to(fr[SUBLANES - 1:SUBLANES], (SUBLANES, S5_SCAN_W))
        car_ref[1, :, cs] = jnp.broadcast_to(fi[SUBLANES - 1:SUBLANES], (SUBLANES, S5_SCAN_W))
        sr = jnp.where(row0, c0r, pltpu.roll(fr, 1, 0))
        si = jnp.where(row0, c0i, pltpu.roll(fi, 1, 0))
        for j in range(seg):
            sr, si = step(ar, ai, sr, si, *bu_rows(j))
            rows = slice(j * SUBLANES, (j + 1) * SUBLANES)
            st_ref[buf, rows, re_cols] = sr
            st_ref[buf, rows, im_cols] = si

    project_in(0)
    for kb in range(S5_KB):
        if kb + 1 < S5_KB:
            project_in(kb + 1)
        if kb >= 1:
            project_out(kb - 1)
        for part in range(S5_CB // S5_SCAN_W):
            scan(kb, part)
    project_out(S5_KB - 1)

    @pl.when(t == pl.num_programs(1) - 1)
    def _():
        n_state = SUBLANES if multi_seq else 1
        sr_ref[0] = car_ref[0, 0:n_state, :]
        si_ref[0] = car_ref[1, 0:n_state, :]


def _s5_call(x, nw, d, wb, wc, a_re, a_im, s0r, s0i, tt, multi_seq):
    n_b, t_len, _ = x.shape
    seg = tt // SUBLANES
    n_state = s0r.shape[1]
    assert n_state == (SUBLANES if multi_seq else 1) and (not multi_seq or t_len == tt)
    coef = _s5_coef(a_re, a_im, seg)
    p_idx = jnp.arange(tt)
    src = (p_idx % SUBLANES) * seg + p_idx // SUBLANES
    perm = (src[:, None] == p_idx[None, :]).astype(BF16)
    const = lambda *shape: pl.BlockSpec(shape, lambda b, t: (0,) * len(shape))
    state = pl.BlockSpec((1, n_state, S5_CH), lambda b, t: (b, 0, 0))
    return pl.pallas_call(
        functools.partial(_s5_body, tt=tt, multi_seq=multi_seq),
        grid=(n_b, t_len // tt),
        in_specs=[pl.BlockSpec((1, tt, D_MODEL), lambda b, t: (b, t, 0)),
                  const(1, D_MODEL), const(1, D_MODEL),
                  const(S5_KB, MXU_K, 2 * S5_CB), const(S5_KB, 2 * S5_CB, MXU_K),
                  const(5, 2, SUBLANES, S5_CH), const(tt, tt), const(tt, tt), state, state],
        out_specs=[pl.BlockSpec((1, tt, D_MODEL), lambda b, t: (b, t, 0)), state, state],
        out_shape=[jax.ShapeDtypeStruct((n_b, t_len, D_MODEL), BF16),
                   jax.ShapeDtypeStruct((n_b, n_state, S5_CH), F32),
                   jax.ShapeDtypeStruct((n_b, n_state, S5_CH), F32)],
        scratch_shapes=[pltpu.VMEM((2, SUBLANES, S5_CH), F32),
                        pltpu.VMEM((tt, D_MODEL), F32),
                        pltpu.VMEM((2, tt, 2 * S5_CB), F32),
                        pltpu.VMEM((2, tt, 2 * S5_CB), F32)],
        compiler_params=_cparams(("arbitrary", "arbitrary")),
        name="s5_mixer",
    )(x, nw, d, wb, wc, coef, perm, perm.T, s0r, s0i)


def _s5_coef(a_re, a_im, seg):
    assert seg & (seg - 1) == 0, seg
    ar, ai = a_re, a_im
    for _ in range(int(math.log2(seg))):
        ar, ai = ar * ar - ai * ai, 2.0 * ar * ai
    pw_r, pw_i = [ar], [ai]
    for _ in range(SUBLANES - 1):
        pr, pi = pw_r[-1], pw_i[-1]
        pw_r.append(pr * ar - pi * ai)
        pw_i.append(pr * ai + pi * ar)
    row = jnp.arange(SUBLANES)[:, None]
    levels = []
    for sh in (1, 2, 4):
        m = (row >= sh).astype(F32)
        levels.append(jnp.stack([m * pw_r[sh - 1], m * pw_i[sh - 1]]))
    levels.append(jnp.stack([jnp.concatenate(pw_r, axis=0), jnp.concatenate(pw_i, axis=0)]))
    ones = jnp.ones((SUBLANES, 1), F32)
    levels.append(jnp.stack([ones * a_re, ones * a_im]))
    return jnp.stack(levels)


def _s5_params(lam_re, lam_im, log_dt, b_re, b_im, c_re, c_im):
    lr = jnp.minimum(lam_re.astype(F32), -1e-4)
    li = lam_im.astype(F32)
    dt = jnp.exp(log_dt.astype(F32))[:, None]
    mag = jnp.exp(lr * dt)
    ang = li * dt
    a_re = mag * jnp.cos(ang)
    a_im = mag * jnp.sin(ang)
    nr = a_re - 1.0
    den = lr * lr + li * li
    k_re = (nr * lr + a_im * li) / den
    k_im = (a_im * lr - nr * li) / den
    br = b_re.astype(F32)
    bi = b_im.astype(F32)
    bb_re = k_re[..., None] * br - k_im[..., None] * bi
    bb_im = k_re[..., None] * bi + k_im[..., None] * br

    gl = S5_GROUPS // S5_KB
    eye = jnp.eye(gl, dtype=F32)

    def blockdiag_in(bb):
        bbk = bb.reshape(S5_KB, gl, S5_STATE, S5_GROUP)
        w = jnp.einsum('kgph,gf->kghfp', bbk, eye)
        return w.reshape(S5_KB, gl * S5_GROUP, gl * S5_STATE)

    def blockdiag_out(c):
        ck = c.reshape(S5_KB, gl, S5_GROUP, S5_STATE)
        w = jnp.einsum('kghp,gf->kgpfh', ck, eye)
        return w.reshape(S5_KB, gl * S5_STATE, gl * S5_GROUP)

    wb = jnp.concatenate([blockdiag_in(bb_re), blockdiag_in(bb_im)], axis=2).astype(BF16)
    wc = jnp.concatenate([blockdiag_out(c_re.astype(F32)),
                          -blockdiag_out(c_im.astype(F32))], axis=1).astype(BF16)

    return wb, wc, a_re.reshape(1, S5_CH), a_im.reshape(1, S5_CH)


def _glu_body(g_ref, wa_ref, wg_ref, x_ref, o_ref):
    g = g_ref[...]
    za = jnp.dot(g, wa_ref[...], preferred_element_type=F32)
    zg = jnp.dot(g, wg_ref[...], preferred_element_type=F32)
    o_ref[...] = x_ref[...] + za * jax.nn.sigmoid(zg)


def _glu_call(g, w_glu, x, tm, tn):
    m = g.shape[0]
    nj = D_MODEL // tn
    return pl.pallas_call(
        _glu_body,
        grid=(m // tm, nj),
        in_specs=[pl.BlockSpec((tm, D_MODEL), lambda i, j: (i, 0)),
                  pl.BlockSpec((D_MODEL, tn), lambda i, j: (0, j)),
                  pl.BlockSpec((D_MODEL, tn), lambda i, j: (0, j + nj)),
                  pl.BlockSpec((tm, tn), lambda i, j: (i, j))],
        out_specs=pl.BlockSpec((tm, tn), lambda i, j: (i, j)),
        out_shape=jax.ShapeDtypeStruct((m, D_MODEL), F32),
        compiler_params=_cparams(("parallel", "arbitrary")),
        name="glu_residual",
    )(g, w_glu, w_glu, x)


def _ffn_body(x_ref, nw_ref, wu_ref, wd_ref, fw_ref, o_ref, h_ref, *, final_norm):
    f = pl.program_id(1)

    @pl.when(f == 0)
    def _():
        x = x_ref[...]
        h_ref[...] = _rms(x, nw_ref[...]).astype(BF16)
        o_ref[...] = x

    a = jnp.maximum(jnp.dot(h_ref[...], wu_ref[...], preferred_element_type=F32), 0.0)
    o_ref[...] += jnp.dot((a * a).astype(BF16), wd_ref[...], preferred_element_type=F32)

    if final_norm:
        @pl.when(f == pl.num_programs(1) - 1)
        def _():
            o_ref[...] = _rms(o_ref[...], fw_ref[...])


def _ffn_call(x, nw, w_up, w_down, layer, fw, tm, tf, final_norm):
    m = x.shape[0]
    d_ff = w_up.shape[2]
    return pl.pallas_call(
        functools.partial(_ffn_body, final_norm=final_norm),
        grid=(m // tm, d_ff // tf),
        in_specs=[pl.BlockSpec((tm, D_MODEL), lambda i, f: (i, 0)),
                  pl.BlockSpec((1, D_MODEL), lambda i, f: (0, 0)),
                  pl.BlockSpec((None, D_MODEL, tf), lambda i, f: (layer, 0, f)),
                  pl.BlockSpec((None, tf, D_MODEL), lambda i, f: (layer, f, 0)),
                  pl.BlockSpec((1, D_MODEL), lambda i, f: (0, 0))],
        out_specs=pl.BlockSpec((tm, D_MODEL), lambda i, f: (i, 0)),
        out_shape=jax.ShapeDtypeStruct((m, D_MODEL), F32),
        scratch_shapes=[pltpu.VMEM((tm, D_MODEL), BF16)],
        compiler_params=_cparams(("parallel", "arbitrary")),
        name="ffn_residual",
    )(x, nw, w_up, w_down, fw)


def _matmul_res_body(a_ref, w_ref, x_ref, o_ref):
    o_ref[...] = x_ref[...] + jnp.dot(a_ref[...], w_ref[...], preferred_element_type=F32)


def _matmul_res_call(a, w, x, tm, tn):
    m, k = a.shape
    n = w.shape[1]
    return pl.pallas_call(
        _matmul_res_body,
        grid=(m // tm, n // tn),
        in_specs=[pl.BlockSpec((tm, k), lambda i, j: (i, 0)),
                  pl.BlockSpec((k, tn), lambda i, j: (0, j)),
                  pl.BlockSpec((tm, tn), lambda i, j: (i, j))],
        out_specs=pl.BlockSpec((tm, tn), lambda i, j: (i, j)),
        out_shape=jax.ShapeDtypeStruct((m, n), F32),
        compiler_params=_cparams(("parallel", "arbitrary")),
        name="matmul_residual",
    )(a, w, x)


INPROJ_TN = 512
N_CONV_TILES = CONV_DIM // INPROJ_TN
N_Z_TILES = VAL_DIM // INPROJ_TN
N_QK_TILES = 2 * KEY_DIM // INPROJ_TN
N_Q_TILES = KEY_DIM // INPROJ_TN
CONV_ROWS = 64


def _inproj_body(x_ref, nw_ref, w_ref, wba_ref, cprev_ref, cw_ref, alog_ref, dtb_ref,
                 qkv_ref, z_ref, g_ref, beta_ref, nc_ref, hs_ref, hist_ref, ra_ref, rb_ref,
                 *, tm, seq_rows, tiles_per_seq, gate_group):
    n_seq = tm // seq_rows
    i = pl.program_id(0)
    j = pl.program_id(1)
    tn = INPROJ_TN
    n_sub = tn // MXU_K
    tail = CONV_W - 1
    pad = SUBLANES

    @pl.when(j == 0)
    def _():
        hs_ref[...] = _rms(x_ref[...], nw_ref[...]).astype(BF16)
        ba = jnp.dot(hs_ref[...], wba_ref[...], preferred_element_type=F32)
        zz = ba[:, V_HEADS:2 * V_HEADS] + dtb_ref[...]
        softplus = jnp.maximum(zz, 0.0) + jnp.log1p(jnp.exp(-jnp.abs(zz)))
        g_all = -jnp.exp(alog_ref[...]) * softplus
        beta_all = jax.nn.sigmoid(ba[:, 0:V_HEADS])
        for gi in range(V_HEADS // gate_group):
            g_ref[gi] = g_all[:, gi * gate_group:(gi + 1) * gate_group]
            beta_ref[gi] = beta_all[:, gi * gate_group:(gi + 1) * gate_group]

    @pl.when((i % tiles_per_seq == 0) & (j == 0))
    def _():
        hist_ref[...] = jnp.concatenate(
            [jnp.zeros((pad - tail, CONV_DIM), F32), cprev_ref[0]], axis=0)

    def project(r_ref, half, s):
        c0 = half * tn + s * MXU_K
        r_ref[s, pad:pad + tm, :] = jnp.dot(hs_ref[...], w_ref[:, c0:c0 + MXU_K],
                                            preferred_element_type=F32)

    def conv_rows(win, c0):
        acc = win[pad:] * cw_ref[tail:tail + 1, c0:c0 + MXU_K]
        for d in range(1, CONV_W):
            shifted = pltpu.roll(win, d, 0)[pad:]
            acc = acc + shifted * cw_ref[tail - d:tail - d + 1, c0:c0 + MXU_K]
        return acc

    def epilogue(r_ref, tile, out_half, s):
        c0 = tile * tn + s * MXU_K
        o0 = out_half * tn + s * MXU_K
        l2 = tile < N_QK_TILES
        scale = HEAD ** -0.5 if tile < N_Q_TILES else 1.0

        def finish(acc, rows):
            c = acc * jax.nn.sigmoid(acc)
            if l2:
                for hd in range(MXU_K // HEAD):
                    ch = c[:, hd * HEAD:(hd + 1) * HEAD]
                    ch = ch * (lax.rsqrt(jnp.sum(ch * ch, axis=-1, keepdims=True) + L2_EPS) * scale)
                    qkv_ref[rows, o0 + hd * HEAD:o0 + (hd + 1) * HEAD] = ch.astype(BF16)
            else:
                qkv_ref[rows, o0:o0 + MXU_K] = c.astype(BF16)

        r_ref[s, 0:pad, :] = hist_ref[:, c0:c0 + MXU_K]
        hist_ref[:, c0:c0 + MXU_K] = r_ref[s, tm:tm + pad, :]
        for sq in range(n_seq):
            end = pad + (sq + 1) * seq_rows
            nc_ref[sq, :, c0:c0 + MXU_K] = r_ref[s, end - tail:end, :]
        for r0 in range(0, tm, CONV_ROWS):
            rc = min(CONV_ROWS, tm - r0)
            finish(conv_rows(r_ref[s, r0:r0 + rc + pad, :], c0), slice(r0, r0 + rc))
        for sq in range(1, n_seq):
            r0 = sq * seq_rows
            win = jnp.concatenate([jnp.zeros((pad - tail, MXU_K), F32), cprev_ref[sq, :, c0:c0 + MXU_K],
                                   r_ref[s, pad + r0:pad + r0 + pad, :]], axis=0)
            finish(conv_rows(win, c0), slice(r0, r0 + pad))

    def project_z(half, s):
        c0 = half * tn + s * MXU_K
        zz = jnp.dot(hs_ref[...], w_ref[:, c0:c0 + MXU_K], preferred_element_type=F32)
        z_ref[:, c0:c0 + MXU_K] = (zz * jax.nn.sigmoid(zz)).astype(BF16)

    def step_body(step):
        cur, prev = step % 2, (step - 1) % 2
        for half, r_ref in ((0, ra_ref), (1, rb_ref)):
            for s in range(n_sub):
                if step < n_conv_steps:
                    project(r_ref.at[cur], half, s)
                else:
                    project_z(half, s)
                if 0 < step <= n_conv_steps:
                    epilogue(r_ref.at[prev], 2 * (step - 1) + half, half, s)

    n_conv_steps = N_CONV_TILES // 2
    for step in range(n_conv_steps + 1):
        pl.when(j == step)(functools.partial(step_body, step))
    pl.when(j > n_conv_steps)(functools.partial(step_body, n_conv_steps + 1))


def _inproj_call(x, nw, w_qkvz, w_ba, conv_prev, conv_w, a_log, dt_bias, t_len, tm, gate_group):
    m = x.shape[0]
    n_grp = V_HEADS // gate_group
    tn2 = 2 * INPROJ_TN
    n_cs = N_CONV_TILES // 2
    n_zs = N_Z_TILES // 2
    seq_rows = min(t_len, tm)
    n_seq = tm // seq_rows
    tpb = t_len // seq_rows
    return pl.pallas_call(
        functools.partial(_inproj_body, tm=tm, seq_rows=seq_rows, tiles_per_seq=tpb,
                          gate_group=gate_group),
        grid=(m // tm, n_cs + n_zs),
        in_specs=[pl.BlockSpec((tm, D_MODEL), lambda i, j: (i, 0)),
                  pl.BlockSpec((1, D_MODEL), lambda i, j: (0, 0)),
                  pl.BlockSpec((D_MODEL, tn2), lambda i, j: (0, j)),
                  pl.BlockSpec((D_MODEL, LANES), lambda i, j: (0, 0)),
                  pl.BlockSpec((n_seq, CONV_W - 1, CONV_DIM), lambda i, j: (i // tpb, 0, 0)),
                  pl.BlockSpec((CONV_W, CONV_DIM), lambda i, j: (0, 0)),
                  pl.BlockSpec((1, V_HEADS), lambda i, j: (0, 0)),
                  pl.BlockSpec((1, V_HEADS), lambda i, j: (0, 0))],
        out_specs=[pl.BlockSpec((tm, tn2), lambda i, j: (i, jnp.clip(j - 1, 0, n_cs - 1))),
                   pl.BlockSpec((tm, tn2), lambda i, j: (i, jnp.maximum(j - n_cs, 0))),
                   pl.BlockSpec((n_grp, tm, gate_group), lambda i, j: (0, i, 0)),
                   pl.BlockSpec((n_grp, tm, gate_group), lambda i, j: (0, i, 0)),
                   pl.BlockSpec((n_seq, CONV_W - 1, CONV_DIM), lambda i, j: (i, 0, 0))],
        out_shape=[jax.ShapeDtypeStruct((m, CONV_DIM), BF16),
                   jax.ShapeDtypeStruct((m, VAL_DIM), BF16),
                   jax.ShapeDtypeStruct((n_grp, m, gate_group), F32),
                   jax.ShapeDtypeStruct((n_grp, m, gate_group), F32),
                   jax.ShapeDtypeStruct((m // seq_rows, CONV_W - 1, CONV_DIM), F32)],
        scratch_shapes=[pltpu.VMEM((tm, D_MODEL), BF16),
                        pltpu.VMEM((SUBLANES, CONV_DIM), F32),
                        pltpu.VMEM((2, INPROJ_TN // MXU_K, tm + SUBLANES, MXU_K), F32),
                        pltpu.VMEM((2, INPROJ_TN // MXU_K, tm + SUBLANES, MXU_K), F32)],
        compiler_params=_cparams(("arbitrary", "arbitrary")),
        name="gdn_inproj_conv",
    )(x, nw, w_qkvz, w_ba, conv_prev, conv_w, a_log, dt_bias)


INV_BASE = 16


def _mm16(a, b):
    return jnp.dot(a.astype(BF16), b.astype(BF16), preferred_element_type=F32)


def _tri_inverse_all(nmats, ri, ci, chunk):
    base = min(INV_BASE, chunk)
    sh = int(math.log2(base))
    same = (ri >> sh) == (ci >> sh)
    eye = (ri == ci).astype(F32)
    ms = [jnp.where(same, -n, 0.0) for n in nmats]
    ps = [eye + m for m in ms]
    for _ in range(sh - 1):
        ms = [_mm16(m, m) for m in ms]
        ps = [p + _mm16(p, m) for p, m in zip(ps, ms)]
    size = base
    while size < chunk:
        sh += 1
        size *= 2
        same_next = (ri >> sh) == (ci >> sh)
        sel = same_next & jnp.logical_not(same)
        ts = [_mm16(p, jnp.where(sel, n, 0.0)) for p, n in zip(ps, nmats)]
        ps = [p - _mm16(t, p) for p, t in zip(ps, ts)]
        same = same_next
    return ps


def _delta_body(q_ref, k_ref, v_ref, z_ref, gc_ref, bc_ref, s0_ref, nw_ref,
                o_ref, sf_ref, s_ref, *, chunk, cpb, hb):
    c_idx = pl.program_id(2)

    @pl.when(c_idx == 0)
    def _():
        s_ref[...] = s0_ref[0]

    ri = lax.broadcasted_iota(jnp.int32, (chunk, chunk), 0)
    ci = lax.broadcasted_iota(jnp.int32, (chunk, chunk), 1)
    causal = ri >= ci
    strict = ri > ci
    tril = causal.astype(F32)
    triu = (ri <= ci).astype(F32)
    eye = (ri == ci).astype(F32)
    hi = lax.Precision.HIGHEST
    nt = (((1,), (1,)), ((), ()))
    tn = (((0,), (0,)), ((), ()))
    hq = hb // 2

    kqk, pre = {}, {}
    for c in range(cpb):
        rows = slice(c * chunk, (c + 1) * chunk)
        for j in range(hq):
            k16 = k_ref[0, rows, j * HEAD:(j + 1) * HEAD]
            q16 = q_ref[0, rows, j * HEAD:(j + 1) * HEAD]
            kqk[c, j] = lax.dot_general(jnp.concatenate([k16, q16], axis=0), k16, nt,
                                        preferred_element_type=F32)
    nmats, keys = [], []
    for c in range(cpb):
        rows = slice(c * chunk, (c + 1) * chunk)
        g_raw = gc_ref[0, rows, :]
        gcol = jnp.dot(tril, g_raw, precision=hi, preferred_element_type=F32)
        grow = lax.dot_general(g_raw, triu, tn, precision=hi, preferred_element_type=F32)
        bcol = bc_ref[0, rows, :]
        brow = lax.dot_general(bcol, eye, tn, precision=hi, preferred_element_type=F32)
        for hh in range(hb):
            j = hh // 2
            k = k_ref[0, rows, j * HEAD:(j + 1) * HEAD].astype(F32)
            q = q_ref[0, rows, j * HEAD:(j + 1) * HEAD].astype(F32)
            v16 = v_ref[0, rows, hh * HEAD:(hh + 1) * HEAD]
            gc = gcol[:, hh:hh + 1]
            gr = grow[hh:hh + 1, :]
            beta = bcol[:, hh:hh + 1]
            decay = jnp.exp(jnp.where(causal, gc - gr, -jnp.inf))
            egc = jnp.exp(gc)
            g_last = gc[chunk - 1:chunk, :]
            kk = kqk[c, j][:chunk]
            qk = kqk[c, j][chunk:]
            nmats.append(jnp.where(strict, (kk * beta) * decay, 0.0))
            keys.append((c, hh))
            pre[c, hh] = dict(
                rhs=jnp.concatenate([v16, (k * egc).astype(BF16)], axis=1),
                brow=brow[hh:hh + 1, :],
                qg=(q * egc).astype(BF16),
                intra=(qk * decay).astype(BF16),
                kd=(k * jnp.exp(g_last - gc)).astype(BF16),
                sdec=jnp.exp(g_last))
    tinv = _tri_inverse_all(nmats, ri, ci, chunk)
    for key, t in zip(keys, tinv):
        pre[key]['sol'] = jnp.dot((t * pre[key]['brow']).astype(BF16), pre[key]['rhs'],
                                  preferred_element_type=F32)

    for c in range(cpb):
        rows = slice(c * chunk, (c + 1) * chunk)
        ws = []
        for hh in range(hb):
            d = pre[c, hh]
            wq = jnp.concatenate([d['sol'][:, HEAD:].astype(BF16), d['qg']], axis=0)
            ws.append(jnp.dot(wq, s_ref[hh].astype(BF16), preferred_element_type=F32))
        for hh in range(hb):
            d = pre[c, hh]
            v16 = (d['sol'][:, :HEAD] - ws[hh][:chunk]).astype(BF16)
            o = ws[hh][chunk:] + jnp.dot(d['intra'], v16, preferred_element_type=F32)
            s_ref[hh] = s_ref[hh] * d['sdec'] + lax.dot_general(d['kd'], v16, tn,
                                                                preferred_element_type=F32)
            o = o * lax.rsqrt(jnp.mean(o * o, axis=-1, keepdims=True) + RMS_EPS) * nw_ref[...]
            gate = z_ref[0, rows, hh * HEAD:(hh + 1) * HEAD]
            o_ref[0, rows, hh * HEAD:(hh + 1) * HEAD] = (o * gate).astype(BF16)

    @pl.when(c_idx == pl.num_programs(2) - 1)
    def _():
        sf_ref[0] = s_ref[...]


def _delta_call(qkv, z, g, beta, s0, norm_w, chunk, cpb, hb):
    n_b, t_len, _ = qkv.shape
    nhg = V_HEADS // hb
    tc = chunk * cpb
    n_tc = t_len // tc
    assert g.shape == (nhg, n_b * t_len, hb), g.shape
    hq = hb // 2
    assert KEY_DIM % (hb * HEAD) == 0, hb
    q_off = 0
    k_off = KEY_DIM // (hq * HEAD)
    v_off = 2 * KEY_DIM // (hb * HEAD)
    return pl.pallas_call(
        functools.partial(_delta_body, chunk=chunk, cpb=cpb, hb=hb),
        grid=(n_b, nhg, t_len // tc),
        in_specs=[pl.BlockSpec((1, tc, hq * HEAD), lambda b, h, c: (b, c, q_off + h)),
                  pl.BlockSpec((1, tc, hq * HEAD), lambda b, h, c: (b, c, k_off + h)),
                  pl.BlockSpec((1, tc, hb * HEAD), lambda b, h, c: (b, c, v_off + h)),
                  pl.BlockSpec((1, tc, hb * HEAD), lambda b, h, c: (b, c, h)),
                  pl.BlockSpec((1, tc, hb), lambda b, h, c: (h, b * n_tc + c, 0)),
                  pl.BlockSpec((1, tc, hb), lambda b, h, c: (h, b * n_tc + c, 0)),
                  pl.BlockSpec((1, hb, HEAD, HEAD), lambda b, h, c: (b, h, 0, 0)),
                  pl.BlockSpec((1, HEAD), lambda b, h, c: (0, 0))],
        out_specs=[pl.BlockSpec((1, tc, hb * HEAD), lambda b, h, c: (b, c, h)),
                   pl.BlockSpec((1, hb, HEAD, HEAD), lambda b, h, c: (b, h, 0, 0))],
        out_shape=[jax.ShapeDtypeStruct((n_b, t_len, VAL_DIM), BF16),
                   jax.ShapeDtypeStruct((n_b, V_HEADS, HEAD, HEAD), F32)],
        scratch_shapes=[pltpu.VMEM((hb, HEAD, HEAD), F32)],
        compiler_params=_cparams(("arbitrary", "arbitrary", "arbitrary")),
        name="gated_delta_rule",
    )(qkv, qkv, qkv, z, g, beta, s0, norm_w)


def _tile(m, pref):
    t = min(m, pref)
    assert m % t == 0, (m, t)
    return t


def _trunk(x, s5_re0, s5_im0, conv0, s0, p):
    n_b, t_len, _ = x.shape
    m = n_b * t_len
    tm = _tile(m, 512)
    tm_wide = _tile(m, 1024)

    tt = _tile(t_len, 256)
    multi_seq = SUBLANES * t_len <= 256 and n_b % SUBLANES == 0
    n_blk, n_state = (n_b // SUBLANES, SUBLANES) if multi_seq else (n_b, 1)
    if multi_seq:
        tt = SUBLANES * t_len
    g, sr, si = _s5_call(x.reshape(n_blk, n_state * t_len, D_MODEL), p['nm0'], p['s5_d'], p['wb'], p['wc'],
                         p['a_re'], p['a_im'], s5_re0.reshape(n_blk, n_state, S5_CH),
                         s5_im0.reshape(n_blk, n_state, S5_CH), tt, multi_seq)
    x2 = x.reshape(m, D_MODEL)
    x2 = _glu_call(g.reshape(m, D_MODEL), p['w_glu'], x2, tm_wide, 1024)
    x2 = _ffn_call(x2, p['nf0'], p['w_up'], p['w_down'], 0, p['nfin'], tm_wide, 1024, False)

    chunk = min(t_len, 64)
    cpb = _tile(t_len // chunk, 8)
    hb = 8 if cpb > 1 else 16
    tm_in = tm_wide if t_len >= tm_wide else tm
    qkv, z, gg, beta, tails = _inproj_call(
        x2, p['nm1'], p['w_qkvz'], p['w_ba'], conv0, p['conv_w'], p['a_log'], p['dt_bias'],
        t_len, tm_in, hb)
    tpb = t_len // min(t_len, tm_in)
    new_conv = tails[tpb - 1::tpb]
    og, s_fin = _delta_call(qkv.reshape(n_b, t_len, -1), z.reshape(n_b, t_len, VAL_DIM),
                            gg, beta, s0, p['gdn_norm_w'], chunk, cpb, hb)
    x2 = _matmul_res_call(og.reshape(m, VAL_DIM), p['w_out'], x2, tm_wide, 1024)
    y = _ffn_call(x2, p['nf1'], p['w_up'], p['w_down'], 1, p['nfin'], tm_wide, 1024, True)

    return (y.reshape(n_b, t_len, D_MODEL),
            sr.reshape(1, n_b, S5_GROUPS, S5_STATE), si.reshape(1, n_b, S5_GROUPS, S5_STATE),
            new_conv[None], s_fin[None])


def kernel(x_prompt, x_sample, state_s5_re, state_s5_im, state_gdn_conv, state_gdn_s,
           norm_mix, norm_ffn, norm_final,
           s5_lam_re, s5_lam_im, s5_log_dt, s5_b_re, s5_b_im, s5_c_re, s5_c_im,
           s5_d, s5_w_glu,
           gdn_w_in, gdn_conv_w, gdn_a_log, gdn_dt_bias, gdn_norm_w, gdn_w_out,
           ffn_w_up, ffn_w_down):
    wb, wc, a_re, a_im = _s5_params(s5_lam_re[0], s5_lam_im[0], s5_log_dt[0], s5_b_re[0], s5_b_im[0],
                              s5_c_re[0], s5_c_im[0])
    w_in = gdn_w_in[0]
    ba = w_in[:, CONV_DIM + VAL_DIM:]
    w_ba = jnp.concatenate([ba, jnp.zeros((D_MODEL, LANES - ba.shape[1]), w_in.dtype)], axis=1)
    row = lambda a: a.reshape(1, -1).astype(F32)
    p = dict(
        nm0=row(norm_mix[0]), nm1=row(norm_mix[1]), nf0=row(norm_ffn[0]), nf1=row(norm_ffn[1]),
        nfin=row(norm_final), s5_d=row(s5_d[0]), wb=wb, wc=wc, a_re=a_re, a_im=a_im,
        w_glu=s5_w_glu[0].astype(BF16),
        w_up=ffn_w_up.astype(BF16), w_down=ffn_w_down.astype(BF16),
        w_qkvz=w_in.astype(BF16), w_ba=w_ba.astype(BF16),
        conv_w=gdn_conv_w[0].astype(F32), a_log=row(gdn_a_log[0]), dt_bias=row(gdn_dt_bias[0]),
        gdn_norm_w=row(gdn_norm_w[0]), w_out=gdn_w_out[0].astype(BF16),
    )
    n_bp = x_prompt.shape[0]
    z_state = jnp.zeros((n_bp, S5_GROUPS, S5_STATE), F32)
    z_conv = jnp.zeros((n_bp, CONV_W - 1, CONV_DIM), F32)
    z_s = jnp.zeros((n_bp, V_HEADS, HEAD, HEAD), F32)

    yp, p_re, p_im, p_conv, p_s = _trunk(x_prompt, z_state, z_state, z_conv, z_s, p)
    ys, s_re, s_im, s_conv, s_s = _trunk(x_sample, state_s5_re[0], state_s5_im[0],
                                         state_gdn_conv[0], state_gdn_s[0], p)
    return (yp, ys, p_re, p_im, p_conv, p_s, s_re, s_im, s_conv, s_s)
```

```python
import functools
import math

import jax
import jax.numpy as jnp
from jax import lax
from jax.experimental import pallas as pl
from jax.experimental.pallas import tpu as pltpu

F32 = jnp.float32
BF16 = jnp.bfloat16

RMS_EPS = 1e-6
L2_EPS = 1e-6
D_MODEL = 2048
S5_GROUP = 16
S5_STATE = 64
S5_GROUPS = D_MODEL // S5_GROUP
S5_CH = S5_GROUPS * S5_STATE
HEAD = 128
QK_HEADS = D_MODEL // HEAD
V_HEADS = 2 * QK_HEADS
KEY_DIM = QK_HEADS * HEAD
VAL_DIM = V_HEADS * HEAD
CONV_DIM = 2 * KEY_DIM + VAL_DIM
CONV_W = 4

SUBLANES = 8
LANES = 128
MXU_K = 256
S5_KB = D_MODEL // MXU_K
S5_CB = S5_CH // S5_KB
S5_SCAN_W = S5_CB
VMEM_LIMIT = 60 * 1024 * 1024


def _cparams(sem):
    return pltpu.CompilerParams(dimension_semantics=sem, vmem_limit_bytes=VMEM_LIMIT)


def _rms(x, w):
    return x * lax.rsqrt(jnp.mean(x * x, axis=-1, keepdims=True) + RMS_EPS) * w


def _s5_body(x_ref, nw_ref, d_ref, wb_ref, wc_ref, coef_ref, perm_ref, permt_ref, s0r_ref, s0i_ref,
             g_ref, sr_ref, si_ref, car_ref, h_ref, bu_ref, st_ref, *, tt, multi_seq):
    t = pl.program_id(1)
    seg = tt // SUBLANES

    @pl.when(t == 0)
    def _():
        car_ref[0] = jnp.broadcast_to(s0r_ref[0], (SUBLANES, S5_CH))
        car_ref[1] = jnp.broadcast_to(s0i_ref[0], (SUBLANES, S5_CH))

    h = _rms(x_ref[0], nw_ref[...])
    h_ref[...] = h
    hp = jnp.dot(perm_ref[...], h.astype(BF16), preferred_element_type=F32).astype(BF16)
    row0 = lax.broadcasted_iota(jnp.int32, (SUBLANES, S5_SCAN_W), 0) == 0

    def step(ar, ai, sr, si, xr, xi):
        return ar * sr - ai * si + xr, ar * si + ai * sr + xi

    def project_in(kb):
        ks = slice(kb * MXU_K, (kb + 1) * MXU_K)
        bu_ref[kb % 2] = jnp.dot(hp[:, ks], wb_ref[kb], preferred_element_type=F32)

    def project_out(kb):
        ks = slice(kb * MXU_K, (kb + 1) * MXU_K)
        yp = jnp.dot(st_ref[kb % 2].astype(BF16), wc_ref[kb], preferred_element_type=F32)
        y_hi = yp.astype(BF16)
        y_lo = (yp - y_hi.astype(F32)).astype(BF16)
        y2 = jnp.dot(permt_ref[...], jnp.concatenate([y_hi, y_lo], axis=1),
                     preferred_element_type=F32)
        y = y2[:, :MXU_K] + y2[:, MXU_K:] + d_ref[:, ks] * h_ref[:, ks]
        g_ref[0, :, ks] = jax.nn.gelu(y).astype(BF16)

    def scan(kb, part):
        lo = part * S5_SCAN_W
        re_cols = slice(lo, lo + S5_SCAN_W)
        im_cols = slice(S5_CB + lo, S5_CB + lo + S5_SCAN_W)
        cs = slice(kb * S5_CB + lo, kb * S5_CB + lo + S5_SCAN_W)
        buf = kb % 2

        def bu_rows(j):
            rows = slice(j * SUBLANES, (j + 1) * SUBLANES)
            return bu_ref[buf, rows, re_cols], bu_ref[buf, rows, im_cols]

        ar = coef_ref[4, 0, :, cs]
        ai = coef_ref[4, 1, :, cs]
        if multi_seq:
            sr = car_ref[0, :, cs]
            si = car_ref[1, :, cs]
            for j in range(seg):
                sr, si = step(ar, ai, sr, si, *bu_rows(j))
                rows = slice(j * SUBLANES, (j + 1) * SUBLANES)
                st_ref[buf, rows, re_cols] = sr
                st_ref[buf, rows, im_cols] = si
            car_ref[0, :, cs] = sr
            car_ref[1, :, cs] = si
            return
        er, ei = bu_rows(0)
        for j in range(1, seg):
            er, ei = step(ar, ai, er, ei, *bu_rows(j))
        for lvl, sh in enumerate((1, 2, 4)):
            cr_ = coef_ref[lvl, 0, :, cs]
            ci_ = coef_ref[lvl, 1, :, cs]
            pr = pltpu.roll(er, sh, 0)
            pi = pltpu.roll(ei, sh, 0)
            er, ei = er + cr_ * pr - ci_ * pi, ei + cr_ * pi + ci_ * pr
        c0r = car_ref[0, :, cs]
        c0i = car_ref[1, :, cs]
        cr_ = coef_ref[3, 0, :, cs]
        ci_ = coef_ref[3, 1, :, cs]
        fr = er + cr_ * c0r - ci_ * c0i
        fi = ei + cr_ * c0i + ci_ * c0r
        car_ref[0, :, cs] = jnp.broadcast_to(fr[SUBLANES - 1:SUBLANES], (SUBLANES, S5_SCAN_W))
        car_ref[1, :, cs] = jnp.broadcast_to(fi[SUBLANES - 1:SUBLANES], (SUBLANES, S5_SCAN_W))
        sr = jnp.where(row0, c0r, pltpu.roll(fr, 1, 0))
        si = jnp.where(row0, c0i, pltpu.roll(fi, 1, 0))
        for j in range(seg):
            sr, si = step(ar, ai, sr, si, *bu_rows(j))
            rows = slice(j * SUBLANES, (j + 1) * SUBLANES)
            st_ref[buf, rows, re_cols] = sr
            st_ref[buf, rows, im_cols] = si

    project_in(0)
    for kb in range(S5_KB):
        if kb + 1 < S5_KB:
            project_in(kb + 1)
        if kb >= 1:
            project_out(kb - 1)
        for part in range(S5_CB // S5_SCAN_W):
            scan(kb, part)
    project_out(S5_KB - 1)

    @pl.when(t == pl.num_programs(1) - 1)
    def _():
        n_state = SUBLANES if multi_seq else 1
        sr_ref[0] = car_ref[0, 0:n_state, :]
        si_ref[0] = car_ref[1, 0:n_state, :]


def _s5_call(x, nw, d, wb, wc, a_re, a_im, s0r, s0i, tt, multi_seq):
    n_b, t_len, _ = x.shape
    seg = tt // SUBLANES
    n_state = s0r.shape[1]
    assert n_state == (SUBLANES if multi_seq else 1) and (not multi_seq or t_len == tt)
    coef = _s5_coef(a_re, a_im, seg)
    p_idx = jnp.arange(tt)
    src = (p_idx % SUBLANES) * seg + p_idx // SUBLANES
    perm = (src[:, None] == p_idx[None, :]).astype(BF16)
    const = lambda *shape: pl.BlockSpec(shape, lambda b, t: (0,) * len(shape))
    state = pl.BlockSpec((1, n_state, S5_CH), lambda b, t: (b, 0, 0))
    return pl.pallas_call(
        functools.partial(_s5_body, tt=tt, multi_seq=multi_seq),
        grid=(n_b, t_len // tt),
        in_specs=[pl.BlockSpec((1, tt, D_MODEL), lambda b, t: (b, t, 0)),
                  const(1, D_MODEL), const(1, D_MODEL),
                  const(S5_KB, MXU_K, 2 * S5_CB), const(S5_KB, 2 * S5_CB, MXU_K),
                  const(5, 2, SUBLANES, S5_CH), const(tt, tt), const(tt, tt), state, state],
        out_specs=[pl.BlockSpec((1, tt, D_MODEL), lambda b, t: (b, t, 0)), state, state],
        out_shape=[jax.ShapeDtypeStruct((n_b, t_len, D_MODEL), BF16),
                   jax.ShapeDtypeStruct((n_b, n_state, S5_CH), F32),
                   jax.ShapeDtypeStruct((n_b, n_state, S5_CH), F32)],
        scratch_shapes=[pltpu.VMEM((2, SUBLANES, S5_CH), F32),
                        pltpu.VMEM((tt, D_MODEL), F32),
                        pltpu.VMEM((2, tt, 2 * S5_CB), F32),
                        pltpu.VMEM((2, tt, 2 * S5_CB), F32)],
        compiler_params=_cparams(("arbitrary", "arbitrary")),
        name="s5_mixer",
    )(x, nw, d, wb, wc, coef, perm, perm.T, s0r, s0i)


def _s5_coef(a_re, a_im, seg):
    assert seg & (seg - 1) == 0, seg
    ar, ai = a_re, a_im
    for _ in range(int(math.log2(seg))):
        ar, ai = ar * ar - ai * ai, 2.0 * ar * ai
    pw_r, pw_i = [ar], [ai]
    for _ in range(SUBLANES - 1):
        pr, pi = pw_r[-1], pw_i[-1]
        pw_r.append(pr * ar - pi * ai)
        pw_i.append(pr * ai + pi * ar)
    row = jnp.arange(SUBLANES)[:, None]
    levels = []
    for sh in (1, 2, 4):
        m = (row >= sh).astype(F32)
        levels.append(jnp.stack([m * pw_r[sh - 1], m * pw_i[sh - 1]]))
    levels.append(jnp.stack([jnp.concatenate(pw_r, axis=0), jnp.concatenate(pw_i, axis=0)]))
    ones = jnp.ones((SUBLANES, 1), F32)
    levels.append(jnp.stack([ones * a_re, ones * a_im]))
    return jnp.stack(levels)


def _s5_params(lam_re, lam_im, log_dt, b_re, b_im, c_re, c_im):
    lr = jnp.minimum(lam_re.astype(F32), -1e-4)
    li = lam_im.astype(F32)
    dt = jnp.exp(log_dt.astype(F32))[:, None]
    mag = jnp.exp(lr * dt)
    ang = li * dt
    a_re = mag * jnp.cos(ang)
    a_im = mag * jnp.sin(ang)
    nr = a_re - 1.0
    den = lr * lr + li * li
    k_re = (nr * lr + a_im * li) / den
    k_im = (a_im * lr - nr * li) / den
    br = b_re.astype(F32)
    bi = b_im.astype(F32)
    bb_re = k_re[..., None] * br - k_im[..., None] * bi
    bb_im = k_re[..., None] * bi + k_im[..., None] * br

    gl = S5_GROUPS // S5_KB
    eye = jnp.eye(gl, dtype=F32)

    def blockdiag_in(bb):
        bbk = bb.reshape(S5_KB, gl, S5_STATE, S5_GROUP)
        w = jnp.einsum('kgph,gf->kghfp', bbk, eye)
        return w.reshape(S5_KB, gl * S5_GROUP, gl * S5_STATE)

    def blockdiag_out(c):
        ck = c.reshape(S5_KB, gl, S5_GROUP, S5_STATE)
        w = jnp.einsum('kghp,gf->kgpfh', ck, eye)
        return w.reshape(S5_KB, gl * S5_STATE, gl * S5_GROUP)

    wb = jnp.concatenate([blockdiag_in(bb_re), blockdiag_in(bb_im)], axis=2).astype(BF16)
    wc = jnp.concatenate([blockdiag_out(c_re.astype(F32)),
                          -blockdiag_out(c_im.astype(F32))], axis=1).astype(BF16)

    return wb, wc, a_re.reshape(1, S5_CH), a_im.reshape(1, S5_CH)


def _glu_body(g_ref, wa_ref, wg_ref, x_ref, o_ref):
    g = g_ref[...]
    za = jnp.dot(g, wa_ref[...], preferred_element_type=F32)
    zg = jnp.dot(g, wg_ref[...], preferred_element_type=F32)
    o_ref[...] = x_ref[...] + za * jax.nn.sigmoid(zg)


def _glu_call(g, w_glu, x, tm, tn):
    m = g.shape[0]
    nj = D_MODEL // tn
    return pl.pallas_call(
        _glu_body,
        grid=(m // tm, nj),
        in_specs=[pl.BlockSpec((tm, D_MODEL), lambda i, j: (i, 0)),
                  pl.BlockSpec((D_MODEL, tn), lambda i, j: (0, j)),
                  pl.BlockSpec((D_MODEL, tn), lambda i, j: (0, j + nj)),
                  pl.BlockSpec((tm, tn), lambda i, j: (i, j))],
        out_specs=pl.BlockSpec((tm, tn), lambda i, j: (i, j)),
        out_shape=jax.ShapeDtypeStruct((m, D_MODEL), F32),
        compiler_params=_cparams(("parallel", "arbitrary")),
        name="glu_residual",
    )(g, w_glu, w_glu, x)


def _ffn_body(x_ref, nw_ref, wu_ref, wd_ref, fw_ref, o_ref, h_ref, *, final_norm):
    f = pl.program_id(1)

    @pl.when(f == 0)
    def _():
        x = x_ref[...]
        h_ref[...] = _rms(x, nw_ref[...]).astype(BF16)
        o_ref[...] = x

    a = jnp.maximum(jnp.dot(h_ref[...], wu_ref[...], preferred_element_type=F32), 0.0)
    o_ref[...] += jnp.dot((a * a).astype(BF16), wd_ref[...], preferred_element_type=F32)

    if final_norm:
        @pl.when(f == pl.num_programs(1) - 1)
        def _():
            o_ref[...] = _rms(o_ref[...], fw_ref[...])


def _ffn_call(x, nw, w_up, w_down, layer, fw, tm, tf, final_norm):
    m = x.shape[0]
    d_ff = w_up.shape[2]
    return pl.pallas_call(
        functools.partial(_ffn_body, final_norm=final_norm),
        grid=(m // tm, d_ff // tf),
        in_specs=[pl.BlockSpec((tm, D_MODEL), lambda i, f: (i, 0)),
                  pl.BlockSpec((1, D_MODEL), lambda i, f: (0, 0)),
                  pl.BlockSpec((None, D_MODEL, tf), lambda i, f: (layer, 0, f)),
                  pl.BlockSpec((None, tf, D_MODEL), lambda i, f: (layer, f, 0)),
                  pl.BlockSpec((1, D_MODEL), lambda i, f: (0, 0))],
        out_specs=pl.BlockSpec((tm, D_MODEL), lambda i, f: (i, 0)),
        out_shape=jax.ShapeDtypeStruct((m, D_MODEL), F32),
        scratch_shapes=[pltpu.VMEM((tm, D_MODEL), BF16)],
        compiler_params=_cparams(("parallel", "arbitrary")),
        name="ffn_residual",
    )(x, nw, w_up, w_down, fw)


def _matmul_res_body(a_ref, w_ref, x_ref, o_ref):
    o_ref[...] = x_ref[...] + jnp.dot(a_ref[...], w_ref[...], preferred_element_type=F32)


def _matmul_res_call(a, w, x, tm, tn):
    m, k = a.shape
    n = w.shape[1]
    return pl.pallas_call(
        _matmul_res_body,
        grid=(m // tm, n // tn),
        in_specs=[pl.BlockSpec((tm, k), lambda i, j: (i, 0)),
                  pl.BlockSpec((k, tn), lambda i, j: (0, j)),
                  pl.BlockSpec((tm, tn), lambda i, j: (i, j))],
        out_specs=pl.BlockSpec((tm, tn), lambda i, j: (i, j)),
        out_shape=jax.ShapeDtypeStruct((m, n), F32),
        compiler_params=_cparams(("parallel", "arbitrary")),
        name="matmul_residual",
    )(a, w, x)


INPROJ_TN = 1024
N_CONV_TILES = CONV_DIM // INPROJ_TN
N_Z_TILES = VAL_DIM // INPROJ_TN
N_QK_TILES = 2 * KEY_DIM // INPROJ_TN
N_Q_TILES = KEY_DIM // INPROJ_TN
CONV_ROWS = 64


def _inproj_body(x_ref, nw_ref, w_ref, wba_ref, cprev_ref, cw_ref, alog_ref, dtb_ref,
                 qkv_ref, z_ref, g_ref, beta_ref, nc_ref, hs_ref, hist_ref, ra_ref, rb_ref,
                 *, tm, seq_rows, tiles_per_seq, gate_group):
    n_seq = tm // seq_rows
    i = pl.program_id(0)
    j = pl.program_id(1)
    tn = INPROJ_TN
    n_sub = tn // MXU_K
    tail = CONV_W - 1
    pad = SUBLANES

    @pl.when(j == 0)
    def _():
        hs_ref[...] = _rms(x_ref[...], nw_ref[...]).astype(BF16)
        ba = jnp.dot(hs_ref[...], wba_ref[...], preferred_element_type=F32)
        zz = ba[:, V_HEADS:2 * V_HEADS] + dtb_ref[...]
        softplus = jnp.maximum(zz, 0.0) + jnp.log1p(jnp.exp(-jnp.abs(zz)))
        g_all = -jnp.exp(alog_ref[...]) * softplus
        beta_all = jax.nn.sigmoid(ba[:, 0:V_HEADS])
        for gi in range(V_HEADS // gate_group):
            g_ref[gi] = g_all[:, gi * gate_group:(gi + 1) * gate_group]
            beta_ref[gi] = beta_all[:, gi * gate_group:(gi + 1) * gate_group]

    @pl.when((i % tiles_per_seq == 0) & (j == 0))
    def _():
        hist_ref[...] = jnp.concatenate(
            [jnp.zeros((pad - tail, CONV_DIM), F32), cprev_ref[0]], axis=0)

    def project(r_ref, half, s):
        c0 = half * tn + s * MXU_K
        r_ref[s, pad:pad + tm, :] = jnp.dot(hs_ref[...], w_ref[:, c0:c0 + MXU_K],
                                            preferred_element_type=F32)

    def conv_rows(win, c0):
        acc = win[pad:] * cw_ref[tail:tail + 1, c0:c0 + MXU_K]
        for d in range(1, CONV_W):
            shifted = pltpu.roll(win, d, 0)[pad:]
            acc = acc + shifted * cw_ref[tail - d:tail - d + 1, c0:c0 + MXU_K]
        return acc

    def epilogue(r_ref, tile, out_half, s):
        c0 = tile * tn + s * MXU_K
        o0 = out_half * tn + s * MXU_K
        l2 = tile < N_QK_TILES
        scale = HEAD ** -0.5 if tile < N_Q_TILES else 1.0

        def finish(acc, rows):
            c = acc * jax.nn.sigmoid(acc)
            if l2:
                for hd in range(MXU_K // HEAD):
                    ch = c[:, hd * HEAD:(hd + 1) * HEAD]
                    ch = ch * (lax.rsqrt(jnp.sum(ch * ch, axis=-1, keepdims=True) + L2_EPS) * scale)
                    qkv_ref[rows, o0 + hd * HEAD:o0 + (hd + 1) * HEAD] = ch.astype(BF16)
            else:
                qkv_ref[rows, o0:o0 + MXU_K] = c.astype(BF16)

        r_ref[s, 0:pad, :] = hist_ref[:, c0:c0 + MXU_K]
        hist_ref[:, c0:c0 + MXU_K] = r_ref[s, tm:tm + pad, :]
        for sq in range(n_seq):
            end = pad + (sq + 1) * seq_rows
            nc_ref[sq, :, c0:c0 + MXU_K] = r_ref[s, end - tail:end, :]
        for r0 in range(0, tm, CONV_ROWS):
            rc = min(CONV_ROWS, tm - r0)
            finish(conv_rows(r_ref[s, r0:r0 + rc + pad, :], c0), slice(r0, r0 + rc))
        for sq in range(1, n_seq):
            r0 = sq * seq_rows
            win = jnp.concatenate([jnp.zeros((pad - tail, MXU_K), F32), cprev_ref[sq, :, c0:c0 + MXU_K],
                                   r_ref[s, pad + r0:pad + r0 + pad, :]], axis=0)
            finish(conv_rows(win, c0), slice(r0, r0 + pad))

    def project_z(half, s):
        c0 = half * tn + s * MXU_K
        zz = jnp.dot(hs_ref[...], w_ref[:, c0:c0 + MXU_K], preferred_element_type=F32)
        z_ref[:, c0:c0 + MXU_K] = (zz * jax.nn.sigmoid(zz)).astype(BF16)

    def step_body(step):
        cur, prev = step % 2, (step - 1) % 2
        for half, r_ref in ((0, ra_ref), (1, rb_ref)):
            for s in range(n_sub):
                if step < n_conv_steps:
                    project(r_ref.at[cur], half, s)
                else:
                    project_z(half, s)
                if 0 < step <= n_conv_steps:
                    epilogue(r_ref.at[prev], 2 * (step - 1) + half, half, s)

    n_conv_steps = N_CONV_TILES // 2
    for step in range(n_conv_steps + 1):
        pl.when(j == step)(functools.partial(step_body, step))
    pl.when(j > n_conv_steps)(functools.partial(step_body, n_conv_steps + 1))


def _inproj_call(x, nw, w_qkvz, w_ba, conv_prev, conv_w, a_log, dt_bias, t_len, tm, gate_group):
    m = x.shape[0]
    n_grp = V_HEADS // gate_group
    tn2 = 2 * INPROJ_TN
    n_cs = N_CONV_TILES // 2
    n_zs = N_Z_TILES // 2
    seq_rows = min(t_len, tm)
    n_seq = tm // seq_rows
    tpb = t_len // seq_rows
    return pl.pallas_call(
        functools.partial(_inproj_body, tm=tm, seq_rows=seq_rows, tiles_per_seq=tpb,
                          gate_group=gate_group),
        grid=(m // tm, n_cs + n_zs),
        in_specs=[pl.BlockSpec((tm, D_MODEL), lambda i, j: (i, 0)),
                  pl.BlockSpec((1, D_MODEL), lambda i, j: (0, 0)),
                  pl.BlockSpec((D_MODEL, tn2), lambda i, j: (0, j)),
                  pl.BlockSpec((D_MODEL, LANES), lambda i, j: (0, 0)),
                  pl.BlockSpec((n_seq, CONV_W - 1, CONV_DIM), lambda i, j: (i // tpb, 0, 0)),
                  pl.BlockSpec((CONV_W, CONV_DIM), lambda i, j: (0, 0)),
                  pl.BlockSpec((1, V_HEADS), lambda i, j: (0, 0)),
                  pl.BlockSpec((1, V_HEADS), lambda i, j: (0, 0))],
        out_specs=[pl.BlockSpec((tm, tn2), lambda i, j: (i, jnp.clip(j - 1, 0, n_cs - 1))),
                   pl.BlockSpec((tm, tn2), lambda i, j: (i, jnp.maximum(j - n_cs, 0))),
                   pl.BlockSpec((n_grp, tm, gate_group), lambda i, j: (0, i, 0)),
                   pl.BlockSpec((n_grp, tm, gate_group), lambda i, j: (0, i, 0)),
                   pl.BlockSpec((n_seq, CONV_W - 1, CONV_DIM), lambda i, j: (i, 0, 0))],
        out_shape=[jax.ShapeDtypeStruct((m, CONV_DIM), BF16),
                   jax.ShapeDtypeStruct((m, VAL_DIM), BF16),
                   jax.ShapeDtypeStruct((n_grp, m, gate_group), F32),
                   jax.ShapeDtypeStruct((n_grp, m, gate_group), F32),
                   jax.ShapeDtypeStruct((m // seq_rows, CONV_W - 1, CONV_DIM), F32)],
        scratch_shapes=[pltpu.VMEM((tm, D_MODEL), BF16),
                        pltpu.VMEM((SUBLANES, CONV_DIM), F32),
                        pltpu.VMEM((2, INPROJ_TN // MXU_K, tm + SUBLANES, MXU_K), F32),
                        pltpu.VMEM((2, INPROJ_TN // MXU_K, tm + SUBLANES, MXU_K), F32)],
        compiler_params=_cparams(("arbitrary", "arbitrary")),
        name="gdn_inproj_conv",
    )(x, nw, w_qkvz, w_ba, conv_prev, conv_w, a_log, dt_bias)


INV_BASE = 16


def _mm16(a, b):
    return jnp.dot(a.astype(BF16), b.astype(BF16), preferred_element_type=F32)


def _tri_inverse_all(nmats, ri, ci, chunk):
    base = min(INV_BASE, chunk)
    sh = int(math.log2(base))
    same = (ri >> sh) == (ci >> sh)
    eye = (ri == ci).astype(F32)
    ms = [jnp.where(same, -n, 0.0) for n in nmats]
    ps = [eye + m for m in ms]
    for _ in range(sh - 1):
        ms = [_mm16(m, m) for m in ms]
        ps = [p + _mm16(p, m) for p, m in zip(ps, ms)]
    size = base
    while size < chunk:
        sh += 1
        size *= 2
        same_next = (ri >> sh) == (ci >> sh)
        sel = same_next & jnp.logical_not(same)
        ts = [_mm16(p, jnp.where(sel, n, 0.0)) for p, n in zip(ps, nmats)]
        ps = [p - _mm16(t, p) for p, t in zip(ps, ts)]
        same = same_next
    return ps


def _delta_body(q_ref, k_ref, v_ref, z_ref, gc_ref, bc_ref, s0_ref, nw_ref,
                o_ref, sf_ref, s_ref, *, chunk, cpb, hb):
    c_idx = pl.program_id(2)

    @pl.when(c_idx == 0)
    def _():
        s_ref[...] = s0_ref[0]

    ri = lax.broadcasted_iota(jnp.int32, (chunk, chunk), 0)
    ci = lax.broadcasted_iota(jnp.int32, (chunk, chunk), 1)
    causal = ri >= ci
    strict = ri > ci
    tril = causal.astype(F32)
    triu = (ri <= ci).astype(F32)
    eye = (ri == ci).astype(F32)
    hi = lax.Precision.HIGHEST
    nt = (((1,), (1,)), ((), ()))
    tn = (((0,), (0,)), ((), ()))
    hq = hb // 2

    kqk, pre = {}, {}
    for c in range(cpb):
        rows = slice(c * chunk, (c + 1) * chunk)
        for j in range(hq):
            k16 = k_ref[0, rows, j * HEAD:(j + 1) * HEAD]
            q16 = q_ref[0, rows, j * HEAD:(j + 1) * HEAD]
            kqk[c, j] = lax.dot_general(jnp.concatenate([k16, q16], axis=0), k16, nt,
                                        preferred_element_type=F32)
    nmats, keys = [], []
    for c in range(cpb):
        rows = slice(c * chunk, (c + 1) * chunk)
        g_raw = gc_ref[0, rows, :]
        gcol = jnp.dot(tril, g_raw, precision=hi, preferred_element_type=F32)
        grow = lax.dot_general(g_raw, triu, tn, precision=hi, preferred_element_type=F32)
        bcol = bc_ref[0, rows, :]
        brow = lax.dot_general(bcol, eye, tn, precision=hi, preferred_element_type=F32)
        for hh in range(hb):
            j = hh // 2
            k = k_ref[0, rows, j * HEAD:(j + 1) * HEAD].astype(F32)
            q = q_ref[0, rows, j * HEAD:(j + 1) * HEAD].astype(F32)
            v16 = v_ref[0, rows, hh * HEAD:(hh + 1) * HEAD]
            gc = gcol[:, hh:hh + 1]
            gr = grow[hh:hh + 1, :]
            beta = bcol[:, hh:hh + 1]
            decay = jnp.exp(jnp.where(causal, gc - gr, -jnp.inf))
            egc = jnp.exp(gc)
            g_last = gc[chunk - 1:chunk, :]
            kk = kqk[c, j][:chunk]
            qk = kqk[c, j][chunk:]
            nmats.append(jnp.where(strict, (kk * beta) * decay, 0.0))
            keys.append((c, hh))
            pre[c, hh] = dict(
                rhs=jnp.concatenate([v16, (k * egc).astype(BF16)], axis=1),
                brow=brow[hh:hh + 1, :],
                qg=(q * egc).astype(BF16),
                intra=(qk * decay).astype(BF16),
                kd=(k * jnp.exp(g_last - gc)).astype(BF16),
                sdec=jnp.exp(g_last))
    tinv = _tri_inverse_all(nmats, ri, ci, chunk)
    for key, t in zip(keys, tinv):
        pre[key]['sol'] = jnp.dot((t * pre[key]['brow']).astype(BF16), pre[key]['rhs'],
                                  preferred_element_type=F32)

    for c in range(cpb):
        rows = slice(c * chunk, (c + 1) * chunk)
        ws = []
        for hh in range(hb):
            d = pre[c, hh]
            wq = jnp.concatenate([d['sol'][:, HEAD:].astype(BF16), d['qg']], axis=0)
            ws.append(jnp.dot(wq, s_ref[hh].astype(BF16), preferred_element_type=F32))
        for hh in range(hb):
            d = pre[c, hh]
            v16 = (d['sol'][:, :HEAD] - ws[hh][:chunk]).astype(BF16)
            o = ws[hh][chunk:] + jnp.dot(d['intra'], v16, preferred_element_type=F32)
            s_ref[hh] = s_ref[hh] * d['sdec'] + lax.dot_general(d['kd'], v16, tn,
                                                                preferred_element_type=F32)
            o = o * lax.rsqrt(jnp.mean(o * o, axis=-1, keepdims=True) + RMS_EPS) * nw_ref[...]
            gate = z_ref[0, rows, hh * HEAD:(hh + 1) * HEAD]
            o_ref[0, rows, hh * HEAD:(hh + 1) * HEAD] = (o * gate).astype(BF16)

    @pl.when(c_idx == pl.num_programs(2) - 1)
    def _():
        sf_ref[0] = s_ref[...]


def _delta_call(qkv, z, g, beta, s0, norm_w, chunk, cpb, hb):
    n_b, t_len, _ = qkv.shape
    nhg = V_HEADS // hb
    tc = chunk * cpb
    n_tc = t_len // tc
    assert g.shape == (nhg, n_b * t_len, hb), g.shape
    hq = hb // 2
    assert KEY_DIM % (hb * HEAD) == 0, hb
    q_off = 0
    k_off = KEY_DIM // (hq * HEAD)
    v_off = 2 * KEY_DIM // (hb * HEAD)
    return pl.pallas_call(
        functools.partial(_delta_body, chunk=chunk, cpb=cpb, hb=hb),
        grid=(n_b, nhg, t_len // tc),
        in_specs=[pl.BlockSpec((1, tc, hq * HEAD), lambda b, h, c: (b, c, q_off + h)),
                  pl.BlockSpec((1, tc, hq * HEAD), lambda b, h, c: (b, c, k_off + h)),
                  pl.BlockSpec((1, tc, hb * HEAD), lambda b, h, c: (b, c, v_off + h)),
                  pl.BlockSpec((1, tc, hb * HEAD), lambda b, h, c: (b, c, h)),
                  pl.BlockSpec((1, tc, hb), lambda b, h, c: (h, b * n_tc + c, 0)),
                  pl.BlockSpec((1, tc, hb), lambda b, h, c: (h, b * n_tc + c, 0)),
                  pl.BlockSpec((1, hb, HEAD, HEAD), lambda b, h, c: (b, h, 0, 0)),
                  pl.BlockSpec((1, HEAD), lambda b, h, c: (0, 0))],
        out_specs=[pl.BlockSpec((1, tc, hb * HEAD), lambda b, h, c: (b, c, h)),
                   pl.BlockSpec((1, hb, HEAD, HEAD), lambda b, h, c: (b, h, 0, 0))],
        out_shape=[jax.ShapeDtypeStruct((n_b, t_len, VAL_DIM), BF16),
                   jax.ShapeDtypeStruct((n_b, V_HEADS, HEAD, HEAD), F32)],
        scratch_shapes=[pltpu.VMEM((hb, HEAD, HEAD), F32)],
        compiler_params=_cparams(("arbitrary", "arbitrary", "arbitrary")),
        name="gated_delta_rule",
    )(qkv, qkv, qkv, z, g, beta, s0, norm_w)


def _tile(m, pref):
    t = min(m, pref)
    assert m % t == 0, (m, t)
    return t


def _trunk(x, s5_re0, s5_im0, conv0, s0, p):
    n_b, t_len, _ = x.shape
    m = n_b * t_len
    tm = _tile(m, 512)
    tm_wide = _tile(m, 1024)

    tt = _tile(t_len, 256)
    multi_seq = SUBLANES * t_len <= 256 and n_b % SUBLANES == 0
    n_blk, n_state = (n_b // SUBLANES, SUBLANES) if multi_seq else (n_b, 1)
    if multi_seq:
        tt = SUBLANES * t_len
    g, sr, si = _s5_call(x.reshape(n_blk, n_state * t_len, D_MODEL), p['nm0'], p['s5_d'], p['wb'], p['wc'],
                         p['a_re'], p['a_im'], s5_re0.reshape(n_blk, n_state, S5_CH),
                         s5_im0.reshape(n_blk, n_state, S5_CH), tt, multi_seq)
    x2 = x.reshape(m, D_MODEL)
    x2 = _glu_call(g.reshape(m, D_MODEL), p['w_glu'], x2, tm_wide, 1024)
    x2 = _ffn_call(x2, p['nf0'], p['w_up'], p['w_down'], 0, p['nfin'], tm_wide, 1024, False)

    chunk = min(t_len, 64)
    cpb = _tile(t_len // chunk, 4)
    hb = 16
    qkv, z, gg, beta, tails = _inproj_call(
        x2, p['nm1'], p['w_qkvz'], p['w_ba'], conv0, p['conv_w'], p['a_log'], p['dt_bias'],
        t_len, tm, hb)
    tpb = t_len // min(t_len, tm)
    new_conv = tails[tpb - 1::tpb]
    og, s_fin = _delta_call(qkv.reshape(n_b, t_len, -1), z.reshape(n_b, t_len, VAL_DIM),
                            gg, beta, s0, p['gdn_norm_w'], chunk, cpb, hb)
    x2 = _matmul_res_call(og.reshape(m, VAL_DIM), p['w_out'], x2, tm_wide, 1024)
    y = _ffn_call(x2, p['nf1'], p['w_up'], p['w_down'], 1, p['nfin'], tm_wide, 1024, True)

    return (y.reshape(n_b, t_len, D_MODEL),
            sr.reshape(1, n_b, S5_GROUPS, S5_STATE), si.reshape(1, n_b, S5_GROUPS, S5_STATE),
            new_conv[None], s_fin[None])


def kernel(x_prompt, x_sample, state_s5_re, state_s5_im, state_gdn_conv, state_gdn_s,
           norm_mix, norm_ffn, norm_final,
           s5_lam_re, s5_lam_im, s5_log_dt, s5_b_re, s5_b_im, s5_c_re, s5_c_im,
           s5_d, s5_w_glu,
           gdn_w_in, gdn_conv_w, gdn_a_log, gdn_dt_bias, gdn_norm_w, gdn_w_out,
           ffn_w_up, ffn_w_down):
    wb, wc, a_re, a_im = _s5_params(s5_lam_re[0], s5_lam_im[0], s5_log_dt[0], s5_b_re[0], s5_b_im[0],
                              s5_c_re[0], s5_c_im[0])
    w_in = gdn_w_in[0]
    ba = w_in[:, CONV_DIM + VAL_DIM:]
    w_ba = jnp.concatenate([ba, jnp.zeros((D_MODEL, LANES - ba.shape[1]), w_in.dtype)], axis=1)
    row = lambda a: a.reshape(1, -1).astype(F32)
    p = dict(
        nm0=row(norm_mix[0]), nm1=row(norm_mix[1]), nf0=row(norm_ffn[0]), nf1=row(norm_ffn[1]),
        nfin=row(norm_final), s5_d=row(s5_d[0]), wb=wb, wc=wc, a_re=a_re, a_im=a_im,
        w_glu=s5_w_glu[0].astype(BF16),
        w_up=ffn_w_up.astype(BF16), w_down=ffn_w_down.astype(BF16),
        w_qkvz=w_in.astype(BF16), w_ba=w_ba.astype(BF16),
        conv_w=gdn_conv_w[0].astype(F32), a_log=row(gdn_a_log[0]), dt_bias=row(gdn_dt_bias[0]),
        gdn_norm_w=row(gdn_norm_w[0]), w_out=gdn_w_out[0].astype(BF16),
    )
    n_bp = x_prompt.shape[0]
    z_state = jnp.zeros((n_bp, S5_GROUPS, S5_STATE), F32)
    z_conv = jnp.zeros((n_bp, CONV_W - 1, CONV_DIM), F32)
    z_s = jnp.zeros((n_bp, V_HEADS, HEAD, HEAD), F32)

    yp, p_re, p_im, p_conv, p_s = _trunk(x_prompt, z_state, z_state, z_conv, z_s, p)
    ys, s_re, s_im, s_conv, s_s = _trunk(x_sample, state_s5_re[0], state_s5_im[0],
                                         state_gdn_conv[0], state_gdn_s[0], p)
    return (yp, ys, p_re, p_im, p_conv, p_s, s_re, s_im, s_conv, s_s)
```

```python
import functools
import math

import jax
import jax.numpy as jnp
from jax import lax
from jax.experimental import pallas as pl
from jax.experimental.pallas import tpu as pltpu

F32 = jnp.float32
BF16 = jnp.bfloat16

RMS_EPS = 1e-6
L2_EPS = 1e-6
D_MODEL = 2048
S5_GROUP = 16
S5_STATE = 64
S5_GROUPS = D_MODEL // S5_GROUP
S5_CH = S5_GROUPS * S5_STATE
HEAD = 128
QK_HEADS = D_MODEL // HEAD
V_HEADS = 2 * QK_HEADS
KEY_DIM = QK_HEADS * HEAD
VAL_DIM = V_HEADS * HEAD
CONV_DIM = 2 * KEY_DIM + VAL_DIM
CONV_W = 4

SUBLANES = 8
LANES = 128
MXU_K = 256
S5_KB = D_MODEL // MXU_K
S5_CB = S5_CH // S5_KB
S5_SCAN_W = S5_CB
VMEM_LIMIT = 60 * 1024 * 1024


def _cparams(sem):
    return pltpu.CompilerParams(dimension_semantics=sem, vmem_limit_bytes=VMEM_LIMIT)


def _rms(x, w):
    return x * lax.rsqrt(jnp.mean(x * x, axis=-1, keepdims=True) + RMS_EPS) * w


def _s5_body(x_ref, nw_ref, d_ref, wb_ref, wc_ref, coef_ref, perm_ref, permt_ref, s0r_ref, s0i_ref,
             g_ref, sr_ref, si_ref, car_ref, h_ref, bu_ref, st_ref, *, tt, multi_seq):
    t = pl.program_id(1)
    seg = tt // SUBLANES

    @pl.when(t == 0)
    def _():
        car_ref[0] = jnp.broadcast_to(s0r_ref[0], (SUBLANES, S5_CH))
        car_ref[1] = jnp.broadcast_to(s0i_ref[0], (SUBLANES, S5_CH))

    h = _rms(x_ref[0], nw_ref[...])
    h_ref[...] = h
    hp = jnp.dot(perm_ref[...], h.astype(BF16), preferred_element_type=F32).astype(BF16)
    row0 = lax.broadcasted_iota(jnp.int32, (SUBLANES, S5_SCAN_W), 0) == 0

    def step(ar, ai, sr, si, xr, xi):
        return ar * sr - ai * si + xr, ar * si + ai * sr + xi

    def project_in(kb):
        ks = slice(kb * MXU_K, (kb + 1) * MXU_K)
        bu_ref[kb % 2] = jnp.dot(hp[:, ks], wb_ref[kb], preferred_element_type=F32)

    def project_out(kb):
        ks = slice(kb * MXU_K, (kb + 1) * MXU_K)
        yp = jnp.dot(st_ref[kb % 2].astype(BF16), wc_ref[kb], preferred_element_type=F32)
        y_hi = yp.astype(BF16)
        y_lo = (yp - y_hi.astype(F32)).astype(BF16)
        y2 = jnp.dot(permt_ref[...], jnp.concatenate([y_hi, y_lo], axis=1),
                     preferred_element_type=F32)
        y = y2[:, :MXU_K] + y2[:, MXU_K:] + d_ref[:, ks] * h_ref[:, ks]
        g_ref[0, :, ks] = jax.nn.gelu(y).astype(BF16)

    def scan(kb, part):
        lo = part * S5_SCAN_W
        re_cols = slice(lo, lo + S5_SCAN_W)
        im_cols = slice(S5_CB + lo, S5_CB + lo + S5_SCAN_W)
        cs = slice(kb * S5_CB + lo, kb * S5_CB + lo + S5_SCAN_W)
        buf = kb % 2

        def bu_rows(j):
            rows = slice(j * SUBLANES, (j + 1) * SUBLANES)
            return bu_ref[buf, rows, re_cols], bu_ref[buf, rows, im_cols]

        ar = coef_ref[4, 0, :, cs]
        ai = coef_ref[4, 1, :, cs]
        if multi_seq:
            sr = car_ref[0, :, cs]
            si = car_ref[1, :, cs]
            for j in range(seg):
                sr, si = step(ar, ai, sr, si, *bu_rows(j))
                rows = slice(j * SUBLANES, (j + 1) * SUBLANES)
                st_ref[buf, rows, re_cols] = sr
                st_ref[buf, rows, im_cols] = si
            car_ref[0, :, cs] = sr
            car_ref[1, :, cs] = si
            return
        er, ei = bu_rows(0)
        for j in range(1, seg):
            er, ei = step(ar, ai, er, ei, *bu_rows(j))
        for lvl, sh in enumerate((1, 2, 4)):
            cr_ = coef_ref[lvl, 0, :, cs]
            ci_ = coef_ref[lvl, 1, :, cs]
            pr = pltpu.roll(er, sh, 0)
            pi = pltpu.roll(ei, sh, 0)
            er, ei = er + cr_ * pr - ci_ * pi, ei + cr_ * pi + ci_ * pr
        c0r = car_ref[0, :, cs]
        c0i = car_ref[1, :, cs]
        cr_ = coef_ref[3, 0, :, cs]
        ci_ = coef_ref[3, 1, :, cs]
        fr = er + cr_ * c0r - ci_ * c0i
        fi = ei + cr_ * c0i + ci_ * c0r
        car_ref[0, :, cs] = jnp.broadcast_to(fr[SUBLANES - 1:SUBLANES], (SUBLANES, S5_SCAN_W))
        car_ref[1, :, cs] = jnp.broadcast_to(fi[SUBLANES - 1:SUBLANES], (SUBLANES, S5_SCAN_W))
        sr = jnp.where(row0, c0r, pltpu.roll(fr, 1, 0))
        si = jnp.where(row0, c0i, pltpu.roll(fi, 1, 0))
        for j in range(seg):
            sr, si = step(ar, ai, sr, si, *bu_rows(j))
            rows = slice(j * SUBLANES, (j + 1) * SUBLANES)
            st_ref[buf, rows, re_cols] = sr
            st_ref[buf, rows, im_cols] = si

    project_in(0)
    for kb in range(S5_KB):
        if kb + 1 < S5_KB:
            project_in(kb + 1)
        if kb >= 1:
            project_out(kb - 1)
        for part in range(S5_CB // S5_SCAN_W):
            scan(kb, part)
    project_out(S5_KB - 1)

    @pl.when(t == pl.num_programs(1) - 1)
    def _():
        n_state = SUBLANES if multi_seq else 1
        sr_ref[0] = car_ref[0, 0:n_state, :]
        si_ref[0] = car_ref[1, 0:n_state, :]


def _s5_call(x, nw, d, wb, wc, a_re, a_im, s0r, s0i, tt, multi_seq):
    n_b, t_len, _ = x.shape
    seg = tt // SUBLANES
    n_state = s0r.shape[1]
    assert n_state == (SUBLANES if multi_seq else 1) and (not multi_seq or t_len == tt)
    coef = _s5_coef(a_re, a_im, seg)
    p_idx = jnp.arange(tt)
    src = (p_idx % SUBLANES) * seg + p_idx // SUBLANES
    perm = (src[:, None] == p_idx[None, :]).astype(BF16)
    const = lambda *shape: pl.BlockSpec(shape, lambda b, t: (0,) * len(shape))
    state = pl.BlockSpec((1, n_state, S5_CH), lambda b, t: (b, 0, 0))
    return pl.pallas_call(
        functools.partial(_s5_body, tt=tt, multi_seq=multi_seq),
        grid=(n_b, t_len // tt),
        in_specs=[pl.BlockSpec((1, tt, D_MODEL), lambda b, t: (b, t, 0)),
                  const(1, D_MODEL), const(1, D_MODEL),
                  const(S5_KB, MXU_K, 2 * S5_CB), const(S5_KB, 2 * S5_CB, MXU_K),
                  const(5, 2, SUBLANES, S5_CH), const(tt, tt), const(tt, tt), state, state],
        out_specs=[pl.BlockSpec((1, tt, D_MODEL), lambda b, t: (b, t, 0)), state, state],
        out_shape=[jax.ShapeDtypeStruct((n_b, t_len, D_MODEL), BF16),
                   jax.ShapeDtypeStruct((n_b, n_state, S5_CH), F32),
                   jax.ShapeDtypeStruct((n_b, n_state, S5_CH), F32)],
        scratch_shapes=[pltpu.VMEM((2, SUBLANES, S5_CH), F32),
                        pltpu.VMEM((tt, D_MODEL), F32),
                        pltpu.VMEM((2, tt, 2 * S5_CB), F32),
                        pltpu.VMEM((2, tt, 2 * S5_CB), F32)],
        compiler_params=_cparams(("arbitrary", "arbitrary")),
        name="s5_mixer",
    )(x, nw, d, wb, wc, coef, perm, perm.T, s0r, s0i)


def _s5_coef(a_re, a_im, seg):
    assert seg & (seg - 1) == 0, seg
    ar, ai = a_re, a_im
    for _ in range(int(math.log2(seg))):
        ar, ai = ar * ar - ai * ai, 2.0 * ar * ai
    pw_r, pw_i = [ar], [ai]
    for _ in range(SUBLANES - 1):
        pr, pi = pw_r[-1], pw_i[-1]
        pw_r.append(pr * ar - pi * ai)
        pw_i.append(pr * ai + pi * ar)
    row = jnp.arange(SUBLANES)[:, None]
    levels = []
    for sh in (1, 2, 4):
        m = (row >= sh).astype(F32)
        levels.append(jnp.stack([m * pw_r[sh - 1], m * pw_i[sh - 1]]))
    levels.append(jnp.stack([jnp.concatenate(pw_r, axis=0), jnp.concatenate(pw_i, axis=0)]))
    ones = jnp.ones((SUBLANES, 1), F32)
    levels.append(jnp.stack([ones * a_re, ones * a_im]))
    return jnp.stack(levels)


def _s5_params(lam_re, lam_im, log_dt, b_re, b_im, c_re, c_im):
    lr = jnp.minimum(lam_re.astype(F32), -1e-4)
    li = lam_im.astype(F32)
    dt = jnp.exp(log_dt.astype(F32))[:, None]
    mag = jnp.exp(lr * dt)
    ang = li * dt
    a_re = mag * jnp.cos(ang)
    a_im = mag * jnp.sin(ang)
    nr = a_re - 1.0
    den = lr * lr + li * li
    k_re = (nr * lr + a_im * li) / den
    k_im = (a_im * lr - nr * li) / den
    br = b_re.astype(F32)
    bi = b_im.astype(F32)
    bb_re = k_re[..., None] * br - k_im[..., None] * bi
    bb_im = k_re[..., None] * bi + k_im[..., None] * br

    gl = S5_GROUPS // S5_KB
    eye = jnp.eye(gl, dtype=F32)

    def blockdiag_in(bb):
        bbk = bb.reshape(S5_KB, gl, S5_STATE, S5_GROUP)
        w = jnp.einsum('kgph,gf->kghfp', bbk, eye)
        return w.reshape(S5_KB, gl * S5_GROUP, gl * S5_STATE)

    def blockdiag_out(c):
        ck = c.reshape(S5_KB, gl, S5_GROUP, S5_STATE)
        w = jnp.einsum('kghp,gf->kgpfh', ck, eye)
        return w.reshape(S5_KB, gl * S5_STATE, gl * S5_GROUP)

    wb = jnp.concatenate([blockdiag_in(bb_re), blockdiag_in(bb_im)], axis=2).astype(BF16)
    wc = jnp.concatenate([blockdiag_out(c_re.astype(F32)),
                          -blockdiag_out(c_im.astype(F32))], axis=1).astype(BF16)

    return wb, wc, a_re.reshape(1, S5_CH), a_im.reshape(1, S5_CH)


def _glu_body(g_ref, wa_ref, wg_ref, x_ref, o_ref):
    g = g_ref[...]
    za = jnp.dot(g, wa_ref[...], preferred_element_type=F32)
    zg = jnp.dot(g, wg_ref[...], preferred_element_type=F32)
    o_ref[...] = x_ref[...] + za * jax.nn.sigmoid(zg)


def _glu_call(g, w_glu, x, tm, tn):
    m = g.shape[0]
    nj = D_MODEL // tn
    return pl.pallas_call(
        _glu_body,
        grid=(m // tm, nj),
        in_specs=[pl.BlockSpec((tm, D_MODEL), lambda i, j: (i, 0)),
                  pl.BlockSpec((D_MODEL, tn), lambda i, j: (0, j)),
                  pl.BlockSpec((D_MODEL, tn), lambda i, j: (0, j + nj)),
                  pl.BlockSpec((tm, tn), lambda i, j: (i, j))],
        out_specs=pl.BlockSpec((tm, tn), lambda i, j: (i, j)),
        out_shape=jax.ShapeDtypeStruct((m, D_MODEL), F32),
        compiler_params=_cparams(("parallel", "arbitrary")),
        name="glu_residual",
    )(g, w_glu, w_glu, x)


def _ffn_body(x_ref, nw_ref, wu_ref, wd_ref, fw_ref, o_ref, h_ref, *, final_norm):
    f = pl.program_id(1)

    @pl.when(f == 0)
    def _():
        x = x_ref[...]
        h_ref[...] = _rms(x, nw_ref[...]).astype(BF16)
        o_ref[...] = x

    a = jnp.maximum(jnp.dot(h_ref[...], wu_ref[...], preferred_element_type=F32), 0.0)
    o_ref[...] += jnp.dot((a * a).astype(BF16), wd_ref[...], preferred_element_type=F32)

    if final_norm:
        @pl.when(f == pl.num_programs(1) - 1)
        def _():
            o_ref[...] = _rms(o_ref[...], fw_ref[...])


def _ffn_call(x, nw, w_up, w_down, layer, fw, tm, tf, final_norm):
    m = x.shape[0]
    d_ff = w_up.shape[2]
    return pl.pallas_call(
        functools.partial(_ffn_body, final_norm=final_norm),
        grid=(m // tm, d_ff // tf),
        in_specs=[pl.BlockSpec((tm, D_MODEL), lambda i, f: (i, 0)),
                  pl.BlockSpec((1, D_MODEL), lambda i, f: (0, 0)),
                  pl.BlockSpec((None, D_MODEL, tf), lambda i, f: (layer, 0, f)),
                  pl.BlockSpec((None, tf, D_MODEL), lambda i, f: (layer, f, 0)),
                  pl.BlockSpec((1, D_MODEL), lambda i, f: (0, 0))],
        out_specs=pl.BlockSpec((tm, D_MODEL), lambda i, f: (i, 0)),
        out_shape=jax.ShapeDtypeStruct((m, D_MODEL), F32),
        scratch_shapes=[pltpu.VMEM((tm, D_MODEL), BF16)],
        compiler_params=_cparams(("parallel", "arbitrary")),
        name="ffn_residual",
    )(x, nw, w_up, w_down, fw)


def _matmul_res_body(a_ref, w_ref, x_ref, o_ref):
    o_ref[...] = x_ref[...] + jnp.dot(a_ref[...], w_ref[...], preferred_element_type=F32)


def _matmul_res_call(a, w, x, tm, tn):
    m, k = a.shape
    n = w.shape[1]
    return pl.pallas_call(
        _matmul_res_body,
        grid=(m // tm, n // tn),
        in_specs=[pl.BlockSpec((tm, k), lambda i, j: (i, 0)),
                  pl.BlockSpec((k, tn), lambda i, j: (0, j)),
                  pl.BlockSpec((tm, tn), lambda i, j: (i, j))],
        out_specs=pl.BlockSpec((tm, tn), lambda i, j: (i, j)),
        out_shape=jax.ShapeDtypeStruct((m, n), F32),
        compiler_params=_cparams(("parallel", "arbitrary")),
        name="matmul_residual",
    )(a, w, x)


INPROJ_TN = 1024
N_CONV_TILES = CONV_DIM // INPROJ_TN
N_Z_TILES = VAL_DIM // INPROJ_TN
N_QK_TILES = 2 * KEY_DIM // INPROJ_TN
N_Q_TILES = KEY_DIM // INPROJ_TN
CONV_ROWS = 64


def _inproj_body(x_ref, nw_ref, w_ref, wba_ref, cprev_ref, cw_ref, alog_ref, dtb_ref,
                 qkv_ref, z_ref, g_ref, beta_ref, nc_ref, hs_ref, hist_ref, ra_ref, rb_ref,
                 *, tm, seq_rows, tiles_per_seq, gate_group):
    n_seq = tm // seq_rows
    i = pl.program_id(0)
    j = pl.program_id(1)
    tn = INPROJ_TN
    n_sub = tn // MXU_K
    tail = CONV_W - 1
    pad = SUBLANES

    @pl.when(j == 0)
    def _():
        hs_ref[...] = _rms(x_ref[...], nw_ref[...]).astype(BF16)
        ba = jnp.dot(hs_ref[...], wba_ref[...], preferred_element_type=F32)
        zz = ba[:, V_HEADS:2 * V_HEADS] + dtb_ref[...]
        softplus = jnp.maximum(zz, 0.0) + jnp.log1p(jnp.exp(-jnp.abs(zz)))
        g_all = -jnp.exp(alog_ref[...]) * softplus
        beta_all = jax.nn.sigmoid(ba[:, 0:V_HEADS])
        for gi in range(V_HEADS // gate_group):
            g_ref[gi] = g_all[:, gi * gate_group:(gi + 1) * gate_group]
            beta_ref[gi] = beta_all[:, gi * gate_group:(gi + 1) * gate_group]

    @pl.when((i % tiles_per_seq == 0) & (j == 0))
    def _():
        hist_ref[...] = jnp.concatenate(
            [jnp.zeros((pad - tail, CONV_DIM), F32), cprev_ref[0]], axis=0)

    def project(r_ref, half, s):
        c0 = half * tn + s * MXU_K
        r_ref[s, pad:pad + tm, :] = jnp.dot(hs_ref[...], w_ref[:, c0:c0 + MXU_K],
                                            preferred_element_type=F32)

    def conv_rows(win, c0):
        acc = win[pad:] * cw_ref[tail:tail + 1, c0:c0 + MXU_K]
        for d in range(1, CONV_W):
            shifted = pltpu.roll(win, d, 0)[pad:]
            acc = acc + shifted * cw_ref[tail - d:tail - d + 1, c0:c0 + MXU_K]
        return acc

    def epilogue(r_ref, tile, out_half, s):
        c0 = tile * tn + s * MXU_K
        o0 = out_half * tn + s * MXU_K
        l2 = tile < N_QK_TILES
        scale = HEAD ** -0.5 if tile < N_Q_TILES else 1.0

        def finish(acc, rows):
            c = acc * jax.nn.sigmoid(acc)
            if l2:
                for hd in range(MXU_K // HEAD):
                    ch = c[:, hd * HEAD:(hd + 1) * HEAD]
                    ch = ch * (lax.rsqrt(jnp.sum(ch * ch, axis=-1, keepdims=True) + L2_EPS) * scale)
                    qkv_ref[rows, o0 + hd * HEAD:o0 + (hd + 1) * HEAD] = ch.astype(BF16)
            else:
                qkv_ref[rows, o0:o0 + MXU_K] = c.astype(BF16)

        r_ref[s, 0:pad, :] = hist_ref[:, c0:c0 + MXU_K]
        hist_ref[:, c0:c0 + MXU_K] = r_ref[s, tm:tm + pad, :]
        for sq in range(n_seq):
            end = pad + (sq + 1) * seq_rows
            nc_ref[sq, :, c0:c0 + MXU_K] = r_ref[s, end - tail:end, :]
        for r0 in range(0, tm, CONV_ROWS):
            rc = min(CONV_ROWS, tm - r0)
            finish(conv_rows(r_ref[s, r0:r0 + rc + pad, :], c0), slice(r0, r0 + rc))
        for sq in range(1, n_seq):
            r0 = sq * seq_rows
            win = jnp.concatenate([jnp.zeros((pad - tail, MXU_K), F32), cprev_ref[sq, :, c0:c0 + MXU_K],
                                   r_ref[s, pad + r0:pad + r0 + pad, :]], axis=0)
            finish(conv_rows(win, c0), slice(r0, r0 + pad))

    def project_z(half, s):
        c0 = half * tn + s * MXU_K
        zz = jnp.dot(hs_ref[...], w_ref[:, c0:c0 + MXU_K], preferred_element_type=F32)
        z_ref[:, c0:c0 + MXU_K] = (zz * jax.nn.sigmoid(zz)).astype(BF16)

    def step_body(step):
        cur, prev = step % 2, (step - 1) % 2
        for half, r_ref in ((0, ra_ref), (1, rb_ref)):
            for s in range(n_sub):
                if step < n_conv_steps:
                    project(r_ref.at[cur], half, s)
                else:
                    project_z(half, s)
                if 0 < step <= n_conv_steps:
                    epilogue(r_ref.at[prev], 2 * (step - 1) + half, half, s)

    n_conv_steps = N_CONV_TILES // 2
    for step in range(n_conv_steps + 1):
        pl.when(j == step)(functools.partial(step_body, step))
    pl.when(j > n_conv_steps)(functools.partial(step_body, n_conv_steps + 1))


def _inproj_call(x, nw, w_qkvz, w_ba, conv_prev, conv_w, a_log, dt_bias, t_len, tm, gate_group):
    m = x.shape[0]
    n_grp = V_HEADS // gate_group
    tn2 = 2 * INPROJ_TN
    n_cs = N_CONV_TILES // 2
    n_zs = N_Z_TILES // 2
    seq_rows = min(t_len, tm)
    n_seq = tm // seq_rows
    tpb = t_len // seq_rows
    return pl.pallas_call(
        functools.partial(_inproj_body, tm=tm, seq_rows=seq_rows, tiles_per_seq=tpb,
                          gate_group=gate_group),
        grid=(m // tm, n_cs + n_zs),
        in_specs=[pl.BlockSpec((tm, D_MODEL), lambda i, j: (i, 0)),
                  pl.BlockSpec((1, D_MODEL), lambda i, j: (0, 0)),
                  pl.BlockSpec((D_MODEL, tn2), lambda i, j: (0, j)),
                  pl.BlockSpec((D_MODEL, LANES), lambda i, j: (0, 0)),
                  pl.BlockSpec((n_seq, CONV_W - 1, CONV_DIM), lambda i, j: (i // tpb, 0, 0)),
                  pl.BlockSpec((CONV_W, CONV_DIM), lambda i, j: (0, 0)),
                  pl.BlockSpec((1, V_HEADS), lambda i, j: (0, 0)),
                  pl.BlockSpec((1, V_HEADS), lambda i, j: (0, 0))],
        out_specs=[pl.BlockSpec((tm, tn2), lambda i, j: (i, jnp.clip(j - 1, 0, n_cs - 1))),
                   pl.BlockSpec((tm, tn2), lambda i, j: (i, jnp.maximum(j - n_cs, 0))),
                   pl.BlockSpec((n_grp, tm, gate_group), lambda i, j: (0, i, 0)),
                   pl.BlockSpec((n_grp, tm, gate_group), lambda i, j: (0, i, 0)),
                   pl.BlockSpec((n_seq, CONV_W - 1, CONV_DIM), lambda i, j: (i, 0, 0))],
        out_shape=[jax.ShapeDtypeStruct((m, CONV_DIM), BF16),
                   jax.ShapeDtypeStruct((m, VAL_DIM), BF16),
                   jax.ShapeDtypeStruct((n_grp, m, gate_group), F32),
                   jax.ShapeDtypeStruct((n_grp, m, gate_group), F32),
                   jax.ShapeDtypeStruct((m // seq_rows, CONV_W - 1, CONV_DIM), F32)],
        scratch_shapes=[pltpu.VMEM((tm, D_MODEL), BF16),
                        pltpu.VMEM((SUBLANES, CONV_DIM), F32),
                        pltpu.VMEM((2, INPROJ_TN // MXU_K, tm + SUBLANES, MXU_K), F32),
                        pltpu.VMEM((2, INPROJ_TN // MXU_K, tm + SUBLANES, MXU_K), F32)],
        compiler_params=_cparams(("arbitrary", "arbitrary")),
        name="gdn_inproj_conv",
    )(x, nw, w_qkvz, w_ba, conv_prev, conv_w, a_log, dt_bias)


INV_BASE = 16


def _mm16(a, b):
    return jnp.dot(a.astype(BF16), b.astype(BF16), preferred_element_type=F32)


def _tri_inverse_all(nmats, ri, ci, chunk):
    base = min(INV_BASE, chunk)
    sh = int(math.log2(base))
    same = (ri >> sh) == (ci >> sh)
    eye = (ri == ci).astype(F32)
    ms = [jnp.where(same, -n, 0.0) for n in nmats]
    ps = [eye + m for m in ms]
    for _ in range(sh - 1):
        ms = [_mm16(m, m) for m in ms]
        ps = [p + _mm16(p, m) for p, m in zip(ps, ms)]
    size = base
    while size < chunk:
        sh += 1
        size *= 2
        same_next = (ri >> sh) == (ci >> sh)
        sel = same_next & jnp.logical_not(same)
        ts = [_mm16(p, jnp.where(sel, n, 0.0)) for p, n in zip(ps, nmats)]
        ps = [p - _mm16(t, p) for p, t in zip(ps, ts)]
        same = same_next
    return ps


def _delta_body(q_ref, k_ref, v_ref, z_ref, gc_ref, bc_ref, s0_ref, nw_ref,
                o_ref, sf_ref, s_ref, *, chunk, cpb, hb):
    c_idx = pl.program_id(2)

    @pl.when(c_idx == 0)
    def _():
        s_ref[...] = s0_ref[0]

    ri = lax.broadcasted_iota(jnp.int32, (chunk, chunk), 0)
    ci = lax.broadcasted_iota(jnp.int32, (chunk, chunk), 1)
    causal = ri >= ci
    strict = ri > ci
    tril = causal.astype(F32)
    triu = (ri <= ci).astype(F32)
    eye = (ri == ci).astype(F32)
    hi = lax.Precision.HIGHEST
    nt = (((1,), (1,)), ((), ()))
    tn = (((0,), (0,)), ((), ()))
    hq = hb // 2

    kqk, pre = {}, {}
    for c in range(cpb):
        rows = slice(c * chunk, (c + 1) * chunk)
        for j in range(hq):
            k16 = k_ref[0, rows, j * HEAD:(j + 1) * HEAD]
            q16 = q_ref[0, rows, j * HEAD:(j + 1) * HEAD]
            kqk[c, j] = lax.dot_general(jnp.concatenate([k16, q16], axis=0), k16, nt,
                                        preferred_element_type=F32)
    nmats, keys = [], []
    for c in range(cpb):
        rows = slice(c * chunk, (c + 1) * chunk)
        g_raw = gc_ref[0, rows, :]
        gcol = jnp.dot(tril, g_raw, precision=hi, preferred_element_type=F32)
        grow = lax.dot_general(g_raw, triu, tn, precision=hi, preferred_element_type=F32)
        bcol = bc_ref[0, rows, :]
        brow = lax.dot_general(bcol, eye, tn, precision=hi, preferred_element_type=F32)
        for hh in range(hb):
            j = hh // 2
            k = k_ref[0, rows, j * HEAD:(j + 1) * HEAD].astype(F32)
            q = q_ref[0, rows, j * HEAD:(j + 1) * HEAD].astype(F32)
            v16 = v_ref[0, rows, hh * HEAD:(hh + 1) * HEAD]
            gc = gcol[:, hh:hh + 1]
            gr = grow[hh:hh + 1, :]
            beta = bcol[:, hh:hh + 1]
            decay = jnp.exp(jnp.where(causal, gc - gr, -jnp.inf))
            egc = jnp.exp(gc)
            g_last = gc[chunk - 1:chunk, :]
            kk = kqk[c, j][:chunk]
            qk = kqk[c, j][chunk:]
            nmats.append(jnp.where(strict, (kk * beta) * decay, 0.0))
            keys.append((c, hh))
            pre[c, hh] = dict(
                rhs=jnp.concatenate([v16, (k * egc).astype(BF16)], axis=1),
                brow=brow[hh:hh + 1, :],
                qg=(q * egc).astype(BF16),
                intra=(qk * decay).astype(BF16),
                kd=(k * jnp.exp(g_last - gc)).astype(BF16),
                sdec=jnp.exp(g_last))
    tinv = _tri_inverse_all(nmats, ri, ci, chunk)
    for key, t in zip(keys, tinv):
        pre[key]['sol'] = jnp.dot((t * pre[key]['brow']).astype(BF16), pre[key]['rhs'],
                                  preferred_element_type=F32)

    for c in range(cpb):
        rows = slice(c * chunk, (c + 1) * chunk)
        ws = []
        for hh in range(hb):
            d = pre[c, hh]
            wq = jnp.concatenate([d['sol'][:, HEAD:].astype(BF16), d['qg']], axis=0)
            ws.append(jnp.dot(wq, s_ref[hh].astype(BF16), preferred_element_type=F32))
        for hh in range(hb):
            d = pre[c, hh]
            v16 = (d['sol'][:, :HEAD] - ws[hh][:chunk]).astype(BF16)
            o = ws[hh][chunk:] + jnp.dot(d['intra'], v16, preferred_element_type=F32)
            s_ref[hh] = s_ref[hh] * d['sdec'] + lax.dot_general(d['kd'], v16, tn,
                                                                preferred_element_type=F32)
            o = o * lax.rsqrt(jnp.mean(o * o, axis=-1, keepdims=True) + RMS_EPS) * nw_ref[...]
            gate = z_ref[0, rows, hh * HEAD:(hh + 1) * HEAD]
            o_ref[0, rows, hh * HEAD:(hh + 1) * HEAD] = (o * gate).astype(BF16)

    @pl.when(c_idx == pl.num_programs(2) - 1)
    def _():
        sf_ref[0] = s_ref[...]


def _delta_call(qkv, z, g, beta, s0, norm_w, chunk, cpb, hb):
    n_b, t_len, _ = qkv.shape
    nhg = V_HEADS // hb
    tc = chunk * cpb
    n_tc = t_len // tc
    assert g.shape == (nhg, n_b * t_len, hb), g.shape
    hq = hb // 2
    assert KEY_DIM % (hq * HEAD) == 0 and (2 * KEY_DIM) % (hb * HEAD) == 0, hb
    q_off = 0
    k_off = KEY_DIM // (hq * HEAD)
    v_off = 2 * KEY_DIM // (hb * HEAD)
    return pl.pallas_call(
        functools.partial(_delta_body, chunk=chunk, cpb=cpb, hb=hb),
        grid=(n_b, nhg, t_len // tc),
        in_specs=[pl.BlockSpec((1, tc, hq * HEAD), lambda b, h, c: (b, c, q_off + h)),
                  pl.BlockSpec((1, tc, hq * HEAD), lambda b, h, c: (b, c, k_off + h)),
                  pl.BlockSpec((1, tc, hb * HEAD), lambda b, h, c: (b, c, v_off + h)),
                  pl.BlockSpec((1, tc, hb * HEAD), lambda b, h, c: (b, c, h)),
                  pl.BlockSpec((1, tc, hb), lambda b, h, c: (h, b * n_tc + c, 0)),
                  pl.BlockSpec((1, tc, hb), lambda b, h, c: (h, b * n_tc + c, 0)),
                  pl.BlockSpec((1, hb, HEAD, HEAD), lambda b, h, c: (b, h, 0, 0)),
                  pl.BlockSpec((1, HEAD), lambda b, h, c: (0, 0))],
        out_specs=[pl.BlockSpec((1, tc, hb * HEAD), lambda b, h, c: (b, c, h)),
                   pl.BlockSpec((1, hb, HEAD, HEAD), lambda b, h, c: (b, h, 0, 0))],
        out_shape=[jax.ShapeDtypeStruct((n_b, t_len, VAL_DIM), BF16),
                   jax.ShapeDtypeStruct((n_b, V_HEADS, HEAD, HEAD), F32)],
        scratch_shapes=[pltpu.VMEM((hb, HEAD, HEAD), F32)],
        compiler_params=_cparams(("arbitrary", "arbitrary", "arbitrary")),
        name="gated_delta_rule",
    )(qkv, qkv, qkv, z, g, beta, s0, norm_w)


def _tile(m, pref):
    t = min(m, pref)
    assert m % t == 0, (m, t)
    return t


def _trunk(x, s5_re0, s5_im0, conv0, s0, p):
    n_b, t_len, _ = x.shape
    m = n_b * t_len
    tm = _tile(m, 512)
    tm_wide = _tile(m, 1024)

    tt = _tile(t_len, 256)
    multi_seq = SUBLANES * t_len <= 256 and n_b % SUBLANES == 0
    n_blk, n_state = (n_b // SUBLANES, SUBLANES) if multi_seq else (n_b, 1)
    if multi_seq:
        tt = SUBLANES * t_len
    g, sr, si = _s5_call(x.reshape(n_blk, n_state * t_len, D_MODEL), p['nm0'], p['s5_d'], p['wb'], p['wc'],
                         p['a_re'], p['a_im'], s5_re0.reshape(n_blk, n_state, S5_CH),
                         s5_im0.reshape(n_blk, n_state, S5_CH), tt, multi_seq)
    x2 = x.reshape(m, D_MODEL)
    x2 = _glu_call(g.reshape(m, D_MODEL), p['w_glu'], x2, tm_wide, 1024)
    x2 = _ffn_call(x2, p['nf0'], p['w_up'], p['w_down'], 0, p['nfin'], tm_wide, 1024, False)

    chunk = min(t_len, 64)
    cpb = _tile(t_len // chunk, 4)
    hb = 32
    qkv, z, gg, beta, tails = _inproj_call(
        x2, p['nm1'], p['w_qkvz'], p['w_ba'], conv0, p['conv_w'], p['a_log'], p['dt_bias'],
        t_len, tm, hb)
    tpb = t_len // min(t_len, tm)
    new_conv = tails[tpb - 1::tpb]
    og, s_fin = _delta_call(qkv.reshape(n_b, t_len, -1), z.reshape(n_b, t_len, VAL_DIM),
                            gg, beta, s0, p['gdn_norm_w'], chunk, cpb, hb)
    x2 = _matmul_res_call(og.reshape(m, VAL_DIM), p['w_out'], x2, tm_wide, 1024)
    y = _ffn_call(x2, p['nf1'], p['w_up'], p['w_down'], 1, p['nfin'], tm_wide, 1024, True)

    return (y.reshape(n_b, t_len, D_MODEL),
            sr.reshape(1, n_b, S5_GROUPS, S5_STATE), si.reshape(1, n_b, S5_GROUPS, S5_STATE),
            new_conv[None], s_fin[None])


def kernel(x_prompt, x_sample, state_s5_re, state_s5_im, state_gdn_conv, state_gdn_s,
           norm_mix, norm_ffn, norm_final,
           s5_lam_re, s5_lam_im, s5_log_dt, s5_b_re, s5_b_im, s5_c_re, s5_c_im,
           s5_d, s5_w_glu,
           gdn_w_in, gdn_conv_w, gdn_a_log, gdn_dt_bias, gdn_norm_w, gdn_w_out,
           ffn_w_up, ffn_w_down):
    wb, wc, a_re, a_im = _s5_params(s5_lam_re[0], s5_lam_im[0], s5_log_dt[0], s5_b_re[0], s5_b_im[0],
                              s5_c_re[0], s5_c_im[0])
    w_in = gdn_w_in[0]
    ba = w_in[:, CONV_DIM + VAL_DIM:]
    w_ba = jnp.concatenate([ba, jnp.zeros((D_MODEL, LANES - ba.shape[1]), w_in.dtype)], axis=1)
    row = lambda a: a.reshape(1, -1).astype(F32)
    p = dict(
        nm0=row(norm_mix[0]), nm1=row(norm_mix[1]), nf0=row(norm_ffn[0]), nf1=row(norm_ffn[1]),
        nfin=row(norm_final), s5_d=row(s5_d[0]), wb=wb, wc=wc, a_re=a_re, a_im=a_im,
        w_glu=s5_w_glu[0].astype(BF16),
        w_up=ffn_w_up.astype(BF16), w_down=ffn_w_down.astype(BF16),
        w_qkvz=w_in.astype(BF16), w_ba=w_ba.astype(BF16),
        conv_w=gdn_conv_w[0].astype(F32), a_log=row(gdn_a_log[0]), dt_bias=row(gdn_dt_bias[0]),
        gdn_norm_w=row(gdn_norm_w[0]), w_out=gdn_w_out[0].astype(BF16),
    )
    n_bp = x_prompt.shape[0]
    z_state = jnp.zeros((n_bp, S5_GROUPS, S5_STATE), F32)
    z_conv = jnp.zeros((n_bp, CONV_W - 1, CONV_DIM), F32)
    z_s = jnp.zeros((n_bp, V_HEADS, HEAD, HEAD), F32)

    yp, p_re, p_im, p_conv, p_s = _trunk(x_prompt, z_state, z_state, z_conv, z_s, p)
    ys, s_re, s_im, s_conv, s_s = _trunk(x_sample, state_s5_re[0], state_s5_im[0],
                                         state_gdn_conv[0], state_gdn_s[0], p)
    return (yp, ys, p_re, p_im, p_conv, p_s, s_re, s_im, s_conv, s_s)
```

```python
import functools
import math

import jax
import jax.numpy as jnp
from jax import lax
from jax.experimental import pallas as pl
from jax.experimental.pallas import tpu as pltpu

F32 = jnp.float32
BF16 = jnp.bfloat16

RMS_EPS = 1e-6
L2_EPS = 1e-6
D_MODEL = 2048
S5_GROUP = 16
S5_STATE = 64
S5_GROUPS = D_MODEL // S5_GROUP
S5_CH = S5_GROUPS * S5_STATE
HEAD = 128
QK_HEADS = D_MODEL // HEAD
V_HEADS = 2 * QK_HEADS
KEY_DIM = QK_HEADS * HEAD
VAL_DIM = V_HEADS * HEAD
CONV_DIM = 2 * KEY_DIM + VAL_DIM
CONV_W = 4

SUBLANES = 8
LANES = 128
MXU_K = 256
S5_KB = D_MODEL // MXU_K
S5_CB = S5_CH // S5_KB
S5_SCAN_W = S5_CB
VMEM_LIMIT = 60 * 1024 * 1024


def _cparams(sem):
    return pltpu.CompilerParams(dimension_semantics=sem, vmem_limit_bytes=VMEM_LIMIT)


def _rms(x, w):
    return x * lax.rsqrt(jnp.mean(x * x, axis=-1, keepdims=True) + RMS_EPS) * w


def _s5_body(x_ref, nw_ref, d_ref, wb_ref, wc_ref, coef_ref, perm_ref, permt_ref, s0r_ref, s0i_ref,
             g_ref, sr_ref, si_ref, car_ref, h_ref, bu_ref, st_ref, *, tt, multi_seq):
    t = pl.program_id(1)
    seg = tt // SUBLANES

    @pl.when(t == 0)
    def _():
        car_ref[0] = jnp.broadcast_to(s0r_ref[0], (SUBLANES, S5_CH))
        car_ref[1] = jnp.broadcast_to(s0i_ref[0], (SUBLANES, S5_CH))

    h = _rms(x_ref[0], nw_ref[...])
    h_ref[...] = h
    hp = jnp.dot(perm_ref[...], h.astype(BF16), preferred_element_type=F32).astype(BF16)
    row0 = lax.broadcasted_iota(jnp.int32, (SUBLANES, S5_SCAN_W), 0) == 0

    def step(ar, ai, sr, si, xr, xi):
        return ar * sr - ai * si + xr, ar * si + ai * sr + xi

    def project_in(kb):
        ks = slice(kb * MXU_K, (kb + 1) * MXU_K)
        bu_ref[kb % 2] = jnp.dot(hp[:, ks], wb_ref[kb], preferred_element_type=F32)

    def project_out(kb):
        ks = slice(kb * MXU_K, (kb + 1) * MXU_K)
        yp = jnp.dot(st_ref[kb % 2].astype(BF16), wc_ref[kb], preferred_element_type=F32)
        y_hi = yp.astype(BF16)
        y_lo = (yp - y_hi.astype(F32)).astype(BF16)
        y2 = jnp.dot(permt_ref[...], jnp.concatenate([y_hi, y_lo], axis=1),
                     preferred_element_type=F32)
        y = y2[:, :MXU_K] + y2[:, MXU_K:] + d_ref[:, ks] * h_ref[:, ks]
        g_ref[0, :, ks] = jax.nn.gelu(y).astype(BF16)

    def scan(kb, part):
        lo = part * S5_SCAN_W
        re_cols = slice(lo, lo + S5_SCAN_W)
        im_cols = slice(S5_CB + lo, S5_CB + lo + S5_SCAN_W)
        cs = slice(kb * S5_CB + lo, kb * S5_CB + lo + S5_SCAN_W)
        buf = kb % 2

        def bu_rows(j):
            rows = slice(j * SUBLANES, (j + 1) * SUBLANES)
            return bu_ref[buf, rows, re_cols], bu_ref[buf, rows, im_cols]

        ar = coef_ref[4, 0, :, cs]
        ai = coef_ref[4, 1, :, cs]
        if multi_seq:
            sr = car_ref[0, :, cs]
            si = car_ref[1, :, cs]
            for j in range(seg):
                sr, si = step(ar, ai, sr, si, *bu_rows(j))
                rows = slice(j * SUBLANES, (j + 1) * SUBLANES)
                st_ref[buf, rows, re_cols] = sr
                st_ref[buf, rows, im_cols] = si
            car_ref[0, :, cs] = sr
            car_ref[1, :, cs] = si
            return
        er, ei = bu_rows(0)
        for j in range(1, seg):
            er, ei = step(ar, ai, er, ei, *bu_rows(j))
        for lvl, sh in enumerate((1, 2, 4)):
            cr_ = coef_ref[lvl, 0, :, cs]
            ci_ = coef_ref[lvl, 1, :, cs]
            pr = pltpu.roll(er, sh, 0)
            pi = pltpu.roll(ei, sh, 0)
            er, ei = er + cr_ * pr - ci_ * pi, ei + cr_ * pi + ci_ * pr
        c0r = car_ref[0, :, cs]
        c0i = car_ref[1, :, cs]
        cr_ = coef_ref[3, 0, :, cs]
        ci_ = coef_ref[3, 1, :, cs]
        fr = er + cr_ * c0r - ci_ * c0i
        fi = ei + cr_ * c0i + ci_ * c0r
        car_ref[0, :, cs] = jnp.broadcast_to(fr[SUBLANES - 1:SUBLANES], (SUBLANES, S5_SCAN_W))
        car_ref[1, :, cs] = jnp.broadcast_to(fi[SUBLANES - 1:SUBLANES], (SUBLANES, S5_SCAN_W))
        sr = jnp.where(row0, c0r, pltpu.roll(fr, 1, 0))
        si = jnp.where(row0, c0i, pltpu.roll(fi, 1, 0))
        for j in range(seg):
            sr, si = step(ar, ai, sr, si, *bu_rows(j))
            rows = slice(j * SUBLANES, (j + 1) * SUBLANES)
            st_ref[buf, rows, re_cols] = sr
            st_ref[buf, rows, im_cols] = si

    project_in(0)
    for kb in range(S5_KB):
        if kb + 1 < S5_KB:
            project_in(kb + 1)
        if kb >= 1:
            project_out(kb - 1)
        for part in range(S5_CB // S5_SCAN_W):
            scan(kb, part)
    project_out(S5_KB - 1)

    @pl.when(t == pl.num_programs(1) - 1)
    def _():
        n_state = SUBLANES if multi_seq else 1
        sr_ref[0] = car_ref[0, 0:n_state, :]
        si_ref[0] = car_ref[1, 0:n_state, :]


def _s5_call(x, nw, d, wb, wc, a_re, a_im, s0r, s0i, tt, multi_seq):
    n_b, t_len, _ = x.shape
    seg = tt // SUBLANES
    n_state = s0r.shape[1]
    assert n_state == (SUBLANES if multi_seq else 1) and (not multi_seq or t_len == tt)
    coef = _s5_coef(a_re, a_im, seg)
    p_idx = jnp.arange(tt)
    src = (p_idx % SUBLANES) * seg + p_idx // SUBLANES
    perm = (src[:, None] == p_idx[None, :]).astype(BF16)
    const = lambda *shape: pl.BlockSpec(shape, lambda b, t: (0,) * len(shape))
    state = pl.BlockSpec((1, n_state, S5_CH), lambda b, t: (b, 0, 0))
    return pl.pallas_call(
        functools.partial(_s5_body, tt=tt, multi_seq=multi_seq),
        grid=(n_b, t_len // tt),
        in_specs=[pl.BlockSpec((1, tt, D_MODEL), lambda b, t: (b, t, 0)),
                  const(1, D_MODEL), const(1, D_MODEL),
                  const(S5_KB, MXU_K, 2 * S5_CB), const(S5_KB, 2 * S5_CB, MXU_K),
                  const(5, 2, SUBLANES, S5_CH), const(tt, tt), const(tt, tt), state, state],
        out_specs=[pl.BlockSpec((1, tt, D_MODEL), lambda b, t: (b, t, 0)), state, state],
        out_shape=[jax.ShapeDtypeStruct((n_b, t_len, D_MODEL), BF16),
                   jax.ShapeDtypeStruct((n_b, n_state, S5_CH), F32),
                   jax.ShapeDtypeStruct((n_b, n_state, S5_CH), F32)],
        scratch_shapes=[pltpu.VMEM((2, SUBLANES, S5_CH), F32),
                        pltpu.VMEM((tt, D_MODEL), F32),
                        pltpu.VMEM((2, tt, 2 * S5_CB), F32),
                        pltpu.VMEM((2, tt, 2 * S5_CB), F32)],
        compiler_params=_cparams(("arbitrary", "arbitrary")),
        name="s5_mixer",
    )(x, nw, d, wb, wc, coef, perm, perm.T, s0r, s0i)


def _s5_coef(a_re, a_im, seg):
    assert seg & (seg - 1) == 0, seg
    ar, ai = a_re, a_im
    for _ in range(int(math.log2(seg))):
        ar, ai = ar * ar - ai * ai, 2.0 * ar * ai
    pw_r, pw_i = [ar], [ai]
    for _ in range(SUBLANES - 1):
        pr, pi = pw_r[-1], pw_i[-1]
        pw_r.append(pr * ar - pi * ai)
        pw_i.append(pr * ai + pi * ar)
    row = jnp.arange(SUBLANES)[:, None]
    levels = []
    for sh in (1, 2, 4):
        m = (row >= sh).astype(F32)
        levels.append(jnp.stack([m * pw_r[sh - 1], m * pw_i[sh - 1]]))
    levels.append(jnp.stack([jnp.concatenate(pw_r, axis=0), jnp.concatenate(pw_i, axis=0)]))
    ones = jnp.ones((SUBLANES, 1), F32)
    levels.append(jnp.stack([ones * a_re, ones * a_im]))
    return jnp.stack(levels)


def _s5_params(lam_re, lam_im, log_dt, b_re, b_im, c_re, c_im):
    lr = jnp.minimum(lam_re.astype(F32), -1e-4)
    li = lam_im.astype(F32)
    dt = jnp.exp(log_dt.astype(F32))[:, None]
    mag = jnp.exp(lr * dt)
    ang = li * dt
    a_re = mag * jnp.cos(ang)
    a_im = mag * jnp.sin(ang)
    nr = a_re - 1.0
    den = lr * lr + li * li
    k_re = (nr * lr + a_im * li) / den
    k_im = (a_im * lr - nr * li) / den
    br = b_re.astype(F32)
    bi = b_im.astype(F32)
    bb_re = k_re[..., None] * br - k_im[..., None] * bi
    bb_im = k_re[..., None] * bi + k_im[..., None] * br

    gl = S5_GROUPS // S5_KB
    eye = jnp.eye(gl, dtype=F32)

    def blockdiag_in(bb):
        bbk = bb.reshape(S5_KB, gl, S5_STATE, S5_GROUP)
        w = jnp.einsum('kgph,gf->kghfp', bbk, eye)
        return w.reshape(S5_KB, gl * S5_GROUP, gl * S5_STATE)

    def blockdiag_out(c):
        ck = c.reshape(S5_KB, gl, S5_GROUP, S5_STATE)
        w = jnp.einsum('kghp,gf->kgpfh', ck, eye)
        return w.reshape(S5_KB, gl * S5_STATE, gl * S5_GROUP)

    wb = jnp.concatenate([blockdiag_in(bb_re), blockdiag_in(bb_im)], axis=2).astype(BF16)
    wc = jnp.concatenate([blockdiag_out(c_re.astype(F32)),
                          -blockdiag_out(c_im.astype(F32))], axis=1).astype(BF16)

    return wb, wc, a_re.reshape(1, S5_CH), a_im.reshape(1, S5_CH)


def _glu_body(g_ref, wa_ref, wg_ref, x_ref, o_ref):
    g = g_ref[...]
    za = jnp.dot(g, wa_ref[...], preferred_element_type=F32)
    zg = jnp.dot(g, wg_ref[...], preferred_element_type=F32)
    o_ref[...] = x_ref[...] + za * jax.nn.sigmoid(zg)


def _glu_call(g, w_glu, x, tm, tn):
    m = g.shape[0]
    nj = D_MODEL // tn
    return pl.pallas_call(
        _glu_body,
        grid=(m // tm, nj),
        in_specs=[pl.BlockSpec((tm, D_MODEL), lambda i, j: (i, 0)),
                  pl.BlockSpec((D_MODEL, tn), lambda i, j: (0, j)),
                  pl.BlockSpec((D_MODEL, tn), lambda i, j: (0, j + nj)),
                  pl.BlockSpec((tm, tn), lambda i, j: (i, j))],
        out_specs=pl.BlockSpec((tm, tn), lambda i, j: (i, j)),
        out_shape=jax.ShapeDtypeStruct((m, D_MODEL), F32),
        compiler_params=_cparams(("parallel", "arbitrary")),
        name="glu_residual",
    )(g, w_glu, w_glu, x)


def _ffn_body(x_ref, nw_ref, wu_ref, wd_ref, fw_ref, o_ref, h_ref, *, final_norm):
    f = pl.program_id(1)

    @pl.when(f == 0)
    def _():
        x = x_ref[...]
        h_ref[...] = _rms(x, nw_ref[...]).astype(BF16)
        o_ref[...] = x

    a = jnp.maximum(jnp.dot(h_ref[...], wu_ref[...], preferred_element_type=F32), 0.0)
    o_ref[...] += jnp.dot((a * a).astype(BF16), wd_ref[...], preferred_element_type=F32)

    if final_norm:
        @pl.when(f == pl.num_programs(1) - 1)
        def _():
            o_ref[...] = _rms(o_ref[...], fw_ref[...])


def _ffn_call(x, nw, w_up, w_down, layer, fw, tm, tf, final_norm):
    m = x.shape[0]
    d_ff = w_up.shape[2]
    return pl.pallas_call(
        functools.partial(_ffn_body, final_norm=final_norm),
        grid=(m // tm, d_ff // tf),
        in_specs=[pl.BlockSpec((tm, D_MODEL), lambda i, f: (i, 0)),
                  pl.BlockSpec((1, D_MODEL), lambda i, f: (0, 0)),
                  pl.BlockSpec((None, D_MODEL, tf), lambda i, f: (layer, 0, f)),
                  pl.BlockSpec((None, tf, D_MODEL), lambda i, f: (layer, f, 0)),
                  pl.BlockSpec((1, D_MODEL), lambda i, f: (0, 0))],
        out_specs=pl.BlockSpec((tm, D_MODEL), lambda i, f: (i, 0)),
        out_shape=jax.ShapeDtypeStruct((m, D_MODEL), F32),
        scratch_shapes=[pltpu.VMEM((tm, D_MODEL), BF16)],
        compiler_params=_cparams(("parallel", "arbitrary")),
        name="ffn_residual",
    )(x, nw, w_up, w_down, fw)


def _matmul_res_body(a_ref, w_ref, x_ref, o_ref):
    o_ref[...] = x_ref[...] + jnp.dot(a_ref[...], w_ref[...], preferred_element_type=F32)


def _matmul_res_call(a, w, x, tm, tn):
    m, k = a.shape
    n = w.shape[1]
    return pl.pallas_call(
        _matmul_res_body,
        grid=(m // tm, n // tn),
        in_specs=[pl.BlockSpec((tm, k), lambda i, j: (i, 0)),
                  pl.BlockSpec((k, tn), lambda i, j: (0, j)),
                  pl.BlockSpec((tm, tn), lambda i, j: (i, j))],
        out_specs=pl.BlockSpec((tm, tn), lambda i, j: (i, j)),
        out_shape=jax.ShapeDtypeStruct((m, n), F32),
        compiler_params=_cparams(("parallel", "arbitrary")),
        name="matmul_residual",
    )(a, w, x)


INPROJ_TN = 1024
N_CONV_TILES = CONV_DIM // INPROJ_TN
N_Z_TILES = VAL_DIM // INPROJ_TN
N_QK_TILES = 2 * KEY_DIM // INPROJ_TN
N_Q_TILES = KEY_DIM // INPROJ_TN
CONV_ROWS = 64


def _inproj_body(x_ref, nw_ref, w_ref, wba_ref, cprev_ref, cw_ref, alog_ref, dtb_ref,
                 qkv_ref, z_ref, g_ref, beta_ref, nc_ref, hs_ref, hist_ref, ra_ref, rb_ref,
                 *, tm, seq_rows, tiles_per_seq, gate_group):
    n_seq = tm // seq_rows
    i = pl.program_id(0)
    j = pl.program_id(1)
    tn = INPROJ_TN
    n_sub = tn // MXU_K
    tail = CONV_W - 1
    pad = SUBLANES

    @pl.when(j == 0)
    def _():
        hs_ref[...] = _rms(x_ref[...], nw_ref[...]).astype(BF16)
        ba = jnp.dot(hs_ref[...], wba_ref[...], preferred_element_type=F32)
        zz = ba[:, V_HEADS:2 * V_HEADS] + dtb_ref[...]
        softplus = jnp.maximum(zz, 0.0) + jnp.log1p(jnp.exp(-jnp.abs(zz)))
        g_all = -jnp.exp(alog_ref[...]) * softplus
        beta_all = jax.nn.sigmoid(ba[:, 0:V_HEADS])
        for gi in range(V_HEADS // gate_group):
            g_ref[gi] = g_all[:, gi * gate_group:(gi + 1) * gate_group]
            beta_ref[gi] = beta_all[:, gi * gate_group:(gi + 1) * gate_group]

    @pl.when((i % tiles_per_seq == 0) & (j == 0))
    def _():
        hist_ref[...] = jnp.concatenate(
            [jnp.zeros((pad - tail, CONV_DIM), F32), cprev_ref[0]], axis=0)

    def project(r_ref, half, s):
        c0 = half * tn + s * MXU_K
        r_ref[s, pad:pad + tm, :] = jnp.dot(hs_ref[...], w_ref[:, c0:c0 + MXU_K],
                                            preferred_element_type=F32)

    def conv_rows(win, c0):
        acc = win[pad:] * cw_ref[tail:tail + 1, c0:c0 + MXU_K]
        for d in range(1, CONV_W):
            shifted = pltpu.roll(win, d, 0)[pad:]
            acc = acc + shifted * cw_ref[tail - d:tail - d + 1, c0:c0 + MXU_K]
        return acc

    def epilogue(r_ref, tile, out_half, s):
        c0 = tile * tn + s * MXU_K
        o0 = out_half * tn + s * MXU_K
        l2 = tile < N_QK_TILES
        scale = HEAD ** -0.5 if tile < N_Q_TILES else 1.0

        def finish(acc, rows):
            c = acc * jax.nn.sigmoid(acc)
            if l2:
                for hd in range(MXU_K // HEAD):
                    ch = c[:, hd * HEAD:(hd + 1) * HEAD]
                    ch = ch * (lax.rsqrt(jnp.sum(ch * ch, axis=-1, keepdims=True) + L2_EPS) * scale)
                    qkv_ref[rows, o0 + hd * HEAD:o0 + (hd + 1) * HEAD] = ch.astype(BF16)
            else:
                qkv_ref[rows, o0:o0 + MXU_K] = c.astype(BF16)

        r_ref[s, 0:pad, :] = hist_ref[:, c0:c0 + MXU_K]
        hist_ref[:, c0:c0 + MXU_K] = r_ref[s, tm:tm + pad, :]
        for sq in range(n_seq):
            end = pad + (sq + 1) * seq_rows
            nc_ref[sq, :, c0:c0 + MXU_K] = r_ref[s, end - tail:end, :]
        for r0 in range(0, tm, CONV_ROWS):
            rc = min(CONV_ROWS, tm - r0)
            finish(conv_rows(r_ref[s, r0:r0 + rc + pad, :], c0), slice(r0, r0 + rc))
        for sq in range(1, n_seq):
            r0 = sq * seq_rows
            win = jnp.concatenate([jnp.zeros((pad - tail, MXU_K), F32), cprev_ref[sq, :, c0:c0 + MXU_K],
                                   r_ref[s, pad + r0:pad + r0 + pad, :]], axis=0)
            finish(conv_rows(win, c0), slice(r0, r0 + pad))

    def project_z(half, s):
        c0 = half * tn + s * MXU_K
        zz = jnp.dot(hs_ref[...], w_ref[:, c0:c0 + MXU_K], preferred_element_type=F32)
        z_ref[:, c0:c0 + MXU_K] = (zz * jax.nn.sigmoid(zz)).astype(BF16)

    def step_body(step):
        cur, prev = step % 2, (step - 1) % 2
        for half, r_ref in ((0, ra_ref), (1, rb_ref)):
            for s in range(n_sub):
                if step < n_conv_steps:
                    project(r_ref.at[cur], half, s)
                else:
                    project_z(half, s)
                if 0 < step <= n_conv_steps:
                    epilogue(r_ref.at[prev], 2 * (step - 1) + half, half, s)

    n_conv_steps = N_CONV_TILES // 2
    for step in range(n_conv_steps + 1):
        pl.when(j == step)(functools.partial(step_body, step))
    pl.when(j > n_conv_steps)(functools.partial(step_body, n_conv_steps + 1))


def _inproj_call(x, nw, w_qkvz, w_ba, conv_prev, conv_w, a_log, dt_bias, t_len, tm, gate_group):
    m = x.shape[0]
    n_grp = V_HEADS // gate_group
    tn2 = 2 * INPROJ_TN
    n_cs = N_CONV_TILES // 2
    n_zs = N_Z_TILES // 2
    seq_rows = min(t_len, tm)
    n_seq = tm // seq_rows
    tpb = t_len // seq_rows
    return pl.pallas_call(
        functools.partial(_inproj_body, tm=tm, seq_rows=seq_rows, tiles_per_seq=tpb,
                          gate_group=gate_group),
        grid=(m // tm, n_cs + n_zs),
        in_specs=[pl.BlockSpec((tm, D_MODEL), lambda i, j: (i, 0)),
                  pl.BlockSpec((1, D_MODEL), lambda i, j: (0, 0)),
                  pl.BlockSpec((D_MODEL, tn2), lambda i, j: (0, j)),
                  pl.BlockSpec((D_MODEL, LANES), lambda i, j: (0, 0)),
                  pl.BlockSpec((n_seq, CONV_W - 1, CONV_DIM), lambda i, j: (i // tpb, 0, 0)),
                  pl.BlockSpec((CONV_W, CONV_DIM), lambda i, j: (0, 0)),
                  pl.BlockSpec((1, V_HEADS), lambda i, j: (0, 0)),
                  pl.BlockSpec((1, V_HEADS), lambda i, j: (0, 0))],
        out_specs=[pl.BlockSpec((tm, tn2), lambda i, j: (i, jnp.clip(j - 1, 0, n_cs - 1))),
                   pl.BlockSpec((tm, tn2), lambda i, j: (i, jnp.maximum(j - n_cs, 0))),
                   pl.BlockSpec((n_grp, tm, gate_group), lambda i, j: (0, i, 0)),
                   pl.BlockSpec((n_grp, tm, gate_group), lambda i, j: (0, i, 0)),
                   pl.BlockSpec((n_seq, CONV_W - 1, CONV_DIM), lambda i, j: (i, 0, 0))],
        out_shape=[jax.ShapeDtypeStruct((m, CONV_DIM), BF16),
                   jax.ShapeDtypeStruct((m, VAL_DIM), BF16),
                   jax.ShapeDtypeStruct((n_grp, m, gate_group), F32),
                   jax.ShapeDtypeStruct((n_grp, m, gate_group), F32),
                   jax.ShapeDtypeStruct((m // seq_rows, CONV_W - 1, CONV_DIM), F32)],
        scratch_shapes=[pltpu.VMEM((tm, D_MODEL), BF16),
                        pltpu.VMEM((SUBLANES, CONV_DIM), F32),
                        pltpu.VMEM((2, INPROJ_TN // MXU_K, tm + SUBLANES, MXU_K), F32),
                        pltpu.VMEM((2, INPROJ_TN // MXU_K, tm + SUBLANES, MXU_K), F32)],
        compiler_params=_cparams(("arbitrary", "arbitrary")),
        name="gdn_inproj_conv",
    )(x, nw, w_qkvz, w_ba, conv_prev, conv_w, a_log, dt_bias)


INV_BASE = 16


def _mm16(a, b):
    return jnp.dot(a.astype(BF16), b.astype(BF16), preferred_element_type=F32)


def _tri_inverse_all(nmats, ri, ci, chunk):
    base = min(INV_BASE, chunk)
    sh = int(math.log2(base))
    same = (ri >> sh) == (ci >> sh)
    eye = (ri == ci).astype(F32)
    ms = [jnp.where(same, -n, 0.0) for n in nmats]
    ps = [eye + m for m in ms]
    for _ in range(sh - 1):
        ms = [_mm16(m, m) for m in ms]
        ps = [p + _mm16(p, m) for p, m in zip(ps, ms)]
    size = base
    while size < chunk:
        sh += 1
        size *= 2
        same_next = (ri >> sh) == (ci >> sh)
        sel = same_next & jnp.logical_not(same)
        ts = [_mm16(p, jnp.where(sel, n, 0.0)) for p, n in zip(ps, nmats)]
        ps = [p - _mm16(t, p) for p, t in zip(ps, ts)]
        same = same_next
    return ps


def _delta_body(q_ref, k_ref, v_ref, z_ref, gc_ref, bc_ref, s0_ref, nw_ref,
                o_ref, sf_ref, s_ref, *, chunk, cpb, hb):
    c_idx = pl.program_id(2)

    @pl.when(c_idx == 0)
    def _():
        s_ref[...] = s0_ref[0]

    ri = lax.broadcasted_iota(jnp.int32, (chunk, chunk), 0)
    ci = lax.broadcasted_iota(jnp.int32, (chunk, chunk), 1)
    causal = ri >= ci
    strict = ri > ci
    tril = causal.astype(F32)
    triu = (ri <= ci).astype(F32)
    eye = (ri == ci).astype(F32)
    hi = lax.Precision.HIGHEST
    nt = (((1,), (1,)), ((), ()))
    tn = (((0,), (0,)), ((), ()))
    hq = hb // 2

    kqk, pre = {}, {}
    for c in range(cpb):
        rows = slice(c * chunk, (c + 1) * chunk)
        for j in range(hq):
            k16 = k_ref[0, rows, j * HEAD:(j + 1) * HEAD]
            q16 = q_ref[0, rows, j * HEAD:(j + 1) * HEAD]
            kqk[c, j] = lax.dot_general(jnp.concatenate([k16, q16], axis=0), k16, nt,
                                        preferred_element_type=F32)
    nmats, keys = [], []
    for c in range(cpb):
        rows = slice(c * chunk, (c + 1) * chunk)
        g_raw = gc_ref[0, rows, :]
        gcol = jnp.dot(tril, g_raw, precision=hi, preferred_element_type=F32)
        grow = lax.dot_general(g_raw, triu, tn, precision=hi, preferred_element_type=F32)
        bcol = bc_ref[0, rows, :]
        brow = lax.dot_general(bcol, eye, tn, precision=hi, preferred_element_type=F32)
        for hh in range(hb):
            j = hh // 2
            k = k_ref[0, rows, j * HEAD:(j + 1) * HEAD].astype(F32)
            q = q_ref[0, rows, j * HEAD:(j + 1) * HEAD].astype(F32)
            v16 = v_ref[0, rows, hh * HEAD:(hh + 1) * HEAD]
            gc = gcol[:, hh:hh + 1]
            gr = grow[hh:hh + 1, :]
            beta = bcol[:, hh:hh + 1]
            decay = jnp.exp(jnp.where(causal, gc - gr, -jnp.inf))
            egc = jnp.exp(gc)
            g_last = gc[chunk - 1:chunk, :]
            kk = kqk[c, j][:chunk]
            qk = kqk[c, j][chunk:]
            nmats.append(jnp.where(strict, (kk * beta) * decay, 0.0))
            keys.append((c, hh))
            pre[c, hh] = dict(
                rhs=jnp.concatenate([v16, (k * egc).astype(BF16)], axis=1),
                brow=brow[hh:hh + 1, :],
                qg=(q * egc).astype(BF16),
                intra=(qk * decay).astype(BF16),
                kd=(k * jnp.exp(g_last - gc)).astype(BF16),
                sdec=jnp.exp(g_last))
    tinv = _tri_inverse_all(nmats, ri, ci, chunk)
    for key, t in zip(keys, tinv):
        pre[key]['sol'] = jnp.dot((t * pre[key]['brow']).astype(BF16), pre[key]['rhs'],
                                  preferred_element_type=F32)

    for c in range(cpb):
        rows = slice(c * chunk, (c + 1) * chunk)
        ws = []
        for hh in range(hb):
            d = pre[c, hh]
            wq = jnp.concatenate([d['sol'][:, HEAD:].astype(BF16), d['qg']], axis=0)
            ws.append(jnp.dot(wq, s_ref[hh].astype(BF16), preferred_element_type=F32))
        for hh in range(hb):
            d = pre[c, hh]
            v16 = (d['sol'][:, :HEAD] - ws[hh][:chunk]).astype(BF16)
            o = ws[hh][chunk:] + jnp.dot(d['intra'], v16, preferred_element_type=F32)
            s_ref[hh] = s_ref[hh] * d['sdec'] + lax.dot_general(d['kd'], v16, tn,
                                                                preferred_element_type=F32)
            o = o * lax.rsqrt(jnp.mean(o * o, axis=-1, keepdims=True) + RMS_EPS) * nw_ref[...]
            gate = z_ref[0, rows, hh * HEAD:(hh + 1) * HEAD]
            o_ref[0, rows, hh * HEAD:(hh + 1) * HEAD] = (o * gate).astype(BF16)

    @pl.when(c_idx == pl.num_programs(2) - 1)
    def _():
        sf_ref[0] = s_ref[...]


def _delta_call(qkv, z, g, beta, s0, norm_w, chunk, cpb, hb):
    n_b, t_len, _ = qkv.shape
    nhg = V_HEADS // hb
    tc = chunk * cpb
    n_tc = t_len // tc
    assert g.shape == (nhg, n_b * t_len, hb), g.shape
    hq = hb // 2
    assert KEY_DIM % (hq * HEAD) == 0 and (2 * KEY_DIM) % (hb * HEAD) == 0, hb
    q_off = 0
    k_off = KEY_DIM // (hq * HEAD)
    v_off = 2 * KEY_DIM // (hb * HEAD)
    return pl.pallas_call(
        functools.partial(_delta_body, chunk=chunk, cpb=cpb, hb=hb),
        grid=(n_b, nhg, t_len // tc),
        in_specs=[pl.BlockSpec((1, tc, hq * HEAD), lambda b, h, c: (b, c, q_off + h)),
                  pl.BlockSpec((1, tc, hq * HEAD), lambda b, h, c: (b, c, k_off + h)),
                  pl.BlockSpec((1, tc, hb * HEAD), lambda b, h, c: (b, c, v_off + h)),
                  pl.BlockSpec((1, tc, hb * HEAD), lambda b, h, c: (b, c, h)),
                  pl.BlockSpec((1, tc, hb), lambda b, h, c: (h, b * n_tc + c, 0)),
                  pl.BlockSpec((1, tc, hb), lambda b, h, c: (h, b * n_tc + c, 0)),
                  pl.BlockSpec((1, hb, HEAD, HEAD), lambda b, h, c: (b, h, 0, 0)),
                  pl.BlockSpec((1, HEAD), lambda b, h, c: (0, 0))],
        out_specs=[pl.BlockSpec((1, tc, hb * HEAD), lambda b, h, c: (b, c, h)),
                   pl.BlockSpec((1, hb, HEAD, HEAD), lambda b, h, c: (b, h, 0, 0))],
        out_shape=[jax.ShapeDtypeStruct((n_b, t_len, VAL_DIM), BF16),
                   jax.ShapeDtypeStruct((n_b, V_HEADS, HEAD, HEAD), F32)],
        scratch_shapes=[pltpu.VMEM((hb, HEAD, HEAD), F32)],
        compiler_params=_cparams(("arbitrary", "arbitrary", "arbitrary")),
        name="gated_delta_rule",
    )(qkv, qkv, qkv, z, g, beta, s0, norm_w)


def _tile(m, pref):
    t = min(m, pref)
    assert m % t == 0, (m, t)
    return t


def _trunk(x, s5_re0, s5_im0, conv0, s0, p):
    n_b, t_len, _ = x.shape
    m = n_b * t_len
    tm = _tile(m, 512)
    tm_wide = _tile(m, 1024)

    tt = _tile(t_len, 256)
    multi_seq = SUBLANES * t_len <= 256 and n_b % SUBLANES == 0
    n_blk, n_state = (n_b // SUBLANES, SUBLANES) if multi_seq else (n_b, 1)
    if multi_seq:
        tt = SUBLANES * t_len
    g, sr, si = _s5_call(x.reshape(n_blk, n_state * t_len, D_MODEL), p['nm0'], p['s5_d'], p['wb'], p['wc'],
                         p['a_re'], p['a_im'], s5_re0.reshape(n_blk, n_state, S5_CH),
                         s5_im0.reshape(n_blk, n_state, S5_CH), tt, multi_seq)
    x2 = x.reshape(m, D_MODEL)
    x2 = _glu_call(g.reshape(m, D_MODEL), p['w_glu'], x2, tm_wide, 1024)
    x2 = _ffn_call(x2, p['nf0'], p['w_up'], p['w_down'], 0, p['nfin'], tm_wide, 1024, False)

    chunk = min(t_len, 64)
    cpb = _tile(t_len // chunk, 4)
    hb = V_HEADS
    qkv, z, gg, beta, tails = _inproj_call(
        x2, p['nm1'], p['w_qkvz'], p['w_ba'], conv0, p['conv_w'], p['a_log'], p['dt_bias'],
        t_len, tm, hb)
    tpb = t_len // min(t_len, tm)
    new_conv = tails[tpb - 1::tpb]
    og, s_fin = _delta_call(qkv.reshape(n_b, t_len, -1), z.reshape(n_b, t_len, VAL_DIM),
                            gg, beta, s0, p['gdn_norm_w'], chunk, cpb, hb)
    x2 = _matmul_res_call(og.reshape(m, VAL_DIM), p['w_out'], x2, tm_wide, 1024)
    y = _ffn_call(x2, p['nf1'], p['w_up'], p['w_down'], 1, p['nfin'], tm_wide, 1024, True)

    return (y.reshape(n_b, t_len, D_MODEL),
            sr.reshape(1, n_b, S5_GROUPS, S5_STATE), si.reshape(1, n_b, S5_GROUPS, S5_STATE),
            new_conv[None], s_fin[None])


def kernel(x_prompt, x_sample, state_s5_re, state_s5_im, state_gdn_conv, state_gdn_s,
           norm_mix, norm_ffn, norm_final,
           s5_lam_re, s5_lam_im, s5_log_dt, s5_b_re, s5_b_im, s5_c_re, s5_c_im,
           s5_d, s5_w_glu,
           gdn_w_in, gdn_conv_w, gdn_a_log, gdn_dt_bias, gdn_norm_w, gdn_w_out,
           ffn_w_up, ffn_w_down):
    wb, wc, a_re, a_im = _s5_params(s5_lam_re[0], s5_lam_im[0], s5_log_dt[0], s5_b_re[0], s5_b_im[0],
                              s5_c_re[0], s5_c_im[0])
    w_in = gdn_w_in[0]
    ba = w_in[:, CONV_DIM + VAL_DIM:]
    w_ba = jnp.concatenate([ba, jnp.zeros((D_MODEL, LANES - ba.shape[1]), w_in.dtype)], axis=1)
    row = lambda a: a.reshape(1, -1).astype(F32)
    p = dict(
        nm0=row(norm_mix[0]), nm1=row(norm_mix[1]), nf0=row(norm_ffn[0]), nf1=row(norm_ffn[1]),
        nfin=row(norm_final), s5_d=row(s5_d[0]), wb=wb, wc=wc, a_re=a_re, a_im=a_im,
        w_glu=s5_w_glu[0].astype(BF16),
        w_up=ffn_w_up.astype(BF16), w_down=ffn_w_down.astype(BF16),
        w_qkvz=w_in.astype(BF16), w_ba=w_ba.astype(BF16),
        conv_w=gdn_conv_w[0].astype(F32), a_log=row(gdn_a_log[0]), dt_bias=row(gdn_dt_bias[0]),
        gdn_norm_w=row(gdn_norm_w[0]), w_out=gdn_w_out[0].astype(BF16),
    )
    n_bp = x_prompt.shape[0]
    z_state = jnp.zeros((n_bp, S5_GROUPS, S5_STATE), F32)
    z_conv = jnp.zeros((n_bp, CONV_W - 1, CONV_DIM), F32)
    z_s = jnp.zeros((n_bp, V_HEADS, HEAD, HEAD), F32)

    yp, p_re, p_im, p_conv, p_s = _trunk(x_prompt, z_state, z_state, z_conv, z_s, p)
    ys, s_re, s_im, s_conv, s_s = _trunk(x_sample, state_s5_re[0], state_s5_im[0],
                                         state_gdn_conv[0], state_gdn_s[0], p)
    return (yp, ys, p_re, p_im, p_conv, p_s, s_re, s_im, s_conv, s_s)
```

```python
import functools
import math

import jax
import jax.numpy as jnp
from jax import lax
from jax.experimental import pallas as pl
from jax.experimental.pallas import tpu as pltpu

F32 = jnp.float32
BF16 = jnp.bfloat16

RMS_EPS = 1e-6
L2_EPS = 1e-6
D_MODEL = 2048
S5_GROUP = 16
S5_STATE = 64
S5_GROUPS = D_MODEL // S5_GROUP
S5_CH = S5_GROUPS * S5_STATE
HEAD = 128
QK_HEADS = D_MODEL // HEAD
V_HEADS = 2 * QK_HEADS
KEY_DIM = QK_HEADS * HEAD
VAL_DIM = V_HEADS * HEAD
CONV_DIM = 2 * KEY_DIM + VAL_DIM
CONV_W = 4

SUBLANES = 8
LANES = 128
MXU_K = 256
S5_KB = D_MODEL // MXU_K
S5_CB = S5_CH // S5_KB
S5_SCAN_W = S5_CB
VMEM_LIMIT = 60 * 1024 * 1024


def _cparams(sem):
    return pltpu.CompilerParams(dimension_semantics=sem, vmem_limit_bytes=VMEM_LIMIT)


def _rms(x, w):
    return x * lax.rsqrt(jnp.mean(x * x, axis=-1, keepdims=True) + RMS_EPS) * w


def _s5_body(x_ref, nw_ref, d_ref, wb_ref, wc_ref, coef_ref, perm_ref, permt_ref, s0r_ref, s0i_ref,
             g_ref, sr_ref, si_ref, car_ref, h_ref, bu_ref, st_ref, *, tt, multi_seq):
    t = pl.program_id(1)
    seg = tt // SUBLANES

    @pl.when(t == 0)
    def _():
        car_ref[0] = jnp.broadcast_to(s0r_ref[0], (SUBLANES, S5_CH))
        car_ref[1] = jnp.broadcast_to(s0i_ref[0], (SUBLANES, S5_CH))

    h = _rms(x_ref[0], nw_ref[...])
    h_ref[...] = h
    hp = jnp.dot(perm_ref[...], h.astype(BF16), preferred_element_type=F32).astype(BF16)
    row0 = lax.broadcasted_iota(jnp.int32, (SUBLANES, S5_SCAN_W), 0) == 0

    def step(ar, ai, sr, si, xr, xi):
        return ar * sr - ai * si + xr, ar * si + ai * sr + xi

    def project_in(kb):
        ks = slice(kb * MXU_K, (kb + 1) * MXU_K)
        bu_ref[kb % 2] = jnp.dot(hp[:, ks], wb_ref[kb], preferred_element_type=F32)

    def project_out(kb):
        ks = slice(kb * MXU_K, (kb + 1) * MXU_K)
        yp = jnp.dot(st_ref[kb % 2].astype(BF16), wc_ref[kb], preferred_element_type=F32)
        y_hi = yp.astype(BF16)
        y_lo = (yp - y_hi.astype(F32)).astype(BF16)
        y2 = jnp.dot(permt_ref[...], jnp.concatenate([y_hi, y_lo], axis=1),
                     preferred_element_type=F32)
        y = y2[:, :MXU_K] + y2[:, MXU_K:] + d_ref[:, ks] * h_ref[:, ks]
        g_ref[0, :, ks] = jax.nn.gelu(y).astype(BF16)

    def scan(kb, part):
        lo = part * S5_SCAN_W
        re_cols = slice(lo, lo + S5_SCAN_W)
        im_cols = slice(S5_CB + lo, S5_CB + lo + S5_SCAN_W)
        cs = slice(kb * S5_CB + lo, kb * S5_CB + lo + S5_SCAN_W)
        buf = kb % 2

        def bu_rows(j):
            rows = slice(j * SUBLANES, (j + 1) * SUBLANES)
            return bu_ref[buf, rows, re_cols], bu_ref[buf, rows, im_cols]

        ar = coef_ref[4, 0, :, cs]
        ai = coef_ref[4, 1, :, cs]
        if multi_seq:
            sr = car_ref[0, :, cs]
            si = car_ref[1, :, cs]
            for j in range(seg):
                sr, si = step(ar, ai, sr, si, *bu_rows(j))
                rows = slice(j * SUBLANES, (j + 1) * SUBLANES)
                st_ref[buf, rows, re_cols] = sr
                st_ref[buf, rows, im_cols] = si
            car_ref[0, :, cs] = sr
            car_ref[1, :, cs] = si
            return
        er, ei = bu_rows(0)
        for j in range(1, seg):
            er, ei = step(ar, ai, er, ei, *bu_rows(j))
        for lvl, sh in enumerate((1, 2, 4)):
            cr_ = coef_ref[lvl, 0, :, cs]
            ci_ = coef_ref[lvl, 1, :, cs]
            pr = pltpu.roll(er, sh, 0)
            pi = pltpu.roll(ei, sh, 0)
            er, ei = er + cr_ * pr - ci_ * pi, ei + cr_ * pi + ci_ * pr
        c0r = car_ref[0, :, cs]
        c0i = car_ref[1, :, cs]
        cr_ = coef_ref[3, 0, :, cs]
        ci_ = coef_ref[3, 1, :, cs]
        fr = er + cr_ * c0r - ci_ * c0i
        fi = ei + cr_ * c0i + ci_ * c0r
        car_ref[0, :, cs] = jnp.broadcast_to(fr[SUBLANES - 1:SUBLANES], (SUBLANES, S5_SCAN_W))
        car_ref[1, :, cs] = jnp.broadcast_to(fi[SUBLANES - 1:SUBLANES], (SUBLANES, S5_SCAN_W))
        sr = jnp.where(row0, c0r, pltpu.roll(fr, 1, 0))
        si = jnp.where(row0, c0i, pltpu.roll(fi, 1, 0))
        for j in range(seg):
            sr, si = step(ar, ai, sr, si, *bu_rows(j))
            rows = slice(j * SUBLANES, (j + 1) * SUBLANES)
            st_ref[buf, rows, re_cols] = sr
            st_ref[buf, rows, im_cols] = si

    project_in(0)
    for kb in range(S5_KB):
        if kb + 1 < S5_KB:
            project_in(kb + 1)
        if kb >= 1:
            project_out(kb - 1)
        for part in range(S5_CB // S5_SCAN_W):
            scan(kb, part)
    project_out(S5_KB - 1)

    @pl.when(t == pl.num_programs(1) - 1)
    def _():
        n_state = SUBLANES if multi_seq else 1
        sr_ref[0] = car_ref[0, 0:n_state, :]
        si_ref[0] = car_ref[1, 0:n_state, :]


def _s5_call(x, nw, d, wb, wc, a_re, a_im, s0r, s0i, tt, multi_seq):
    n_b, t_len, _ = x.shape
    seg = tt // SUBLANES
    n_state = s0r.shape[1]
    assert n_state == (SUBLANES if multi_seq else 1) and (not multi_seq or t_len == tt)
    coef = _s5_coef(a_re, a_im, seg)
    p_idx = jnp.arange(tt)
    src = (p_idx % SUBLANES) * seg + p_idx // SUBLANES
    perm = (src[:, None] == p_idx[None, :]).astype(BF16)
    const = lambda *shape: pl.BlockSpec(shape, lambda b, t: (0,) * len(shape))
    state = pl.BlockSpec((1, n_state, S5_CH), lambda b, t: (b, 0, 0))
    return pl.pallas_call(
        functools.partial(_s5_body, tt=tt, multi_seq=multi_seq),
        grid=(n_b, t_len // tt),
        in_specs=[pl.BlockSpec((1, tt, D_MODEL), lambda b, t: (b, t, 0)),
                  const(1, D_MODEL), const(1, D_MODEL),
                  const(S5_KB, MXU_K, 2 * S5_CB), const(S5_KB, 2 * S5_CB, MXU_K),
                  const(5, 2, SUBLANES, S5_CH), const(tt, tt), const(tt, tt), state, state],
        out_specs=[pl.BlockSpec((1, tt, D_MODEL), lambda b, t: (b, t, 0)), state, state],
        out_shape=[jax.ShapeDtypeStruct((n_b, t_len, D_MODEL), BF16),
                   jax.ShapeDtypeStruct((n_b, n_state, S5_CH), F32),
                   jax.ShapeDtypeStruct((n_b, n_state, S5_CH), F32)],
        scratch_shapes=[pltpu.VMEM((2, SUBLANES, S5_CH), F32),
                        pltpu.VMEM((tt, D_MODEL), F32),
                        pltpu.VMEM((2, tt, 2 * S5_CB), F32),
                        pltpu.VMEM((2, tt, 2 * S5_CB), F32)],
        compiler_params=_cparams(("arbitrary", "arbitrary")),
        name="s5_mixer",
    )(x, nw, d, wb, wc, coef, perm, perm.T, s0r, s0i)


def _s5_coef(a_re, a_im, seg):
    assert seg & (seg - 1) == 0, seg
    ar, ai = a_re, a_im
    for _ in range(int(math.log2(seg))):
        ar, ai = ar * ar - ai * ai, 2.0 * ar * ai
    pw_r, pw_i = [ar], [ai]
    for _ in range(SUBLANES - 1):
        pr, pi = pw_r[-1], pw_i[-1]
        pw_r.append(pr * ar - pi * ai)
        pw_i.append(pr * ai + pi * ar)
    row = jnp.arange(SUBLANES)[:, None]
    levels = []
    for sh in (1, 2, 4):
        m = (row >= sh).astype(F32)
        levels.append(jnp.stack([m * pw_r[sh - 1], m * pw_i[sh - 1]]))
    levels.append(jnp.stack([jnp.concatenate(pw_r, axis=0), jnp.concatenate(pw_i, axis=0)]))
    ones = jnp.ones((SUBLANES, 1), F32)
    levels.append(jnp.stack([ones * a_re, ones * a_im]))
    return jnp.stack(levels)


def _s5_params(lam_re, lam_im, log_dt, b_re, b_im, c_re, c_im):
    lr = jnp.minimum(lam_re.astype(F32), -1e-4)
    li = lam_im.astype(F32)
    dt = jnp.exp(log_dt.astype(F32))[:, None]
    mag = jnp.exp(lr * dt)
    ang = li * dt
    a_re = mag * jnp.cos(ang)
    a_im = mag * jnp.sin(ang)
    nr = a_re - 1.0
    den = lr * lr + li * li
    k_re = (nr * lr + a_im * li) / den
    k_im = (a_im * lr - nr * li) / den
    br = b_re.astype(F32)
    bi = b_im.astype(F32)
    bb_re = k_re[..., None] * br - k_im[..., None] * bi
    bb_im = k_re[..., None] * bi + k_im[..., None] * br

    gl = S5_GROUPS // S5_KB
    eye = jnp.eye(gl, dtype=F32)

    def blockdiag_in(bb):
        bbk = bb.reshape(S5_KB, gl, S5_STATE, S5_GROUP)
        w = jnp.einsum('kgph,gf->kghfp', bbk, eye)
        return w.reshape(S5_KB, gl * S5_GROUP, gl * S5_STATE)

    def blockdiag_out(c):
        ck = c.reshape(S5_KB, gl, S5_GROUP, S5_STATE)
        w = jnp.einsum('kghp,gf->kgpfh', ck, eye)
        return w.reshape(S5_KB, gl * S5_STATE, gl * S5_GROUP)

    wb = jnp.concatenate([blockdiag_in(bb_re), blockdiag_in(bb_im)], axis=2).astype(BF16)
    wc = jnp.concatenate([blockdiag_out(c_re.astype(F32)),
                          -blockdiag_out(c_im.astype(F32))], axis=1).astype(BF16)

    return wb, wc, a_re.reshape(1, S5_CH), a_im.reshape(1, S5_CH)


def _glu_body(g_ref, wa_ref, wg_ref, x_ref, o_ref):
    g = g_ref[...]
    za = jnp.dot(g, wa_ref[...], preferred_element_type=F32)
    zg = jnp.dot(g, wg_ref[...], preferred_element_type=F32)
    o_ref[...] = x_ref[...] + za * jax.nn.sigmoid(zg)


def _glu_call(g, w_glu, x, tm, tn):
    m = g.shape[0]
    nj = D_MODEL // tn
    return pl.pallas_call(
        _glu_body,
        grid=(m // tm, nj),
        in_specs=[pl.BlockSpec((tm, D_MODEL), lambda i, j: (i, 0)),
                  pl.BlockSpec((D_MODEL, tn), lambda i, j: (0, j)),
                  pl.BlockSpec((D_MODEL, tn), lambda i, j: (0, j + nj)),
                  pl.BlockSpec((tm, tn), lambda i, j: (i, j))],
        out_specs=pl.BlockSpec((tm, tn), lambda i, j: (i, j)),
        out_shape=jax.ShapeDtypeStruct((m, D_MODEL), F32),
        compiler_params=_cparams(("parallel", "arbitrary")),
        name="glu_residual",
    )(g, w_glu, w_glu, x)


def _ffn_body(x_ref, nw_ref, wu_ref, wd_ref, fw_ref, o_ref, h_ref, *, final_norm):
    f = pl.program_id(1)

    @pl.when(f == 0)
    def _():
        x = x_ref[...]
        h_ref[...] = _rms(x, nw_ref[...]).astype(BF16)
        o_ref[...] = x

    a = jnp.maximum(jnp.dot(h_ref[...], wu_ref[...], preferred_element_type=F32), 0.0)
    o_ref[...] += jnp.dot((a * a).astype(BF16), wd_ref[...], preferred_element_type=F32)

    if final_norm:
        @pl.when(f == pl.num_programs(1) - 1)
        def _():
            o_ref[...] = _rms(o_ref[...], fw_ref[...])


def _ffn_call(x, nw, w_up, w_down, layer, fw, tm, tf, final_norm):
    m = x.shape[0]
    d_ff = w_up.shape[2]
    return pl.pallas_call(
        functools.partial(_ffn_body, final_norm=final_norm),
        grid=(m // tm, d_ff // tf),
        in_specs=[pl.BlockSpec((tm, D_MODEL), lambda i, f: (i, 0)),
                  pl.BlockSpec((1, D_MODEL), lambda i, f: (0, 0)),
                  pl.BlockSpec((None, D_MODEL, tf), lambda i, f: (layer, 0, f)),
                  pl.BlockSpec((None, tf, D_MODEL), lambda i, f: (layer, f, 0)),
                  pl.BlockSpec((1, D_MODEL), lambda i, f: (0, 0))],
        out_specs=pl.BlockSpec((tm, D_MODEL), lambda i, f: (i, 0)),
        out_shape=jax.ShapeDtypeStruct((m, D_MODEL), F32),
        scratch_shapes=[pltpu.VMEM((tm, D_MODEL), BF16)],
        compiler_params=_cparams(("parallel", "arbitrary")),
        name="ffn_residual",
    )(x, nw, w_up, w_down, fw)


def _matmul_res_body(a_ref, w_ref, x_ref, o_ref):
    o_ref[...] = x_ref[...] + jnp.dot(a_ref[...], w_ref[...], preferred_element_type=F32)


def _matmul_res_call(a, w, x, tm, tn):
    m, k = a.shape
    n = w.shape[1]
    return pl.pallas_call(
        _matmul_res_body,
        grid=(m // tm, n // tn),
        in_specs=[pl.BlockSpec((tm, k), lambda i, j: (i, 0)),
                  pl.BlockSpec((k, tn), lambda i, j: (0, j)),
                  pl.BlockSpec((tm, tn), lambda i, j: (i, j))],
        out_specs=pl.BlockSpec((tm, tn), lambda i, j: (i, j)),
        out_shape=jax.ShapeDtypeStruct((m, n), F32),
        compiler_params=_cparams(("parallel", "arbitrary")),
        name="matmul_residual",
    )(a, w, x)


INPROJ_TN = 1024
N_CONV_TILES = CONV_DIM // INPROJ_TN
N_Z_TILES = VAL_DIM // INPROJ_TN
N_QK_TILES = 2 * KEY_DIM // INPROJ_TN
N_Q_TILES = KEY_DIM // INPROJ_TN
CONV_ROWS = 64
W_RING = 3


def _inproj_body(x_ref, nw_ref, w_hbm, wba_ref, cprev_ref, cw_ref, alog_ref, dtb_ref,
                 qkv_ref, z_ref, g_ref, beta_ref, nc_ref, hs_ref, hist_ref, ra_ref, rb_ref,
                 wbuf_ref, wsem,
                 *, tm, seq_rows, tiles_per_seq, gate_group):
    n_seq = tm // seq_rows
    i = pl.program_id(0)
    j = pl.program_id(1)
    tn = INPROJ_TN
    n_sub = tn // MXU_K
    tail = CONV_W - 1
    pad = SUBLANES

    n_j = pl.num_programs(1)
    step = i * n_j + j
    n_steps = pl.num_programs(0) * n_j

    def w_copy(s):
        col = pl.multiple_of((s % n_j) * (2 * tn), 2 * tn)
        slot = s % W_RING
        return pltpu.make_async_copy(w_hbm.at[:, pl.ds(col, 2 * tn)], wbuf_ref.at[slot], wsem.at[slot])

    @pl.when(step == 0)
    def _():
        for s0 in range(W_RING - 1):
            w_copy(s0).start()

    @pl.when(step + (W_RING - 1) < n_steps)
    def _():
        w_copy(step + (W_RING - 1)).start()

    w_copy(step).wait()
    w_ref = wbuf_ref.at[step % W_RING]

    @pl.when(j == 0)
    def _():
        hs_ref[...] = _rms(x_ref[...], nw_ref[...]).astype(BF16)
        ba = jnp.dot(hs_ref[...], wba_ref[...], preferred_element_type=F32)
        zz = ba[:, V_HEADS:2 * V_HEADS] + dtb_ref[...]
        softplus = jnp.maximum(zz, 0.0) + jnp.log1p(jnp.exp(-jnp.abs(zz)))
        g_all = -jnp.exp(alog_ref[...]) * softplus
        beta_all = jax.nn.sigmoid(ba[:, 0:V_HEADS])
        for gi in range(V_HEADS // gate_group):
            g_ref[gi] = g_all[:, gi * gate_group:(gi + 1) * gate_group]
            beta_ref[gi] = beta_all[:, gi * gate_group:(gi + 1) * gate_group]

    @pl.when((i % tiles_per_seq == 0) & (j == 0))
    def _():
        hist_ref[...] = jnp.concatenate(
            [jnp.zeros((pad - tail, CONV_DIM), F32), cprev_ref[0]], axis=0)

    def project(r_ref, half, s):
        c0 = half * tn + s * MXU_K
        r_ref[s, pad:pad + tm, :] = jnp.dot(hs_ref[...], w_ref[:, c0:c0 + MXU_K],
                                            preferred_element_type=F32)

    def conv_rows(win, c0):
        acc = win[pad:] * cw_ref[tail:tail + 1, c0:c0 + MXU_K]
        for d in range(1, CONV_W):
            shifted = pltpu.roll(win, d, 0)[pad:]
            acc = acc + shifted * cw_ref[tail - d:tail - d + 1, c0:c0 + MXU_K]
        return acc

    def epilogue(r_ref, tile, out_half, s):
        c0 = tile * tn + s * MXU_K
        o0 = out_half * tn + s * MXU_K
        l2 = tile < N_QK_TILES
        scale = HEAD ** -0.5 if tile < N_Q_TILES else 1.0

        def finish(acc, rows):
            c = acc * jax.nn.sigmoid(acc)
            if l2:
                for hd in range(MXU_K // HEAD):
                    ch = c[:, hd * HEAD:(hd + 1) * HEAD]
                    ch = ch * (lax.rsqrt(jnp.sum(ch * ch, axis=-1, keepdims=True) + L2_EPS) * scale)
                    qkv_ref[rows, o0 + hd * HEAD:o0 + (hd + 1) * HEAD] = ch.astype(BF16)
            else:
                qkv_ref[rows, o0:o0 + MXU_K] = c.astype(BF16)

        r_ref[s, 0:pad, :] = hist_ref[:, c0:c0 + MXU_K]
        hist_ref[:, c0:c0 + MXU_K] = r_ref[s, tm:tm + pad, :]
        for sq in range(n_seq):
            end = pad + (sq + 1) * seq_rows
            nc_ref[sq, :, c0:c0 + MXU_K] = r_ref[s, end - tail:end, :]
        for r0 in range(0, tm, CONV_ROWS):
            rc = min(CONV_ROWS, tm - r0)
            finish(conv_rows(r_ref[s, r0:r0 + rc + pad, :], c0), slice(r0, r0 + rc))
        for sq in range(1, n_seq):
            r0 = sq * seq_rows
            win = jnp.concatenate([jnp.zeros((pad - tail, MXU_K), F32), cprev_ref[sq, :, c0:c0 + MXU_K],
                                   r_ref[s, pad + r0:pad + r0 + pad, :]], axis=0)
            finish(conv_rows(win, c0), slice(r0, r0 + pad))

    def project_z(half, s):
        c0 = half * tn + s * MXU_K
        zz = jnp.dot(hs_ref[...], w_ref[:, c0:c0 + MXU_K], preferred_element_type=F32)
        z_ref[:, c0:c0 + MXU_K] = (zz * jax.nn.sigmoid(zz)).astype(BF16)

    def step_body(step):
        cur, prev = step % 2, (step - 1) % 2
        for half, r_ref in ((0, ra_ref), (1, rb_ref)):
            for s in range(n_sub):
                if step < n_conv_steps:
                    project(r_ref.at[cur], half, s)
                else:
                    project_z(half, s)
                if 0 < step <= n_conv_steps:
                    epilogue(r_ref.at[prev], 2 * (step - 1) + half, half, s)

    n_conv_steps = N_CONV_TILES // 2
    for step in range(n_conv_steps + 1):
        pl.when(j == step)(functools.partial(step_body, step))
    pl.when(j > n_conv_steps)(functools.partial(step_body, n_conv_steps + 1))


def _inproj_call(x, nw, w_qkvz, w_ba, conv_prev, conv_w, a_log, dt_bias, t_len, tm, gate_group):
    m = x.shape[0]
    n_grp = V_HEADS // gate_group
    tn2 = 2 * INPROJ_TN
    n_cs = N_CONV_TILES // 2
    n_zs = N_Z_TILES // 2
    seq_rows = min(t_len, tm)
    n_seq = tm // seq_rows
    tpb = t_len // seq_rows
    return pl.pallas_call(
        functools.partial(_inproj_body, tm=tm, seq_rows=seq_rows, tiles_per_seq=tpb,
                          gate_group=gate_group),
        grid=(m // tm, n_cs + n_zs),
        in_specs=[pl.BlockSpec((tm, D_MODEL), lambda i, j: (i, 0), pipeline_mode=pl.Buffered(1)),
                  pl.BlockSpec((1, D_MODEL), lambda i, j: (0, 0)),
                  pl.BlockSpec(memory_space=pl.ANY),
                  pl.BlockSpec((D_MODEL, LANES), lambda i, j: (0, 0)),
                  pl.BlockSpec((n_seq, CONV_W - 1, CONV_DIM), lambda i, j: (i // tpb, 0, 0)),
                  pl.BlockSpec((CONV_W, CONV_DIM), lambda i, j: (0, 0)),
                  pl.BlockSpec((1, V_HEADS), lambda i, j: (0, 0)),
                  pl.BlockSpec((1, V_HEADS), lambda i, j: (0, 0))],
        out_specs=[pl.BlockSpec((tm, tn2), lambda i, j: (i, jnp.clip(j - 1, 0, n_cs - 1))),
                   pl.BlockSpec((tm, tn2), lambda i, j: (i, jnp.maximum(j - n_cs, 0))),
                   pl.BlockSpec((n_grp, tm, gate_group), lambda i, j: (0, i, 0)),
                   pl.BlockSpec((n_grp, tm, gate_group), lambda i, j: (0, i, 0)),
                   pl.BlockSpec((n_seq, CONV_W - 1, CONV_DIM), lambda i, j: (i, 0, 0))],
        out_shape=[jax.ShapeDtypeStruct((m, CONV_DIM), BF16),
                   jax.ShapeDtypeStruct((m, VAL_DIM), BF16),
                   jax.ShapeDtypeStruct((n_grp, m, gate_group), F32),
                   jax.ShapeDtypeStruct((n_grp, m, gate_group), F32),
                   jax.ShapeDtypeStruct((m // seq_rows, CONV_W - 1, CONV_DIM), F32)],
        scratch_shapes=[pltpu.VMEM((tm, D_MODEL), BF16),
                        pltpu.VMEM((SUBLANES, CONV_DIM), F32),
                        pltpu.VMEM((2, INPROJ_TN // MXU_K, tm + SUBLANES, MXU_K), F32),
                        pltpu.VMEM((2, INPROJ_TN // MXU_K, tm + SUBLANES, MXU_K), F32),
                        pltpu.VMEM((W_RING, D_MODEL, tn2), BF16),
                        pltpu.SemaphoreType.DMA((W_RING,))],
        compiler_params=_cparams(("arbitrary", "arbitrary")),
        name="gdn_inproj_conv",
    )(x, nw, w_qkvz, w_ba, conv_prev, conv_w, a_log, dt_bias)


INV_BASE = 16


def _mm16(a, b):
    return jnp.dot(a.astype(BF16), b.astype(BF16), preferred_element_type=F32)


def _tri_inverse_all(nmats, ri, ci, chunk):
    base = min(INV_BASE, chunk)
    sh = int(math.log2(base))
    same = (ri >> sh) == (ci >> sh)
    eye = (ri == ci).astype(F32)
    ms = [jnp.where(same, -n, 0.0) for n in nmats]
    ps = [eye + m for m in ms]
    for _ in range(sh - 1):
        ms = [_mm16(m, m) for m in ms]
        ps = [p + _mm16(p, m) for p, m in zip(ps, ms)]
    size = base
    while size < chunk:
        sh += 1
        size *= 2
        same_next = (ri >> sh) == (ci >> sh)
        sel = same_next & jnp.logical_not(same)
        ts = [_mm16(p, jnp.where(sel, n, 0.0)) for p, n in zip(ps, nmats)]
        ps = [p - _mm16(t, p) for p, t in zip(ps, ts)]
        same = same_next
    return ps


def _delta_body(q_ref, k_ref, v_ref, z_ref, gc_ref, bc_ref, s0_ref, nw_ref,
                o_ref, sf_ref, s_ref, *, chunk, cpb, hb):
    c_idx = pl.program_id(2)

    @pl.when(c_idx == 0)
    def _():
        s_ref[...] = s0_ref[0]

    ri = lax.broadcasted_iota(jnp.int32, (chunk, chunk), 0)
    ci = lax.broadcasted_iota(jnp.int32, (chunk, chunk), 1)
    causal = ri >= ci
    strict = ri > ci
    tril = causal.astype(F32)
    triu = (ri <= ci).astype(F32)
    eye = (ri == ci).astype(F32)
    hi = lax.Precision.HIGHEST
    nt = (((1,), (1,)), ((), ()))
    tn = (((0,), (0,)), ((), ()))
    hq = hb // 2

    kqk, pre = {}, {}
    for c in range(cpb):
        rows = slice(c * chunk, (c + 1) * chunk)
        for j in range(hq):
            k16 = k_ref[0, rows, j * HEAD:(j + 1) * HEAD]
            q16 = q_ref[0, rows, j * HEAD:(j + 1) * HEAD]
            kqk[c, j] = lax.dot_general(jnp.concatenate([k16, q16], axis=0), k16, nt,
                                        preferred_element_type=F32)
    nmats, keys = [], []
    for c in range(cpb):
        rows = slice(c * chunk, (c + 1) * chunk)
        g_raw = gc_ref[0, rows, :]
        gcol = jnp.dot(tril, g_raw, precision=hi, preferred_element_type=F32)
        grow = lax.dot_general(g_raw, triu, tn, precision=hi, preferred_element_type=F32)
        bcol = bc_ref[0, rows, :]
        brow = lax.dot_general(bcol, eye, tn, precision=hi, preferred_element_type=F32)
        for hh in range(hb):
            j = hh // 2
            k = k_ref[0, rows, j * HEAD:(j + 1) * HEAD].astype(F32)
            q = q_ref[0, rows, j * HEAD:(j + 1) * HEAD].astype(F32)
            v16 = v_ref[0, rows, hh * HEAD:(hh + 1) * HEAD]
            gc = gcol[:, hh:hh + 1]
            gr = grow[hh:hh + 1, :]
            beta = bcol[:, hh:hh + 1]
            decay = jnp.exp(jnp.where(causal, gc - gr, -jnp.inf))
            egc = jnp.exp(gc)
            g_last = gc[chunk - 1:chunk, :]
            kk = kqk[c, j][:chunk]
            qk = kqk[c, j][chunk:]
            nmats.append(jnp.where(strict, (kk * beta) * decay, 0.0))
            keys.append((c, hh))
            pre[c, hh] = dict(
                rhs=jnp.concatenate([v16, (k * egc).astype(BF16)], axis=1),
                brow=brow[hh:hh + 1, :],
                qg=(q * egc).astype(BF16),
                intra=(qk * decay).astype(BF16),
                kd=(k * jnp.exp(g_last - gc)).astype(BF16),
                sdec=jnp.exp(g_last))
    tinv = _tri_inverse_all(nmats, ri, ci, chunk)
    for key, t in zip(keys, tinv):
        pre[key]['sol'] = jnp.dot((t * pre[key]['brow']).astype(BF16), pre[key]['rhs'],
                                  preferred_element_type=F32)

    for c in range(cpb):
        rows = slice(c * chunk, (c + 1) * chunk)
        ws = []
        for hh in range(hb):
            d = pre[c, hh]
            wq = jnp.concatenate([d['sol'][:, HEAD:].astype(BF16), d['qg']], axis=0)
            ws.append(jnp.dot(wq, s_ref[hh].astype(BF16), preferred_element_type=F32))
        for hh in range(hb):
            d = pre[c, hh]
            v16 = (d['sol'][:, :HEAD] - ws[hh][:chunk]).astype(BF16)
            o = ws[hh][chunk:] + jnp.dot(d['intra'], v16, preferred_element_type=F32)
            s_ref[hh] = s_ref[hh] * d['sdec'] + lax.dot_general(d['kd'], v16, tn,
                                                                preferred_element_type=F32)
            o = o * lax.rsqrt(jnp.mean(o * o, axis=-1, keepdims=True) + RMS_EPS) * nw_ref[...]
            gate = z_ref[0, rows, hh * HEAD:(hh + 1) * HEAD]
            o_ref[0, rows, hh * HEAD:(hh + 1) * HEAD] = (o * gate).astype(BF16)

    @pl.when(c_idx == pl.num_programs(2) - 1)
    def _():
        sf_ref[0] = s_ref[...]


def _delta_call(qkv, z, g, beta, s0, norm_w, chunk, cpb, hb):
    n_b, t_len, _ = qkv.shape
    nhg = V_HEADS // hb
    tc = chunk * cpb
    n_tc = t_len // tc
    assert g.shape == (nhg, n_b * t_len, hb), g.shape
    hq = hb // 2
    assert KEY_DIM % (hq * HEAD) == 0 and (2 * KEY_DIM) % (hb * HEAD) == 0, hb
    q_off = 0
    k_off = KEY_DIM // (hq * HEAD)
    v_off = 2 * KEY_DIM // (hb * HEAD)
    return pl.pallas_call(
        functools.partial(_delta_body, chunk=chunk, cpb=cpb, hb=hb),
        grid=(n_b, nhg, t_len // tc),
        in_specs=[pl.BlockSpec((1, tc, hq * HEAD), lambda b, h, c: (b, c, q_off + h)),
                  pl.BlockSpec((1, tc, hq * HEAD), lambda b, h, c: (b, c, k_off + h)),
                  pl.BlockSpec((1, tc, hb * HEAD), lambda b, h, c: (b, c, v_off + h)),
                  pl.BlockSpec((1, tc, hb * HEAD), lambda b, h, c: (b, c, h)),
                  pl.BlockSpec((1, tc, hb), lambda b, h, c: (h, b * n_tc + c, 0)),
                  pl.BlockSpec((1, tc, hb), lambda b, h, c: (h, b * n_tc + c, 0)),
                  pl.BlockSpec((1, hb, HEAD, HEAD), lambda b, h, c: (b, h, 0, 0)),
                  pl.BlockSpec((1, HEAD), lambda b, h, c: (0, 0))],
        out_specs=[pl.BlockSpec((1, tc, hb * HEAD), lambda b, h, c: (b, c, h)),
                   pl.BlockSpec((1, hb, HEAD, HEAD), lambda b, h, c: (b, h, 0, 0))],
        out_shape=[jax.ShapeDtypeStruct((n_b, t_len, VAL_DIM), BF16),
                   jax.ShapeDtypeStruct((n_b, V_HEADS, HEAD, HEAD), F32)],
        scratch_shapes=[pltpu.VMEM((hb, HEAD, HEAD), F32)],
        compiler_params=_cparams(("arbitrary", "arbitrary", "arbitrary")),
        name="gated_delta_rule",
    )(qkv, qkv, qkv, z, g, beta, s0, norm_w)


def _tile(m, pref):
    t = min(m, pref)
    assert m % t == 0, (m, t)
    return t


def _trunk(x, s5_re0, s5_im0, conv0, s0, p):
    n_b, t_len, _ = x.shape
    m = n_b * t_len
    tm = _tile(m, 512)
    tm_wide = _tile(m, 1024)

    tt = _tile(t_len, 256)
    multi_seq = SUBLANES * t_len <= 256 and n_b % SUBLANES == 0
    n_blk, n_state = (n_b // SUBLANES, SUBLANES) if multi_seq else (n_b, 1)
    if multi_seq:
        tt = SUBLANES * t_len
    g, sr, si = _s5_call(x.reshape(n_blk, n_state * t_len, D_MODEL), p['nm0'], p['s5_d'], p['wb'], p['wc'],
                         p['a_re'], p['a_im'], s5_re0.reshape(n_blk, n_state, S5_CH),
                         s5_im0.reshape(n_blk, n_state, S5_CH), tt, multi_seq)
    x2 = x.reshape(m, D_MODEL)
    x2 = _glu_call(g.reshape(m, D_MODEL), p['w_glu'], x2, tm_wide, 1024)
    x2 = _ffn_call(x2, p['nf0'], p['w_up'], p['w_down'], 0, p['nfin'], tm_wide, 1024, False)

    chunk = min(t_len, 64)
    cpb = _tile(t_len // chunk, 4)
    hb = V_HEADS
    qkv, z, gg, beta, tails = _inproj_call(
        x2, p['nm1'], p['w_qkvz'], p['w_ba'], conv0, p['conv_w'], p['a_log'], p['dt_bias'],
        t_len, tm, hb)
    tpb = t_len // min(t_len, tm)
    new_conv = tails[tpb - 1::tpb]
    og, s_fin = _delta_call(qkv.reshape(n_b, t_len, -1), z.reshape(n_b, t_len, VAL_DIM),
                            gg, beta, s0, p['gdn_norm_w'], chunk, cpb, hb)
    x2 = _matmul_res_call(og.reshape(m, VAL_DIM), p['w_out'], x2, tm_wide, 1024)
    y = _ffn_call(x2, p['nf1'], p['w_up'], p['w_down'], 1, p['nfin'], tm_wide, 1024, True)

    return (y.reshape(n_b, t_len, D_MODEL),
            sr.reshape(1, n_b, S5_GROUPS, S5_STATE), si.reshape(1, n_b, S5_GROUPS, S5_STATE),
            new_conv[None], s_fin[None])


def kernel(x_prompt, x_sample, state_s5_re, state_s5_im, state_gdn_conv, state_gdn_s,
           norm_mix, norm_ffn, norm_final,
           s5_lam_re, s5_lam_im, s5_log_dt, s5_b_re, s5_b_im, s5_c_re, s5_c_im,
           s5_d, s5_w_glu,
           gdn_w_in, gdn_conv_w, gdn_a_log, gdn_dt_bias, gdn_norm_w, gdn_w_out,
           ffn_w_up, ffn_w_down):
    wb, wc, a_re, a_im = _s5_params(s5_lam_re[0], s5_lam_im[0], s5_log_dt[0], s5_b_re[0], s5_b_im[0],
                              s5_c_re[0], s5_c_im[0])
    w_in = gdn_w_in[0]
    ba = w_in[:, CONV_DIM + VAL_DIM:]
    w_ba = jnp.concatenate([ba, jnp.zeros((D_MODEL, LANES - ba.shape[1]), w_in.dtype)], axis=1)
    row = lambda a: a.reshape(1, -1).astype(F32)
    p = dict(
        nm0=row(norm_mix[0]), nm1=row(norm_mix[1]), nf0=row(norm_ffn[0]), nf1=row(norm_ffn[1]),
        nfin=row(norm_final), s5_d=row(s5_d[0]), wb=wb, wc=wc, a_re=a_re, a_im=a_im,
        w_glu=s5_w_glu[0].astype(BF16),
        w_up=ffn_w_up.astype(BF16), w_down=ffn_w_down.astype(BF16),
        w_qkvz=w_in.astype(BF16), w_ba=w_ba.astype(BF16),
        conv_w=gdn_conv_w[0].astype(F32), a_log=row(gdn_a_log[0]), dt_bias=row(gdn_dt_bias[0]),
        gdn_norm_w=row(gdn_norm_w[0]), w_out=gdn_w_out[0].astype(BF16),
    )
    n_bp = x_prompt.shape[0]
    z_state = jnp.zeros((n_bp, S5_GROUPS, S5_STATE), F32)
    z_conv = jnp.zeros((n_bp, CONV_W - 1, CONV_DIM), F32)
    z_s = jnp.zeros((n_bp, V_HEADS, HEAD, HEAD), F32)

    yp, p_re, p_im, p_conv, p_s = _trunk(x_prompt, z_state, z_state, z_conv, z_s, p)
    ys, s_re, s_im, s_conv, s_s = _trunk(x_sample, state_s5_re[0], state_s5_im[0],
                                         state_gdn_conv[0], state_gdn_s[0], p)
    return (yp, ys, p_re, p_im, p_conv, p_s, s_re, s_im, s_conv, s_s)
```
